```python
import jax
import jax.numpy as jnp
from jax import lax

D_MODEL = 2048
BATCH = 2
SEQ = 4096
DEPTH = 1
DEC_BATCH = 128
DEC_SEQ = 1
PAST_LEN = 2048
PAGE_SIZE = 128

N_HEADS_NSA = 16
HD_NSA = 64
D_NSA = N_HEADS_NSA * HD_NSA
KV_HEADS = 4
GQA = N_HEADS_NSA // KV_HEADS
D_KV = KV_HEADS * HD_NSA
ROPE_DIM = HD_NSA // 4
ROPE_THETA = 500000.0
CMP_BLOCK = 32
CMP_STRIDE = 16
CMP_HIDDEN = 128
SEL_BLOCK = 64
SEL_TOPN = 16
N_LOCAL_FORCED = 2
WINDOW = 512
Q_BLOCK = 128
N_HEADS_RWKV = 16
HS = 64
D_RWKV = N_HEADS_RWKV * HS
LORA_W = 64
LORA_A = 64
LORA_G = 160
D_FF = 5632
RMS_EPS = 1e-6
GN_EPS = 64e-5
NEG = -1e30
FORCE_BONUS = 1e4
NSA_COLS = D_NSA + 6 * D_KV + 3 * N_HEADS_NSA
RWKV_COLS = 3 * D_RWKV + LORA_W + LORA_A + LORA_G
IN_COLS = NSA_COLS + RWKV_COLS
NSA_SPLITS = (D_NSA, D_NSA + D_KV, D_NSA + 2 * D_KV, D_NSA + 3 * D_KV, D_NSA + 4 * D_KV, D_NSA + 5 * D_KV, D_NSA + 6 * D_KV)
RWKV_SPLITS = (D_RWKV, 2 * D_RWKV, 3 * D_RWKV, 3 * D_RWKV + LORA_W, 3 * D_RWKV + LORA_W + LORA_A)

kernel_name = 'hybrid_nsa_rwkv7_macaron_step'


def rms_norm(x, g):
    xf = x.astype(jnp.float32)
    y = xf * lax.rsqrt(jnp.mean(xf * xf, axis=-1, keepdims=True) + RMS_EPS)
    return (y * g.astype(jnp.float32)).astype(x.dtype)


def ffn_half(x, g_pre, w_gu, w_dn, g_post):
    gate, up = jnp.split(rms_norm(x, g_pre) @ w_gu, 2, axis=-1)
    return x + 0.5 * rms_norm((jax.nn.silu(gate) * up) @ w_dn, g_post)


def rope_partial(x, pos):
    half = ROPE_DIM // 2
    inv = ROPE_THETA ** (-jnp.arange(half, dtype=jnp.float32) / half)
    ang = pos.astype(jnp.float32)[:, None] * inv[None, :]
    cos, sin = jnp.cos(ang)[:, None, :], jnp.sin(ang)[:, None, :]
    x1 = x[..., :half].astype(jnp.float32)
    x2 = x[..., half:ROPE_DIM].astype(jnp.float32)
    rot = jnp.concatenate([x1 * cos - x2 * sin, x2 * cos + x1 * sin], axis=-1).astype(x.dtype)
    return jnp.concatenate([rot, x[..., ROPE_DIM:]], axis=-1)


def masked_softmax(s, mask):
    p = jax.nn.softmax(jnp.where(mask, s.astype(jnp.float32), NEG), axis=-1)
    return p * mask


def compress(x, pe, w1, w2):
    b, t = x.shape[:2]
    r = CMP_BLOCK // CMP_STRIDE
    nc = (t - CMP_BLOCK) // CMP_STRIDE + 1
    nseg = nc + r - 1
    seg = x[:, :nseg * CMP_STRIDE].reshape(b, nseg, CMP_STRIDE, KV_HEADS, HD_NSA)
    w1s = w1.reshape(r, CMP_STRIDE, HD_NSA, CMP_HIDDEN)
    pre = jnp.einsum('ld,ldh->h', pe, w1)
    for m in range(r):
        pre = pre + jnp.einsum('bnskd,sdh->bnkh', seg[:, m:m + nc], w1s[m])
    return jnp.einsum('bnkh,hd->bnkd', jax.nn.silu(pre), w2)


def cmp_to_sel(nc, ns):
    i = jnp.arange(nc)[:, None] * CMP_STRIDE
    j = jnp.arange(ns)[None, :] * SEL_BLOCK
    ov = jnp.minimum(i + CMP_BLOCK, j + SEL_BLOCK) - jnp.maximum(i, j)
    return (jnp.maximum(ov, 0) // CMP_STRIDE).astype(jnp.float32)


def nsa_attend(q_raw, q_rot, gates, q_pos, k_cmp, v_cmp, k_sel_blk, v_sel_blk, k_win, v_win, win_pos):
    b, nq = q_raw.shape[:2]
    scale = HD_NSA ** -0.5
    qr = q_raw.reshape(b, nq, KV_HEADS, GQA, HD_NSA)
    qs = q_rot.reshape(b, nq, KV_HEADS, GQA, HD_NSA)
    nc = k_cmp.shape[1]
    cmp_end = jnp.arange(nc) * CMP_STRIDE + CMP_BLOCK - 1
    m_cmp = (cmp_end[None, :] <= q_pos[:, None])[None, :, None, None, :]
    p_cmp = masked_softmax(jnp.einsum('bqkgd,bnkd->bqkgn', qr, k_cmp) * scale, m_cmp)
    o_cmp = jnp.einsum('bqkgn,bnkd->bqkgd', p_cmp.astype(v_cmp.dtype), v_cmp)
    ns = k_sel_blk.shape[1]
    imp = jnp.einsum('bqkn,nj->bqkj', p_cmp.sum(axis=3), cmp_to_sel(nc, ns))
    jb = jnp.arange(ns)[None, :]
    cur = (q_pos // SEL_BLOCK)[:, None]
    valid = jb <= cur
    forced = (jb == 0) | ((cur - jb >= 0) & (cur - jb < N_LOCAL_FORCED))
    score = jnp.where(valid[None, :, None, :], imp + jnp.where(forced, FORCE_BONUS, 0.0)[None, :, None, :], NEG)
    n_top = min(SEL_TOPN, ns)
    top_val, top_idx = lax.top_k(score, n_top)
    top_ok = top_val > NEG / 2
    bi = jnp.arange(b)[:, None, None, None]
    hi = jnp.arange(KV_HEADS)[None, None, :, None]
    k_g = k_sel_blk.transpose(0, 3, 1, 2, 4)[bi, hi, top_idx]
    v_g = v_sel_blk.transpose(0, 3, 1, 2, 4)[bi, hi, top_idx]
    tok_pos = top_idx[..., None] * SEL_BLOCK + jnp.arange(SEL_BLOCK)
    m_sel = (top_ok[..., None] & (tok_pos <= q_pos[None, :, None, None, None]))
    m_sel = m_sel.reshape(b, nq, KV_HEADS, 1, n_top * SEL_BLOCK)
    s_sel = jnp.einsum('bqkgd,bqknld->bqkgnl', qs, k_g) * scale
    p_sel = masked_softmax(s_sel.reshape(b, nq, KV_HEADS, GQA, n_top * SEL_BLOCK), m_sel)
    p_sel = p_sel.reshape(b, nq, KV_HEADS, GQA, n_top, SEL_BLOCK).astype(v_g.dtype)
    o_sel = jnp.einsum('bqkgnl,bqknld->bqkgd', p_sel, v_g)
    dlt = q_pos[:, None] - win_pos[None, :]
    m_win = ((dlt >= 0) & (dlt <= WINDOW) & (win_pos[None, :] >= 0))[None, :, None, None, :]
    p_win = masked_softmax(jnp.einsum('bqkgd,bwkd->bqkgw', qs, k_win) * scale, m_win)
    o_win = jnp.einsum('bqkgw,bwkd->bqkgd', p_win.astype(v_win.dtype), v_win)
    g = gates.reshape(b, nq, KV_HEADS, GQA, 3)
    o = g[..., 0:1] * o_cmp + g[..., 1:2] * o_sel + g[..., 2:3] * o_win
    return o.reshape(b, nq, D_NSA)


def nsa_project(cols, pos):
    b, t = cols.shape[:2]
    q, kc, vc, ks, vs, kw, vw, gl = jnp.split(cols, NSA_SPLITS, axis=-1)
    q = q.reshape(b, t, N_HEADS_NSA, HD_NSA)
    kvh = lambda a: a.reshape(b, t, KV_HEADS, HD_NSA)
    gates = jax.nn.sigmoid(gl).reshape(b, t, N_HEADS_NSA, 3)
    return (q, rope_partial(q, pos), gates, kvh(kc), kvh(vc), rope_partial(kvh(ks), pos), kvh(vs),
            rope_partial(kvh(kw), pos), kvh(vw))


def nsa_prompt(cols, wb, pe_k, w1_k, w2_k, pe_v, w1_v, w2_v):
    b, t = cols.shape[:2]
    q, q_rot, gates, kc, vc, ks, vs, kw, vw = nsa_project(cols, jnp.arange(t))
    k_cmp = compress(kc, pe_k, w1_k, w2_k)
    v_cmp = compress(vc, pe_v, w1_v, w2_v)
    ns = t // SEL_BLOCK
    k_sel = ks.reshape(b, ns, SEL_BLOCK, KV_HEADS, HD_NSA)
    v_sel = vs.reshape(b, ns, SEL_BLOCK, KV_HEADS, HD_NSA)
    pad = ((0, 0), (WINDOW, 0), (0, 0), (0, 0))
    kw_pad, vw_pad = jnp.pad(kw, pad), jnp.pad(vw, pad)

    def q_block(i):
        s0 = i * Q_BLOCK
        take = lambda a, n: lax.dynamic_slice_in_dim(a, s0, n, axis=1)
        return nsa_attend(take(q, Q_BLOCK), take(q_rot, Q_BLOCK), take(gates, Q_BLOCK), s0 + jnp.arange(Q_BLOCK),
                          k_cmp, v_cmp, k_sel, v_sel, take(kw_pad, WINDOW + Q_BLOCK), take(vw_pad, WINDOW + Q_BLOCK),
                          s0 - WINDOW + jnp.arange(WINDOW + Q_BLOCK))

    o = lax.map(q_block, jnp.arange(t // Q_BLOCK))
    o = jnp.moveaxis(o, 0, 1).reshape(b, t, D_NSA)
    kv_rows = jnp.stack([kc, vc, ks, vs], axis=2)
    win = jnp.pad(jnp.stack([kw, vw], axis=2), ((0, 0), (wb, 0), (0, 0), (0, 0), (0, 0)))[:, -wb:]
    return o, kv_rows, win


def nsa_sample(cols, cache, page_table, win_state, pe_k, w1_k, w2_k, pe_v, w1_v, w2_v):
    b, sn = cols.shape[:2]
    past_len = page_table.shape[1] * cache.shape[1]
    pos = past_len + jnp.arange(sn)
    q, q_rot, gates, kc, vc, ks, vs, kw, vw = nsa_project(cols, pos)

    def full(c, new):
        past = cache[page_table, :, c].reshape(b, past_len, KV_HEADS, HD_NSA)
        return jnp.concatenate([past, new.astype(past.dtype)], axis=1)

    t = past_len + sn
    k_cmp = compress(full(0, kc), pe_k, w1_k, w2_k)
    v_cmp = compress(full(1, vc), pe_v, w1_v, w2_v)
    t_pad = -(-t // SEL_BLOCK) * SEL_BLOCK
    blocks = lambda a: jnp.pad(a, ((0, 0), (0, t_pad - t), (0, 0), (0, 0))).reshape(
        b, t_pad // SEL_BLOCK, SEL_BLOCK, KV_HEADS, HD_NSA)
    k_sel = blocks(full(2, ks))
    v_sel = blocks(full(3, vs))
    wb = win_state.shape[1]
    win_all = jnp.concatenate([win_state, jnp.stack([kw, vw], axis=2).astype(win_state.dtype)], axis=1)
    win_pos = past_len - wb + jnp.arange(wb + sn)
    o = nsa_attend(q, q_rot, gates, pos, k_cmp, v_cmp, k_sel, v_sel, win_all[:, :, 0], win_all[:, :, 1], win_pos)
    return o, jnp.stack([kc, vc, ks, vs], axis=2), win_all[:, -wb:]


def rwkv_mix(cols, shift_prev, wkv_prev, mu, w0, w2, a0, a2, g2, k_k, k_a, r_k, gn_w, gn_b):
    f32 = jnp.float32
    b, t = cols.shape[:2]
    prev = jnp.concatenate([shift_prev.astype(cols.dtype), cols[:, :-1]], axis=1)
    xs = cols + (prev - cols) * mu
    r, k, v, wd, ad, gd = jnp.split(xs, RWKV_SPLITS, axis=-1)
    w_log = -jax.nn.softplus(-(w0 + jnp.tanh(wd) @ w2).astype(f32)) - 0.5
    decay = jnp.exp(-jnp.exp(w_log))
    a = jax.nn.sigmoid((a0 + ad @ a2).astype(f32))
    g = jax.nn.sigmoid(gd) @ g2
    heads = lambda z: z.astype(f32).reshape(b, t, N_HEADS_RWKV, HS)
    kk = heads(k * k_k)
    kk = kk / jnp.maximum(jnp.sqrt(jnp.sum(kk * kk, axis=-1, keepdims=True)), 1e-12)
    k_mod = k.astype(f32) * (1.0 + (a - 1.0) * k_a)
    rh, kh, vh, ah, dh = heads(r), heads(k_mod), heads(v), heads(a), heads(decay)

    def step(state, inp):
        r_t, d_t, k_t, v_t, kk_t, a_t = inp
        sa = jnp.einsum('bhij,bhj->bhi', state, -kk_t)
        state = (state * d_t[:, :, None, :] + sa[..., None] * (kk_t * a_t)[:, :, None, :]
                 + v_t[..., None] * k_t[:, :, None, :])
        return state, jnp.einsum('bhij,bhj->bhi', state, r_t)

    seq_major = lambda z: jnp.swapaxes(z, 0, 1)
    wkv, y = lax.scan(step, wkv_prev.astype(f32), tuple(seq_major(z) for z in (rh, dh, kh, vh, kk, ah)))
    y = seq_major(y)
    mean = jnp.mean(y, axis=-1, keepdims=True)
    var = jnp.mean(jnp.square(y - mean), axis=-1, keepdims=True)
    y = ((y - mean) * lax.rsqrt(var + GN_EPS)).reshape(b, t, D_RWKV) * gn_w + gn_b
    bonus = (jnp.sum(rh * kh * r_k, axis=-1, keepdims=True) * vh).reshape(b, t, D_RWKV)
    return ((y + bonus) * g).astype(cols.dtype), wkv, cols[:, -1:]


def mix_sublayer(h, attn_fn, shift0, wkv0, g_pre, w_in, rw_w, w_out, g_post):
    p = rms_norm(h, g_pre) @ w_in
    o_attn, kv_rows, win = attn_fn(p[..., :NSA_COLS])
    o_rwkv, wkv, shift = rwkv_mix(p[..., NSA_COLS:], shift0, wkv0, *rw_w)
    mixed = jnp.concatenate([o_attn.astype(h.dtype), o_rwkv], axis=-1) @ w_out
    return h + rms_norm(mixed, g_post), kv_rows, win, wkv, shift


def setup_inputs(seed: int = 0) -> dict:
    key = jax.random.key(seed)
    keys = iter(jax.random.split(key, 64))
    nrm = lambda shape, scale: jax.random.normal(next(keys), shape, jnp.float32) * scale
    uni = lambda shape, lo, hi: jax.random.uniform(next(keys), shape, jnp.float32, lo, hi)
    gain = lambda: 1.0 + nrm((DEPTH, D_MODEL), 0.05)
    n_pages = PAST_LEN // PAGE_SIZE
    n_phys = (5 * DEC_BATCH * n_pages + 3) // 4
    wb = min(WINDOW, PAST_LEN)
    page_table = jax.random.permutation(next(keys), n_phys)[:DEC_BATCH * n_pages].reshape(
        DEC_BATCH, n_pages).astype(jnp.int32)
    return {
        'x_prompt': nrm((BATCH, SEQ, D_MODEL), 1.0),
        'x_sample': nrm((DEC_BATCH, DEC_SEQ, D_MODEL), 1.0),
        'cache_nsa': nrm((DEPTH, n_phys, PAGE_SIZE, 4, KV_HEADS, HD_NSA), 1.0),
        'page_table': page_table,
        'state_win': nrm((DEPTH, DEC_BATCH, wb, 2, KV_HEADS, HD_NSA), 1.0),
        'state_wkv': nrm((DEPTH, DEC_BATCH, N_HEADS_RWKV, HS, HS), 0.5),
        'state_shift': nrm((DEPTH, DEC_BATCH, 1, RWKV_COLS), 1.0),
        'norm_f1_pre': gain(),
        'norm_f1_post': gain(),
        'ffn1_gu': nrm((DEPTH, D_MODEL, 2 * D_FF), D_MODEL ** -0.5),
        'ffn1_dn': nrm((DEPTH, D_FF, D_MODEL), D_FF ** -0.5),
        'norm_mix_pre': gain(),
        'w_in': nrm((DEPTH, D_MODEL, IN_COLS), D_MODEL ** -0.5),
        'cmp_pe_k': nrm((DEPTH, CMP_BLOCK, HD_NSA), 0.1),
        'cmp_w1_k': nrm((DEPTH, CMP_BLOCK, HD_NSA, CMP_HIDDEN), (CMP_BLOCK * HD_NSA) ** -0.5),
        'cmp_w2_k': nrm((DEPTH, CMP_HIDDEN, HD_NSA), CMP_HIDDEN ** -0.5),
        'cmp_pe_v': nrm((DEPTH, CMP_BLOCK, HD_NSA), 0.1),
        'cmp_w1_v': nrm((DEPTH, CMP_BLOCK, HD_NSA, CMP_HIDDEN), (CMP_BLOCK * HD_NSA) ** -0.5),
        'cmp_w2_v': nrm((DEPTH, CMP_HIDDEN, HD_NSA), CMP_HIDDEN ** -0.5),
        'rw_mu': uni((DEPTH, RWKV_COLS), 0.0, 1.0),
        'rw_w0': uni((DEPTH, D_RWKV), -5.0, 1.0),
        'rw_w2': nrm((DEPTH, LORA_W, D_RWKV), 0.5 * LORA_W ** -0.5),
        'rw_a0': nrm((DEPTH, D_RWKV), 0.5),
        'rw_a2': nrm((DEPTH, LORA_A, D_RWKV), 0.5 * LORA_A ** -0.5),
        'rw_g2': nrm((DEPTH, LORA_G, D_RWKV), LORA_G ** -0.5),
        'rw_kk': 0.85 + nrm((DEPTH, D_RWKV), 0.05),
        'rw_ka': 1.0 + nrm((DEPTH, D_RWKV), 0.05),
        'rw_rk': nrm((DEPTH, N_HEADS_RWKV, HS), 0.1),
        'rw_gn_w': 1.0 + nrm((DEPTH, D_RWKV), 0.05),
        'rw_gn_b': nrm((DEPTH, D_RWKV), 0.02),
        'w_out': nrm((DEPTH, D_NSA + D_RWKV, D_MODEL), (D_NSA + D_RWKV) ** -0.5),
        'norm_mix_post': gain(),
        'norm_f2_pre': gain(),
        'ffn2_gu': nrm((DEPTH, D_MODEL, 2 * D_FF), D_MODEL ** -0.5),
        'ffn2_dn': nrm((DEPTH, D_FF, D_MODEL), D_FF ** -0.5),
        'norm_f2_post': gain(),
    }


def reference(x_prompt, x_sample, cache_nsa, page_table, state_win, state_wkv, state_shift,
              norm_f1_pre, norm_f1_post, ffn1_gu, ffn1_dn, norm_mix_pre, w_in,
              cmp_pe_k, cmp_w1_k, cmp_w2_k, cmp_pe_v, cmp_w1_v, cmp_w2_v,
              rw_mu, rw_w0, rw_w2, rw_a0, rw_a2, rw_g2, rw_kk, rw_ka, rw_rk, rw_gn_w, rw_gn_b,
              w_out, norm_mix_post, norm_f2_pre, ffn2_gu, ffn2_dn, norm_f2_post):
    bp = x_prompt.shape[0]
    wb = state_win.shape[2]
    yp, ys = x_prompt, x_sample
    kvp, kvs, winp, wins, wkvp, wkvs, shp, shs = [], [], [], [], [], [], [], []
    for l in range(DEPTH):
        cmp_w = (cmp_pe_k[l], cmp_w1_k[l], cmp_w2_k[l], cmp_pe_v[l], cmp_w1_v[l], cmp_w2_v[l])
        rw_w = (rw_mu[l], rw_w0[l], rw_w2[l], rw_a0[l], rw_a2[l], rw_g2[l], rw_kk[l], rw_ka[l], rw_rk[l],
                rw_gn_w[l], rw_gn_b[l])
        hp = ffn_half(yp, norm_f1_pre[l], ffn1_gu[l], ffn1_dn[l], norm_f1_post[l])
        hp, kv_p, win_p, wkv_p, sh_p = mix_sublayer(
            hp, lambda c: nsa_prompt(c, wb, *cmp_w),
            jnp.zeros((bp, 1, RWKV_COLS), hp.dtype), jnp.zeros((bp, N_HEADS_RWKV, HS, HS), jnp.float32),
            norm_mix_pre[l], w_in[l], rw_w, w_out[l], norm_mix_post[l])
        yp = ffn_half(hp, norm_f2_pre[l], ffn2_gu[l], ffn2_dn[l], norm_f2_post[l])
        hs = ffn_half(ys, norm_f1_pre[l], ffn1_gu[l], ffn1_dn[l], norm_f1_post[l])
        hs, kv_s, win_s, wkv_s, sh_s = mix_sublayer(
            hs, lambda c: nsa_sample(c, cache_nsa[l], page_table, state_win[l], *cmp_w),
            state_shift[l], state_wkv[l],
            norm_mix_pre[l], w_in[l], rw_w, w_out[l], norm_mix_post[l])
        ys = ffn_half(hs, norm_f2_pre[l], ffn2_gu[l], ffn2_dn[l], norm_f2_post[l])
        kvp.append(kv_p); kvs.append(kv_s); winp.append(win_p); wins.append(win_s)
        wkvp.append(wkv_p); wkvs.append(wkv_s); shp.append(sh_p); shs.append(sh_s)
    return (yp, ys, jnp.stack(kvp), jnp.stack(kvs), jnp.stack(winp), jnp.stack(wins),
            jnp.stack(wkvp), jnp.stack(wkvs), jnp.stack(shp), jnp.stack(shs))
```

```python
import functools

import numpy as np
import jax
import jax.numpy as jnp
from jax import lax
from jax.experimental import pallas as pl
from jax.experimental.pallas import tpu as pltpu

F32 = jnp.float32
BF16 = jnp.bfloat16

HD = 64
KVH = 4
GQA = 4
NH = 16
DH = NH * HD
DKV = KVH * HD
ROPE_HALF = 8
ROPE_THETA = 500000.0
CMP_BLOCK = 32
CMP_STRIDE = 16
CMP_HIDDEN = 128
SEL_BLOCK = 64
SEL_TOPN = 16
N_LOCAL_FORCED = 2
WINDOW = 512
Q_BLOCK = 128
LORA_W = 64
LORA_A = 64
LORA_G = 160
RMS_EPS = 1e-6
GN_EPS = 64e-5
NEG = -1e30
FORCE_BONUS = 1e4
NSA_COLS = DH + 6 * DKV + 3 * NH
RWKV_COLS = 3 * DH + LORA_W + LORA_A + LORA_G
NSA_PAD = 2688
RWKV_PAD = 3456
LORA_G_PAD = 256
RW_CHUNK = 64
SEL_KEY_CHUNK = 512
N_SEL_LANES = 64

VMEM_LIMIT_BYTES = 56 * 1024 * 1024


def _cparams(n_axes):
    return pltpu.CompilerParams(dimension_semantics=("arbitrary",) * n_axes,
                                vmem_limit_bytes=VMEM_LIMIT_BYTES)


def _dot(a, b):
    return jnp.dot(a, b, preferred_element_type=F32)


def _dot_nt(a, b):
    return lax.dot_general(a, b, (((1,), (1,)), ((), ())), preferred_element_type=F32)


def _dot_tn(a, b):
    return lax.dot_general(a, b, (((0,), (0,)), ((), ())), preferred_element_type=F32)


def _dot_split3(a, b_bf16):
    hi = a.astype(BF16)
    r1 = a - hi.astype(F32)
    mid = r1.astype(BF16)
    lo = (r1 - mid.astype(F32)).astype(BF16)
    return _dot(hi, b_bf16) + _dot(mid, b_bf16) + _dot(lo, b_bf16)


def _rms(x, g):
    ms = jnp.mean(x * x, axis=-1, keepdims=True)
    return x * lax.rsqrt(ms + RMS_EPS) * g


def _sigmoid(x):
    return jax.nn.sigmoid(x)


def _ffn_body(has_next, x_ref, gpre_ref, wg_ref, wu_ref, wd_ref, gpost_ref, *rest):
    if has_next:
        gnext_ref, o_ref, on_ref, xn_ref, acc_ref = rest
    else:
        o_ref, xn_ref, acc_ref = rest
    j = pl.program_id(1)

    @pl.when(j == 0)
    def _():
        xn_ref[...] = _rms(x_ref[...], gpre_ref[...]).astype(BF16)
        acc_ref[...] = jnp.zeros_like(acc_ref)

    xn = xn_ref[...]
    g = _dot(xn, wg_ref[...])
    u = _dot(xn, wu_ref[...])
    act = ((g * _sigmoid(g)) * u).astype(BF16)
    acc_ref[...] += _dot(act, wd_ref[...])

    @pl.when(j == pl.num_programs(1) - 1)
    def _():
        y = x_ref[...] + 0.5 * _rms(acc_ref[...], gpost_ref[...])
        o_ref[...] = y
        if has_next:
            on_ref[...] = _rms(y, gnext_ref[...]).astype(BF16)


def _ffn_half(x, g_pre, w_gu, w_dn, g_post, g_next, tm, tf):
    n, d = x.shape
    f = w_dn.shape[0]
    nj = f // tf
    has_next = g_next is not None
    row = lambda i, j: (i, 0)
    const = lambda i, j: (0, 0)
    in_specs = [
        pl.BlockSpec((tm, d), row),
        pl.BlockSpec((1, d), const),
        pl.BlockSpec((d, tf), lambda i, j: (0, j)),
        pl.BlockSpec((d, tf), lambda i, j: (0, j + nj)),
        pl.BlockSpec((tf, d), lambda i, j: (j, 0)),
        pl.BlockSpec((1, d), const),
    ]
    args = [x, g_pre, w_gu, w_gu, w_dn, g_post]
    out_shape = [jax.ShapeDtypeStruct((n, d), F32)]
    out_specs = [pl.BlockSpec((tm, d), row)]
    if has_next:
        in_specs.append(pl.BlockSpec((1, d), const))
        args.append(g_next)
        out_shape.append(jax.ShapeDtypeStruct((n, d), BF16))
        out_specs.append(pl.BlockSpec((tm, d), row))
    res = pl.pallas_call(
        functools.partial(_ffn_body, has_next),
        grid=(n // tm, nj),
        in_specs=in_specs,
        out_specs=out_specs,
        out_shape=out_shape,
        scratch_shapes=[pltpu.VMEM((tm, d), BF16), pltpu.VMEM((tm, d), F32)],
        compiler_params=_cparams(2),
        name="ffn_half",
    )(*args)
    return res if has_next else res[0]


def _rope(x, c, s_lo, s_hi):
    w = x.shape[1]
    reps = w // 128
    tile = lambda t: t if reps == 1 else jnp.concatenate([t] * reps, axis=1)
    up = pltpu.roll(x, w - ROPE_HALF, 1)
    dn = pltpu.roll(x, ROPE_HALF, 1)
    return x * tile(c) + up * tile(s_lo) + dn * tile(s_hi)


def _inproj_nsa_body(hn_ref, w_ref, c_ref, slo_ref, shi_ref,
                     qraw_ref, qrot_ref, kv_ref, kwvw_ref, ks_ref, vs_ref, kw_ref, vw_ref, gates_ref):
    p = _dot(hn_ref[...], w_ref[...])
    c, s_lo, s_hi = c_ref[...], slo_ref[...], shi_ref[...]
    q = p[:, 0:DH]
    qraw_ref[...] = q.astype(BF16)
    qrot_ref[...] = _rope(q, c, s_lo, s_hi).astype(BF16)
    o = DH
    ks = _rope(p[:, o + 2 * DKV:o + 3 * DKV], c, s_lo, s_hi)
    vs = p[:, o + 3 * DKV:o + 4 * DKV]
    kw = _rope(p[:, o + 4 * DKV:o + 5 * DKV], c, s_lo, s_hi)
    vw = p[:, o + 5 * DKV:o + 6 * DKV]
    kv_ref[:, 0:2 * DKV] = p[:, o:o + 2 * DKV]
    kv_ref[:, 2 * DKV:3 * DKV] = ks
    kv_ref[:, 3 * DKV:4 * DKV] = vs
    kwvw_ref[:, 0:DKV] = kw
    kwvw_ref[:, DKV:2 * DKV] = vw
    for k in range(KVH):
        sl = slice(HD * k, HD * (k + 1))
        ks_ref[k] = ks[:, sl].astype(BF16)
        vs_ref[k] = vs[:, sl].astype(BF16)
        kw_ref[k] = kw[:, sl].astype(BF16)
        vw_ref[k] = vw[:, sl].astype(BF16)
    gates_ref[...] = _sigmoid(p[:, o + 6 * DKV:o + 6 * DKV + 128])


def _inproj_nsa(hn, w_nsa, rope_c, rope_slo, rope_shi, tm):
    n, d = hn.shape
    row = lambda i: (i, 0)
    hm = lambda i: (0, i, 0)
    hm_shape = jax.ShapeDtypeStruct((KVH, n, HD), BF16)
    hm_spec = pl.BlockSpec((KVH, tm, HD), hm)
    return pl.pallas_call(
        _inproj_nsa_body,
        grid=(n // tm,),
        in_specs=[pl.BlockSpec((tm, d), row),
                  pl.BlockSpec((d, NSA_PAD), lambda i: (0, 0)),
                  pl.BlockSpec((tm, 128), row), pl.BlockSpec((tm, 128), row), pl.BlockSpec((tm, 128), row)],
        out_specs=[pl.BlockSpec((tm, DH), row), pl.BlockSpec((tm, DH), row),
                   pl.BlockSpec((tm, 4 * DKV), row), pl.BlockSpec((tm, 2 * DKV), row),
                   hm_spec, hm_spec, hm_spec, hm_spec,
                   pl.BlockSpec((tm, 128), row)],
        out_shape=[jax.ShapeDtypeStruct((n, DH), BF16), jax.ShapeDtypeStruct((n, DH), BF16),
                   jax.ShapeDtypeStruct((n, 4 * DKV), F32), jax.ShapeDtypeStruct((n, 2 * DKV), F32),
                   hm_shape, hm_shape, hm_shape, hm_shape,
                   jax.ShapeDtypeStruct((n, 128), F32)],
        compiler_params=_cparams(1),
        name="inproj_nsa",
    )(hn, w_nsa, rope_c, rope_slo, rope_shi)


def _matmul_body(x_ref, w_ref, o_ref):
    o_ref[...] = _dot(x_ref[...], w_ref[...])


def _inproj_rwkv(hn, w_rw, tm):
    n, d = hn.shape
    c = w_rw.shape[1]
    return pl.pallas_call(
        _matmul_body,
        grid=(n // tm,),
        in_specs=[pl.BlockSpec((tm, d), lambda i: (i, 0)), pl.BlockSpec((d, c), lambda i: (0, 0))],
        out_specs=pl.BlockSpec((tm, c), lambda i: (i, 0)),
        out_shape=jax.ShapeDtypeStruct((n, c), F32),
        compiler_params=_cparams(1),
        name="inproj_rwkv",
    )(hn, w_rw)


def _compress_accumulate(stage_ref, w1c_refs, n_seg):
    accs = [[jnp.zeros((n_seg, 2 * CMP_HIDDEN), F32) for _ in range(KVH)] for _ in range(2)]
    for s in range(CMP_STRIDE):
        for kv in range(2):
            w = w1c_refs[kv][s]
            for pair in range(KVH // 2):
                x = stage_ref[2 * kv + pair, pl.ds(s, n_seg, stride=CMP_STRIDE), :]
                for j in range(2):
                    k = 2 * pair + j
                    accs[kv][k] = accs[kv][k] + _dot(x[:, HD * j:HD * (j + 1)].astype(BF16), w)
    return accs


def _compress_finish(acc, pe0, w2, n_seg):
    a0 = acc[:, :CMP_HIDDEN]
    a1 = pltpu.roll(acc[:, CMP_HIDDEN:], n_seg - 1, 0)
    pre = pe0 + a0 + a1
    return _dot((pre * _sigmoid(pre)).astype(BF16), w2)


def _cmp_prompt_body(kv_ref, w1k_ref, w1v_ref, pek_ref, pev_ref, w1kf_ref, w1vf_ref, w2k_ref, w2v_ref,
                     kc_ref, vc_ref, stage_ref):
    n_seg = kv_ref.shape[0] // CMP_STRIDE
    for grp in range(4):
        stage_ref[grp] = kv_ref[:, 128 * grp:128 * (grp + 1)]
    accs = _compress_accumulate(stage_ref, (w1k_ref, w1v_ref), n_seg)
    for kv, (pe_ref, w1f_ref, w2_ref, out_ref) in enumerate(
            ((pek_ref, w1kf_ref, w2k_ref, kc_ref), (pev_ref, w1vf_ref, w2v_ref, vc_ref))):
        pe0 = _dot(pe_ref[...], w1f_ref[...])
        w2 = w2_ref[...]
        for k in range(KVH):
            out_ref[0, k] = _compress_finish(accs[kv][k], pe0, w2, n_seg).astype(BF16)


def _cmp_weights(pe, w1, w2):
    w1c = jnp.concatenate([w1[:CMP_STRIDE], w1[CMP_STRIDE:]], axis=-1).astype(BF16)
    return w1c, pe.reshape(1, CMP_BLOCK * HD), w1.reshape(CMP_BLOCK * HD, CMP_HIDDEN), w2.astype(BF16)


def _compress_prompt(kv_rows, cmp_k, cmp_v, n_batch, t):
    n_seg = t // CMP_STRIDE
    w1k, pek, w1kf, w2k = cmp_k
    w1v, pev, w1vf, w2v = cmp_v
    full = lambda a: pl.BlockSpec(a.shape, lambda b: (0,) * a.ndim)
    out_shape = jax.ShapeDtypeStruct((n_batch, KVH, n_seg, HD), BF16)
    out_spec = pl.BlockSpec((1, KVH, n_seg, HD), lambda b: (b, 0, 0, 0))
    return pl.pallas_call(
        _cmp_prompt_body,
        grid=(n_batch,),
        in_specs=[pl.BlockSpec((t, 2 * DKV), lambda b: (b, 0)),
                  full(w1k), full(w1v), full(pek), full(pev), full(w1kf), full(w1vf), full(w2k), full(w2v)],
        out_specs=[out_spec, out_spec],
        out_shape=[out_shape, out_shape],
        scratch_shapes=[pltpu.VMEM((4, t, 128), F32)],
        compiler_params=_cparams(1),
        name="nsa_compress_prompt",
    )(kv_rows, w1k, w1v, pek, pev, w1kf, w1vf, w2k, w2v)


def _select_blocks(imp, cur, n_blocks):
    jb = lax.broadcasted_iota(jnp.int32, (1, N_SEL_LANES), 1)
    valid = (jb <= cur) & (jb < n_blocks)
    rel = cur - jb
    forced = (jb == 0) | ((rel >= 0) & (rel < N_LOCAL_FORCED))
    score = jnp.where(valid, imp + jnp.where(forced, FORCE_BONUS, 0.0), NEG)
    rank = jnp.zeros(score.shape, F32)
    for jp in range(min(n_blocks, N_SEL_LANES)):
        col = score[:, jp:jp + 1]
        tie = jnp.where(jb > jp, 1.0, 0.0)
        rank = rank + jnp.where(col > score, 1.0, jnp.where(col == score, tie, 0.0))
    return jnp.where(valid, jnp.where(rank < SEL_TOPN, 1.0, 0.0), 0.0)


def _softmax_rows(s, ok):
    sm = jnp.where(ok, s, NEG)
    m = jnp.max(sm, axis=-1, keepdims=True)
    e = jnp.where(ok, jnp.exp(sm - m), 0.0)
    l = jnp.sum(e, axis=-1, keepdims=True)
    return e / jnp.maximum(l, 1e-30)


def _attn_prompt_body(qraw_ref, qrot_ref, gates_ref, kc_ref, vc_ref, ks_ref, vs_ref, kw_ref, vw_ref,
                      mcs_ref, e_ref, o_ref):
    t = ks_ref.shape[1]
    n_seg = kc_ref.shape[2]
    i = pl.program_id(1)
    s0 = i * Q_BLOCK
    scale = HD ** -0.5
    qpos = s0 + lax.broadcasted_iota(jnp.int32, (Q_BLOCK, 1), 0)
    qpos4 = jnp.concatenate([qpos] * GQA, axis=0)
    n_idx = lax.broadcasted_iota(jnp.int32, (1, n_seg), 1)
    ok_cmp = (n_idx * CMP_STRIDE + (CMP_BLOCK - 1) <= qpos4) & (n_idx < n_seg - 1)
    gates = gates_ref[...]
    mcs = mcs_ref[...]
    n_key_chunks = i // (SEL_KEY_CHUNK // Q_BLOCK) + 1
    win_start = pl.multiple_of(jnp.maximum(s0 - WINDOW, 0), Q_BLOCK)
    win_len = WINDOW + Q_BLOCK
    wpos = win_start + lax.broadcasted_iota(jnp.int32, (1, win_len), 1)
    dlt = qpos4 - wpos
    ok_win = (dlt >= 0) & (dlt <= WINDOW)

    for k in range(KVH):
        qr = jnp.concatenate([qraw_ref[:, HD * (GQA * k + g):HD * (GQA * k + g + 1)] for g in range(GQA)], axis=0)
        qs = jnp.concatenate([qrot_ref[:, HD * (GQA * k + g):HD * (GQA * k + g + 1)] for g in range(GQA)], axis=0)
        p_cmp = _softmax_rows(_dot_nt(qr, kc_ref[0, k]) * scale, ok_cmp)
        o_cmp = _dot(p_cmp.astype(BF16), vc_ref[0, k])
        p_sum = p_cmp[0:Q_BLOCK]
        for g in range(1, GQA):
            p_sum = p_sum + p_cmp[g * Q_BLOCK:(g + 1) * Q_BLOCK]
        imp = _dot_split3(p_sum, mcs)
        sel = _select_blocks(imp, qpos // SEL_BLOCK, t // SEL_BLOCK).astype(BF16)

        def chunk(c, carry, qs=qs, sel=sel, k=k):
            m, l, acc = carry
            k0 = pl.multiple_of(c * SEL_KEY_CHUNK, SEL_KEY_CHUNK)
            s = _dot_nt(qs, ks_ref[k, pl.ds(k0, SEL_KEY_CHUNK), :]) * scale
            selk = _dot(sel, e_ref[:, pl.ds(k0, SEL_KEY_CHUNK)])
            kpos = k0 + lax.broadcasted_iota(jnp.int32, (1, SEL_KEY_CHUNK), 1)
            okf = jnp.where(kpos <= qpos, selk, 0.0)
            ok = jnp.concatenate([okf] * GQA, axis=0) > 0.5
            sm = jnp.where(ok, s, NEG)
            m_new = jnp.maximum(m, jnp.max(sm, axis=-1, keepdims=True))
            p = jnp.where(ok, jnp.exp(sm - m_new), 0.0)
            alpha = jnp.exp(m - m_new)
            l = alpha * l + jnp.sum(p, axis=-1, keepdims=True)
            acc = alpha * acc + _dot(p.astype(BF16), vs_ref[k, pl.ds(k0, SEL_KEY_CHUNK), :])
            return m_new, l, acc

        init = (jnp.full((GQA * Q_BLOCK, 1), NEG, F32), jnp.zeros((GQA * Q_BLOCK, 1), F32),
                jnp.zeros((GQA * Q_BLOCK, HD), F32))
        _, l_sel, acc_sel = lax.fori_loop(0, n_key_chunks, chunk, init)
        o_sel = acc_sel / jnp.maximum(l_sel, 1e-30)

        s_w = _dot_nt(qs, kw_ref[k, pl.ds(win_start, win_len), :]) * scale
        p_w = _softmax_rows(s_w, ok_win)
        o_win = _dot(p_w.astype(BF16), vw_ref[k, pl.ds(win_start, win_len), :])

        for g in range(GQA):
            h = GQA * k + g
            rows = slice(g * Q_BLOCK, (g + 1) * Q_BLOCK)
            o_h = (gates[:, 3 * h:3 * h + 1] * o_cmp[rows] + gates[:, 3 * h + 1:3 * h + 2] * o_sel[rows]
                   + gates[:, 3 * h + 2:3 * h + 3] * o_win[rows])
            o_ref[:, HD * h:HD * (h + 1)] = o_h.astype(o_ref.dtype)


def _attn_prompt(q_raw, q_rot, gates, k_cmp, v_cmp, ks, vs, kw, vw, mcs, e_mat, n_batch, t):
    nqb = t // Q_BLOCK
    n_seg = t // CMP_STRIDE
    qrow = lambda b, i: (b * nqb + i, 0)
    cmp_spec = pl.BlockSpec((1, KVH, n_seg, HD), lambda b, i: (b, 0, 0, 0))
    kv_spec = pl.BlockSpec((KVH, t, HD), lambda b, i: (0, b, 0))
    return pl.pallas_call(
        _attn_prompt_body,
        grid=(n_batch, nqb),
        in_specs=[pl.BlockSpec((Q_BLOCK, DH), qrow), pl.BlockSpec((Q_BLOCK, DH), qrow),
                  pl.BlockSpec((Q_BLOCK, 128), qrow),
                  cmp_spec, cmp_spec, kv_spec, kv_spec, kv_spec, kv_spec,
                  pl.BlockSpec(mcs.shape, lambda b, i: (0, 0)),
                  pl.BlockSpec(e_mat.shape, lambda b, i: (0, 0))],
        out_specs=pl.BlockSpec((Q_BLOCK, DH), qrow),
        out_shape=jax.ShapeDtypeStruct((n_batch * t, DH), F32),
        compiler_params=_cparams(2),
        name="nsa_attn_prompt",
    )(q_raw, q_rot, gates, k_cmp, v_cmp, ks, vs, kw, vw, mcs, e_mat)


def _pick_kv_group(full):
    hk = lax.broadcasted_iota(jnp.int32, (NH, 1), 0) // GQA
    out = jnp.zeros((NH, HD), F32)
    for k in range(KVH):
        out = out + jnp.where(hk == k, full[:, HD * k:HD * (k + 1)], 0.0)
    return out


def _nsa_sample_body(n_pages, pt_ref, *refs):
    pages = refs[:n_pages]
    (win_ref, qraw_ref, qrot_ref, gates_ref, kvnew_ref, kwnew_ref,
     w1k_ref, w1v_ref, pek_ref, pev_ref, w1kf_ref, w1vf_ref, w2k_ref, w2v_ref, mcs_ref, e_ref,
     o_ref, winout_ref, stage_ref) = refs[n_pages:]
    page = pages[0].shape[1]
    past = n_pages * page
    n_seg = past // CMP_STRIDE
    scale = HD ** -0.5

    def bdiag(q):
        qt = jnp.concatenate([q] * KVH, axis=1)
        hk = lax.broadcasted_iota(jnp.int32, (NH, DKV), 0) // GQA
        lk = lax.broadcasted_iota(jnp.int32, (NH, DKV), 1) // HD
        return jnp.where(hk == lk, qt, 0.0)

    qr = bdiag(qraw_ref[0])
    qs = bdiag(qrot_ref[0])
    qr_b, qs_b = qr.astype(BF16), qs.astype(BF16)

    for kk, pg in enumerate(pages):
        for grp in range(4):
            stage_ref[grp, page * kk:page * (kk + 1), :] = pg[0, :, 128 * grp:128 * (grp + 1)]
    accs = _compress_accumulate(stage_ref, (w1k_ref, w1v_ref), n_seg)
    cmp_kv = []
    for kv, (pe_ref, w1f_ref, w2_ref) in enumerate(((pek_ref, w1kf_ref, w2k_ref), (pev_ref, w1vf_ref, w2v_ref))):
        pe0 = _dot(pe_ref[...], w1f_ref[...])
        w2 = w2_ref[...]
        cmp_kv.append(jnp.concatenate(
            [_compress_finish(accs[kv][k], pe0, w2, n_seg) for k in range(KVH)], axis=1).astype(BF16))
    k_cmp, v_cmp = cmp_kv

    n_idx = lax.broadcasted_iota(jnp.int32, (1, n_seg), 1)
    ok_cmp = (n_idx * CMP_STRIDE + (CMP_BLOCK - 1) <= past) & (n_idx < n_seg - 1)
    p_cmp = _softmax_rows(_dot_nt(qr_b, k_cmp) * scale, ok_cmp)
    o_cmp = _pick_kv_group(_dot(p_cmp.astype(BF16), v_cmp))
    p_sum = jnp.concatenate(
        [jnp.sum(p_cmp[GQA * k:GQA * (k + 1)], axis=0, keepdims=True) for k in range(KVH)], axis=0)
    imp = _dot_split3(p_sum, mcs_ref[...])
    n_blocks = past // SEL_BLOCK + 1
    cur = jnp.full((KVH, 1), past // SEL_BLOCK, jnp.int32)
    sel = _select_blocks(imp, cur, n_blocks)
    sel16 = jnp.concatenate([jnp.broadcast_to(sel[k:k + 1], (GQA, N_SEL_LANES)) for k in range(KVH)], axis=0)

    kv_new = kvnew_ref[0]
    ks_new, vs_new = kv_new[:, 2 * DKV:3 * DKV], kv_new[:, 3 * DKV:4 * DKV]
    k_sel = jnp.concatenate([pg[0, :, 2 * DKV:3 * DKV] for pg in pages], axis=0).astype(BF16)
    v_sel = jnp.concatenate([pg[0, :, 3 * DKV:4 * DKV] for pg in pages], axis=0).astype(BF16)
    s_c = _dot_nt(qs_b, k_sel) * scale
    s_n = jnp.sum(qs * ks_new, axis=-1, keepdims=True) * scale
    ok_c = _dot(sel16.astype(BF16), e_ref[...]) > 0.5
    new_blk = past // SEL_BLOCK
    ok_n = sel16[:, new_blk:new_blk + 1] > 0.5
    sm_c = jnp.where(ok_c, s_c, NEG)
    sm_n = jnp.where(ok_n, s_n, NEG)
    m = jnp.maximum(jnp.max(sm_c, axis=-1, keepdims=True), sm_n)
    e_c = jnp.where(ok_c, jnp.exp(sm_c - m), 0.0)
    e_n = jnp.where(ok_n, jnp.exp(sm_n - m), 0.0)
    inv = 1.0 / jnp.maximum(jnp.sum(e_c, axis=-1, keepdims=True) + e_n, 1e-30)
    o_sel = _pick_kv_group(_dot((e_c * inv).astype(BF16), v_sel)
                           + (e_n * inv).astype(BF16).astype(F32) * vs_new.astype(BF16).astype(F32))

    kw_new = kwnew_ref[0]
    kwn, vwn = kw_new[:, 0:DKV], kw_new[:, DKV:2 * DKV]
    s_w = _dot_nt(qs_b, win_ref[0, :, 0:DKV].astype(BF16)) * scale
    s_wn = jnp.sum(qs * kwn, axis=-1, keepdims=True) * scale
    mw = jnp.maximum(jnp.max(s_w, axis=-1, keepdims=True), s_wn)
    e_w = jnp.exp(s_w - mw)
    e_wn = jnp.exp(s_wn - mw)
    inv_w = 1.0 / (jnp.sum(e_w, axis=-1, keepdims=True) + e_wn)
    o_win = _pick_kv_group(_dot((e_w * inv_w).astype(BF16), win_ref[0, :, DKV:2 * DKV].astype(BF16))
                           + (e_wn * inv_w).astype(BF16).astype(F32) * vwn.astype(BF16).astype(F32))

    g = gates_ref[0]
    o_ref[0] = g[:, 0:1] * o_cmp + g[:, 1:2] * o_sel + g[:, 2:3] * o_win
    wb = win_ref.shape[1]
    winout_ref[0, 0:wb - 1, :] = win_ref[0, 1:wb, :]
    winout_ref[0, wb - 1:wb, :] = kw_new


def _nsa_sample(cache3, page_table, state_win, q_raw, q_rot, gates, kv_new, kw_new, cmp_k, cmp_v, mcs, e_mat):
    nb, n_pages = page_table.shape
    page = cache3.shape[1]
    wb = state_win.shape[1]
    w1k, pek, w1kf, w2k = cmp_k
    w1v, pev, w1vf, w2v = cmp_v
    consts = [w1k, w1v, pek, pev, w1kf, w1vf, w2k, w2v, mcs, e_mat]
    full = lambda a: pl.BlockSpec(a.shape, lambda b, pt: (0,) * a.ndim)
    page_specs = [pl.BlockSpec((1, page, cache3.shape[2]), functools.partial(
        lambda b, pt, kk: (pt[b * n_pages + kk], 0, 0), kk=kk)) for kk in range(n_pages)]
    per_b = lambda shape: pl.BlockSpec((1,) + shape, lambda b, pt: (b, 0, 0))
    grid_spec = pltpu.PrefetchScalarGridSpec(
        num_scalar_prefetch=1,
        grid=(nb,),
        in_specs=page_specs + [per_b((wb, 2 * DKV)), per_b((NH, HD)), per_b((NH, HD)), per_b((NH, 3)),
                               per_b((1, 4 * DKV)), per_b((1, 2 * DKV))] + [full(a) for a in consts],
        out_specs=[per_b((NH, HD)), per_b((wb, 2 * DKV))],
        scratch_shapes=[pltpu.VMEM((4, n_pages * page, 128), F32)],
    )
    return pl.pallas_call(
        functools.partial(_nsa_sample_body, n_pages),
        grid_spec=grid_spec,
        out_shape=[jax.ShapeDtypeStruct((nb, NH, HD), F32), jax.ShapeDtypeStruct((nb, wb, 2 * DKV), F32)],
        compiler_params=_cparams(1),
        name="nsa_sample",
    )(page_table.reshape(-1), *([cache3] * n_pages), state_win, q_raw, q_rot, gates, kv_new, kw_new, *consts)


def _rwkv_prep(cols, prev, mu, w0, w2, a0, a2, g2, kk_p, ka):
    xs = cols + (prev - cols) * mu
    r = xs[:, 0:DH]
    k = xs[:, DH:2 * DH]
    v = xs[:, 2 * DH:3 * DH]
    o = 3 * DH
    wd = xs[:, o:o + LORA_W]
    ad = xs[:, o + LORA_W:o + LORA_W + LORA_A]
    gd = xs[:, o + LORA_W + LORA_A:o + LORA_W + LORA_A + LORA_G_PAD]
    w = w0 + _dot(jnp.tanh(wd).astype(BF16), w2)
    w_log = -jax.nn.softplus(-w) - 0.5
    lw = -jnp.exp(w_log)
    a = _sigmoid(a0 + _dot(ad.astype(BF16), a2))
    g = _dot(_sigmoid(gd).astype(BF16), g2)
    kkv = k * kk_p
    k_mod = k * (1.0 + (a - 1.0) * ka)
    return r, k_mod, v, kkv, a, lw, g


def _head_norm(kkv_h):
    return kkv_h / jnp.maximum(jnp.sqrt(jnp.sum(kkv_h * kkv_h, axis=-1, keepdims=True)), 1e-12)


def _rwkv_head_out(y, r_h, k_h, v_h, g_h, rk_h, gnw_h, gnb_h):
    mean = jnp.mean(y, axis=-1, keepdims=True)
    var = jnp.mean(jnp.square(y - mean), axis=-1, keepdims=True)
    yn = (y - mean) * lax.rsqrt(var + GN_EPS) * gnw_h + gnb_h
    bonus = jnp.sum(r_h * k_h * rk_h, axis=-1, keepdims=True) * v_h
    return (yn + bonus) * g_h


def _tri_inverse(a, blk16, blk32):
    c = a.shape[0]
    eye = jnp.where(lax.broadcasted_iota(jnp.int32, (c, c), 0) == lax.broadcasted_iota(jnp.int32, (c, c), 1),
                    1.0, 0.0)
    b = lambda x: x.astype(BF16)
    a16 = jnp.where(blk16, a, 0.0)
    p = _dot(b(a16), b(a16))
    t = _dot(b(eye - a16), b(eye + p))
    p = _dot(b(p), b(p))
    t = _dot(b(t), b(eye + p))
    p = _dot(b(p), b(p))
    t = _dot(b(t), b(eye + p))
    a32 = jnp.where(blk32 & jnp.logical_not(blk16), a, 0.0)
    t = t - _dot(b(_dot(b(t), b(a32))), b(t))
    a64 = jnp.where(blk32, 0.0, a)
    t = t - _dot(b(_dot(b(t), b(a64))), b(t))
    return t


def _rwkv_prompt_body(cols_ref, mu_ref, w0_ref, w2_ref, a0_ref, a2_ref, g2_ref, kk_ref, ka_ref, rk_ref,
                      gnw_ref, gnb_ref, o_ref, wkv_ref, s_ref, last_ref):
    c = pl.program_id(1)
    ch = RW_CHUNK

    @pl.when(c == 0)
    def _():
        s_ref[...] = jnp.zeros_like(s_ref)
        last_ref[...] = jnp.zeros_like(last_ref)

    cols = cols_ref[...]
    row = lax.broadcasted_iota(jnp.int32, (ch, 1), 0)
    prev = jnp.where(row == 0, last_ref[...], pltpu.roll(cols, 1, 0))
    last_ref[...] = cols[ch - 1:ch, :]
    r, k_mod, v, kkv, a, lw, g = _rwkv_prep(cols, prev, mu_ref[...], w0_ref[...], w2_ref[...], a0_ref[...],
                                            a2_ref[...], g2_ref[...], kk_ref[...], ka_ref[...])
    ti = lax.broadcasted_iota(jnp.int32, (ch, ch), 0)
    si = lax.broadcasted_iota(jnp.int32, (ch, ch), 1)
    ltri = jnp.where(ti >= si, 1.0, 0.0).astype(BF16)
    cl = _cumsum_rows(lw, ltri)
    e_in = jnp.exp(cl)
    e_ex = jnp.exp(cl - lw)
    e_ng = jnp.exp(-cl)
    cl_end = cl[ch - 1:ch, :]
    e_end = jnp.exp(cl_end - cl)
    g_end = jnp.exp(cl_end)
    strict = ti > si
    causal2 = (lax.broadcasted_iota(jnp.int32, (ch, 2 * ch), 0)
               >= lax.broadcasted_iota(jnp.int32, (ch, 2 * ch), 1) % ch)
    blk16 = (ti // 16) == (si // 16)
    blk32 = (ti // 32) == (si // 32)
    rk, gnw, gnb = rk_ref[...], gnw_ref[...], gnb_ref[...]
    b = lambda x: x.astype(BF16)

    for h in range(NH):
        sl = slice(HD * h, HD * (h + 1))
        kk_h = _head_norm(kkv[:, sl])
        beta = kk_h * a[:, sl]
        kq = kk_h * e_ex[:, sl]
        rq = r[:, sl] * e_in[:, sl]
        bd = beta * e_ng[:, sl]
        kd = k_mod[:, sl] * e_ng[:, sl]
        v_h = v[:, sl]
        s0 = s_ref[h]
        lhs = b(jnp.concatenate([kq, rq], axis=0))
        rhs = b(jnp.concatenate([bd, kd], axis=0))
        quad = _dot_nt(lhs, rhs)
        s0t = _dot_nt(lhs, b(s0))
        a_b = jnp.where(strict, quad[0:ch, 0:ch], 0.0)
        a_k = jnp.where(strict, quad[0:ch, ch:2 * ch], 0.0)
        b_bk = jnp.where(causal2, quad[ch:2 * ch, :], 0.0)
        rhs_u = s0t[0:ch] + _dot(b(a_k), b(v_h))
        u = -_dot(b(_tri_inverse(a_b, blk16, blk32)), b(rhs_u))
        uv = b(jnp.concatenate([u, v_h], axis=0))
        y = s0t[ch:2 * ch] + _dot(b(b_bk), uv)
        dec = b(jnp.concatenate([beta * e_end[:, sl], k_mod[:, sl] * e_end[:, sl]], axis=0))
        s_ref[h] = s0 * g_end[:, sl] + _dot_tn(uv, dec)
        o_h = _rwkv_head_out(y, r[:, sl], k_mod[:, sl], v_h, g[:, sl], rk[:, sl], gnw[:, sl], gnb[:, sl])
        o_ref[:, sl] = o_h.astype(o_ref.dtype)

    @pl.when(c == pl.num_programs(1) - 1)
    def _():
        wkv_ref[0] = s_ref[...]


def _cumsum_rows(x, ltri):
    hi = x.astype(BF16)
    r1 = x - hi.astype(F32)
    mid = r1.astype(BF16)
    lo = (r1 - mid.astype(F32)).astype(BF16)
    return _dot(ltri, hi) + _dot(ltri, mid) + _dot(ltri, lo)


def _rwkv_prompt(cols, rw, n_batch, t):
    nc = t // RW_CHUNK
    full = lambda a: pl.BlockSpec(a.shape, lambda b, c: (0,) * a.ndim)
    row = lambda b, c: (b * nc + c, 0)
    return pl.pallas_call(
        _rwkv_prompt_body,
        grid=(n_batch, nc),
        in_specs=[pl.BlockSpec((RW_CHUNK, RWKV_PAD), row)] + [full(a) for a in rw],
        out_specs=[pl.BlockSpec((RW_CHUNK, DH), row),
                   pl.BlockSpec((1, NH, HD, HD), lambda b, c: (b, 0, 0, 0))],
        out_shape=[jax.ShapeDtypeStruct((n_batch * t, DH), F32),
                   jax.ShapeDtypeStruct((n_batch, NH, HD, HD), F32)],
        scratch_shapes=[pltpu.VMEM((NH, HD, HD), F32), pltpu.VMEM((1, RWKV_PAD), F32)],
        compiler_params=_cparams(2),
        name="rwkv_prompt",
    )(cols, *rw)


def _rwkv_sample_prep_body(cols_ref, prev_ref, mu_ref, w0_ref, w2_ref, a0_ref, a2_ref, g2_ref, kk_ref, ka_ref,
                           r_ref, k_ref, v_ref, kkn_ref, a_ref, d_ref, g_ref):
    r, k_mod, v, kkv, a, lw, g = _rwkv_prep(cols_ref[...], prev_ref[...], mu_ref[...], w0_ref[...], w2_ref[...],
                                            a0_ref[...], a2_ref[...], g2_ref[...], kk_ref[...], ka_ref[...])
    r_ref[...] = r
    k_ref[...] = k_mod
    v_ref[...] = v
    a_ref[...] = a
    d_ref[...] = jnp.exp(lw)
    g_ref[...] = g
    for h in range(NH):
        sl = slice(HD * h, HD * (h + 1))
        kkn_ref[:, sl] = _head_norm(kkv[:, sl])


def _rwkv_sample_prep(cols, prev, rw_prep):
    n = cols.shape[0]
    args = [cols, prev] + list(rw_prep)
    full = lambda a: pl.BlockSpec(a.shape, lambda i: (0,) * a.ndim)
    out = jax.ShapeDtypeStruct((n, DH), F32)
    return pl.pallas_call(
        _rwkv_sample_prep_body,
        grid=(1,),
        in_specs=[full(a) for a in args],
        out_specs=[pl.BlockSpec((n, DH), lambda i: (0, 0))] * 7,
        out_shape=[out] * 7,
        compiler_params=_cparams(1),
        name="rwkv_sample_prep",
    )(*args)


def _rwkv_sample_step_body(s_ref, r_ref, k_ref, v_ref, kk_ref, a_ref, d_ref, g_ref, rk_ref, gnw_ref, gnb_ref,
                           o_ref, sout_ref):
    s = s_ref[0]
    r, k, v, kk, a, d, g = (x[0] for x in (r_ref, k_ref, v_ref, kk_ref, a_ref, d_ref, g_ref))
    eye = jnp.where(lax.broadcasted_iota(jnp.int32, (HD, HD), 0) == lax.broadcasted_iota(jnp.int32, (HD, HD), 1),
                    1.0, 0.0)
    row = lambda x: x[:, None, :]
    col = lambda x: jnp.sum(eye[None] * x[:, None, :], axis=-1, keepdims=True)
    sa = -jnp.sum(s * row(kk), axis=-1, keepdims=True)
    s_new = s * row(d) + sa * row(kk * a) + col(v) * row(k)
    sout_ref[0] = s_new
    y_col = jnp.sum(s_new * row(r), axis=-1, keepdims=True)
    y = jnp.sum(y_col * eye[None], axis=1)
    o_ref[0] = _rwkv_head_out(y, r, k, v, g, rk_ref[...], gnw_ref[...], gnb_ref[...])


def _rwkv_sample_step(state, vecs, rk, gnw, gnb):
    nb = state.shape[0]
    per_b3 = pl.BlockSpec((1, NH, HD), lambda b: (b, 0, 0))
    per_b4 = pl.BlockSpec((1, NH, HD, HD), lambda b: (b, 0, 0, 0))
    full = pl.BlockSpec((NH, HD), lambda b: (0, 0))
    return pl.pallas_call(
        _rwkv_sample_step_body,
        grid=(nb,),
        in_specs=[per_b4] + [per_b3] * 7 + [full] * 3,
        out_specs=[per_b3, per_b4],
        out_shape=[jax.ShapeDtypeStruct((nb, NH, HD), F32), jax.ShapeDtypeStruct((nb, NH, HD, HD), F32)],
        compiler_params=_cparams(1),
        name="rwkv_sample_step",
    )(state, *vecs, rk, gnw, gnb)


def _outproj_body(oa_ref, orw_ref, wa_ref, wr_ref, h_ref, g_ref, o_ref):
    mixed = _dot(oa_ref[...].astype(BF16), wa_ref[...]) + _dot(orw_ref[...].astype(BF16), wr_ref[...])
    o_ref[...] = h_ref[...] + _rms(mixed, g_ref[...])


def _outproj(o_attn, o_rwkv, w_a, w_r, h, g_post, tm):
    n, d = h.shape
    row = lambda i: (i, 0)
    const = lambda i: (0, 0)
    return pl.pallas_call(
        _outproj_body,
        grid=(n // tm,),
        in_specs=[pl.BlockSpec((tm, DH), row), pl.BlockSpec((tm, DH), row),
                  pl.BlockSpec((DH, d), const), pl.BlockSpec((DH, d), const),
                  pl.BlockSpec((tm, d), row), pl.BlockSpec((1, d), const)],
        out_specs=pl.BlockSpec((tm, d), row),
        out_shape=jax.ShapeDtypeStruct((n, d), F32),
        compiler_params=_cparams(1),
        name="outproj",
    )(o_attn, o_rwkv, w_a, w_r, h, g_post)


def _rope_tables(pos):
    inv = ROPE_THETA ** (-jnp.arange(ROPE_HALF, dtype=F32) / ROPE_HALF)
    ang = pos.astype(F32)[:, None] * inv[None, :]
    cos, sin = jnp.cos(ang), jnp.sin(ang)
    n = pos.shape[0]
    rest = HD - 2 * ROPE_HALF
    c = jnp.concatenate([cos, cos, jnp.ones((n, rest), F32)], axis=1)
    s_lo = jnp.concatenate([-sin, jnp.zeros((n, HD - ROPE_HALF), F32)], axis=1)
    s_hi = jnp.concatenate([jnp.zeros((n, ROPE_HALF), F32), sin, jnp.zeros((n, rest), F32)], axis=1)
    two = lambda x: jnp.concatenate([x, x], axis=1)
    return two(c), two(s_lo), two(s_hi)


def _cmp_to_sel(n_cmp, n_sel, rows):
    i = np.arange(n_cmp)[:, None] * CMP_STRIDE
    j = np.arange(n_sel)[None, :] * SEL_BLOCK
    ov = np.minimum(i + CMP_BLOCK, j + SEL_BLOCK) - np.maximum(i, j)
    m = np.zeros((rows, N_SEL_LANES), np.float32)
    m[:n_cmp, :n_sel] = np.maximum(ov, 0) // CMP_STRIDE
    return jnp.asarray(m, BF16)


def _block_expand(n_keys):
    j = np.arange(N_SEL_LANES)[:, None]
    t = np.arange(n_keys)[None, :]
    return jnp.asarray((t // SEL_BLOCK == j).astype(np.float32), BF16)


def _pad_cols(x, width):
    return jnp.pad(x, ((0, 0), (0, width - x.shape[1])))


def _row_tile(n, cap):
    tm = min(n, cap)
    while n % tm:
        tm //= 2
    return tm


def kernel(x_prompt, x_sample, cache_nsa, page_table, state_win, state_wkv, state_shift, norm_f1_pre, norm_f1_post, ffn1_gu, ffn1_dn, norm_mix_pre, w_in, cmp_pe_k, cmp_w1_k, cmp_w2_k, cmp_pe_v, cmp_w1_v, cmp_w2_v, rw_mu, rw_w0, rw_w2, rw_a0, rw_a2, rw_g2, rw_kk, rw_ka, rw_rk, rw_gn_w, rw_gn_b, w_out, norm_mix_post, norm_f2_pre, ffn2_gu, ffn2_dn, norm_f2_post):
    depth = w_in.shape[0]
    assert depth == 1, "single-layer step"
    bp, t, d = x_prompt.shape
    bs, ts, _ = x_sample.shape
    assert ts == 1 and t % Q_BLOCK == 0 and t % RW_CHUNK == 0 and t >= WINDOW
    n_pages = page_table.shape[1]
    page = cache_nsa.shape[2]
    past = n_pages * page
    wb = state_win.shape[2]
    assert wb == min(WINDOW, past) and past % SEL_BLOCK == 0 and page % CMP_STRIDE == 0
    l = 0

    w1g, w1d = ffn1_gu[l].astype(BF16), ffn1_dn[l].astype(BF16)
    w2g, w2d = ffn2_gu[l].astype(BF16), ffn2_dn[l].astype(BF16)
    wi = w_in[l]
    w_nsa = _pad_cols(wi[:, :NSA_COLS], NSA_PAD).astype(BF16)
    w_rw = _pad_cols(wi[:, NSA_COLS:], RWKV_PAD).astype(BF16)
    w_oa, w_or = w_out[l, :DH].astype(BF16), w_out[l, DH:].astype(BF16)
    vec = lambda a: a.reshape(1, -1)
    cmp_k = _cmp_weights(cmp_pe_k[l], cmp_w1_k[l], cmp_w2_k[l])
    cmp_v = _cmp_weights(cmp_pe_v[l], cmp_w1_v[l], cmp_w2_v[l])
    g2_pad = jnp.pad(rw_g2[l], ((0, LORA_G_PAD - LORA_G), (0, 0))).astype(BF16)
    rw_prep = [_pad_cols(vec(rw_mu[l]), RWKV_PAD), vec(rw_w0[l]), rw_w2[l].astype(BF16), vec(rw_a0[l]),
               rw_a2[l].astype(BF16), g2_pad, vec(rw_kk[l]), vec(rw_ka[l])]
    rk_row, gnw_row, gnb_row = vec(rw_rk[l]), vec(rw_gn_w[l]), vec(rw_gn_b[l])

    outs = {}
    for name, x2, n_batch in (("p", x_prompt.reshape(bp * t, d), bp), ("s", x_sample.reshape(bs, d), bs)):
        n = x2.shape[0]
        tm = _row_tile(n, 512)
        is_prompt = name == "p"
        h1, hn = _ffn_half(x2, vec(norm_f1_pre[l]), w1g, w1d, vec(norm_f1_post[l]), vec(norm_mix_pre[l]),
                           tm, 512)
        pos = jnp.tile(jnp.arange(t), bp) if is_prompt else jnp.full((n,), past, jnp.int32)
        rc, rlo, rhi = _rope_tables(pos)
        q_raw, q_rot, kv_rows, kwvw, ks, vs, kw, vw, gates = _inproj_nsa(hn, w_nsa, rc, rlo, rhi, tm)
        cols = _inproj_rwkv(hn, w_rw, tm)
        if is_prompt:
            k_cmp, v_cmp = _compress_prompt(kv_rows, cmp_k, cmp_v, n_batch, t)
            n_seg = t // CMP_STRIDE
            mcs = _cmp_to_sel(n_seg - 1, t // SEL_BLOCK, n_seg)
            o_attn = _attn_prompt(q_raw, q_rot, gates, k_cmp, v_cmp, ks, vs, kw, vw, mcs, _block_expand(t),
                                  n_batch, t)
            o_rwkv, wkv = _rwkv_prompt(cols, rw_prep + [rk_row, gnw_row, gnb_row], n_batch, t)
            win = kwvw.reshape(n_batch, t, 2 * DKV)[:, t - wb:]
            shift = cols.reshape(n_batch, t, RWKV_PAD)[:, t - 1:, :RWKV_COLS]
        else:
            cache3 = cache_nsa[l].reshape(cache_nsa.shape[1], page, 4 * DKV)
            n_seg = past // CMP_STRIDE
            mcs = _cmp_to_sel(n_seg - 1, past // SEL_BLOCK + 1, n_seg)
            o_attn, win = _nsa_sample(
                cache3, page_table, state_win[l].reshape(n, wb, 2 * DKV),
                q_raw.astype(F32).reshape(n, NH, HD), q_rot.astype(F32).reshape(n, NH, HD),
                gates[:, :3 * NH].reshape(n, NH, 3), kv_rows.reshape(n, 1, 4 * DKV), kwvw.reshape(n, 1, 2 * DKV),
                cmp_k, cmp_v, mcs, _block_expand(past))
            o_attn = o_attn.reshape(n, DH)
            prev = _pad_cols(state_shift[l].reshape(n, RWKV_COLS), RWKV_PAD)
            vecs = _rwkv_sample_prep(cols, prev, rw_prep)
            o_rwkv, wkv = _rwkv_sample_step(state_wkv[l], [v_.reshape(n, NH, HD) for v_ in vecs],
                                            rw_rk[l], rw_gn_w[l].reshape(NH, HD), rw_gn_b[l].reshape(NH, HD))
            o_rwkv = o_rwkv.reshape(n, DH)
            shift = cols[:, None, :RWKV_COLS]
        h2 = _outproj(o_attn, o_rwkv, w_oa, w_or, h1, vec(norm_mix_post[l]), tm)
        y = _ffn_half(h2, vec(norm_f2_pre[l]), w2g, w2d, vec(norm_f2_post[l]), None, tm, 512)
        outs[name] = (y, kv_rows, win, wkv, shift)

    yp, kvp, winp, wkvp, shp = outs["p"]
    ys, kvs, wins, wkvs, shs = outs["s"]
    return (yp.reshape(bp, t, d), ys.reshape(bs, 1, d),
            kvp.reshape(1, bp, t, 4, KVH, HD), kvs.reshape(1, bs, 1, 4, KVH, HD),
            winp.reshape(1, bp, wb, 2, KVH, HD), wins.reshape(1, bs, wb, 2, KVH, HD),
            wkvp[None], wkvs[None], shp[None], shs[None])
```

```python
import functools

import numpy as np
import jax
import jax.numpy as jnp
from jax import lax
from jax.experimental import pallas as pl
from jax.experimental.pallas import tpu as pltpu

F32 = jnp.float32
BF16 = jnp.bfloat16

HD = 64
KVH = 4
GQA = 4
NH = 16
DH = NH * HD
DKV = KVH * HD
ROPE_HALF = 8
ROPE_THETA = 500000.0
CMP_BLOCK = 32
CMP_STRIDE = 16
CMP_HIDDEN = 128
SEL_BLOCK = 64
SEL_TOPN = 16
N_LOCAL_FORCED = 2
WINDOW = 512
Q_BLOCK = 128
LORA_W = 64
LORA_A = 64
LORA_G = 160
RMS_EPS = 1e-6
GN_EPS = 64e-5
NEG = -1e30
FORCE_BONUS = 1e4
MASK_BIAS = -1e30
M_FLOOR = -1e20
NSA_COLS = DH + 6 * DKV + 3 * NH
RWKV_COLS = 3 * DH + LORA_W + LORA_A + LORA_G
NSA_PAD = 2688
RWKV_PAD = 3456
LORA_G_PAD = 256
RW_CHUNK = 64
SEL_KEY_CHUNK = 512
N_SEL_LANES = 64

VMEM_LIMIT_BYTES = 56 * 1024 * 1024


def _cparams(n_axes):
    return pltpu.CompilerParams(dimension_semantics=("arbitrary",) * n_axes,
                                vmem_limit_bytes=VMEM_LIMIT_BYTES)


def _dot(a, b):
    return jnp.dot(a, b, preferred_element_type=F32)


def _dot_nt(a, b):
    return lax.dot_general(a, b, (((1,), (1,)), ((), ())), preferred_element_type=F32)


def _dot_tn(a, b):
    return lax.dot_general(a, b, (((0,), (0,)), ((), ())), preferred_element_type=F32)


def _dot_split3(a, b_bf16):
    hi = a.astype(BF16)
    r1 = a - hi.astype(F32)
    mid = r1.astype(BF16)
    lo = (r1 - mid.astype(F32)).astype(BF16)
    return _dot(hi, b_bf16) + _dot(mid, b_bf16) + _dot(lo, b_bf16)


def _rms(x, g):
    ms = jnp.mean(x * x, axis=-1, keepdims=True)
    return x * lax.rsqrt(ms + RMS_EPS) * g


def _sigmoid(x):
    return jax.nn.sigmoid(x)


def _ffn_body(has_next, x_ref, gpre_ref, wg_ref, wu_ref, wd_ref, gpost_ref, *rest):
    if has_next:
        gnext_ref, o_ref, on_ref, xn_ref, acc_ref = rest
    else:
        o_ref, xn_ref, acc_ref = rest
    j = pl.program_id(1)

    @pl.when(j == 0)
    def _():
        xn_ref[...] = _rms(x_ref[...], gpre_ref[...]).astype(BF16)
        acc_ref[...] = jnp.zeros_like(acc_ref)

    xn = xn_ref[...]
    g = _dot(xn, wg_ref[...])
    u = _dot(xn, wu_ref[...])
    act = ((g * _sigmoid(g)) * u).astype(BF16)
    acc_ref[...] += _dot(act, wd_ref[...])

    @pl.when(j == pl.num_programs(1) - 1)
    def _():
        y = x_ref[...] + 0.5 * _rms(acc_ref[...], gpost_ref[...])
        o_ref[...] = y
        if has_next:
            on_ref[...] = _rms(y, gnext_ref[...]).astype(BF16)


def _ffn_half(x, g_pre, w_gu, w_dn, g_post, g_next, tm, tf):
    n, d = x.shape
    f = w_dn.shape[0]
    nj = f // tf
    has_next = g_next is not None
    row = lambda i, j: (i, 0)
    const = lambda i, j: (0, 0)
    in_specs = [
        pl.BlockSpec((tm, d), row),
        pl.BlockSpec((1, d), const),
        pl.BlockSpec((d, tf), lambda i, j: (0, j)),
        pl.BlockSpec((d, tf), lambda i, j: (0, j + nj)),
        pl.BlockSpec((tf, d), lambda i, j: (j, 0)),
        pl.BlockSpec((1, d), const),
    ]
    args = [x, g_pre, w_gu, w_gu, w_dn, g_post]
    out_shape = [jax.ShapeDtypeStruct((n, d), F32)]
    out_specs = [pl.BlockSpec((tm, d), row)]
    if has_next:
        in_specs.append(pl.BlockSpec((1, d), const))
        args.append(g_next)
        out_shape.append(jax.ShapeDtypeStruct((n, d), BF16))
        out_specs.append(pl.BlockSpec((tm, d), row))
    res = pl.pallas_call(
        functools.partial(_ffn_body, has_next),
        grid=(n // tm, nj),
        in_specs=in_specs,
        out_specs=out_specs,
        out_shape=out_shape,
        scratch_shapes=[pltpu.VMEM((tm, d), BF16), pltpu.VMEM((tm, d), F32)],
        compiler_params=_cparams(2),
        name="ffn_half",
    )(*args)
    return res if has_next else res[0]


def _rope(x, c, s_lo, s_hi):
    w = x.shape[1]
    reps = w // 128
    tile = lambda t: t if reps == 1 else jnp.concatenate([t] * reps, axis=1)
    up = pltpu.roll(x, w - ROPE_HALF, 1)
    dn = pltpu.roll(x, ROPE_HALF, 1)
    return x * tile(c) + up * tile(s_lo) + dn * tile(s_hi)


def _inproj_nsa_body(hn_ref, w_ref, c_ref, slo_ref, shi_ref,
                     qraw_ref, qrot_ref, kv_ref, kwvw_ref, ks_ref, vs_ref, kw_ref, vw_ref, gates_ref):
    p = _dot(hn_ref[...], w_ref[...])
    c, s_lo, s_hi = c_ref[...], slo_ref[...], shi_ref[...]
    q = p[:, 0:DH] * HD ** -0.5
    qraw_ref[...] = q.astype(BF16)
    qrot_ref[...] = _rope(q, c, s_lo, s_hi).astype(BF16)
    o = DH
    ks = _rope(p[:, o + 2 * DKV:o + 3 * DKV], c, s_lo, s_hi)
    vs = p[:, o + 3 * DKV:o + 4 * DKV]
    kw = _rope(p[:, o + 4 * DKV:o + 5 * DKV], c, s_lo, s_hi)
    vw = p[:, o + 5 * DKV:o + 6 * DKV]
    kv_ref[:, 0:2 * DKV] = p[:, o:o + 2 * DKV]
    kv_ref[:, 2 * DKV:3 * DKV] = ks
    kv_ref[:, 3 * DKV:4 * DKV] = vs
    kwvw_ref[:, 0:DKV] = kw
    kwvw_ref[:, DKV:2 * DKV] = vw
    for k in range(KVH):
        sl = slice(HD * k, HD * (k + 1))
        ks_ref[k] = ks[:, sl].astype(BF16)
        vs_ref[k] = vs[:, sl].astype(BF16)
        kw_ref[k] = kw[:, sl].astype(BF16)
        vw_ref[k] = vw[:, sl].astype(BF16)
    gates_ref[...] = _sigmoid(p[:, o + 6 * DKV:o + 6 * DKV + 128])


def _inproj_nsa(hn, w_nsa, rope_c, rope_slo, rope_shi, tm):
    n, d = hn.shape
    row = lambda i: (i, 0)
    hm = lambda i: (0, i, 0)
    hm_shape = jax.ShapeDtypeStruct((KVH, n, HD), BF16)
    hm_spec = pl.BlockSpec((KVH, tm, HD), hm)
    return pl.pallas_call(
        _inproj_nsa_body,
        grid=(n // tm,),
        in_specs=[pl.BlockSpec((tm, d), row),
                  pl.BlockSpec((d, NSA_PAD), lambda i: (0, 0)),
                  pl.BlockSpec((tm, 128), row), pl.BlockSpec((tm, 128), row), pl.BlockSpec((tm, 128), row)],
        out_specs=[pl.BlockSpec((tm, DH), row), pl.BlockSpec((tm, DH), row),
                   pl.BlockSpec((tm, 4 * DKV), row), pl.BlockSpec((tm, 2 * DKV), row),
                   hm_spec, hm_spec, hm_spec, hm_spec,
                   pl.BlockSpec((tm, 128), row)],
        out_shape=[jax.ShapeDtypeStruct((n, DH), BF16), jax.ShapeDtypeStruct((n, DH), BF16),
                   jax.ShapeDtypeStruct((n, 4 * DKV), F32), jax.ShapeDtypeStruct((n, 2 * DKV), F32),
                   hm_shape, hm_shape, hm_shape, hm_shape,
                   jax.ShapeDtypeStruct((n, 128), F32)],
        compiler_params=_cparams(1),
        name="inproj_nsa",
    )(hn, w_nsa, rope_c, rope_slo, rope_shi)


def _matmul_body(x_ref, w_ref, o_ref):
    o_ref[...] = _dot(x_ref[...], w_ref[...])


def _inproj_rwkv(hn, w_rw, tm):
    n, d = hn.shape
    c = w_rw.shape[1]
    return pl.pallas_call(
        _matmul_body,
        grid=(n // tm,),
        in_specs=[pl.BlockSpec((tm, d), lambda i: (i, 0)), pl.BlockSpec((d, c), lambda i: (0, 0))],
        out_specs=pl.BlockSpec((tm, c), lambda i: (i, 0)),
        out_shape=jax.ShapeDtypeStruct((n, c), F32),
        compiler_params=_cparams(1),
        name="inproj_rwkv",
    )(hn, w_rw)


def _compress_accumulate(stage_ref, w1c_refs, n_seg):
    accs = [[jnp.zeros((n_seg, 2 * CMP_HIDDEN), F32) for _ in range(KVH)] for _ in range(2)]
    for s in range(CMP_STRIDE):
        for kv in range(2):
            w = w1c_refs[kv][s]
            for pair in range(KVH // 2):
                x = stage_ref[2 * kv + pair, pl.ds(s, n_seg, stride=CMP_STRIDE), :]
                for j in range(2):
                    k = 2 * pair + j
                    accs[kv][k] = accs[kv][k] + _dot(x[:, HD * j:HD * (j + 1)].astype(BF16), w)
    return accs


def _compress_finish(acc, pe0, w2, n_seg):
    a0 = acc[:, :CMP_HIDDEN]
    a1 = pltpu.roll(acc[:, CMP_HIDDEN:], n_seg - 1, 0)
    pre = pe0 + a0 + a1
    return _dot((pre * _sigmoid(pre)).astype(BF16), w2)


def _cmp_prompt_body(kv_ref, w1k_ref, w1v_ref, pek_ref, pev_ref, w1kf_ref, w1vf_ref, w2k_ref, w2v_ref,
                     kc_ref, vc_ref, stage_ref):
    n_seg = kv_ref.shape[0] // CMP_STRIDE
    for grp in range(4):
        stage_ref[grp] = kv_ref[:, 128 * grp:128 * (grp + 1)]
    accs = _compress_accumulate(stage_ref, (w1k_ref, w1v_ref), n_seg)
    for kv, (pe_ref, w1f_ref, w2_ref, out_ref) in enumerate(
            ((pek_ref, w1kf_ref, w2k_ref, kc_ref), (pev_ref, w1vf_ref, w2v_ref, vc_ref))):
        pe0 = _dot(pe_ref[...], w1f_ref[...])
        w2 = w2_ref[...]
        for k in range(KVH):
            out_ref[0, k] = _compress_finish(accs[kv][k], pe0, w2, n_seg).astype(BF16)


def _cmp_weights(pe, w1, w2):
    w1c = jnp.concatenate([w1[:CMP_STRIDE], w1[CMP_STRIDE:]], axis=-1).astype(BF16)
    return w1c, pe.reshape(1, CMP_BLOCK * HD), w1.reshape(CMP_BLOCK * HD, CMP_HIDDEN), w2.astype(BF16)


def _compress_prompt(kv_rows, cmp_k, cmp_v, n_batch, t):
    n_seg = t // CMP_STRIDE
    w1k, pek, w1kf, w2k = cmp_k
    w1v, pev, w1vf, w2v = cmp_v
    full = lambda a: pl.BlockSpec(a.shape, lambda b: (0,) * a.ndim)
    out_shape = jax.ShapeDtypeStruct((n_batch, KVH, n_seg, HD), BF16)
    out_spec = pl.BlockSpec((1, KVH, n_seg, HD), lambda b: (b, 0, 0, 0))
    return pl.pallas_call(
        _cmp_prompt_body,
        grid=(n_batch,),
        in_specs=[pl.BlockSpec((t, 2 * DKV), lambda b: (b, 0)),
                  full(w1k), full(w1v), full(pek), full(pev), full(w1kf), full(w1vf), full(w2k), full(w2v)],
        out_specs=[out_spec, out_spec],
        out_shape=[out_shape, out_shape],
        scratch_shapes=[pltpu.VMEM((4, t, 128), F32)],
        compiler_params=_cparams(1),
        name="nsa_compress_prompt",
    )(kv_rows, w1k, w1v, pek, pev, w1kf, w1vf, w2k, w2v)


def _select_blocks(imp, cur, n_blocks):
    jb = lax.broadcasted_iota(jnp.int32, (1, N_SEL_LANES), 1)
    valid = (jb <= cur) & (jb < n_blocks)
    rel = cur - jb
    forced = (jb == 0) | ((rel >= 0) & (rel < N_LOCAL_FORCED))
    score = jnp.where(valid, imp + jnp.where(forced, FORCE_BONUS, 0.0), NEG)
    rank = jnp.zeros(score.shape, F32)
    for jp in range(min(n_blocks, N_SEL_LANES)):
        col = score[:, jp:jp + 1]
        tie = jnp.where(jb > jp, 1.0, 0.0)
        rank = rank + jnp.where(col > score, 1.0, jnp.where(col == score, tie, 0.0))
    return jnp.where(valid, jnp.where(rank < SEL_TOPN, 1.0, 0.0), 0.0)


def _softmax_rows(s, ok):
    sm = jnp.where(ok, s, NEG)
    m = jnp.max(sm, axis=-1, keepdims=True)
    e = jnp.where(ok, jnp.exp(sm - m), 0.0)
    l = jnp.sum(e, axis=-1, keepdims=True)
    return e / jnp.maximum(l, 1e-30)


def _softmax_bias(s, bias):
    sb = s + bias
    m = jnp.maximum(jnp.max(sb, axis=-1, keepdims=True), M_FLOOR)
    e = jnp.exp(sb - m)
    return e, 1.0 / jnp.maximum(jnp.sum(e, axis=-1, keepdims=True), 1e-30)


def _select_blocks_t(imp_t, cur, n_blocks, n_live, score_ref):
    jb = lax.broadcasted_iota(jnp.int32, (N_SEL_LANES, 1), 0)
    valid = (jb <= cur) & (jb < n_blocks)
    rel = cur - jb
    forced = (jb == 0) | ((rel >= 0) & (rel < N_LOCAL_FORCED))
    score = jnp.where(valid, imp_t + jnp.where(forced, FORCE_BONUS, 0.0), NEG)
    score_ref[...] = score

    def body(jp, rank):
        other = score_ref[pl.ds(jp, 1), :]
        tie = jnp.where(jb > jp, 1.0, 0.0)
        return rank + jnp.where(other > score, 1.0, jnp.where(other == score, tie, 0.0))

    rank = lax.fori_loop(0, n_live, body, jnp.zeros(score.shape, F32))
    return jnp.where(valid, jnp.where(rank < SEL_TOPN, 1.0, 0.0), 0.0)


def _attn_prompt_body(qraw_ref, qrot_ref, gates_ref, kc_ref, vc_ref, ks_ref, vs_ref, kw_ref, vw_ref,
                      mcst_ref, e_ref, o_ref, score_ref):
    t = ks_ref.shape[1]
    n_seg = kc_ref.shape[2]
    i = pl.program_id(1)
    s0 = i * Q_BLOCK
    qpos = s0 + lax.broadcasted_iota(jnp.int32, (Q_BLOCK, 1), 0)
    qpos4 = jnp.concatenate([qpos] * GQA, axis=0)
    qpos_row = s0 + lax.broadcasted_iota(jnp.int32, (1, Q_BLOCK), 1)
    n_idx = lax.broadcasted_iota(jnp.int32, (1, n_seg), 1)
    ok_cmp = (n_idx * CMP_STRIDE + (CMP_BLOCK - 1) <= qpos4) & (n_idx < n_seg - 1)
    bias_cmp = jnp.where(ok_cmp, 0.0, MASK_BIAS)
    gates = gates_ref[...]
    mcst = mcst_ref[...]
    n_key_chunks = i // (SEL_KEY_CHUNK // Q_BLOCK) + 1
    n_live_blocks = (s0 + Q_BLOCK) // SEL_BLOCK
    win_start = pl.multiple_of(jnp.maximum(s0 - WINDOW, 0), Q_BLOCK)
    win_len = WINDOW + Q_BLOCK
    wpos = win_start + lax.broadcasted_iota(jnp.int32, (1, win_len), 1)
    dlt = qpos4 - wpos
    bias_win = jnp.where((dlt >= 0) & (dlt <= WINDOW), 0.0, MASK_BIAS)

    for k in range(KVH):
        qr = jnp.concatenate([qraw_ref[:, HD * (GQA * k + g):HD * (GQA * k + g + 1)] for g in range(GQA)], axis=0)
        qs = jnp.concatenate([qrot_ref[:, HD * (GQA * k + g):HD * (GQA * k + g + 1)] for g in range(GQA)], axis=0)
        e_c, inv_c = _softmax_bias(_dot_nt(qr, kc_ref[0, k]), bias_cmp)
        p_cmp = e_c * inv_c
        o_cmp = _dot(p_cmp.astype(BF16), vc_ref[0, k])
        p_sum = p_cmp[0:Q_BLOCK]
        for g in range(1, GQA):
            p_sum = p_sum + p_cmp[g * Q_BLOCK:(g + 1) * Q_BLOCK]
        hi = p_sum.astype(BF16)
        r1 = p_sum - hi.astype(F32)
        mid = r1.astype(BF16)
        lo = (r1 - mid.astype(F32)).astype(BF16)
        imp_t = _dot_nt(mcst, hi) + _dot_nt(mcst, mid) + _dot_nt(mcst, lo)
        sel_t = _select_blocks_t(imp_t, qpos_row // SEL_BLOCK, t // SEL_BLOCK, n_live_blocks,
                                 score_ref).astype(BF16)

        def chunk(c, carry, qs=qs, sel_t=sel_t, k=k):
            m, l, acc = carry
            k0 = pl.multiple_of(c * SEL_KEY_CHUNK, SEL_KEY_CHUNK)
            s = _dot_nt(qs, ks_ref[k, pl.ds(k0, SEL_KEY_CHUNK), :])
            selk = _dot_tn(sel_t, e_ref[:, pl.ds(k0, SEL_KEY_CHUNK)])
            kpos = k0 + lax.broadcasted_iota(jnp.int32, (1, SEL_KEY_CHUNK), 1)
            bias = (jnp.where(kpos <= qpos, selk, 0.0) - 1.0) * (-MASK_BIAS)
            sb = s + jnp.concatenate([bias] * GQA, axis=0)
            m_new = jnp.maximum(m, jnp.max(sb, axis=-1, keepdims=True))
            p = jnp.exp(sb - m_new)
            alpha = jnp.exp(m - m_new)
            l = alpha * l + jnp.sum(p, axis=-1, keepdims=True)
            acc = alpha * acc + _dot(p.astype(BF16), vs_ref[k, pl.ds(k0, SEL_KEY_CHUNK), :])
            return m_new, l, acc

        init = (jnp.full((GQA * Q_BLOCK, 1), M_FLOOR, F32), jnp.zeros((GQA * Q_BLOCK, 1), F32),
                jnp.zeros((GQA * Q_BLOCK, HD), F32))
        _, l_sel, acc_sel = lax.fori_loop(0, n_key_chunks, chunk, init)
        o_sel = acc_sel * (1.0 / jnp.maximum(l_sel, 1e-30))

        e_w, inv_w = _softmax_bias(_dot_nt(qs, kw_ref[k, pl.ds(win_start, win_len), :]), bias_win)
        o_win = _dot(e_w.astype(BF16), vw_ref[k, pl.ds(win_start, win_len), :]) * inv_w

        for g in range(GQA):
            h = GQA * k + g
            rows = slice(g * Q_BLOCK, (g + 1) * Q_BLOCK)
            o_h = (gates[:, 3 * h:3 * h + 1] * o_cmp[rows] + gates[:, 3 * h + 1:3 * h + 2] * o_sel[rows]
                   + gates[:, 3 * h + 2:3 * h + 3] * o_win[rows])
            o_ref[:, HD * h:HD * (h + 1)] = o_h.astype(o_ref.dtype)


def _attn_prompt(q_raw, q_rot, gates, k_cmp, v_cmp, ks, vs, kw, vw, mcs_t, e_mat, n_batch, t):
    nqb = t // Q_BLOCK
    n_seg = t // CMP_STRIDE
    qrow = lambda b, i: (b * nqb + i, 0)
    cmp_spec = pl.BlockSpec((1, KVH, n_seg, HD), lambda b, i: (b, 0, 0, 0))
    kv_spec = pl.BlockSpec((KVH, t, HD), lambda b, i: (0, b, 0))
    return pl.pallas_call(
        _attn_prompt_body,
        grid=(n_batch, nqb),
        in_specs=[pl.BlockSpec((Q_BLOCK, DH), qrow), pl.BlockSpec((Q_BLOCK, DH), qrow),
                  pl.BlockSpec((Q_BLOCK, 128), qrow),
                  cmp_spec, cmp_spec, kv_spec, kv_spec, kv_spec, kv_spec,
                  pl.BlockSpec(mcs_t.shape, lambda b, i: (0, 0)),
                  pl.BlockSpec(e_mat.shape, lambda b, i: (0, 0))],
        out_specs=pl.BlockSpec((Q_BLOCK, DH), qrow),
        out_shape=jax.ShapeDtypeStruct((n_batch * t, DH), F32),
        scratch_shapes=[pltpu.VMEM((N_SEL_LANES, Q_BLOCK), F32)],
        compiler_params=_cparams(2),
        name="nsa_attn_prompt",
    )(q_raw, q_rot, gates, k_cmp, v_cmp, ks, vs, kw, vw, mcs_t, e_mat)


def _pick_kv_group(full):
    hk = lax.broadcasted_iota(jnp.int32, (NH, 1), 0) // GQA
    out = jnp.zeros((NH, HD), F32)
    for k in range(KVH):
        out = out + jnp.where(hk == k, full[:, HD * k:HD * (k + 1)], 0.0)
    return out


def _nsa_sample_body(n_pages, pt_ref, *refs):
    pages = refs[:n_pages]
    (win_ref, qraw_ref, qrot_ref, gates_ref, kvnew_ref, kwnew_ref,
     w1k_ref, w1v_ref, pek_ref, pev_ref, w1kf_ref, w1vf_ref, w2k_ref, w2v_ref, mcs_ref, e_ref,
     o_ref, winout_ref, stage_ref) = refs[n_pages:]
    page = pages[0].shape[1]
    past = n_pages * page
    n_seg = past // CMP_STRIDE

    def bdiag(q):
        qt = jnp.concatenate([q] * KVH, axis=1)
        hk = lax.broadcasted_iota(jnp.int32, (NH, DKV), 0) // GQA
        lk = lax.broadcasted_iota(jnp.int32, (NH, DKV), 1) // HD
        return jnp.where(hk == lk, qt, 0.0)

    qr = bdiag(qraw_ref[0])
    qs = bdiag(qrot_ref[0])
    qr_b, qs_b = qr.astype(BF16), qs.astype(BF16)

    for kk, pg in enumerate(pages):
        for grp in range(4):
            stage_ref[grp, page * kk:page * (kk + 1), :] = pg[0, :, 128 * grp:128 * (grp + 1)]
    accs = _compress_accumulate(stage_ref, (w1k_ref, w1v_ref), n_seg)
    cmp_kv = []
    for kv, (pe_ref, w1f_ref, w2_ref) in enumerate(((pek_ref, w1kf_ref, w2k_ref), (pev_ref, w1vf_ref, w2v_ref))):
        pe0 = _dot(pe_ref[...], w1f_ref[...])
        w2 = w2_ref[...]
        cmp_kv.append(jnp.concatenate(
            [_compress_finish(accs[kv][k], pe0, w2, n_seg) for k in range(KVH)], axis=1).astype(BF16))
    k_cmp, v_cmp = cmp_kv

    n_idx = lax.broadcasted_iota(jnp.int32, (1, n_seg), 1)
    ok_cmp = (n_idx * CMP_STRIDE + (CMP_BLOCK - 1) <= past) & (n_idx < n_seg - 1)
    p_cmp = _softmax_rows(_dot_nt(qr_b, k_cmp), ok_cmp)
    o_cmp = _pick_kv_group(_dot(p_cmp.astype(BF16), v_cmp))
    p_sum = jnp.concatenate(
        [jnp.sum(p_cmp[GQA * k:GQA * (k + 1)], axis=0, keepdims=True) for k in range(KVH)], axis=0)
    imp = _dot_split3(p_sum, mcs_ref[...])
    n_blocks = past // SEL_BLOCK + 1
    cur = jnp.full((KVH, 1), past // SEL_BLOCK, jnp.int32)
    sel = _select_blocks(imp, cur, n_blocks)
    sel16 = jnp.concatenate([jnp.broadcast_to(sel[k:k + 1], (GQA, N_SEL_LANES)) for k in range(KVH)], axis=0)

    kv_new = kvnew_ref[0]
    ks_new, vs_new = kv_new[:, 2 * DKV:3 * DKV], kv_new[:, 3 * DKV:4 * DKV]
    k_sel = jnp.concatenate([pg[0, :, 2 * DKV:3 * DKV] for pg in pages], axis=0).astype(BF16)
    v_sel = jnp.concatenate([pg[0, :, 3 * DKV:4 * DKV] for pg in pages], axis=0).astype(BF16)
    s_c = _dot_nt(qs_b, k_sel)
    s_n = jnp.sum(qs * ks_new, axis=-1, keepdims=True)
    ok_c = _dot(sel16.astype(BF16), e_ref[...]) > 0.5
    new_blk = past // SEL_BLOCK
    ok_n = sel16[:, new_blk:new_blk + 1] > 0.5
    sm_c = jnp.where(ok_c, s_c, NEG)
    sm_n = jnp.where(ok_n, s_n, NEG)
    m = jnp.maximum(jnp.max(sm_c, axis=-1, keepdims=True), sm_n)
    e_c = jnp.where(ok_c, jnp.exp(sm_c - m), 0.0)
    e_n = jnp.where(ok_n, jnp.exp(sm_n - m), 0.0)
    inv = 1.0 / jnp.maximum(jnp.sum(e_c, axis=-1, keepdims=True) + e_n, 1e-30)
    o_sel = _pick_kv_group(_dot((e_c * inv).astype(BF16), v_sel)
                           + (e_n * inv).astype(BF16).astype(F32) * vs_new.astype(BF16).astype(F32))

    kw_new = kwnew_ref[0]
    kwn, vwn = kw_new[:, 0:DKV], kw_new[:, DKV:2 * DKV]
    s_w = _dot_nt(qs_b, win_ref[0, :, 0:DKV].astype(BF16))
    s_wn = jnp.sum(qs * kwn, axis=-1, keepdims=True)
    mw = jnp.maximum(jnp.max(s_w, axis=-1, keepdims=True), s_wn)
    e_w = jnp.exp(s_w - mw)
    e_wn = jnp.exp(s_wn - mw)
    inv_w = 1.0 / (jnp.sum(e_w, axis=-1, keepdims=True) + e_wn)
    o_win = _pick_kv_group(_dot((e_w * inv_w).astype(BF16), win_ref[0, :, DKV:2 * DKV].astype(BF16))
                           + (e_wn * inv_w).astype(BF16).astype(F32) * vwn.astype(BF16).astype(F32))

    g = gates_ref[0]
    o_ref[0] = g[:, 0:1] * o_cmp + g[:, 1:2] * o_sel + g[:, 2:3] * o_win
    wb = win_ref.shape[1]
    winout_ref[0, 0:wb - 1, :] = win_ref[0, 1:wb, :]
    winout_ref[0, wb - 1:wb, :] = kw_new


def _nsa_sample(cache3, page_table, state_win, q_raw, q_rot, gates, kv_new, kw_new, cmp_k, cmp_v, mcs, e_mat):
    nb, n_pages = page_table.shape
    page = cache3.shape[1]
    wb = state_win.shape[1]
    w1k, pek, w1kf, w2k = cmp_k
    w1v, pev, w1vf, w2v = cmp_v
    consts = [w1k, w1v, pek, pev, w1kf, w1vf, w2k, w2v, mcs, e_mat]
    full = lambda a: pl.BlockSpec(a.shape, lambda b, pt: (0,) * a.ndim)
    page_specs = [pl.BlockSpec((1, page, cache3.shape[2]), functools.partial(
        lambda b, pt, kk: (pt[b * n_pages + kk], 0, 0), kk=kk)) for kk in range(n_pages)]
    per_b = lambda shape: pl.BlockSpec((1,) + shape, lambda b, pt: (b, 0, 0))
    grid_spec = pltpu.PrefetchScalarGridSpec(
        num_scalar_prefetch=1,
        grid=(nb,),
        in_specs=page_specs + [per_b((wb, 2 * DKV)), per_b((NH, HD)), per_b((NH, HD)), per_b((NH, 3)),
                               per_b((1, 4 * DKV)), per_b((1, 2 * DKV))] + [full(a) for a in consts],
        out_specs=[per_b((NH, HD)), per_b((wb, 2 * DKV))],
        scratch_shapes=[pltpu.VMEM((4, n_pages * page, 128), F32)],
    )
    return pl.pallas_call(
        functools.partial(_nsa_sample_body, n_pages),
        grid_spec=grid_spec,
        out_shape=[jax.ShapeDtypeStruct((nb, NH, HD), F32), jax.ShapeDtypeStruct((nb, wb, 2 * DKV), F32)],
        compiler_params=_cparams(1),
        name="nsa_sample",
    )(page_table.reshape(-1), *([cache3] * n_pages), state_win, q_raw, q_rot, gates, kv_new, kw_new, *consts)


def _rwkv_prep(cols, prev, mu, w0, w2, a0, a2, g2, kk_p, ka):
    xs = cols + (prev - cols) * mu
    r = xs[:, 0:DH]
    k = xs[:, DH:2 * DH]
    v = xs[:, 2 * DH:3 * DH]
    o = 3 * DH
    wd = xs[:, o:o + LORA_W]
    ad = xs[:, o + LORA_W:o + LORA_W + LORA_A]
    gd = xs[:, o + LORA_W + LORA_A:o + LORA_W + LORA_A + LORA_G_PAD]
    w = w0 + _dot(jnp.tanh(wd).astype(BF16), w2)
    w_log = -jax.nn.softplus(-w) - 0.5
    lw = -jnp.exp(w_log)
    a = _sigmoid(a0 + _dot(ad.astype(BF16), a2))
    g = _dot(_sigmoid(gd).astype(BF16), g2)
    kkv = k * kk_p
    k_mod = k * (1.0 + (a - 1.0) * ka)
    return r, k_mod, v, kkv, a, lw, g


def _head_norm(kkv_h):
    return kkv_h / jnp.maximum(jnp.sqrt(jnp.sum(kkv_h * kkv_h, axis=-1, keepdims=True)), 1e-12)


def _rwkv_head_out(y, r_h, k_h, v_h, g_h, rk_h, gnw_h, gnb_h):
    mean = jnp.mean(y, axis=-1, keepdims=True)
    var = jnp.mean(jnp.square(y - mean), axis=-1, keepdims=True)
    yn = (y - mean) * lax.rsqrt(var + GN_EPS) * gnw_h + gnb_h
    bonus = jnp.sum(r_h * k_h * rk_h, axis=-1, keepdims=True) * v_h
    return (yn + bonus) * g_h


def _tri_inverse_all(a_list, blk16, blk32):
    c = a_list[0].shape[0]
    eye = jnp.where(lax.broadcasted_iota(jnp.int32, (c, c), 0) == lax.broadcasted_iota(jnp.int32, (c, c), 1),
                    1.0, 0.0)
    b = lambda x: x.astype(BF16)
    a16 = [b(jnp.where(blk16, a, 0.0)) for a in a_list]
    p = [_dot(x, x) for x in a16]
    t = [_dot(b(eye) - x, b(eye + q)) for x, q in zip(a16, p)]
    for _ in range(2):
        pb = [b(q) for q in p]
        p = [_dot(q, q) for q in pb]
        t = [_dot(b(x), b(eye + q)) for x, q in zip(t, p)]
    off32 = blk32 & jnp.logical_not(blk16)
    for mask in (off32, jnp.logical_not(blk32)):
        tb = [b(x) for x in t]
        m = [_dot(x, b(jnp.where(mask, a, 0.0))) for x, a in zip(tb, a_list)]
        t = [x - _dot(b(y), xb) for x, y, xb in zip(t, m, tb)]
    return t


def _rwkv_prompt_body(cols_ref, mu_ref, w0_ref, w2_ref, a0_ref, a2_ref, g2_ref, kk_ref, ka_ref, rk_ref,
                      gnw_ref, gnb_ref, o_ref, wkv_ref, s_ref, last_ref):
    c = pl.program_id(1)
    ch = RW_CHUNK

    @pl.when(c == 0)
    def _():
        s_ref[...] = jnp.zeros_like(s_ref)
        last_ref[...] = jnp.zeros_like(last_ref)

    cols = cols_ref[...]
    row = lax.broadcasted_iota(jnp.int32, (ch, 1), 0)
    prev = jnp.where(row == 0, last_ref[...], pltpu.roll(cols, 1, 0))
    last_ref[...] = cols[ch - 1:ch, :]
    r, k_mod, v, kkv, a, lw, g = _rwkv_prep(cols, prev, mu_ref[...], w0_ref[...], w2_ref[...], a0_ref[...],
                                            a2_ref[...], g2_ref[...], kk_ref[...], ka_ref[...])
    ti = lax.broadcasted_iota(jnp.int32, (ch, ch), 0)
    si = lax.broadcasted_iota(jnp.int32, (ch, ch), 1)
    ltri = jnp.where(ti >= si, 1.0, 0.0).astype(BF16)
    cl = _cumsum_rows(lw, ltri)
    e_in = jnp.exp(cl)
    e_ex = jnp.exp(cl - lw)
    e_ng = jnp.exp(-cl)
    cl_end = cl[ch - 1:ch, :]
    e_end = jnp.exp(cl_end - cl)
    g_end = jnp.exp(cl_end)
    strict = ti > si
    causal2 = (lax.broadcasted_iota(jnp.int32, (ch, 2 * ch), 0)
               >= lax.broadcasted_iota(jnp.int32, (ch, 2 * ch), 1) % ch)
    blk16 = (ti // 16) == (si // 16)
    blk32 = (ti // 32) == (si // 32)
    rk, gnw, gnb = rk_ref[...], gnw_ref[...], gnb_ref[...]
    b = lambda x: x.astype(BF16)

    heads = range(NH)
    sls = [slice(HD * h, HD * (h + 1)) for h in heads]

    kk_n = [_head_norm(kkv[:, sl]) for sl in sls]
    beta = [kk_n[h] * a[:, sls[h]] for h in heads]
    lhs = [b(jnp.concatenate([kk_n[h] * e_ex[:, sls[h]], r[:, sls[h]] * e_in[:, sls[h]]], axis=0)) for h in heads]
    rhs = [b(jnp.concatenate([beta[h] * e_ng[:, sls[h]], k_mod[:, sls[h]] * e_ng[:, sls[h]]], axis=0))
           for h in heads]
    dec = [b(jnp.concatenate([beta[h] * e_end[:, sls[h]], k_mod[:, sls[h]] * e_end[:, sls[h]]], axis=0))
           for h in heads]
    s0 = [s_ref[h] for h in heads]
    quad = [_dot_nt(lhs[h], rhs[h]) for h in heads]
    s0t = [_dot_nt(lhs[h], b(s0[h])) for h in heads]
    a_b = [jnp.where(strict, q[0:ch, 0:ch], 0.0) for q in quad]
    a_kv = [_dot(b(jnp.where(strict, quad[h][0:ch, ch:2 * ch], 0.0)), b(v[:, sls[h]])) for h in heads]
    b_bk = [b(jnp.where(causal2, q[ch:2 * ch, :], 0.0)) for q in quad]
    t_inv = _tri_inverse_all(a_b, blk16, blk32)
    u = [-_dot(b(t_inv[h]), b(s0t[h][0:ch] + a_kv[h])) for h in heads]
    uv = [b(jnp.concatenate([u[h], v[:, sls[h]]], axis=0)) for h in heads]
    y = [s0t[h][ch:2 * ch] + _dot(b_bk[h], uv[h]) for h in heads]
    s_new = [s0[h] * g_end[:, sls[h]] + _dot_tn(uv[h], dec[h]) for h in heads]
    for h in heads:
        sl = sls[h]
        s_ref[h] = s_new[h]
        o_h = _rwkv_head_out(y[h], r[:, sl], k_mod[:, sl], v[:, sl], g[:, sl], rk[:, sl], gnw[:, sl], gnb[:, sl])
        o_ref[:, sl] = o_h.astype(o_ref.dtype)

    @pl.when(c == pl.num_programs(1) - 1)
    def _():
        wkv_ref[0] = s_ref[...]


def _cumsum_rows(x, ltri):
    hi = x.astype(BF16)
    r1 = x - hi.astype(F32)
    mid = r1.astype(BF16)
    lo = (r1 - mid.astype(F32)).astype(BF16)
    return _dot(ltri, hi) + _dot(ltri, mid) + _dot(ltri, lo)


def _rwkv_prompt(cols, rw, n_batch, t):
    nc = t // RW_CHUNK
    full = lambda a: pl.BlockSpec(a.shape, lambda b, c: (0,) * a.ndim)
    row = lambda b, c: (b * nc + c, 0)
    return pl.pallas_call(
        _rwkv_prompt_body,
        grid=(n_batch, nc),
        in_specs=[pl.BlockSpec((RW_CHUNK, RWKV_PAD), row)] + [full(a) for a in rw],
        out_specs=[pl.BlockSpec((RW_CHUNK, DH), row),
                   pl.BlockSpec((1, NH, HD, HD), lambda b, c: (b, 0, 0, 0))],
        out_shape=[jax.ShapeDtypeStruct((n_batch * t, DH), F32),
                   jax.ShapeDtypeStruct((n_batch, NH, HD, HD), F32)],
        scratch_shapes=[pltpu.VMEM((NH, HD, HD), F32), pltpu.VMEM((1, RWKV_PAD), F32)],
        compiler_params=_cparams(2),
        name="rwkv_prompt",
    )(cols, *rw)


def _rwkv_sample_prep_body(cols_ref, prev_ref, mu_ref, w0_ref, w2_ref, a0_ref, a2_ref, g2_ref, kk_ref, ka_ref,
                           r_ref, k_ref, v_ref, kkn_ref, a_ref, d_ref, g_ref):
    r, k_mod, v, kkv, a, lw, g = _rwkv_prep(cols_ref[...], prev_ref[...], mu_ref[...], w0_ref[...], w2_ref[...],
                                            a0_ref[...], a2_ref[...], g2_ref[...], kk_ref[...], ka_ref[...])
    r_ref[...] = r
    k_ref[...] = k_mod
    v_ref[...] = v
    a_ref[...] = a
    d_ref[...] = jnp.exp(lw)
    g_ref[...] = g
    for h in range(NH):
        sl = slice(HD * h, HD * (h + 1))
        kkn_ref[:, sl] = _head_norm(kkv[:, sl])


def _rwkv_sample_prep(cols, prev, rw_prep):
    n = cols.shape[0]
    args = [cols, prev] + list(rw_prep)
    full = lambda a: pl.BlockSpec(a.shape, lambda i: (0,) * a.ndim)
    out = jax.ShapeDtypeStruct((n, DH), F32)
    return pl.pallas_call(
        _rwkv_sample_prep_body,
        grid=(1,),
        in_specs=[full(a) for a in args],
        out_specs=[pl.BlockSpec((n, DH), lambda i: (0, 0))] * 7,
        out_shape=[out] * 7,
        compiler_params=_cparams(1),
        name="rwkv_sample_prep",
    )(*args)


def _rwkv_sample_step_body(s_ref, r_ref, k_ref, v_ref, kk_ref, a_ref, d_ref, g_ref, rk_ref, gnw_ref, gnb_ref,
                           o_ref, sout_ref):
    s = s_ref[0]
    r, k, v, kk, a, d, g = (x[0] for x in (r_ref, k_ref, v_ref, kk_ref, a_ref, d_ref, g_ref))
    eye = jnp.where(lax.broadcasted_iota(jnp.int32, (HD, HD), 0) == lax.broadcasted_iota(jnp.int32, (HD, HD), 1),
                    1.0, 0.0)
    row = lambda x: x[:, None, :]
    col = lambda x: jnp.sum(eye[None] * x[:, None, :], axis=-1, keepdims=True)
    sa = -jnp.sum(s * row(kk), axis=-1, keepdims=True)
    s_new = s * row(d) + sa * row(kk * a) + col(v) * row(k)
    sout_ref[0] = s_new
    y_col = jnp.sum(s_new * row(r), axis=-1, keepdims=True)
    y = jnp.sum(y_col * eye[None], axis=1)
    o_ref[0] = _rwkv_head_out(y, r, k, v, g, rk_ref[...], gnw_ref[...], gnb_ref[...])


def _rwkv_sample_step(state, vecs, rk, gnw, gnb):
    nb = state.shape[0]
    per_b3 = pl.BlockSpec((1, NH, HD), lambda b: (b, 0, 0))
    per_b4 = pl.BlockSpec((1, NH, HD, HD), lambda b: (b, 0, 0, 0))
    full = pl.BlockSpec((NH, HD), lambda b: (0, 0))
    return pl.pallas_call(
        _rwkv_sample_step_body,
        grid=(nb,),
        in_specs=[per_b4] + [per_b3] * 7 + [full] * 3,
        out_specs=[per_b3, per_b4],
        out_shape=[jax.ShapeDtypeStruct((nb, NH, HD), F32), jax.ShapeDtypeStruct((nb, NH, HD, HD), F32)],
        compiler_params=_cparams(1),
        name="rwkv_sample_step",
    )(state, *vecs, rk, gnw, gnb)


def _outproj_body(oa_ref, orw_ref, wa_ref, wr_ref, h_ref, g_ref, o_ref):
    mixed = _dot(oa_ref[...].astype(BF16), wa_ref[...]) + _dot(orw_ref[...].astype(BF16), wr_ref[...])
    o_ref[...] = h_ref[...] + _rms(mixed, g_ref[...])


def _outproj(o_attn, o_rwkv, w_a, w_r, h, g_post, tm):
    n, d = h.shape
    row = lambda i: (i, 0)
    const = lambda i: (0, 0)
    return pl.pallas_call(
        _outproj_body,
        grid=(n // tm,),
        in_specs=[pl.BlockSpec((tm, DH), row), pl.BlockSpec((tm, DH), row),
                  pl.BlockSpec((DH, d), const), pl.BlockSpec((DH, d), const),
                  pl.BlockSpec((tm, d), row), pl.BlockSpec((1, d), const)],
        out_specs=pl.BlockSpec((tm, d), row),
        out_shape=jax.ShapeDtypeStruct((n, d), F32),
        compiler_params=_cparams(1),
        name="outproj",
    )(o_attn, o_rwkv, w_a, w_r, h, g_post)


def _rope_tables(pos):
    inv = ROPE_THETA ** (-jnp.arange(ROPE_HALF, dtype=F32) / ROPE_HALF)
    ang = pos.astype(F32)[:, None] * inv[None, :]
    cos, sin = jnp.cos(ang), jnp.sin(ang)
    n = pos.shape[0]
    rest = HD - 2 * ROPE_HALF
    c = jnp.concatenate([cos, cos, jnp.ones((n, rest), F32)], axis=1)
    s_lo = jnp.concatenate([-sin, jnp.zeros((n, HD - ROPE_HALF), F32)], axis=1)
    s_hi = jnp.concatenate([jnp.zeros((n, ROPE_HALF), F32), sin, jnp.zeros((n, rest), F32)], axis=1)
    two = lambda x: jnp.concatenate([x, x], axis=1)
    return two(c), two(s_lo), two(s_hi)


def _cmp_to_sel(n_cmp, n_sel, rows):
    i = np.arange(n_cmp)[:, None] * CMP_STRIDE
    j = np.arange(n_sel)[None, :] * SEL_BLOCK
    ov = np.minimum(i + CMP_BLOCK, j + SEL_BLOCK) - np.maximum(i, j)
    m = np.zeros((rows, N_SEL_LANES), np.float32)
    m[:n_cmp, :n_sel] = np.maximum(ov, 0) // CMP_STRIDE
    return jnp.asarray(m, BF16)


def _block_expand(n_keys):
    j = np.arange(N_SEL_LANES)[:, None]
    t = np.arange(n_keys)[None, :]
    return jnp.asarray((t // SEL_BLOCK == j).astype(np.float32), BF16)


def _pad_cols(x, width):
    return jnp.pad(x, ((0, 0), (0, width - x.shape[1])))


def _row_tile(n, cap):
    tm = min(n, cap)
    while n % tm:
        tm //= 2
    return tm


def kernel(x_prompt, x_sample, cache_nsa, page_table, state_win, state_wkv, state_shift, norm_f1_pre, norm_f1_post, ffn1_gu, ffn1_dn, norm_mix_pre, w_in, cmp_pe_k, cmp_w1_k, cmp_w2_k, cmp_pe_v, cmp_w1_v, cmp_w2_v, rw_mu, rw_w0, rw_w2, rw_a0, rw_a2, rw_g2, rw_kk, rw_ka, rw_rk, rw_gn_w, rw_gn_b, w_out, norm_mix_post, norm_f2_pre, ffn2_gu, ffn2_dn, norm_f2_post):
    depth = w_in.shape[0]
    assert depth == 1, "single-layer step"
    bp, t, d = x_prompt.shape
    bs, ts, _ = x_sample.shape
    assert ts == 1 and t % Q_BLOCK == 0 and t % RW_CHUNK == 0 and t >= WINDOW
    n_pages = page_table.shape[1]
    page = cache_nsa.shape[2]
    past = n_pages * page
    wb = state_win.shape[2]
    assert wb == min(WINDOW, past) and past % SEL_BLOCK == 0 and page % CMP_STRIDE == 0
    l = 0

    w1g, w1d = ffn1_gu[l].astype(BF16), ffn1_dn[l].astype(BF16)
    w2g, w2d = ffn2_gu[l].astype(BF16), ffn2_dn[l].astype(BF16)
    wi = w_in[l]
    w_nsa = _pad_cols(wi[:, :NSA_COLS], NSA_PAD).astype(BF16)
    w_rw = _pad_cols(wi[:, NSA_COLS:], RWKV_PAD).astype(BF16)
    w_oa, w_or = w_out[l, :DH].astype(BF16), w_out[l, DH:].astype(BF16)
    vec = lambda a: a.reshape(1, -1)
    cmp_k = _cmp_weights(cmp_pe_k[l], cmp_w1_k[l], cmp_w2_k[l])
    cmp_v = _cmp_weights(cmp_pe_v[l], cmp_w1_v[l], cmp_w2_v[l])
    g2_pad = jnp.pad(rw_g2[l], ((0, LORA_G_PAD - LORA_G), (0, 0))).astype(BF16)
    rw_prep = [_pad_cols(vec(rw_mu[l]), RWKV_PAD), vec(rw_w0[l]), rw_w2[l].astype(BF16), vec(rw_a0[l]),
               rw_a2[l].astype(BF16), g2_pad, vec(rw_kk[l]), vec(rw_ka[l])]
    rk_row, gnw_row, gnb_row = vec(rw_rk[l]), vec(rw_gn_w[l]), vec(rw_gn_b[l])

    outs = {}
    for name, x2, n_batch in (("p", x_prompt.reshape(bp * t, d), bp), ("s", x_sample.reshape(bs, d), bs)):
        n = x2.shape[0]
        tm = _row_tile(n, 512)
        is_prompt = name == "p"
        h1, hn = _ffn_half(x2, vec(norm_f1_pre[l]), w1g, w1d, vec(norm_f1_post[l]), vec(norm_mix_pre[l]),
                           tm, 512)
        pos = jnp.tile(jnp.arange(t), bp) if is_prompt else jnp.full((n,), past, jnp.int32)
        rc, rlo, rhi = _rope_tables(pos)
        q_raw, q_rot, kv_rows, kwvw, ks, vs, kw, vw, gates = _inproj_nsa(hn, w_nsa, rc, rlo, rhi, tm)
        cols = _inproj_rwkv(hn, w_rw, tm)
        if is_prompt:
            k_cmp, v_cmp = _compress_prompt(kv_rows, cmp_k, cmp_v, n_batch, t)
            n_seg = t // CMP_STRIDE
            mcs_t = _cmp_to_sel(n_seg - 1, t // SEL_BLOCK, n_seg).T
            o_attn = _attn_prompt(q_raw, q_rot, gates, k_cmp, v_cmp, ks, vs, kw, vw, mcs_t, _block_expand(t),
                                  n_batch, t)
            o_rwkv, wkv = _rwkv_prompt(cols, rw_prep + [rk_row, gnw_row, gnb_row], n_batch, t)
            win = kwvw.reshape(n_batch, t, 2 * DKV)[:, t - wb:]
            shift = cols.reshape(n_batch, t, RWKV_PAD)[:, t - 1:, :RWKV_COLS]
        else:
            cache3 = cache_nsa[l].reshape(cache_nsa.shape[1], page, 4 * DKV)
            n_seg = past // CMP_STRIDE
            mcs = _cmp_to_sel(n_seg - 1, past // SEL_BLOCK + 1, n_seg)
            o_attn, win = _nsa_sample(
                cache3, page_table, state_win[l].reshape(n, wb, 2 * DKV),
                q_raw.astype(F32).reshape(n, NH, HD), q_rot.astype(F32).reshape(n, NH, HD),
                gates[:, :3 * NH].reshape(n, NH, 3), kv_rows.reshape(n, 1, 4 * DKV), kwvw.reshape(n, 1, 2 * DKV),
                cmp_k, cmp_v, mcs, _block_expand(past))
            o_attn = o_attn.reshape(n, DH)
            prev = _pad_cols(state_shift[l].reshape(n, RWKV_COLS), RWKV_PAD)
            vecs = _rwkv_sample_prep(cols, prev, rw_prep)
            o_rwkv, wkv = _rwkv_sample_step(state_wkv[l], [v_.reshape(n, NH, HD) for v_ in vecs],
                                            rw_rk[l], rw_gn_w[l].reshape(NH, HD), rw_gn_b[l].reshape(NH, HD))
            o_rwkv = o_rwkv.reshape(n, DH)
            shift = cols[:, None, :RWKV_COLS]
        h2 = _outproj(o_attn, o_rwkv, w_oa, w_or, h1, vec(norm_mix_post[l]), tm)
        y = _ffn_half(h2, vec(norm_f2_pre[l]), w2g, w2d, vec(norm_f2_post[l]), None, tm, 512)
        outs[name] = (y, kv_rows, win, wkv, shift)

    yp, kvp, winp, wkvp, shp = outs["p"]
    ys, kvs, wins, wkvs, shs = outs["s"]
    return (yp.reshape(bp, t, d), ys.reshape(bs, 1, d),
            kvp.reshape(1, bp, t, 4, KVH, HD), kvs.reshape(1, bs, 1, 4, KVH, HD),
            winp.reshape(1, bp, wb, 2, KVH, HD), wins.reshape(1, bs, wb, 2, KVH, HD),
            wkvp[None], wkvs[None], shp[None], shs[None])
```

```python
import functools

import numpy as np
import jax
import jax.numpy as jnp
from jax import lax
from jax.experimental import pallas as pl
from jax.experimental.pallas import tpu as pltpu

F32 = jnp.float32
BF16 = jnp.bfloat16

HD = 64
KVH = 4
GQA = 4
NH = 16
DH = NH * HD
DKV = KVH * HD
ROPE_HALF = 8
ROPE_THETA = 500000.0
CMP_BLOCK = 32
CMP_STRIDE = 16
CMP_HIDDEN = 128
SEL_BLOCK = 64
SEL_TOPN = 16
N_LOCAL_FORCED = 2
WINDOW = 512
Q_BLOCK = 128
LORA_W = 64
LORA_A = 64
LORA_G = 160
RMS_EPS = 1e-6
GN_EPS = 64e-5
NEG = -1e30
FORCE_BONUS = 1e4
MASK_BIAS = -1e30
M_FLOOR = -1e20
NSA_COLS = DH + 6 * DKV + 3 * NH
RWKV_COLS = 3 * DH + LORA_W + LORA_A + LORA_G
NSA_PAD = 2688
RWKV_PAD = 3456
LORA_G_PAD = 256
RW_CHUNK = 64
SEL_KEY_CHUNK = 512
N_SEL_LANES = 64

VMEM_LIMIT_BYTES = 56 * 1024 * 1024


def _cparams(n_axes):
    return pltpu.CompilerParams(dimension_semantics=("arbitrary",) * n_axes,
                                vmem_limit_bytes=VMEM_LIMIT_BYTES)


def _dot(a, b):
    return jnp.dot(a, b, preferred_element_type=F32)


def _dot_nt(a, b):
    return lax.dot_general(a, b, (((1,), (1,)), ((), ())), preferred_element_type=F32)


def _dot_tn(a, b):
    return lax.dot_general(a, b, (((0,), (0,)), ((), ())), preferred_element_type=F32)


def _dot_split3(a, b_bf16):
    hi = a.astype(BF16)
    r1 = a - hi.astype(F32)
    mid = r1.astype(BF16)
    lo = (r1 - mid.astype(F32)).astype(BF16)
    return _dot(hi, b_bf16) + _dot(mid, b_bf16) + _dot(lo, b_bf16)


def _rms(x, g):
    ms = jnp.mean(x * x, axis=-1, keepdims=True)
    return x * lax.rsqrt(ms + RMS_EPS) * g


def _sigmoid(x):
    return jax.nn.sigmoid(x)


def _ffn_body(has_next, x_ref, gpre_ref, wg_ref, wu_ref, wd_ref, gpost_ref, *rest):
    if has_next:
        gnext_ref, o_ref, on_ref, xn_ref, acc_ref = rest
    else:
        o_ref, xn_ref, acc_ref = rest
    j = pl.program_id(1)

    @pl.when(j == 0)
    def _():
        xn_ref[...] = _rms(x_ref[...], gpre_ref[...]).astype(BF16)
        acc_ref[...] = jnp.zeros_like(acc_ref)

    xn = xn_ref[...]
    g = _dot(xn, wg_ref[...])
    u = _dot(xn, wu_ref[...])
    act = ((g * _sigmoid(g)) * u).astype(BF16)
    acc_ref[...] += _dot(act, wd_ref[...])

    @pl.when(j == pl.num_programs(1) - 1)
    def _():
        y = x_ref[...] + 0.5 * _rms(acc_ref[...], gpost_ref[...])
        o_ref[...] = y
        if has_next:
            on_ref[...] = _rms(y, gnext_ref[...]).astype(BF16)


def _ffn_half(x, g_pre, w_gu, w_dn, g_post, g_next, tm, tf):
    n, d = x.shape
    f = w_dn.shape[0]
    nj = f // tf
    has_next = g_next is not None
    row = lambda i, j: (i, 0)
    const = lambda i, j: (0, 0)
    in_specs = [
        pl.BlockSpec((tm, d), row),
        pl.BlockSpec((1, d), const),
        pl.BlockSpec((d, tf), lambda i, j: (0, j)),
        pl.BlockSpec((d, tf), lambda i, j: (0, j + nj)),
        pl.BlockSpec((tf, d), lambda i, j: (j, 0)),
        pl.BlockSpec((1, d), const),
    ]
    args = [x, g_pre, w_gu, w_gu, w_dn, g_post]
    out_shape = [jax.ShapeDtypeStruct((n, d), F32)]
    out_specs = [pl.BlockSpec((tm, d), row)]
    if has_next:
        in_specs.append(pl.BlockSpec((1, d), const))
        args.append(g_next)
        out_shape.append(jax.ShapeDtypeStruct((n, d), BF16))
        out_specs.append(pl.BlockSpec((tm, d), row))
    res = pl.pallas_call(
        functools.partial(_ffn_body, has_next),
        grid=(n // tm, nj),
        in_specs=in_specs,
        out_specs=out_specs,
        out_shape=out_shape,
        scratch_shapes=[pltpu.VMEM((tm, d), BF16), pltpu.VMEM((tm, d), F32)],
        compiler_params=_cparams(2),
        name="ffn_half",
    )(*args)
    return res if has_next else res[0]


def _rope(x, c, s_lo, s_hi):
    w = x.shape[1]
    reps = w // 128
    tile = lambda t: t if reps == 1 else jnp.concatenate([t] * reps, axis=1)
    up = pltpu.roll(x, w - ROPE_HALF, 1)
    dn = pltpu.roll(x, ROPE_HALF, 1)
    return x * tile(c) + up * tile(s_lo) + dn * tile(s_hi)


def _inproj_nsa_body(hn_ref, w_ref, c_ref, slo_ref, shi_ref,
                     qraw_ref, qrot_ref, kv_ref, kwvw_ref, ks_ref, vs_ref, kw_ref, vw_ref, gates_ref):
    p = _dot(hn_ref[...], w_ref[...])
    c, s_lo, s_hi = c_ref[...], slo_ref[...], shi_ref[...]
    q = p[:, 0:DH] * HD ** -0.5
    qraw_ref[...] = q.astype(BF16)
    qrot_ref[...] = _rope(q, c, s_lo, s_hi).astype(BF16)
    o = DH
    ks = _rope(p[:, o + 2 * DKV:o + 3 * DKV], c, s_lo, s_hi)
    vs = p[:, o + 3 * DKV:o + 4 * DKV]
    kw = _rope(p[:, o + 4 * DKV:o + 5 * DKV], c, s_lo, s_hi)
    vw = p[:, o + 5 * DKV:o + 6 * DKV]
    kv_ref[:, 0:2 * DKV] = p[:, o:o + 2 * DKV]
    kv_ref[:, 2 * DKV:3 * DKV] = ks
    kv_ref[:, 3 * DKV:4 * DKV] = vs
    kwvw_ref[:, 0:DKV] = kw
    kwvw_ref[:, DKV:2 * DKV] = vw
    for k in range(KVH):
        sl = slice(HD * k, HD * (k + 1))
        ks_ref[k] = ks[:, sl].astype(BF16)
        vs_ref[k] = vs[:, sl].astype(BF16)
        kw_ref[k] = kw[:, sl].astype(BF16)
        vw_ref[k] = vw[:, sl].astype(BF16)
    gates_ref[...] = _sigmoid(p[:, o + 6 * DKV:o + 6 * DKV + 128])


def _inproj_nsa(hn, w_nsa, rope_c, rope_slo, rope_shi, tm):
    n, d = hn.shape
    row = lambda i: (i, 0)
    hm = lambda i: (0, i, 0)
    hm_shape = jax.ShapeDtypeStruct((KVH, n, HD), BF16)
    hm_spec = pl.BlockSpec((KVH, tm, HD), hm)
    return pl.pallas_call(
        _inproj_nsa_body,
        grid=(n // tm,),
        in_specs=[pl.BlockSpec((tm, d), row),
                  pl.BlockSpec((d, NSA_PAD), lambda i: (0, 0)),
                  pl.BlockSpec((tm, 128), row), pl.BlockSpec((tm, 128), row), pl.BlockSpec((tm, 128), row)],
        out_specs=[pl.BlockSpec((tm, DH), row), pl.BlockSpec((tm, DH), row),
                   pl.BlockSpec((tm, 4 * DKV), row), pl.BlockSpec((tm, 2 * DKV), row),
                   hm_spec, hm_spec, hm_spec, hm_spec,
                   pl.BlockSpec((tm, 128), row)],
        out_shape=[jax.ShapeDtypeStruct((n, DH), BF16), jax.ShapeDtypeStruct((n, DH), BF16),
                   jax.ShapeDtypeStruct((n, 4 * DKV), F32), jax.ShapeDtypeStruct((n, 2 * DKV), F32),
                   hm_shape, hm_shape, hm_shape, hm_shape,
                   jax.ShapeDtypeStruct((n, 128), F32)],
        compiler_params=_cparams(1),
        name="inproj_nsa",
    )(hn, w_nsa, rope_c, rope_slo, rope_shi)


def _matmul_body(x_ref, w_ref, o_ref):
    o_ref[...] = _dot(x_ref[...], w_ref[...])


def _inproj_rwkv(hn, w_rw, tm):
    n, d = hn.shape
    c = w_rw.shape[1]
    return pl.pallas_call(
        _matmul_body,
        grid=(n // tm,),
        in_specs=[pl.BlockSpec((tm, d), lambda i: (i, 0)), pl.BlockSpec((d, c), lambda i: (0, 0))],
        out_specs=pl.BlockSpec((tm, c), lambda i: (i, 0)),
        out_shape=jax.ShapeDtypeStruct((n, c), F32),
        compiler_params=_cparams(1),
        name="inproj_rwkv",
    )(hn, w_rw)


def _compress_accumulate(stage_ref, w1c_refs, n_seg):
    accs = [[jnp.zeros((n_seg, 2 * CMP_HIDDEN), F32) for _ in range(KVH)] for _ in range(2)]
    for s in range(CMP_STRIDE):
        for kv in range(2):
            w = w1c_refs[kv][s]
            for pair in range(KVH // 2):
                x = stage_ref[2 * kv + pair, pl.ds(s, n_seg, stride=CMP_STRIDE), :]
                for j in range(2):
                    k = 2 * pair + j
                    accs[kv][k] = accs[kv][k] + _dot(x[:, HD * j:HD * (j + 1)].astype(BF16), w)
    return accs


def _compress_finish(acc, pe0, w2, n_seg):
    a0 = acc[:, :CMP_HIDDEN]
    a1 = pltpu.roll(acc[:, CMP_HIDDEN:], n_seg - 1, 0)
    pre = pe0 + a0 + a1
    return _dot((pre * _sigmoid(pre)).astype(BF16), w2)


def _cmp_prompt_body(kv_ref, w1k_ref, w1v_ref, pek_ref, pev_ref, w1kf_ref, w1vf_ref, w2k_ref, w2v_ref,
                     kc_ref, vc_ref, stage_ref):
    n_seg = kv_ref.shape[0] // CMP_STRIDE
    for grp in range(4):
        stage_ref[grp] = kv_ref[:, 128 * grp:128 * (grp + 1)]
    accs = _compress_accumulate(stage_ref, (w1k_ref, w1v_ref), n_seg)
    for kv, (pe_ref, w1f_ref, w2_ref, out_ref) in enumerate(
            ((pek_ref, w1kf_ref, w2k_ref, kc_ref), (pev_ref, w1vf_ref, w2v_ref, vc_ref))):
        pe0 = _dot(pe_ref[...], w1f_ref[...])
        w2 = w2_ref[...]
        for k in range(KVH):
            out_ref[0, k] = _compress_finish(accs[kv][k], pe0, w2, n_seg).astype(BF16)


def _cmp_weights(pe, w1, w2):
    w1c = jnp.concatenate([w1[:CMP_STRIDE], w1[CMP_STRIDE:]], axis=-1).astype(BF16)
    return w1c, pe.reshape(1, CMP_BLOCK * HD), w1.reshape(CMP_BLOCK * HD, CMP_HIDDEN), w2.astype(BF16)


def _compress_prompt(kv_rows, cmp_k, cmp_v, n_batch, t):
    n_seg = t // CMP_STRIDE
    w1k, pek, w1kf, w2k = cmp_k
    w1v, pev, w1vf, w2v = cmp_v
    full = lambda a: pl.BlockSpec(a.shape, lambda b: (0,) * a.ndim)
    out_shape = jax.ShapeDtypeStruct((n_batch, KVH, n_seg, HD), BF16)
    out_spec = pl.BlockSpec((1, KVH, n_seg, HD), lambda b: (b, 0, 0, 0))
    return pl.pallas_call(
        _cmp_prompt_body,
        grid=(n_batch,),
        in_specs=[pl.BlockSpec((t, 2 * DKV), lambda b: (b, 0)),
                  full(w1k), full(w1v), full(pek), full(pev), full(w1kf), full(w1vf), full(w2k), full(w2v)],
        out_specs=[out_spec, out_spec],
        out_shape=[out_shape, out_shape],
        scratch_shapes=[pltpu.VMEM((4, t, 128), F32)],
        compiler_params=_cparams(1),
        name="nsa_compress_prompt",
    )(kv_rows, w1k, w1v, pek, pev, w1kf, w1vf, w2k, w2v)


def _select_blocks(imp, cur, n_blocks):
    jb = lax.broadcasted_iota(jnp.int32, (1, N_SEL_LANES), 1)
    valid = (jb <= cur) & (jb < n_blocks)
    rel = cur - jb
    forced = (jb == 0) | ((rel >= 0) & (rel < N_LOCAL_FORCED))
    score = jnp.where(valid, imp + jnp.where(forced, FORCE_BONUS, 0.0), NEG)
    rank = jnp.zeros(score.shape, F32)
    for jp in range(min(n_blocks, N_SEL_LANES)):
        col = score[:, jp:jp + 1]
        tie = jnp.where(jb > jp, 1.0, 0.0)
        rank = rank + jnp.where(col > score, 1.0, jnp.where(col == score, tie, 0.0))
    return jnp.where(valid, jnp.where(rank < SEL_TOPN, 1.0, 0.0), 0.0)


def _softmax_rows(s, ok):
    sm = jnp.where(ok, s, NEG)
    m = jnp.max(sm, axis=-1, keepdims=True)
    e = jnp.where(ok, jnp.exp(sm - m), 0.0)
    l = jnp.sum(e, axis=-1, keepdims=True)
    return e / jnp.maximum(l, 1e-30)


def _softmax_bias(s, bias):
    sb = s + bias
    m = jnp.maximum(jnp.max(sb, axis=-1, keepdims=True), M_FLOOR)
    e = jnp.exp(sb - m)
    return e, 1.0 / jnp.maximum(jnp.sum(e, axis=-1, keepdims=True), 1e-30)


def _select_blocks_t(imp_t, cur, n_blocks, n_live, score_ref):
    jb = lax.broadcasted_iota(jnp.int32, (N_SEL_LANES, 1), 0)
    valid = (jb <= cur) & (jb < n_blocks)
    rel = cur - jb
    forced = (jb == 0) | ((rel >= 0) & (rel < N_LOCAL_FORCED))
    score = jnp.where(valid, imp_t + jnp.where(forced, FORCE_BONUS, 0.0), NEG)
    score_ref[...] = score

    def body(jp, rank):
        other = score_ref[pl.ds(jp, 1), :]
        tie = jnp.where(jb > jp, 1.0, 0.0)
        return rank + jnp.where(other > score, 1.0, jnp.where(other == score, tie, 0.0))

    rank = lax.fori_loop(0, n_live, body, jnp.zeros(score.shape, F32))
    return jnp.where(valid, jnp.where(rank < SEL_TOPN, 1.0, 0.0), 0.0)


def _attn_prompt_body(qraw_ref, qrot_ref, gates_ref, kc_ref, vc_ref, ks_ref, vs_ref, kw_ref, vw_ref,
                      mcst_ref, e_ref, o_ref, score_ref):
    t = ks_ref.shape[1]
    n_seg = kc_ref.shape[2]
    i = pl.program_id(1)
    s0 = i * Q_BLOCK
    qpos = s0 + lax.broadcasted_iota(jnp.int32, (Q_BLOCK, 1), 0)
    qpos4 = jnp.concatenate([qpos] * GQA, axis=0)
    qpos_row = s0 + lax.broadcasted_iota(jnp.int32, (1, Q_BLOCK), 1)
    n_idx = lax.broadcasted_iota(jnp.int32, (1, n_seg), 1)
    ok_cmp = (n_idx * CMP_STRIDE + (CMP_BLOCK - 1) <= qpos4) & (n_idx < n_seg - 1)
    bias_cmp = jnp.where(ok_cmp, 0.0, MASK_BIAS)
    gates = gates_ref[...]
    mcst = mcst_ref[...]
    n_key_chunks = i // (SEL_KEY_CHUNK // Q_BLOCK) + 1
    n_live_blocks = (s0 + Q_BLOCK) // SEL_BLOCK
    win_start = pl.multiple_of(jnp.maximum(s0 - WINDOW, 0), Q_BLOCK)
    win_len = WINDOW + Q_BLOCK
    wpos = win_start + lax.broadcasted_iota(jnp.int32, (1, win_len), 1)
    dlt = qpos4 - wpos
    bias_win = jnp.where((dlt >= 0) & (dlt <= WINDOW), 0.0, MASK_BIAS)

    for k in range(KVH):
        qr = jnp.concatenate([qraw_ref[:, HD * (GQA * k + g):HD * (GQA * k + g + 1)] for g in range(GQA)], axis=0)
        qs = jnp.concatenate([qrot_ref[:, HD * (GQA * k + g):HD * (GQA * k + g + 1)] for g in range(GQA)], axis=0)
        e_c, inv_c = _softmax_bias(_dot_nt(qr, kc_ref[0, k]), bias_cmp)
        p_cmp = e_c * inv_c
        o_cmp = _dot(p_cmp.astype(BF16), vc_ref[0, k])
        p_sum = p_cmp[0:Q_BLOCK]
        for g in range(1, GQA):
            p_sum = p_sum + p_cmp[g * Q_BLOCK:(g + 1) * Q_BLOCK]
        hi = p_sum.astype(BF16)
        r1 = p_sum - hi.astype(F32)
        mid = r1.astype(BF16)
        lo = (r1 - mid.astype(F32)).astype(BF16)
        imp_t = _dot_nt(mcst, hi) + _dot_nt(mcst, mid) + _dot_nt(mcst, lo)
        sel_t = _select_blocks_t(imp_t, qpos_row // SEL_BLOCK, t // SEL_BLOCK, n_live_blocks,
                                 score_ref).astype(BF16)

        def chunk(c, carry, qs=qs, sel_t=sel_t, k=k):
            m, l, acc = carry
            k0 = pl.multiple_of(c * SEL_KEY_CHUNK, SEL_KEY_CHUNK)
            s = _dot_nt(qs, ks_ref[k, pl.ds(k0, SEL_KEY_CHUNK), :])
            selk = _dot_tn(sel_t, e_ref[:, pl.ds(k0, SEL_KEY_CHUNK)])
            kpos = k0 + lax.broadcasted_iota(jnp.int32, (1, SEL_KEY_CHUNK), 1)
            bias = (jnp.where(kpos <= qpos, selk, 0.0) - 1.0) * (-MASK_BIAS)
            sb = s + jnp.concatenate([bias] * GQA, axis=0)
            m_new = jnp.maximum(m, jnp.max(sb, axis=-1, keepdims=True))
            p = jnp.exp(sb - m_new)
            alpha = jnp.exp(m - m_new)
            l = alpha * l + jnp.sum(p, axis=-1, keepdims=True)
            acc = alpha * acc + _dot(p.astype(BF16), vs_ref[k, pl.ds(k0, SEL_KEY_CHUNK), :])
            return m_new, l, acc

        init = (jnp.full((GQA * Q_BLOCK, 1), M_FLOOR, F32), jnp.zeros((GQA * Q_BLOCK, 1), F32),
                jnp.zeros((GQA * Q_BLOCK, HD), F32))
        _, l_sel, acc_sel = lax.fori_loop(0, n_key_chunks, chunk, init)
        o_sel = acc_sel * (1.0 / jnp.maximum(l_sel, 1e-30))

        e_w, inv_w = _softmax_bias(_dot_nt(qs, kw_ref[k, pl.ds(win_start, win_len), :]), bias_win)
        o_win = _dot(e_w.astype(BF16), vw_ref[k, pl.ds(win_start, win_len), :]) * inv_w

        for g in range(GQA):
            h = GQA * k + g
            rows = slice(g * Q_BLOCK, (g + 1) * Q_BLOCK)
            o_h = (gates[:, 3 * h:3 * h + 1] * o_cmp[rows] + gates[:, 3 * h + 1:3 * h + 2] * o_sel[rows]
                   + gates[:, 3 * h + 2:3 * h + 3] * o_win[rows])
            o_ref[:, HD * h:HD * (h + 1)] = o_h.astype(o_ref.dtype)


def _attn_prompt(q_raw, q_rot, gates, k_cmp, v_cmp, ks, vs, kw, vw, mcs_t, e_mat, n_batch, t):
    nqb = t // Q_BLOCK
    n_seg = t // CMP_STRIDE
    qrow = lambda b, i: (b * nqb + i, 0)
    cmp_spec = pl.BlockSpec((1, KVH, n_seg, HD), lambda b, i: (b, 0, 0, 0))
    kv_spec = pl.BlockSpec((KVH, t, HD), lambda b, i: (0, b, 0))
    return pl.pallas_call(
        _attn_prompt_body,
        grid=(n_batch, nqb),
        in_specs=[pl.BlockSpec((Q_BLOCK, DH), qrow), pl.BlockSpec((Q_BLOCK, DH), qrow),
                  pl.BlockSpec((Q_BLOCK, 128), qrow),
                  cmp_spec, cmp_spec, kv_spec, kv_spec, kv_spec, kv_spec,
                  pl.BlockSpec(mcs_t.shape, lambda b, i: (0, 0)),
                  pl.BlockSpec(e_mat.shape, lambda b, i: (0, 0))],
        out_specs=pl.BlockSpec((Q_BLOCK, DH), qrow),
        out_shape=jax.ShapeDtypeStruct((n_batch * t, DH), F32),
        scratch_shapes=[pltpu.VMEM((N_SEL_LANES, Q_BLOCK), F32)],
        compiler_params=_cparams(2),
        name="nsa_attn_prompt",
    )(q_raw, q_rot, gates, k_cmp, v_cmp, ks, vs, kw, vw, mcs_t, e_mat)


def _pick_kv_group(full):
    hk = lax.broadcasted_iota(jnp.int32, (NH, 1), 0) // GQA
    out = jnp.zeros((NH, HD), F32)
    for k in range(KVH):
        out = out + jnp.where(hk == k, full[:, HD * k:HD * (k + 1)], 0.0)
    return out


def _nsa_sample_body(n_pages, pt_ref, *refs):
    pages = refs[:n_pages]
    (win_ref, qraw_ref, qrot_ref, gates_ref, kvnew_ref, kwnew_ref,
     w1k_ref, w1v_ref, pek_ref, pev_ref, w1kf_ref, w1vf_ref, w2k_ref, w2v_ref, mcs_ref, e_ref,
     o_ref, winout_ref, stage_ref) = refs[n_pages:]
    page = pages[0].shape[1]
    past = n_pages * page
    n_seg = past // CMP_STRIDE

    def bdiag(q):
        qt = jnp.concatenate([q] * KVH, axis=1)
        hk = lax.broadcasted_iota(jnp.int32, (NH, DKV), 0) // GQA
        lk = lax.broadcasted_iota(jnp.int32, (NH, DKV), 1) // HD
        return jnp.where(hk == lk, qt, 0.0)

    qr = bdiag(qraw_ref[0])
    qs = bdiag(qrot_ref[0])
    qr_b, qs_b = qr.astype(BF16), qs.astype(BF16)

    for kk, pg in enumerate(pages):
        for grp in range(4):
            stage_ref[grp, page * kk:page * (kk + 1), :] = pg[0, :, 128 * grp:128 * (grp + 1)]
    accs = _compress_accumulate(stage_ref, (w1k_ref, w1v_ref), n_seg)
    cmp_kv = []
    for kv, (pe_ref, w1f_ref, w2_ref) in enumerate(((pek_ref, w1kf_ref, w2k_ref), (pev_ref, w1vf_ref, w2v_ref))):
        pe0 = _dot(pe_ref[...], w1f_ref[...])
        w2 = w2_ref[...]
        cmp_kv.append(jnp.concatenate(
            [_compress_finish(accs[kv][k], pe0, w2, n_seg) for k in range(KVH)], axis=1).astype(BF16))
    k_cmp, v_cmp = cmp_kv

    n_idx = lax.broadcasted_iota(jnp.int32, (1, n_seg), 1)
    ok_cmp = (n_idx * CMP_STRIDE + (CMP_BLOCK - 1) <= past) & (n_idx < n_seg - 1)
    p_cmp = _softmax_rows(_dot_nt(qr_b, k_cmp), ok_cmp)
    o_cmp = _pick_kv_group(_dot(p_cmp.astype(BF16), v_cmp))
    p_sum = jnp.concatenate(
        [jnp.sum(p_cmp[GQA * k:GQA * (k + 1)], axis=0, keepdims=True) for k in range(KVH)], axis=0)
    imp = _dot_split3(p_sum, mcs_ref[...])
    n_blocks = past // SEL_BLOCK + 1
    cur = jnp.full((KVH, 1), past // SEL_BLOCK, jnp.int32)
    sel = _select_blocks(imp, cur, n_blocks)
    sel16 = jnp.concatenate([jnp.broadcast_to(sel[k:k + 1], (GQA, N_SEL_LANES)) for k in range(KVH)], axis=0)

    kv_new = kvnew_ref[0]
    ks_new, vs_new = kv_new[:, 2 * DKV:3 * DKV], kv_new[:, 3 * DKV:4 * DKV]
    k_sel = jnp.concatenate([pg[0, :, 2 * DKV:3 * DKV] for pg in pages], axis=0).astype(BF16)
    v_sel = jnp.concatenate([pg[0, :, 3 * DKV:4 * DKV] for pg in pages], axis=0).astype(BF16)
    s_c = _dot_nt(qs_b, k_sel)
    s_n = jnp.sum(qs * ks_new, axis=-1, keepdims=True)
    ok_c = _dot(sel16.astype(BF16), e_ref[...]) > 0.5
    new_blk = past // SEL_BLOCK
    ok_n = sel16[:, new_blk:new_blk + 1] > 0.5
    sm_c = jnp.where(ok_c, s_c, NEG)
    sm_n = jnp.where(ok_n, s_n, NEG)
    m = jnp.maximum(jnp.max(sm_c, axis=-1, keepdims=True), sm_n)
    e_c = jnp.where(ok_c, jnp.exp(sm_c - m), 0.0)
    e_n = jnp.where(ok_n, jnp.exp(sm_n - m), 0.0)
    inv = 1.0 / jnp.maximum(jnp.sum(e_c, axis=-1, keepdims=True) + e_n, 1e-30)
    o_sel = _pick_kv_group(_dot((e_c * inv).astype(BF16), v_sel)
                           + (e_n * inv).astype(BF16).astype(F32) * vs_new.astype(BF16).astype(F32))

    kw_new = kwnew_ref[0]
    kwn, vwn = kw_new[:, 0:DKV], kw_new[:, DKV:2 * DKV]
    s_w = _dot_nt(qs_b, win_ref[0, :, 0:DKV].astype(BF16))
    s_wn = jnp.sum(qs * kwn, axis=-1, keepdims=True)
    mw = jnp.maximum(jnp.max(s_w, axis=-1, keepdims=True), s_wn)
    e_w = jnp.exp(s_w - mw)
    e_wn = jnp.exp(s_wn - mw)
    inv_w = 1.0 / (jnp.sum(e_w, axis=-1, keepdims=True) + e_wn)
    o_win = _pick_kv_group(_dot((e_w * inv_w).astype(BF16), win_ref[0, :, DKV:2 * DKV].astype(BF16))
                           + (e_wn * inv_w).astype(BF16).astype(F32) * vwn.astype(BF16).astype(F32))

    g = gates_ref[0]
    o_ref[0] = g[:, 0:1] * o_cmp + g[:, 1:2] * o_sel + g[:, 2:3] * o_win
    wb = win_ref.shape[1]
    winout_ref[0, 0:wb - 1, :] = win_ref[0, 1:wb, :]
    winout_ref[0, wb - 1:wb, :] = kw_new


def _nsa_sample(cache3, page_table, state_win, q_raw, q_rot, gates, kv_new, kw_new, cmp_k, cmp_v, mcs, e_mat):
    nb, n_pages = page_table.shape
    page = cache3.shape[1]
    wb = state_win.shape[1]
    w1k, pek, w1kf, w2k = cmp_k
    w1v, pev, w1vf, w2v = cmp_v
    consts = [w1k, w1v, pek, pev, w1kf, w1vf, w2k, w2v, mcs, e_mat]
    full = lambda a: pl.BlockSpec(a.shape, lambda b, pt: (0,) * a.ndim)
    page_specs = [pl.BlockSpec((1, page, cache3.shape[2]), functools.partial(
        lambda b, pt, kk: (pt[b * n_pages + kk], 0, 0), kk=kk)) for kk in range(n_pages)]
    per_b = lambda shape: pl.BlockSpec((1,) + shape, lambda b, pt: (b, 0, 0))
    grid_spec = pltpu.PrefetchScalarGridSpec(
        num_scalar_prefetch=1,
        grid=(nb,),
        in_specs=page_specs + [per_b((wb, 2 * DKV)), per_b((NH, HD)), per_b((NH, HD)), per_b((NH, 3)),
                               per_b((1, 4 * DKV)), per_b((1, 2 * DKV))] + [full(a) for a in consts],
        out_specs=[per_b((NH, HD)), per_b((wb, 2 * DKV))],
        scratch_shapes=[pltpu.VMEM((4, n_pages * page, 128), F32)],
    )
    return pl.pallas_call(
        functools.partial(_nsa_sample_body, n_pages),
        grid_spec=grid_spec,
        out_shape=[jax.ShapeDtypeStruct((nb, NH, HD), F32), jax.ShapeDtypeStruct((nb, wb, 2 * DKV), F32)],
        compiler_params=_cparams(1),
        name="nsa_sample",
    )(page_table.reshape(-1), *([cache3] * n_pages), state_win, q_raw, q_rot, gates, kv_new, kw_new, *consts)


def _nsa_sample_t_body(n_pages, pt_ref, *refs):
    pages = refs[:n_pages]
    (win_ref, qraw_ref, qrot_ref, gates_ref, kvnew_ref, kwnew_ref, kwvwt_ref, perm_ref,
     w1k_ref, w1v_ref, pek_ref, pev_ref, w1kf_ref, w1vf_ref, w2k_ref, w2v_ref, mcs_ref, e_ref,
     o_ref, winout_ref, stage_ref) = refs[n_pages:]
    page = pages[0].shape[3]
    past = n_pages * page
    n_seg = past // CMP_STRIDE
    seg_pp = page // CMP_STRIDE
    b_idx = pl.program_id(0)

    def bdiag(q):
        qt = jnp.concatenate([q] * KVH, axis=1)
        hk = lax.broadcasted_iota(jnp.int32, (NH, DKV), 0) // GQA
        lk = lax.broadcasted_iota(jnp.int32, (NH, DKV), 1) // HD
        return jnp.where(hk == lk, qt, 0.0)

    qr = bdiag(qraw_ref[0])
    qs = bdiag(qrot_ref[0])
    qr_b, qs_b = qr.astype(BF16), qs.astype(BF16)

    perm = perm_ref[...]
    for kk, pg in enumerate(pages):
        for c in range(2):
            xs = _dot_nt(perm, pg[0, c].astype(BF16))
            for s in range(CMP_STRIDE):
                stage_ref[c, s, seg_pp * kk:seg_pp * (kk + 1), :] = xs[seg_pp * s:seg_pp * (s + 1), :]
    accs = [[jnp.zeros((n_seg, 2 * CMP_HIDDEN), F32) for _ in range(KVH)] for _ in range(2)]
    for s in range(CMP_STRIDE):
        for kv, w1c_ref in enumerate((w1k_ref, w1v_ref)):
            w = w1c_ref[s]
            x = stage_ref[kv, s].astype(BF16)
            for k in range(KVH):
                accs[kv][k] = accs[kv][k] + _dot(x[:, HD * k:HD * (k + 1)], w)
    cmp_kv = []
    for kv, (pe_ref, w1f_ref, w2_ref) in enumerate(((pek_ref, w1kf_ref, w2k_ref), (pev_ref, w1vf_ref, w2v_ref))):
        pe0 = _dot(pe_ref[...], w1f_ref[...])
        w2 = w2_ref[...]
        cmp_kv.append(jnp.concatenate(
            [_compress_finish(accs[kv][k], pe0, w2, n_seg) for k in range(KVH)], axis=1).astype(BF16))
    k_cmp, v_cmp = cmp_kv

    n_idx = lax.broadcasted_iota(jnp.int32, (1, n_seg), 1)
    ok_cmp = (n_idx * CMP_STRIDE + (CMP_BLOCK - 1) <= past) & (n_idx < n_seg - 1)
    p_cmp = _softmax_rows(_dot_nt(qr_b, k_cmp), ok_cmp)
    o_cmp = _pick_kv_group(_dot(p_cmp.astype(BF16), v_cmp))
    p_sum = jnp.concatenate(
        [jnp.sum(p_cmp[GQA * k:GQA * (k + 1)], axis=0, keepdims=True) for k in range(KVH)], axis=0)
    imp = _dot_split3(p_sum, mcs_ref[...])
    n_blocks = past // SEL_BLOCK + 1
    cur = jnp.full((KVH, 1), past // SEL_BLOCK, jnp.int32)
    sel = _select_blocks(imp, cur, n_blocks)
    sel16 = jnp.concatenate([jnp.broadcast_to(sel[k:k + 1], (GQA, N_SEL_LANES)) for k in range(KVH)], axis=0)

    kv_new = kvnew_ref[0]
    ks_new, vs_new = kv_new[:, 2 * DKV:3 * DKV], kv_new[:, 3 * DKV:4 * DKV]
    s_c = jnp.concatenate([_dot(qs_b, pg[0, 2].astype(BF16)) for pg in pages], axis=1)
    s_n = jnp.sum(qs * ks_new, axis=-1, keepdims=True)
    ok_c = _dot(sel16.astype(BF16), e_ref[...]) > 0.5
    new_blk = past // SEL_BLOCK
    ok_n = sel16[:, new_blk:new_blk + 1] > 0.5
    sm_c = jnp.where(ok_c, s_c, NEG)
    sm_n = jnp.where(ok_n, s_n, NEG)
    m = jnp.maximum(jnp.max(sm_c, axis=-1, keepdims=True), sm_n)
    e_c = jnp.where(ok_c, jnp.exp(sm_c - m), 0.0)
    e_n = jnp.where(ok_n, jnp.exp(sm_n - m), 0.0)
    inv = 1.0 / jnp.maximum(jnp.sum(e_c, axis=-1, keepdims=True) + e_n, 1e-30)
    p_c = (e_c * inv).astype(BF16)
    o_sel_full = (e_n * inv).astype(BF16).astype(F32) * vs_new.astype(BF16).astype(F32)
    for kk, pg in enumerate(pages):
        o_sel_full = o_sel_full + _dot_nt(p_c[:, page * kk:page * (kk + 1)], pg[0, 3].astype(BF16))
    o_sel = _pick_kv_group(o_sel_full)

    kw_new = kwnew_ref[0]
    kwn, vwn = kw_new[:, 0:DKV], kw_new[:, DKV:2 * DKV]
    s_w = _dot(qs_b, win_ref[0, 0].astype(BF16))
    s_wn = jnp.sum(qs * kwn, axis=-1, keepdims=True)
    mw = jnp.maximum(jnp.max(s_w, axis=-1, keepdims=True), s_wn)
    e_w = jnp.exp(s_w - mw)
    e_wn = jnp.exp(s_wn - mw)
    inv_w = 1.0 / (jnp.sum(e_w, axis=-1, keepdims=True) + e_wn)
    o_win = _pick_kv_group(_dot_nt((e_w * inv_w).astype(BF16), win_ref[0, 1].astype(BF16))
                           + (e_wn * inv_w).astype(BF16).astype(F32) * vwn.astype(BF16).astype(F32))

    g = gates_ref[0]
    o_ref[0] = g[:, 0:1] * o_cmp + g[:, 1:2] * o_sel + g[:, 2:3] * o_win

    wb = win_ref.shape[3]
    nb = kwvwt_ref.shape[1]
    mine = lax.broadcasted_iota(jnp.int32, (1, nb), 1) == b_idx
    last = lax.broadcasted_iota(jnp.int32, (1, wb), 1) == wb - 1
    for c in range(2):
        new_col = jnp.sum(jnp.where(mine, kwvwt_ref[DKV * c:DKV * (c + 1), :], 0.0), axis=-1, keepdims=True)
        winout_ref[0, c] = jnp.where(last, new_col, pltpu.roll(win_ref[0, c], wb - 1, 1))


def _nsa_sample_t(cache4, page_table, win4, q_raw, q_rot, gates, kv_new, kw_new, kwvw_t, cmp_k, cmp_v, mcs, e_mat):
    nb, n_pages = page_table.shape
    page = cache4.shape[3]
    wb = win4.shape[3]
    seg_pp = page // CMP_STRIDE
    rows = np.arange(page)
    perm = np.zeros((page, page), np.float32)
    perm[rows, (rows % seg_pp) * CMP_STRIDE + rows // seg_pp] = 1.0
    w1k, pek, w1kf, w2k = cmp_k
    w1v, pev, w1vf, w2v = cmp_v
    consts = [kwvw_t, jnp.asarray(perm, BF16), w1k, w1v, pek, pev, w1kf, w1vf, w2k, w2v, mcs, e_mat]
    full = lambda a: pl.BlockSpec(a.shape, lambda b, pt: (0,) * a.ndim)
    page_specs = [pl.BlockSpec((1, 4, DKV, page), functools.partial(
        lambda b, pt, kk: (pt[b * n_pages + kk], 0, 0, 0), kk=kk)) for kk in range(n_pages)]
    per_b = lambda shape: pl.BlockSpec((1,) + shape, lambda b, pt: (b,) + (0,) * len(shape))
    grid_spec = pltpu.PrefetchScalarGridSpec(
        num_scalar_prefetch=1,
        grid=(nb,),
        in_specs=page_specs + [per_b((2, DKV, wb)), per_b((NH, HD)), per_b((NH, HD)), per_b((NH, 3)),
                               per_b((1, 4 * DKV)), per_b((1, 2 * DKV))] + [full(a) for a in consts],
        out_specs=[per_b((NH, HD)), per_b((2, DKV, wb))],
        scratch_shapes=[pltpu.VMEM((2, CMP_STRIDE, n_pages * seg_pp, DKV), F32)],
    )
    return pl.pallas_call(
        functools.partial(_nsa_sample_t_body, n_pages),
        grid_spec=grid_spec,
        out_shape=[jax.ShapeDtypeStruct((nb, NH, HD), F32), jax.ShapeDtypeStruct((nb, 2, DKV, wb), F32)],
        compiler_params=_cparams(1),
        name="nsa_sample",
    )(page_table.reshape(-1), *([cache4] * n_pages), win4, q_raw, q_rot, gates, kv_new, kw_new, *consts)


def _rwkv_prep(cols, prev, mu, w0, w2, a0, a2, g2, kk_p, ka):
    xs = cols + (prev - cols) * mu
    r = xs[:, 0:DH]
    k = xs[:, DH:2 * DH]
    v = xs[:, 2 * DH:3 * DH]
    o = 3 * DH
    wd = xs[:, o:o + LORA_W]
    ad = xs[:, o + LORA_W:o + LORA_W + LORA_A]
    gd = xs[:, o + LORA_W + LORA_A:o + LORA_W + LORA_A + LORA_G_PAD]
    w = w0 + _dot(jnp.tanh(wd).astype(BF16), w2)
    w_log = -jax.nn.softplus(-w) - 0.5
    lw = -jnp.exp(w_log)
    a = _sigmoid(a0 + _dot(ad.astype(BF16), a2))
    g = _dot(_sigmoid(gd).astype(BF16), g2)
    kkv = k * kk_p
    k_mod = k * (1.0 + (a - 1.0) * ka)
    return r, k_mod, v, kkv, a, lw, g


def _head_norm(kkv_h):
    return kkv_h / jnp.maximum(jnp.sqrt(jnp.sum(kkv_h * kkv_h, axis=-1, keepdims=True)), 1e-12)


def _rwkv_head_out(y, r_h, k_h, v_h, g_h, rk_h, gnw_h, gnb_h):
    mean = jnp.mean(y, axis=-1, keepdims=True)
    var = jnp.mean(jnp.square(y - mean), axis=-1, keepdims=True)
    yn = (y - mean) * lax.rsqrt(var + GN_EPS) * gnw_h + gnb_h
    bonus = jnp.sum(r_h * k_h * rk_h, axis=-1, keepdims=True) * v_h
    return (yn + bonus) * g_h


def _tri_inverse_all(a_list, blk16, blk32):
    c = a_list[0].shape[0]
    eye = jnp.where(lax.broadcasted_iota(jnp.int32, (c, c), 0) == lax.broadcasted_iota(jnp.int32, (c, c), 1),
                    1.0, 0.0)
    b = lambda x: x.astype(BF16)
    a16 = [b(jnp.where(blk16, a, 0.0)) for a in a_list]
    p = [_dot(x, x) for x in a16]
    t = [_dot(b(eye) - x, b(eye + q)) for x, q in zip(a16, p)]
    for _ in range(2):
        pb = [b(q) for q in p]
        p = [_dot(q, q) for q in pb]
        t = [_dot(b(x), b(eye + q)) for x, q in zip(t, p)]
    off32 = blk32 & jnp.logical_not(blk16)
    for mask in (off32, jnp.logical_not(blk32)):
        tb = [b(x) for x in t]
        m = [_dot(x, b(jnp.where(mask, a, 0.0))) for x, a in zip(tb, a_list)]
        t = [x - _dot(b(y), xb) for x, y, xb in zip(t, m, tb)]
    return t


def _rwkv_prompt_body(cols_ref, mu_ref, w0_ref, w2_ref, a0_ref, a2_ref, g2_ref, kk_ref, ka_ref, rk_ref,
                      gnw_ref, gnb_ref, o_ref, wkv_ref, s_ref, last_ref):
    c = pl.program_id(1)
    ch = RW_CHUNK

    @pl.when(c == 0)
    def _():
        s_ref[...] = jnp.zeros_like(s_ref)
        last_ref[...] = jnp.zeros_like(last_ref)

    cols = cols_ref[...]
    row = lax.broadcasted_iota(jnp.int32, (ch, 1), 0)
    prev = jnp.where(row == 0, last_ref[...], pltpu.roll(cols, 1, 0))
    last_ref[...] = cols[ch - 1:ch, :]
    r, k_mod, v, kkv, a, lw, g = _rwkv_prep(cols, prev, mu_ref[...], w0_ref[...], w2_ref[...], a0_ref[...],
                                            a2_ref[...], g2_ref[...], kk_ref[...], ka_ref[...])
    ti = lax.broadcasted_iota(jnp.int32, (ch, ch), 0)
    si = lax.broadcasted_iota(jnp.int32, (ch, ch), 1)
    ltri = jnp.where(ti >= si, 1.0, 0.0).astype(BF16)
    cl = _cumsum_rows(lw, ltri)
    e_in = jnp.exp(cl)
    e_ex = jnp.exp(cl - lw)
    e_ng = jnp.exp(-cl)
    cl_end = cl[ch - 1:ch, :]
    e_end = jnp.exp(cl_end - cl)
    g_end = jnp.exp(cl_end)
    strict = ti > si
    causal2 = (lax.broadcasted_iota(jnp.int32, (ch, 2 * ch), 0)
               >= lax.broadcasted_iota(jnp.int32, (ch, 2 * ch), 1) % ch)
    blk16 = (ti // 16) == (si // 16)
    blk32 = (ti // 32) == (si // 32)
    rk, gnw, gnb = rk_ref[...], gnw_ref[...], gnb_ref[...]
    b = lambda x: x.astype(BF16)

    heads = range(NH)
    sls = [slice(HD * h, HD * (h + 1)) for h in heads]

    kk_n = [_head_norm(kkv[:, sl]) for sl in sls]
    beta = [kk_n[h] * a[:, sls[h]] for h in heads]
    lhs = [b(jnp.concatenate([kk_n[h] * e_ex[:, sls[h]], r[:, sls[h]] * e_in[:, sls[h]]], axis=0)) for h in heads]
    rhs = [b(jnp.concatenate([beta[h] * e_ng[:, sls[h]], k_mod[:, sls[h]] * e_ng[:, sls[h]]], axis=0))
           for h in heads]
    dec = [b(jnp.concatenate([beta[h] * e_end[:, sls[h]], k_mod[:, sls[h]] * e_end[:, sls[h]]], axis=0))
           for h in heads]
    s0 = [s_ref[h] for h in heads]
    quad = [_dot_nt(lhs[h], rhs[h]) for h in heads]
    s0t = [_dot_nt(lhs[h], b(s0[h])) for h in heads]
    a_b = [jnp.where(strict, q[0:ch, 0:ch], 0.0) for q in quad]
    a_kv = [_dot(b(jnp.where(strict, quad[h][0:ch, ch:2 * ch], 0.0)), b(v[:, sls[h]])) for h in heads]
    b_bk = [b(jnp.where(causal2, q[ch:2 * ch, :], 0.0)) for q in quad]
    t_inv = _tri_inverse_all(a_b, blk16, blk32)
    u = [-_dot(b(t_inv[h]), b(s0t[h][0:ch] + a_kv[h])) for h in heads]
    uv = [b(jnp.concatenate([u[h], v[:, sls[h]]], axis=0)) for h in heads]
    y = [s0t[h][ch:2 * ch] + _dot(b_bk[h], uv[h]) for h in heads]
    s_new = [s0[h] * g_end[:, sls[h]] + _dot_tn(uv[h], dec[h]) for h in heads]
    for h in heads:
        sl = sls[h]
        s_ref[h] = s_new[h]
        o_h = _rwkv_head_out(y[h], r[:, sl], k_mod[:, sl], v[:, sl], g[:, sl], rk[:, sl], gnw[:, sl], gnb[:, sl])
        o_ref[:, sl] = o_h.astype(o_ref.dtype)

    @pl.when(c == pl.num_programs(1) - 1)
    def _():
        wkv_ref[0] = s_ref[...]


def _cumsum_rows(x, ltri):
    hi = x.astype(BF16)
    r1 = x - hi.astype(F32)
    mid = r1.astype(BF16)
    lo = (r1 - mid.astype(F32)).astype(BF16)
    return _dot(ltri, hi) + _dot(ltri, mid) + _dot(ltri, lo)


def _rwkv_prompt(cols, rw, n_batch, t):
    nc = t // RW_CHUNK
    full = lambda a: pl.BlockSpec(a.shape, lambda b, c: (0,) * a.ndim)
    row = lambda b, c: (b * nc + c, 0)
    return pl.pallas_call(
        _rwkv_prompt_body,
        grid=(n_batch, nc),
        in_specs=[pl.BlockSpec((RW_CHUNK, RWKV_PAD), row)] + [full(a) for a in rw],
        out_specs=[pl.BlockSpec((RW_CHUNK, DH), row),
                   pl.BlockSpec((1, NH, HD, HD), lambda b, c: (b, 0, 0, 0))],
        out_shape=[jax.ShapeDtypeStruct((n_batch * t, DH), F32),
                   jax.ShapeDtypeStruct((n_batch, NH, HD, HD), F32)],
        scratch_shapes=[pltpu.VMEM((NH, HD, HD), F32), pltpu.VMEM((1, RWKV_PAD), F32)],
        compiler_params=_cparams(2),
        name="rwkv_prompt",
    )(cols, *rw)


def _rwkv_sample_prep_body(cols_ref, prev_ref, mu_ref, w0_ref, w2_ref, a0_ref, a2_ref, g2_ref, kk_ref, ka_ref,
                           r_ref, k_ref, v_ref, kkn_ref, a_ref, d_ref, g_ref):
    r, k_mod, v, kkv, a, lw, g = _rwkv_prep(cols_ref[...], prev_ref[...], mu_ref[...], w0_ref[...], w2_ref[...],
                                            a0_ref[...], a2_ref[...], g2_ref[...], kk_ref[...], ka_ref[...])
    r_ref[...] = r
    k_ref[...] = k_mod
    v_ref[...] = v
    a_ref[...] = a
    d_ref[...] = jnp.exp(lw)
    g_ref[...] = g
    for h in range(NH):
        sl = slice(HD * h, HD * (h + 1))
        kkn_ref[:, sl] = _head_norm(kkv[:, sl])


def _rwkv_sample_prep(cols, prev, rw_prep):
    n = cols.shape[0]
    args = [cols, prev] + list(rw_prep)
    full = lambda a: pl.BlockSpec(a.shape, lambda i: (0,) * a.ndim)
    out = jax.ShapeDtypeStruct((n, DH), F32)
    return pl.pallas_call(
        _rwkv_sample_prep_body,
        grid=(1,),
        in_specs=[full(a) for a in args],
        out_specs=[pl.BlockSpec((n, DH), lambda i: (0, 0))] * 7,
        out_shape=[out] * 7,
        compiler_params=_cparams(1),
        name="rwkv_sample_prep",
    )(*args)


def _rwkv_sample_step_body(s_ref, r_ref, k_ref, v_ref, kk_ref, a_ref, d_ref, g_ref, rk_ref, gnw_ref, gnb_ref,
                           o_ref, sout_ref):
    s = s_ref[0]
    r, k, v, kk, a, d, g = (x[0] for x in (r_ref, k_ref, v_ref, kk_ref, a_ref, d_ref, g_ref))
    eye = jnp.where(lax.broadcasted_iota(jnp.int32, (HD, HD), 0) == lax.broadcasted_iota(jnp.int32, (HD, HD), 1),
                    1.0, 0.0)
    row = lambda x: x[:, None, :]
    col = lambda x: jnp.sum(eye[None] * x[:, None, :], axis=-1, keepdims=True)
    sa = -jnp.sum(s * row(kk), axis=-1, keepdims=True)
    s_new = s * row(d) + sa * row(kk * a) + col(v) * row(k)
    sout_ref[0] = s_new
    y_col = jnp.sum(s_new * row(r), axis=-1, keepdims=True)
    y = jnp.sum(y_col * eye[None], axis=1)
    o_ref[0] = _rwkv_head_out(y, r, k, v, g, rk_ref[...], gnw_ref[...], gnb_ref[...])


def _rwkv_sample_step(state, vecs, rk, gnw, gnb):
    nb = state.shape[0]
    per_b3 = pl.BlockSpec((1, NH, HD), lambda b: (b, 0, 0))
    per_b4 = pl.BlockSpec((1, NH, HD, HD), lambda b: (b, 0, 0, 0))
    full = pl.BlockSpec((NH, HD), lambda b: (0, 0))
    return pl.pallas_call(
        _rwkv_sample_step_body,
        grid=(nb,),
        in_specs=[per_b4] + [per_b3] * 7 + [full] * 3,
        out_specs=[per_b3, per_b4],
        out_shape=[jax.ShapeDtypeStruct((nb, NH, HD), F32), jax.ShapeDtypeStruct((nb, NH, HD, HD), F32)],
        compiler_params=_cparams(1),
        name="rwkv_sample_step",
    )(state, *vecs, rk, gnw, gnb)


def _rwkv_sample_prep_t_body(cols_ref, kv_ref, kwvw_ref, prev_ref, mu_ref, w0_ref, w2t_ref, a0_ref, a2t_ref,
                             g2t_ref, kk_ref, ka_ref,
                             colst_ref, kvt_ref, kwvwt_ref, r_ref, k_ref, v_ref, kkn_ref, kka_ref, d_ref, g_ref,
                             tr_ref):
    nb = cols_ref.shape[0]
    for j in range(cols_ref.shape[1] // 128):
        tr_ref[128 * j:128 * (j + 1), :] = cols_ref[:, 128 * j:128 * (j + 1)].T
    for j in range(kv_ref.shape[1] // 128):
        kvt_ref[128 * j:128 * (j + 1), :] = kv_ref[:, 128 * j:128 * (j + 1)].T
    for j in range(kwvw_ref.shape[1] // 128):
        kwvwt_ref[128 * j:128 * (j + 1), :] = kwvw_ref[:, 128 * j:128 * (j + 1)].T
    cols = tr_ref[0:RWKV_COLS, :]
    colst_ref[...] = cols
    xs = cols + (prev_ref[...] - cols) * mu_ref[...]
    r = xs[0:DH]
    k = xs[DH:2 * DH]
    v = xs[2 * DH:3 * DH]
    o = 3 * DH
    wd = xs[o:o + LORA_W]
    ad = xs[o + LORA_W:o + LORA_W + LORA_A]
    gd = xs[o + LORA_W + LORA_A:RWKV_COLS]
    w = w0_ref[...] + _dot(w2t_ref[...], jnp.tanh(wd).astype(BF16))
    w_log = -jax.nn.softplus(-w) - 0.5
    a = _sigmoid(a0_ref[...] + _dot(a2t_ref[...], ad.astype(BF16)))
    kkv = (k * kk_ref[...]).reshape(NH, HD, nb)
    norm = jnp.maximum(jnp.sqrt(jnp.sum(kkv * kkv, axis=1, keepdims=True)), 1e-12)
    kkn = (kkv / norm).reshape(DH, nb)
    r_ref[...] = r
    k_ref[...] = k * (1.0 + (a - 1.0) * ka_ref[...])
    v_ref[...] = v
    kkn_ref[...] = kkn
    kka_ref[...] = kkn * a
    d_ref[...] = jnp.exp(-jnp.exp(w_log))
    g_ref[...] = _dot(g2t_ref[...], _sigmoid(gd).astype(BF16))


def _rwkv_sample_prep_t(cols, kv_rows, kwvw, prev_t, params_t):
    nb = cols.shape[0]
    args = [cols, kv_rows, kwvw, prev_t] + list(params_t)
    full = lambda shape: pl.BlockSpec(shape, lambda i: (0,) * len(shape))
    out_rows = [RWKV_COLS, kv_rows.shape[1], kwvw.shape[1]] + [DH] * 7
    return pl.pallas_call(
        _rwkv_sample_prep_t_body,
        grid=(1,),
        in_specs=[full(a.shape) for a in args],
        out_specs=[full((rows, nb)) for rows in out_rows],
        out_shape=[jax.ShapeDtypeStruct((rows, nb), F32) for rows in out_rows],
        scratch_shapes=[pltpu.VMEM((cols.shape[1], nb), F32)],
        compiler_params=_cparams(1),
        name="rwkv_sample_prep",
    )(*args)


def _rwkv_sample_step_t_body(s_ref, r_ref, k_ref, v_ref, kkn_ref, kka_ref, d_ref, g_ref, rk_ref, gnw_ref, gnb_ref,
                             o_ref, sout_ref, y_ref):
    r, k, kkn, kka, d = r_ref[...], k_ref[...], kkn_ref[...], kka_ref[...], d_ref[...]

    def value_row(i, carry):
        s_i = s_ref[0, i]
        sa = -jnp.sum(s_i * kkn, axis=0, keepdims=True)
        s_new = s_i * d + sa * kka + v_ref[pl.ds(i, 1), :] * k
        sout_ref[0, i] = s_new
        y_ref[pl.ds(i, 1), :] = jnp.sum(s_new * r, axis=0, keepdims=True)
        return carry

    lax.fori_loop(0, HD, value_row, 0)
    y = y_ref[...]
    v = v_ref[...]
    mean = jnp.mean(y, axis=0, keepdims=True)
    var = jnp.mean(jnp.square(y - mean), axis=0, keepdims=True)
    yn = (y - mean) * lax.rsqrt(var + GN_EPS) * gnw_ref[...] + gnb_ref[...]
    bonus = jnp.sum(r * k * rk_ref[...], axis=0, keepdims=True) * v
    o_ref[...] = (yn + bonus) * g_ref[...]


def _rwkv_sample_step_t(state4, vecs_t, rk_t, gnw_t, gnb_t):
    nh, hs, _, nb = state4.shape
    head_rows = pl.BlockSpec((hs, nb), lambda h: (h, 0))
    state_spec = pl.BlockSpec((1, hs, hs, nb), lambda h: (h, 0, 0, 0))
    return pl.pallas_call(
        _rwkv_sample_step_t_body,
        grid=(nh,),
        in_specs=[state_spec] + [head_rows] * 10,
        out_specs=[head_rows, state_spec],
        out_shape=[jax.ShapeDtypeStruct((nh * hs, nb), F32), jax.ShapeDtypeStruct(state4.shape, F32)],
        scratch_shapes=[pltpu.VMEM((hs, nb), F32)],
        compiler_params=_cparams(1),
        name="rwkv_sample_step",
    )(state4, *vecs_t, rk_t, gnw_t, gnb_t)


def _outproj_t_body(oa_ref, orwt_ref, wa_ref, wr_ref, h_ref, g_ref, o_ref):
    mixed = _dot(oa_ref[...].astype(BF16), wa_ref[...]) + _dot_tn(orwt_ref[...].astype(BF16), wr_ref[...])
    o_ref[...] = h_ref[...] + _rms(mixed, g_ref[...])


def _outproj_t(o_attn, o_rwkv_t, w_a, w_r, h, g_post):
    n, d = h.shape
    full = lambda a: pl.BlockSpec(a.shape, lambda i: (0,) * a.ndim)
    args = [o_attn, o_rwkv_t, w_a, w_r, h, g_post]
    return pl.pallas_call(
        _outproj_t_body,
        grid=(1,),
        in_specs=[full(a) for a in args],
        out_specs=pl.BlockSpec((n, d), lambda i: (0, 0)),
        out_shape=jax.ShapeDtypeStruct((n, d), F32),
        compiler_params=_cparams(1),
        name="outproj_sample",
    )(*args)


def _outproj_body(oa_ref, orw_ref, wa_ref, wr_ref, h_ref, g_ref, o_ref):
    mixed = _dot(oa_ref[...].astype(BF16), wa_ref[...]) + _dot(orw_ref[...].astype(BF16), wr_ref[...])
    o_ref[...] = h_ref[...] + _rms(mixed, g_ref[...])


def _outproj(o_attn, o_rwkv, w_a, w_r, h, g_post, tm):
    n, d = h.shape
    row = lambda i: (i, 0)
    const = lambda i: (0, 0)
    return pl.pallas_call(
        _outproj_body,
        grid=(n // tm,),
        in_specs=[pl.BlockSpec((tm, DH), row), pl.BlockSpec((tm, DH), row),
                  pl.BlockSpec((DH, d), const), pl.BlockSpec((DH, d), const),
                  pl.BlockSpec((tm, d), row), pl.BlockSpec((1, d), const)],
        out_specs=pl.BlockSpec((tm, d), row),
        out_shape=jax.ShapeDtypeStruct((n, d), F32),
        compiler_params=_cparams(1),
        name="outproj",
    )(o_attn, o_rwkv, w_a, w_r, h, g_post)


def _rope_tables(pos):
    inv = ROPE_THETA ** (-jnp.arange(ROPE_HALF, dtype=F32) / ROPE_HALF)
    ang = pos.astype(F32)[:, None] * inv[None, :]
    cos, sin = jnp.cos(ang), jnp.sin(ang)
    n = pos.shape[0]
    rest = HD - 2 * ROPE_HALF
    c = jnp.concatenate([cos, cos, jnp.ones((n, rest), F32)], axis=1)
    s_lo = jnp.concatenate([-sin, jnp.zeros((n, HD - ROPE_HALF), F32)], axis=1)
    s_hi = jnp.concatenate([jnp.zeros((n, ROPE_HALF), F32), sin, jnp.zeros((n, rest), F32)], axis=1)
    two = lambda x: jnp.concatenate([x, x], axis=1)
    return two(c), two(s_lo), two(s_hi)


def _cmp_to_sel(n_cmp, n_sel, rows):
    i = np.arange(n_cmp)[:, None] * CMP_STRIDE
    j = np.arange(n_sel)[None, :] * SEL_BLOCK
    ov = np.minimum(i + CMP_BLOCK, j + SEL_BLOCK) - np.maximum(i, j)
    m = np.zeros((rows, N_SEL_LANES), np.float32)
    m[:n_cmp, :n_sel] = np.maximum(ov, 0) // CMP_STRIDE
    return jnp.asarray(m, BF16)


def _block_expand(n_keys):
    j = np.arange(N_SEL_LANES)[:, None]
    t = np.arange(n_keys)[None, :]
    return jnp.asarray((t // SEL_BLOCK == j).astype(np.float32), BF16)


def _pad_cols(x, width):
    return jnp.pad(x, ((0, 0), (0, width - x.shape[1])))


def _row_tile(n, cap):
    tm = min(n, cap)
    while n % tm:
        tm //= 2
    return tm


def kernel(x_prompt, x_sample, cache_nsa, page_table, state_win, state_wkv, state_shift, norm_f1_pre, norm_f1_post, ffn1_gu, ffn1_dn, norm_mix_pre, w_in, cmp_pe_k, cmp_w1_k, cmp_w2_k, cmp_pe_v, cmp_w1_v, cmp_w2_v, rw_mu, rw_w0, rw_w2, rw_a0, rw_a2, rw_g2, rw_kk, rw_ka, rw_rk, rw_gn_w, rw_gn_b, w_out, norm_mix_post, norm_f2_pre, ffn2_gu, ffn2_dn, norm_f2_post):
    depth = w_in.shape[0]
    assert depth == 1, "single-layer step"
    bp, t, d = x_prompt.shape
    bs, ts, _ = x_sample.shape
    assert ts == 1 and t % Q_BLOCK == 0 and t % RW_CHUNK == 0 and t >= WINDOW
    n_pages = page_table.shape[1]
    page = cache_nsa.shape[2]
    past = n_pages * page
    wb = state_win.shape[2]
    assert wb == min(WINDOW, past) and past % SEL_BLOCK == 0 and page % CMP_STRIDE == 0
    l = 0

    w1g, w1d = ffn1_gu[l].astype(BF16), ffn1_dn[l].astype(BF16)
    w2g, w2d = ffn2_gu[l].astype(BF16), ffn2_dn[l].astype(BF16)
    wi = w_in[l]
    w_nsa = _pad_cols(wi[:, :NSA_COLS], NSA_PAD).astype(BF16)
    w_rw = _pad_cols(wi[:, NSA_COLS:], RWKV_PAD).astype(BF16)
    w_oa, w_or = w_out[l, :DH].astype(BF16), w_out[l, DH:].astype(BF16)
    vec = lambda a: a.reshape(1, -1)
    cmp_k = _cmp_weights(cmp_pe_k[l], cmp_w1_k[l], cmp_w2_k[l])
    cmp_v = _cmp_weights(cmp_pe_v[l], cmp_w1_v[l], cmp_w2_v[l])
    g2_pad = jnp.pad(rw_g2[l], ((0, LORA_G_PAD - LORA_G), (0, 0))).astype(BF16)
    rw_prep = [_pad_cols(vec(rw_mu[l]), RWKV_PAD), vec(rw_w0[l]), rw_w2[l].astype(BF16), vec(rw_a0[l]),
               rw_a2[l].astype(BF16), g2_pad, vec(rw_kk[l]), vec(rw_ka[l])]
    rk_row, gnw_row, gnb_row = vec(rw_rk[l]), vec(rw_gn_w[l]), vec(rw_gn_b[l])

    outs = {}
    for name, x2, n_batch in (("p", x_prompt.reshape(bp * t, d), bp), ("s", x_sample.reshape(bs, d), bs)):
        n = x2.shape[0]
        tm = _row_tile(n, 512)
        is_prompt = name == "p"
        h1, hn = _ffn_half(x2, vec(norm_f1_pre[l]), w1g, w1d, vec(norm_f1_post[l]), vec(norm_mix_pre[l]),
                           tm, 512)
        pos = jnp.tile(jnp.arange(t), bp) if is_prompt else jnp.full((n,), past, jnp.int32)
        rc, rlo, rhi = _rope_tables(pos)
        q_raw, q_rot, kv_rows, kwvw, ks, vs, kw, vw, gates = _inproj_nsa(hn, w_nsa, rc, rlo, rhi, tm)
        cols = _inproj_rwkv(hn, w_rw, tm)
        if is_prompt:
            k_cmp, v_cmp = _compress_prompt(kv_rows, cmp_k, cmp_v, n_batch, t)
            n_seg = t // CMP_STRIDE
            mcs_t = _cmp_to_sel(n_seg - 1, t // SEL_BLOCK, n_seg).T
            o_attn = _attn_prompt(q_raw, q_rot, gates, k_cmp, v_cmp, ks, vs, kw, vw, mcs_t, _block_expand(t),
                                  n_batch, t)
            o_rwkv, wkv = _rwkv_prompt(cols, rw_prep + [rk_row, gnw_row, gnb_row], n_batch, t)
            win = kwvw.reshape(n_batch, t, 2 * DKV)[:, t - wb:]
            shift = cols.reshape(n_batch, t, RWKV_PAD)[:, t - 1:, :RWKV_COLS]
        else:
            cache4 = jnp.transpose(cache_nsa[l], (0, 2, 3, 4, 1)).reshape(cache_nsa.shape[1], 4, DKV, page)
            win4 = jnp.transpose(state_win[l], (0, 2, 3, 4, 1)).reshape(n, 2, DKV, wb)
            wkv4 = jnp.transpose(state_wkv[l], (1, 2, 3, 0))
            prev_t = jnp.transpose(state_shift[l].reshape(n, RWKV_COLS))
            lanes = lambda a: jnp.broadcast_to(a.reshape(-1, 1), (a.size, n))
            params_t = [lanes(rw_mu[l]), lanes(rw_w0[l]), rw_w2[l].T.astype(BF16), lanes(rw_a0[l]),
                        rw_a2[l].T.astype(BF16), rw_g2[l].T.astype(BF16), lanes(rw_kk[l]), lanes(rw_ka[l])]
            cols_t, kv_t, kwvw_t, *vecs_t = _rwkv_sample_prep_t(cols, kv_rows, kwvw, prev_t, params_t)
            n_seg = past // CMP_STRIDE
            mcs = _cmp_to_sel(n_seg - 1, past // SEL_BLOCK + 1, n_seg)
            o_attn, win4_new = _nsa_sample_t(
                cache4, page_table, win4,
                q_raw.astype(F32).reshape(n, NH, HD), q_rot.astype(F32).reshape(n, NH, HD),
                gates[:, :3 * NH].reshape(n, NH, 3), kv_rows.reshape(n, 1, 4 * DKV), kwvw.reshape(n, 1, 2 * DKV),
                kwvw_t, cmp_k, cmp_v, mcs, _block_expand(past))
            o_attn = o_attn.reshape(n, DH)
            o_rwkv_t, wkv4_new = _rwkv_sample_step_t(wkv4, vecs_t, lanes(rw_rk[l]), lanes(rw_gn_w[l]),
                                                     lanes(rw_gn_b[l]))
            kv_rows = jnp.transpose(kv_t.reshape(4, KVH, HD, n), (3, 0, 1, 2))
            win = jnp.transpose(win4_new.reshape(n, 2, KVH, HD, wb), (0, 4, 1, 2, 3))
            wkv = jnp.transpose(wkv4_new, (3, 0, 1, 2))
            shift = jnp.transpose(cols_t)[:, None, :]
            h2 = _outproj_t(o_attn, o_rwkv_t, w_oa, w_or, h1, vec(norm_mix_post[l]))
        if is_prompt:
            h2 = _outproj(o_attn, o_rwkv, w_oa, w_or, h1, vec(norm_mix_post[l]), tm)
        y = _ffn_half(h2, vec(norm_f2_pre[l]), w2g, w2d, vec(norm_f2_post[l]), None, tm, 512)
        outs[name] = (y, kv_rows, win, wkv, shift)

    yp, kvp, winp, wkvp, shp = outs["p"]
    ys, kvs, wins, wkvs, shs = outs["s"]
    return (yp.reshape(bp, t, d), ys.reshape(bs, 1, d),
            kvp.reshape(1, bp, t, 4, KVH, HD), kvs.reshape(1, bs, 1, 4, KVH, HD),
            winp.reshape(1, bp, wb, 2, KVH, HD), wins.reshape(1, bs, wb, 2, KVH, HD),
            wkvp[None], wkvs[None], shp[None], shs[None])
```

```python
import functools

import numpy as np
import jax
import jax.numpy as jnp
from jax import lax
from jax.experimental import pallas as pl
from jax.experimental.pallas import tpu as pltpu

F32 = jnp.float32
BF16 = jnp.bfloat16

HD = 64
KVH = 4
GQA = 4
NH = 16
DH = NH * HD
DKV = KVH * HD
ROPE_HALF = 8
ROPE_THETA = 500000.0
CMP_BLOCK = 32
CMP_STRIDE = 16
CMP_HIDDEN = 128
SEL_BLOCK = 64
SEL_TOPN = 16
N_LOCAL_FORCED = 2
WINDOW = 512
Q_BLOCK = 128
LORA_W = 64
LORA_A = 64
LORA_G = 160
RMS_EPS = 1e-6
GN_EPS = 64e-5
NEG = -1e30
FORCE_BONUS = 1e4
MASK_BIAS = -1e30
M_FLOOR = -1e20
NSA_COLS = DH + 6 * DKV + 3 * NH
RWKV_COLS = 3 * DH + LORA_W + LORA_A + LORA_G
NSA_PAD = 2688
RWKV_PAD = 3456
LORA_G_PAD = 256
RW_CHUNK = 64
SEL_KEY_CHUNK = 512
N_SEL_LANES = 64
SAMPLE_SEQS_PER_STEP = 2

VMEM_LIMIT_BYTES = 56 * 1024 * 1024


def _cparams(n_axes):
    return pltpu.CompilerParams(dimension_semantics=("arbitrary",) * n_axes,
                                vmem_limit_bytes=VMEM_LIMIT_BYTES)


def _dot(a, b):
    return jnp.dot(a, b, preferred_element_type=F32)


def _dot_nt(a, b):
    return lax.dot_general(a, b, (((1,), (1,)), ((), ())), preferred_element_type=F32)


def _dot_tn(a, b):
    return lax.dot_general(a, b, (((0,), (0,)), ((), ())), preferred_element_type=F32)


def _dot_split3(a, b_bf16):
    hi = a.astype(BF16)
    r1 = a - hi.astype(F32)
    mid = r1.astype(BF16)
    lo = (r1 - mid.astype(F32)).astype(BF16)
    return _dot(hi, b_bf16) + _dot(mid, b_bf16) + _dot(lo, b_bf16)


def _rms(x, g):
    ms = jnp.mean(x * x, axis=-1, keepdims=True)
    return x * lax.rsqrt(ms + RMS_EPS) * g


def _sigmoid(x):
    return jax.nn.sigmoid(x)


def _ffn_body(has_next, x_ref, gpre_ref, wg_ref, wu_ref, wd_ref, gpost_ref, *rest):
    if has_next:
        gnext_ref, o_ref, on_ref, xn_ref, acc_ref = rest
    else:
        o_ref, xn_ref, acc_ref = rest
    j = pl.program_id(1)

    @pl.when(j == 0)
    def _():
        xn_ref[...] = _rms(x_ref[...], gpre_ref[...]).astype(BF16)
        acc_ref[...] = jnp.zeros_like(acc_ref)

    xn = xn_ref[...]
    g = _dot(xn, wg_ref[...])
    u = _dot(xn, wu_ref[...])
    act = ((g * _sigmoid(g)) * u).astype(BF16)
    acc_ref[...] += _dot(act, wd_ref[...])

    @pl.when(j == pl.num_programs(1) - 1)
    def _():
        y = x_ref[...] + 0.5 * _rms(acc_ref[...], gpost_ref[...])
        o_ref[...] = y
        if has_next:
            on_ref[...] = _rms(y, gnext_ref[...]).astype(BF16)


def _ffn_half(x, g_pre, w_gu, w_dn, g_post, g_next, tm, tf):
    n, d = x.shape
    f = w_dn.shape[0]
    nj = f // tf
    has_next = g_next is not None
    row = lambda i, j: (i, 0)
    const = lambda i, j: (0, 0)
    in_specs = [
        pl.BlockSpec((tm, d), row),
        pl.BlockSpec((1, d), const),
        pl.BlockSpec((d, tf), lambda i, j: (0, j)),
        pl.BlockSpec((d, tf), lambda i, j: (0, j + nj)),
        pl.BlockSpec((tf, d), lambda i, j: (j, 0)),
        pl.BlockSpec((1, d), const),
    ]
    args = [x, g_pre, w_gu, w_gu, w_dn, g_post]
    out_shape = [jax.ShapeDtypeStruct((n, d), F32)]
    out_specs = [pl.BlockSpec((tm, d), row)]
    if has_next:
        in_specs.append(pl.BlockSpec((1, d), const))
        args.append(g_next)
        out_shape.append(jax.ShapeDtypeStruct((n, d), BF16))
        out_specs.append(pl.BlockSpec((tm, d), row))
    res = pl.pallas_call(
        functools.partial(_ffn_body, has_next),
        grid=(n // tm, nj),
        in_specs=in_specs,
        out_specs=out_specs,
        out_shape=out_shape,
        scratch_shapes=[pltpu.VMEM((tm, d), BF16), pltpu.VMEM((tm, d), F32)],
        compiler_params=_cparams(2),
        name="ffn_half",
    )(*args)
    return res if has_next else res[0]


def _rope(x, c, s_lo, s_hi):
    w = x.shape[1]
    reps = w // 128
    tile = lambda t: t if reps == 1 else jnp.concatenate([t] * reps, axis=1)
    up = pltpu.roll(x, w - ROPE_HALF, 1)
    dn = pltpu.roll(x, ROPE_HALF, 1)
    return x * tile(c) + up * tile(s_lo) + dn * tile(s_hi)


def _inproj_nsa_body(hn_ref, w_ref, c_ref, slo_ref, shi_ref,
                     qraw_ref, qrot_ref, kv_ref, kwvw_ref, ks_ref, vs_ref, kw_ref, vw_ref, gates_ref, *kvt_ref):
    p = _dot(hn_ref[...], w_ref[...])
    c, s_lo, s_hi = c_ref[...], slo_ref[...], shi_ref[...]
    q = p[:, 0:DH] * HD ** -0.5
    qraw_ref[...] = q.astype(BF16)
    qrot_ref[...] = _rope(q, c, s_lo, s_hi).astype(BF16)
    o = DH
    ks = _rope(p[:, o + 2 * DKV:o + 3 * DKV], c, s_lo, s_hi)
    vs = p[:, o + 3 * DKV:o + 4 * DKV]
    kw = _rope(p[:, o + 4 * DKV:o + 5 * DKV], c, s_lo, s_hi)
    vw = p[:, o + 5 * DKV:o + 6 * DKV]
    kv_ref[:, 0:2 * DKV] = p[:, o:o + 2 * DKV]
    kv_ref[:, 2 * DKV:3 * DKV] = ks
    kv_ref[:, 3 * DKV:4 * DKV] = vs
    kwvw_ref[:, 0:DKV] = kw
    kwvw_ref[:, DKV:2 * DKV] = vw
    if kvt_ref:
        for j in range(4 * DKV // 128):
            kvt_ref[0][0, 128 * j:128 * (j + 1), :] = kv_ref[:, 128 * j:128 * (j + 1)].T
    vs_t, vw_t = vs.T, vw.T
    for k in range(KVH):
        sl = slice(HD * k, HD * (k + 1))
        ks_ref[k] = ks[:, sl].astype(BF16)
        vs_ref[k] = vs_t[sl, :].astype(BF16)
        kw_ref[k] = kw[:, sl].astype(BF16)
        vw_ref[k] = vw_t[sl, :].astype(BF16)
    gates_ref[...] = _sigmoid(p[:, o + 6 * DKV:o + 6 * DKV + 128])


def _inproj_nsa(hn, w_nsa, rope_c, rope_slo, rope_shi, tm, seq_len=None):
    n, d = hn.shape
    row = lambda i: (i, 0)
    hm = lambda i: (0, i, 0)
    hm_shape = jax.ShapeDtypeStruct((KVH, n, HD), BF16)
    hm_spec = pl.BlockSpec((KVH, tm, HD), hm)
    hmt_shape = jax.ShapeDtypeStruct((KVH, HD, n), BF16)
    hmt_spec = pl.BlockSpec((KVH, HD, tm), lambda i: (0, 0, i))
    out_specs = [pl.BlockSpec((tm, DH), row), pl.BlockSpec((tm, DH), row),
                 pl.BlockSpec((tm, 4 * DKV), row), pl.BlockSpec((tm, 2 * DKV), row),
                 hm_spec, hmt_spec, hm_spec, hmt_spec,
                 pl.BlockSpec((tm, 128), row)]
    out_shape = [jax.ShapeDtypeStruct((n, DH), BF16), jax.ShapeDtypeStruct((n, DH), BF16),
                 jax.ShapeDtypeStruct((n, 4 * DKV), F32), jax.ShapeDtypeStruct((n, 2 * DKV), F32),
                 hm_shape, hmt_shape, hm_shape, hmt_shape,
                 jax.ShapeDtypeStruct((n, 128), F32)]
    if seq_len is not None:
        per_seq = seq_len // tm
        out_specs.append(pl.BlockSpec((1, 4 * DKV, tm), lambda i: (i // per_seq, 0, i % per_seq)))
        out_shape.append(jax.ShapeDtypeStruct((n // seq_len, 4 * DKV, seq_len), F32))
    return pl.pallas_call(
        _inproj_nsa_body,
        grid=(n // tm,),
        in_specs=[pl.BlockSpec((tm, d), row),
                  pl.BlockSpec((d, NSA_PAD), lambda i: (0, 0)),
                  pl.BlockSpec((tm, 128), row), pl.BlockSpec((tm, 128), row), pl.BlockSpec((tm, 128), row)],
        out_specs=out_specs,
        out_shape=out_shape,
        compiler_params=_cparams(1),
        name="inproj_nsa",
    )(hn, w_nsa, rope_c, rope_slo, rope_shi)


def _matmul_body(x_ref, w_ref, o_ref):
    o_ref[...] = _dot(x_ref[...], w_ref[...])


def _inproj_rwkv(hn, w_rw, tm):
    n, d = hn.shape
    c = w_rw.shape[1]
    return pl.pallas_call(
        _matmul_body,
        grid=(n // tm,),
        in_specs=[pl.BlockSpec((tm, d), lambda i: (i, 0)), pl.BlockSpec((d, c), lambda i: (0, 0))],
        out_specs=pl.BlockSpec((tm, c), lambda i: (i, 0)),
        out_shape=jax.ShapeDtypeStruct((n, c), F32),
        compiler_params=_cparams(1),
        name="inproj_rwkv",
    )(hn, w_rw)


def _compress_accumulate(stage_ref, w1c_refs, n_seg):
    accs = [[jnp.zeros((n_seg, 2 * CMP_HIDDEN), F32) for _ in range(KVH)] for _ in range(2)]
    for s in range(CMP_STRIDE):
        for kv in range(2):
            w = w1c_refs[kv][s]
            for pair in range(KVH // 2):
                x = stage_ref[2 * kv + pair, pl.ds(s, n_seg, stride=CMP_STRIDE), :]
                for j in range(2):
                    k = 2 * pair + j
                    accs[kv][k] = accs[kv][k] + _dot(x[:, HD * j:HD * (j + 1)].astype(BF16), w)
    return accs


def _compress_finish(acc, pe0, w2, n_seg):
    a0 = acc[:, :CMP_HIDDEN]
    a1 = pltpu.roll(acc[:, CMP_HIDDEN:], n_seg - 1, 0)
    pre = pe0 + a0 + a1
    return _dot((pre * _sigmoid(pre)).astype(BF16), w2)


def _cmp_prompt_body(kv_ref, w1k_ref, w1v_ref, pek_ref, pev_ref, w1kf_ref, w1vf_ref, w2k_ref, w2v_ref,
                     kc_ref, vc_ref, stage_ref):
    n_seg = kv_ref.shape[0] // CMP_STRIDE
    for grp in range(4):
        stage_ref[grp] = kv_ref[:, 128 * grp:128 * (grp + 1)]
    accs = _compress_accumulate(stage_ref, (w1k_ref, w1v_ref), n_seg)
    for kv, (pe_ref, w1f_ref, w2_ref, out_ref) in enumerate(
            ((pek_ref, w1kf_ref, w2k_ref, kc_ref), (pev_ref, w1vf_ref, w2v_ref, vc_ref))):
        pe0 = _dot(pe_ref[...], w1f_ref[...])
        w2 = w2_ref[...]
        for k in range(KVH):
            c = _compress_finish(accs[kv][k], pe0, w2, n_seg)
            out_ref[0, k] = (c.T if kv == 1 else c).astype(BF16)


def _cmp_weights(pe, w1, w2):
    w1c = jnp.concatenate([w1[:CMP_STRIDE], w1[CMP_STRIDE:]], axis=-1).astype(BF16)
    return w1c, pe.reshape(1, CMP_BLOCK * HD), w1.reshape(CMP_BLOCK * HD, CMP_HIDDEN), w2.astype(BF16)


def _compress_prompt(kv_rows, cmp_k, cmp_v, n_batch, t):
    n_seg = t // CMP_STRIDE
    w1k, pek, w1kf, w2k = cmp_k
    w1v, pev, w1vf, w2v = cmp_v
    full = lambda a: pl.BlockSpec(a.shape, lambda b: (0,) * a.ndim)
    out_shape = jax.ShapeDtypeStruct((n_batch, KVH, n_seg, HD), BF16)
    out_spec = pl.BlockSpec((1, KVH, n_seg, HD), lambda b: (b, 0, 0, 0))
    out_shape_t = jax.ShapeDtypeStruct((n_batch, KVH, HD, n_seg), BF16)
    out_spec_t = pl.BlockSpec((1, KVH, HD, n_seg), lambda b: (b, 0, 0, 0))
    return pl.pallas_call(
        _cmp_prompt_body,
        grid=(n_batch,),
        in_specs=[pl.BlockSpec((t, 2 * DKV), lambda b: (b, 0)),
                  full(w1k), full(w1v), full(pek), full(pev), full(w1kf), full(w1vf), full(w2k), full(w2v)],
        out_specs=[out_spec, out_spec_t],
        out_shape=[out_shape, out_shape_t],
        scratch_shapes=[pltpu.VMEM((4, t, 128), F32)],
        compiler_params=_cparams(1),
        name="nsa_compress_prompt",
    )(kv_rows, w1k, w1v, pek, pev, w1kf, w1vf, w2k, w2v)


def _select_blocks(imp, cur, n_blocks):
    jb = lax.broadcasted_iota(jnp.int32, (1, N_SEL_LANES), 1)
    valid = (jb <= cur) & (jb < n_blocks)
    rel = cur - jb
    forced = (jb == 0) | ((rel >= 0) & (rel < N_LOCAL_FORCED))
    score = jnp.where(valid, imp + jnp.where(forced, FORCE_BONUS, 0.0), NEG)
    rank = jnp.zeros(score.shape, F32)
    for jp in range(min(n_blocks, N_SEL_LANES)):
        col = score[:, jp:jp + 1]
        tie = jnp.where(jb > jp, 1.0, 0.0)
        rank = rank + jnp.where(col > score, 1.0, jnp.where(col == score, tie, 0.0))
    return jnp.where(valid, jnp.where(rank < SEL_TOPN, 1.0, 0.0), 0.0)


def _softmax_rows(s, ok):
    sm = jnp.where(ok, s, NEG)
    m = jnp.max(sm, axis=-1, keepdims=True)
    e = jnp.where(ok, jnp.exp(sm - m), 0.0)
    l = jnp.sum(e, axis=-1, keepdims=True)
    return e / jnp.maximum(l, 1e-30)


def _softmax_bias(s, bias):
    sb = s + bias
    m = jnp.maximum(jnp.max(sb, axis=-1, keepdims=True), M_FLOOR)
    e = jnp.exp(sb - m)
    return e, 1.0 / jnp.maximum(jnp.sum(e, axis=-1, keepdims=True), 1e-30)


def _select_blocks_t(imp_t, cur, n_blocks, n_live, score_ref):
    jb = lax.broadcasted_iota(jnp.int32, (N_SEL_LANES, 1), 0)
    valid = (jb <= cur) & (jb < n_blocks)
    rel = cur - jb
    forced = (jb == 0) | ((rel >= 0) & (rel < N_LOCAL_FORCED))
    score = jnp.where(valid, imp_t + jnp.where(forced, FORCE_BONUS, 0.0), NEG)
    score_ref[...] = score

    def body(jp, rank):
        other = score_ref[pl.ds(jp, 1), :]
        tie = jnp.where(jb > jp, 1.0, 0.0)
        return rank + jnp.where(other > score, 1.0, jnp.where(other == score, tie, 0.0))

    rank = lax.fori_loop(0, n_live, body, jnp.zeros(score.shape, F32))
    return jnp.where(valid, jnp.where(rank < SEL_TOPN, 1.0, 0.0), 0.0)


def _attn_prompt_body(qraw_ref, qrot_ref, gates_ref, kc_ref, vc_ref, ks_ref, vs_ref, kw_ref, vw_ref,
                      mcst_ref, e_ref, o_ref, score_ref):
    t = ks_ref.shape[1]
    n_seg = kc_ref.shape[2]
    i = pl.program_id(1)
    s0 = i * Q_BLOCK
    qpos = s0 + lax.broadcasted_iota(jnp.int32, (Q_BLOCK, 1), 0)
    qpos_row = s0 + lax.broadcasted_iota(jnp.int32, (1, Q_BLOCK), 1)
    n_idx = lax.broadcasted_iota(jnp.int32, (1, n_seg), 1)
    ok_cmp = (n_idx * CMP_STRIDE + (CMP_BLOCK - 1) <= qpos) & (n_idx < n_seg - 1)
    bias_cmp = jnp.where(ok_cmp, 0.0, MASK_BIAS)
    gates = gates_ref[...]
    mcst = mcst_ref[...]
    n_key_chunks = i // (SEL_KEY_CHUNK // Q_BLOCK) + 1
    n_live_blocks = (s0 + Q_BLOCK) // SEL_BLOCK
    win_start = pl.multiple_of(jnp.maximum(s0 - WINDOW, 0), Q_BLOCK)
    win_len = WINDOW + Q_BLOCK
    wpos = win_start + lax.broadcasted_iota(jnp.int32, (1, win_len), 1)
    dlt = qpos - wpos
    bias_win = jnp.where((dlt >= 0) & (dlt <= WINDOW), 0.0, MASK_BIAS)
    g_rows = [slice(g * Q_BLOCK, (g + 1) * Q_BLOCK) for g in range(GQA)]
    half = Q_BLOCK // 2

    kvhs = range(KVH)
    head_cols = lambda ref, k: jnp.concatenate(
        [ref[:, HD * (GQA * k + g):HD * (GQA * k + g + 1)] for g in range(GQA)], axis=0)
    qr = [head_cols(qraw_ref, k) for k in kvhs]
    qs = [head_cols(qrot_ref, k) for k in kvhs]

    s_cmp = [_dot_nt(qr[k], kc_ref[0, k]) for k in kvhs]
    o_cmp, sel_t = [], []
    for k in kvhs:
        p_heads = []
        for rows in g_rows:
            e_c, inv_c = _softmax_bias(s_cmp[k][rows], bias_cmp)
            p_heads.append(e_c * inv_c)
        o_cmp.append(_dot(jnp.concatenate([p.astype(BF16) for p in p_heads], axis=0), vc_ref[0, k]))
        p_sum = p_heads[0]
        for p in p_heads[1:]:
            p_sum = p_sum + p
        hi = p_sum.astype(BF16)
        r1 = p_sum - hi.astype(F32)
        mid = r1.astype(BF16)
        lo = (r1 - mid.astype(F32)).astype(BF16)
        imp_t = _dot_nt(mcst, hi) + _dot_nt(mcst, mid) + _dot_nt(mcst, lo)
        sel_t.append(_select_blocks_t(imp_t, qpos_row // SEL_BLOCK, t // SEL_BLOCK, n_live_blocks,
                                      score_ref.at[k]).astype(BF16))

    def chunk(c, carry):
        k0 = pl.multiple_of(c * SEL_KEY_CHUNK, SEL_KEY_CHUNK)
        kpos = k0 + lax.broadcasted_iota(jnp.int32, (1, SEL_KEY_CHUNK), 1)
        causal = kpos <= qpos
        e_blk = e_ref[:, pl.ds(k0, SEL_KEY_CHUNK)]
        s = [_dot_nt(qs[k], ks_ref[k, pl.ds(k0, SEL_KEY_CHUNK), :]) for k in kvhs]
        bias = [(jnp.where(causal, _dot_tn(sel_t[k], e_blk), 0.0) - 1.0) * (-MASK_BIAS) for k in kvhs]
        out = []
        for k in kvhs:
            m, l, acc = carry[k]
            m_new, p_b, p_tot = [], [], []
            for rows in g_rows:
                sb = s[k][rows] + bias[k]
                m_g = jnp.maximum(m[rows], jnp.max(sb, axis=-1, keepdims=True))
                p = jnp.exp(sb - m_g)
                m_new.append(m_g)
                p_tot.append(jnp.sum(p, axis=-1, keepdims=True))
                p_b.append(p.astype(BF16))
            m_new = jnp.concatenate(m_new, axis=0)
            alpha = jnp.exp(m - m_new)
            l = alpha * l + jnp.concatenate(p_tot, axis=0)
            acc = alpha * acc + _dot(jnp.concatenate(p_b, axis=0), vs_ref[k, pl.ds(k0, SEL_KEY_CHUNK), :])
            out.append((m_new, l, acc))
        return tuple(out)

    init = tuple((jnp.full((GQA * Q_BLOCK, 1), M_FLOOR, F32), jnp.zeros((GQA * Q_BLOCK, 1), F32),
                  jnp.zeros((GQA * Q_BLOCK, HD), F32)) for _ in kvhs)
    sel_out = lax.fori_loop(0, n_key_chunks, chunk, init)
    o_sel = [acc * (1.0 / jnp.maximum(l, 1e-30)) for _, l, acc in sel_out]

    s_win = [_dot_nt(qs[k], kw_ref[k, pl.ds(win_start, win_len), :]) for k in kvhs]
    for k in kvhs:
        e_b, inv_w = [], []
        for g in range(GQA):
            for hf in range(2):
                rows = slice(g * Q_BLOCK + hf * half, g * Q_BLOCK + (hf + 1) * half)
                e_w, inv = _softmax_bias(s_win[k][rows], bias_win[hf * half:(hf + 1) * half])
                e_b.append(e_w.astype(BF16))
                inv_w.append(inv)
        o_win = (_dot(jnp.concatenate(e_b, axis=0), vw_ref[k, pl.ds(win_start, win_len), :])
                 * jnp.concatenate(inv_w, axis=0))
        for g in range(GQA):
            h = GQA * k + g
            rows = g_rows[g]
            o_h = (gates[:, 3 * h:3 * h + 1] * o_cmp[k][rows] + gates[:, 3 * h + 1:3 * h + 2] * o_sel[k][rows]
                   + gates[:, 3 * h + 2:3 * h + 3] * o_win[rows])
            o_ref[:, HD * h:HD * (h + 1)] = o_h.astype(o_ref.dtype)


def _attn_prompt_t_body(qraw_ref, qrot_ref, gates_ref, kc_ref, vct_ref, ks_ref, vst_ref, kw_ref, vwt_ref,
                        mcst_ref, et_ref, o_ref, score_ref):
    t = ks_ref.shape[1]
    n_seg = kc_ref.shape[2]
    i = pl.program_id(1)
    s0 = i * Q_BLOCK
    qpos = s0 + lax.broadcasted_iota(jnp.int32, (1, Q_BLOCK), 1)
    n_idx = lax.broadcasted_iota(jnp.int32, (n_seg, 1), 0)
    ok_cmp = (n_idx * CMP_STRIDE + (CMP_BLOCK - 1) <= qpos) & (n_idx < n_seg - 1)
    bias_cmp = jnp.where(ok_cmp, 0.0, MASK_BIAS)
    mcst = mcst_ref[...]
    n_key_chunks = i // (SEL_KEY_CHUNK // Q_BLOCK) + 1
    n_live_blocks = (s0 + Q_BLOCK) // SEL_BLOCK
    win_start = pl.multiple_of(jnp.maximum(s0 - WINDOW, 0), Q_BLOCK)
    win_len = WINDOW + Q_BLOCK
    wpos = win_start + lax.broadcasted_iota(jnp.int32, (win_len, 1), 0)
    dlt = qpos - wpos
    bias_win = jnp.where((dlt >= 0) & (dlt <= WINDOW), 0.0, MASK_BIAS)
    g_cols = [slice(g * Q_BLOCK, (g + 1) * Q_BLOCK) for g in range(GQA)]
    kvhs = range(KVH)
    head_rows = lambda ref, k: jnp.concatenate(
        [ref[:, HD * (GQA * k + g):HD * (GQA * k + g + 1)] for g in range(GQA)], axis=0)
    qr = [head_rows(qraw_ref, k) for k in kvhs]
    qs = [head_rows(qrot_ref, k) for k in kvhs]

    def softmax_cols(s_t, bias):
        e_b, inv = [], []
        for cols in g_cols:
            sb = s_t[:, cols] + bias
            m = jnp.maximum(jnp.max(sb, axis=0, keepdims=True), M_FLOOR)
            e = jnp.exp(sb - m)
            e_b.append(e)
            inv.append(1.0 / jnp.maximum(jnp.sum(e, axis=0, keepdims=True), 1e-30))
        return e_b, inv

    s_cmp = [_dot_nt(kc_ref[0, k], qr[k]) for k in kvhs]
    o_cmp, sel_t = [], []
    for k in kvhs:
        e_heads, inv_heads = softmax_cols(s_cmp[k], bias_cmp)
        p_heads = [e * inv for e, inv in zip(e_heads, inv_heads)]
        o_cmp.append(_dot(vct_ref[0, k], jnp.concatenate([p.astype(BF16) for p in p_heads], axis=1)))
        p_sum = p_heads[0]
        for p in p_heads[1:]:
            p_sum = p_sum + p
        hi = p_sum.astype(BF16)
        r1 = p_sum - hi.astype(F32)
        mid = r1.astype(BF16)
        lo = (r1 - mid.astype(F32)).astype(BF16)
        imp_t = _dot(mcst, hi) + _dot(mcst, mid) + _dot(mcst, lo)
        sel_t.append(_select_blocks_t(imp_t, qpos // SEL_BLOCK, t // SEL_BLOCK, n_live_blocks,
                                      score_ref.at[k]).astype(BF16))

    def chunk(c, carry):
        k0 = pl.multiple_of(c * SEL_KEY_CHUNK, SEL_KEY_CHUNK)
        kpos = k0 + lax.broadcasted_iota(jnp.int32, (SEL_KEY_CHUNK, 1), 0)
        causal = kpos <= qpos
        e_blk = et_ref[pl.ds(k0, SEL_KEY_CHUNK), :]
        s = [_dot_nt(ks_ref[k, pl.ds(k0, SEL_KEY_CHUNK), :], qs[k]) for k in kvhs]
        bias = [(jnp.where(causal, _dot(e_blk, sel_t[k]), 0.0) - 1.0) * (-MASK_BIAS) for k in kvhs]
        out = []
        for k in kvhs:
            m, l, acc = carry[k]
            m_new, p_b, p_tot = [], [], []
            for cols in g_cols:
                sb = s[k][:, cols] + bias[k]
                m_g = jnp.maximum(m[:, cols], jnp.max(sb, axis=0, keepdims=True))
                p = jnp.exp(sb - m_g)
                m_new.append(m_g)
                p_tot.append(jnp.sum(p, axis=0, keepdims=True))
                p_b.append(p.astype(BF16))
            m_new = jnp.concatenate(m_new, axis=1)
            alpha = jnp.exp(m - m_new)
            l = alpha * l + jnp.concatenate(p_tot, axis=1)
            acc = alpha * acc + _dot(vst_ref[k, :, pl.ds(k0, SEL_KEY_CHUNK)], jnp.concatenate(p_b, axis=1))
            out.append((m_new, l, acc))
        return tuple(out)

    init = tuple((jnp.full((1, GQA * Q_BLOCK), M_FLOOR, F32), jnp.zeros((1, GQA * Q_BLOCK), F32),
                  jnp.zeros((HD, GQA * Q_BLOCK), F32)) for _ in kvhs)
    sel_out = lax.fori_loop(0, n_key_chunks, chunk, init)
    o_sel = [acc * (1.0 / jnp.maximum(l, 1e-30)) for _, l, acc in sel_out]

    s_win = [_dot_nt(kw_ref[k, pl.ds(win_start, win_len), :], qs[k]) for k in kvhs]
    gates_t = gates_ref[...].T
    o_t = []
    for k in kvhs:
        e_heads, inv_heads = softmax_cols(s_win[k], bias_win)
        o_win = (_dot(vwt_ref[k, :, pl.ds(win_start, win_len)],
                      jnp.concatenate([e.astype(BF16) for e in e_heads], axis=1))
                 * jnp.concatenate(inv_heads, axis=1))
        for g in range(GQA):
            h = GQA * k + g
            cols = g_cols[g]
            o_t.append(gates_t[3 * h:3 * h + 1] * o_cmp[k][:, cols] + gates_t[3 * h + 1:3 * h + 2] * o_sel[k][:, cols]
                       + gates_t[3 * h + 2:3 * h + 3] * o_win[:, cols])
    for j in range(NH // 2):
        pair = jnp.concatenate([o_t[2 * j], o_t[2 * j + 1]], axis=0)
        o_ref[:, 2 * HD * j:2 * HD * (j + 1)] = pair.T.astype(o_ref.dtype)


def _attn_prompt(q_raw, q_rot, gates, k_cmp, v_cmp, ks, vs, kw, vw, mcs_t, e_mat, n_batch, t):
    nqb = t // Q_BLOCK
    n_seg = t // CMP_STRIDE
    qrow = lambda b, i: (b * nqb + i, 0)
    cmp_spec = pl.BlockSpec((1, KVH, n_seg, HD), lambda b, i: (b, 0, 0, 0))
    kv_spec = pl.BlockSpec((KVH, t, HD), lambda b, i: (0, b, 0))
    kvt_spec = pl.BlockSpec((KVH, HD, t), lambda b, i: (0, 0, b))
    cmpt_spec = pl.BlockSpec((1, KVH, HD, n_seg), lambda b, i: (b, 0, 0, 0))
    return pl.pallas_call(
        _attn_prompt_t_body,
        grid=(n_batch, nqb),
        in_specs=[pl.BlockSpec((Q_BLOCK, DH), qrow), pl.BlockSpec((Q_BLOCK, DH), qrow),
                  pl.BlockSpec((Q_BLOCK, 128), qrow),
                  cmp_spec, cmpt_spec, kv_spec, kvt_spec, kv_spec, kvt_spec,
                  pl.BlockSpec(mcs_t.shape, lambda b, i: (0, 0)),
                  pl.BlockSpec(e_mat.shape, lambda b, i: (0, 0))],
        out_specs=pl.BlockSpec((Q_BLOCK, DH), qrow),
        out_shape=jax.ShapeDtypeStruct((n_batch * t, DH), F32),
        scratch_shapes=[pltpu.VMEM((KVH, N_SEL_LANES, Q_BLOCK), F32)],
        compiler_params=_cparams(2),
        name="nsa_attn_prompt",
    )(q_raw, q_rot, gates, k_cmp, v_cmp, ks, vs, kw, vw, mcs_t, e_mat)


def _pick_kv_group(full):
    hk = lax.broadcasted_iota(jnp.int32, (NH, 1), 0) // GQA
    out = jnp.zeros((NH, HD), F32)
    for k in range(KVH):
        out = out + jnp.where(hk == k, full[:, HD * k:HD * (k + 1)], 0.0)
    return out


def _nsa_sample_body(n_pages, pt_ref, *refs):
    pages = refs[:n_pages]
    (win_ref, qraw_ref, qrot_ref, gates_ref, kvnew_ref, kwnew_ref,
     w1k_ref, w1v_ref, pek_ref, pev_ref, w1kf_ref, w1vf_ref, w2k_ref, w2v_ref, mcs_ref, e_ref,
     o_ref, winout_ref, stage_ref) = refs[n_pages:]
    page = pages[0].shape[1]
    past = n_pages * page
    n_seg = past // CMP_STRIDE

    def bdiag(q):
        qt = jnp.concatenate([q] * KVH, axis=1)
        hk = lax.broadcasted_iota(jnp.int32, (NH, DKV), 0) // GQA
        lk = lax.broadcasted_iota(jnp.int32, (NH, DKV), 1) // HD
        return jnp.where(hk == lk, qt, 0.0)

    qr = bdiag(qraw_ref[0])
    qs = bdiag(qrot_ref[0])
    qr_b, qs_b = qr.astype(BF16), qs.astype(BF16)

    for kk, pg in enumerate(pages):
        for grp in range(4):
            stage_ref[grp, page * kk:page * (kk + 1), :] = pg[0, :, 128 * grp:128 * (grp + 1)]
    accs = _compress_accumulate(stage_ref, (w1k_ref, w1v_ref), n_seg)
    cmp_kv = []
    for kv, (pe_ref, w1f_ref, w2_ref) in enumerate(((pek_ref, w1kf_ref, w2k_ref), (pev_ref, w1vf_ref, w2v_ref))):
        pe0 = _dot(pe_ref[...], w1f_ref[...])
        w2 = w2_ref[...]
        cmp_kv.append(jnp.concatenate(
            [_compress_finish(accs[kv][k], pe0, w2, n_seg) for k in range(KVH)], axis=1).astype(BF16))
    k_cmp, v_cmp = cmp_kv

    n_idx = lax.broadcasted_iota(jnp.int32, (1, n_seg), 1)
    ok_cmp = (n_idx * CMP_STRIDE + (CMP_BLOCK - 1) <= past) & (n_idx < n_seg - 1)
    p_cmp = _softmax_rows(_dot_nt(qr_b, k_cmp), ok_cmp)
    o_cmp = _pick_kv_group(_dot(p_cmp.astype(BF16), v_cmp))
    p_sum = jnp.concatenate(
        [jnp.sum(p_cmp[GQA * k:GQA * (k + 1)], axis=0, keepdims=True) for k in range(KVH)], axis=0)
    imp = _dot_split3(p_sum, mcs_ref[...])
    n_blocks = past // SEL_BLOCK + 1
    cur = jnp.full((KVH, 1), past // SEL_BLOCK, jnp.int32)
    sel = _select_blocks(imp, cur, n_blocks)
    sel16 = jnp.concatenate([jnp.broadcast_to(sel[k:k + 1], (GQA, N_SEL_LANES)) for k in range(KVH)], axis=0)

    kv_new = kvnew_ref[0]
    ks_new, vs_new = kv_new[:, 2 * DKV:3 * DKV], kv_new[:, 3 * DKV:4 * DKV]
    k_sel = jnp.concatenate([pg[0, :, 2 * DKV:3 * DKV] for pg in pages], axis=0).astype(BF16)
    v_sel = jnp.concatenate([pg[0, :, 3 * DKV:4 * DKV] for pg in pages], axis=0).astype(BF16)
    s_c = _dot_nt(qs_b, k_sel)
    s_n = jnp.sum(qs * ks_new, axis=-1, keepdims=True)
    ok_c = _dot(sel16.astype(BF16), e_ref[...]) > 0.5
    new_blk = past // SEL_BLOCK
    ok_n = sel16[:, new_blk:new_blk + 1] > 0.5
    sm_c = jnp.where(ok_c, s_c, NEG)
    sm_n = jnp.where(ok_n, s_n, NEG)
    m = jnp.maximum(jnp.max(sm_c, axis=-1, keepdims=True), sm_n)
    e_c = jnp.where(ok_c, jnp.exp(sm_c - m), 0.0)
    e_n = jnp.where(ok_n, jnp.exp(sm_n - m), 0.0)
    inv = 1.0 / jnp.maximum(jnp.sum(e_c, axis=-1, keepdims=True) + e_n, 1e-30)
    o_sel = _pick_kv_group(_dot((e_c * inv).astype(BF16), v_sel)
                           + (e_n * inv).astype(BF16).astype(F32) * vs_new.astype(BF16).astype(F32))

    kw_new = kwnew_ref[0]
    kwn, vwn = kw_new[:, 0:DKV], kw_new[:, DKV:2 * DKV]
    s_w = _dot_nt(qs_b, win_ref[0, :, 0:DKV].astype(BF16))
    s_wn = jnp.sum(qs * kwn, axis=-1, keepdims=True)
    mw = jnp.maximum(jnp.max(s_w, axis=-1, keepdims=True), s_wn)
    e_w = jnp.exp(s_w - mw)
    e_wn = jnp.exp(s_wn - mw)
    inv_w = 1.0 / (jnp.sum(e_w, axis=-1, keepdims=True) + e_wn)
    o_win = _pick_kv_group(_dot((e_w * inv_w).astype(BF16), win_ref[0, :, DKV:2 * DKV].astype(BF16))
                           + (e_wn * inv_w).astype(BF16).astype(F32) * vwn.astype(BF16).astype(F32))

    g = gates_ref[0]
    o_ref[0] = g[:, 0:1] * o_cmp + g[:, 1:2] * o_sel + g[:, 2:3] * o_win
    wb = win_ref.shape[1]
    winout_ref[0, 0:wb - 1, :] = win_ref[0, 1:wb, :]
    winout_ref[0, wb - 1:wb, :] = kw_new


def _nsa_sample(cache3, page_table, state_win, q_raw, q_rot, gates, kv_new, kw_new, cmp_k, cmp_v, mcs, e_mat):
    nb, n_pages = page_table.shape
    page = cache3.shape[1]
    wb = state_win.shape[1]
    w1k, pek, w1kf, w2k = cmp_k
    w1v, pev, w1vf, w2v = cmp_v
    consts = [w1k, w1v, pek, pev, w1kf, w1vf, w2k, w2v, mcs, e_mat]
    full = lambda a: pl.BlockSpec(a.shape, lambda b, pt: (0,) * a.ndim)
    page_specs = [pl.BlockSpec((1, page, cache3.shape[2]), functools.partial(
        lambda b, pt, kk: (pt[b * n_pages + kk], 0, 0), kk=kk)) for kk in range(n_pages)]
    per_b = lambda shape: pl.BlockSpec((1,) + shape, lambda b, pt: (b, 0, 0))
    grid_spec = pltpu.PrefetchScalarGridSpec(
        num_scalar_prefetch=1,
        grid=(nb,),
        in_specs=page_specs + [per_b((wb, 2 * DKV)), per_b((NH, HD)), per_b((NH, HD)), per_b((NH, 3)),
                               per_b((1, 4 * DKV)), per_b((1, 2 * DKV))] + [full(a) for a in consts],
        out_specs=[per_b((NH, HD)), per_b((wb, 2 * DKV))],
        scratch_shapes=[pltpu.VMEM((4, n_pages * page, 128), F32)],
    )
    return pl.pallas_call(
        functools.partial(_nsa_sample_body, n_pages),
        grid_spec=grid_spec,
        out_shape=[jax.ShapeDtypeStruct((nb, NH, HD), F32), jax.ShapeDtypeStruct((nb, wb, 2 * DKV), F32)],
        compiler_params=_cparams(1),
        name="nsa_sample",
    )(page_table.reshape(-1), *([cache3] * n_pages), state_win, q_raw, q_rot, gates, kv_new, kw_new, *consts)


def _kv_block_diag(q):
    qt = jnp.concatenate([q] * KVH, axis=1)
    hk = lax.broadcasted_iota(jnp.int32, (NH, DKV), 0) // GQA
    lk = lax.broadcasted_iota(jnp.int32, (NH, DKV), 1) // HD
    return jnp.where(hk == lk, qt, 0.0)


def _nsa_sample_t_body(n_pages, n_seq, pt_ref, *refs):
    all_pages = refs[:n_seq * n_pages]
    (win_ref, qraw_ref, qrot_ref, gates_ref, kvnew_ref, kwnew_ref, kwvwt_ref, perm_ref,
     w1k_ref, w1v_ref, pek_ref, pev_ref, w1kf_ref, w1vf_ref, w2k_ref, w2v_ref, mcs_ref, e_ref,
     o_ref, winout_ref, stage_ref) = refs[n_seq * n_pages:]
    page = all_pages[0].shape[3]
    n_seg = n_pages * page // CMP_STRIDE
    seg_pp = page // CMP_STRIDE

    perm = perm_ref[...]
    for kk, pg in enumerate(all_pages):
        for c in range(2):
            xs = _dot_nt(perm, pg[0, c].astype(BF16))
            for s in range(CMP_STRIDE):
                stage_ref[c, s, seg_pp * kk:seg_pp * (kk + 1), :] = xs[seg_pp * s:seg_pp * (s + 1), :]
    accs = [[jnp.zeros((n_seq * n_seg, 2 * CMP_HIDDEN), F32) for _ in range(KVH)] for _ in range(2)]
    for s in range(CMP_STRIDE):
        for kv, w1c_ref in enumerate((w1k_ref, w1v_ref)):
            w = w1c_ref[s]
            x = stage_ref[kv, s].astype(BF16)
            for k in range(KVH):
                accs[kv][k] = accs[kv][k] + _dot(x[:, HD * k:HD * (k + 1)], w)
    for q in range(n_seq):
        one = lambda ref: ref.at[pl.ds(q, 1)]
        _nsa_sample_one(
            all_pages[q * n_pages:(q + 1) * n_pages],
            [[a[n_seg * q:n_seg * (q + 1)] for a in row] for row in accs],
            pl.program_id(0) * n_seq + q,
            one(win_ref), one(qraw_ref), one(qrot_ref), one(gates_ref), one(kvnew_ref), one(kwnew_ref), kwvwt_ref,
            pek_ref, pev_ref, w1kf_ref, w1vf_ref, w2k_ref, w2v_ref, mcs_ref, e_ref, one(o_ref), one(winout_ref))


def _nsa_sample_one(pages, accs, b_idx, win_ref, qraw_ref, qrot_ref, gates_ref, kvnew_ref, kwnew_ref, kwvwt_ref,
                    pek_ref, pev_ref, w1kf_ref, w1vf_ref, w2k_ref, w2v_ref, mcs_ref, e_ref, o_ref, winout_ref):
    page = pages[0].shape[3]
    past = len(pages) * page
    n_seg = past // CMP_STRIDE
    qr = _kv_block_diag(qraw_ref[0])
    qs = _kv_block_diag(qrot_ref[0])
    qr_b, qs_b = qr.astype(BF16), qs.astype(BF16)
    cmp_kv = []
    for kv, (pe_ref, w1f_ref, w2_ref) in enumerate(((pek_ref, w1kf_ref, w2k_ref), (pev_ref, w1vf_ref, w2v_ref))):
        pe0 = _dot(pe_ref[...], w1f_ref[...])
        w2 = w2_ref[...]
        cmp_kv.append(jnp.concatenate(
            [_compress_finish(accs[kv][k], pe0, w2, n_seg) for k in range(KVH)], axis=1).astype(BF16))
    k_cmp, v_cmp = cmp_kv

    n_idx = lax.broadcasted_iota(jnp.int32, (1, n_seg), 1)
    ok_cmp = (n_idx * CMP_STRIDE + (CMP_BLOCK - 1) <= past) & (n_idx < n_seg - 1)
    p_cmp = _softmax_rows(_dot_nt(qr_b, k_cmp), ok_cmp)
    o_cmp = _pick_kv_group(_dot(p_cmp.astype(BF16), v_cmp))
    p_sum = jnp.concatenate(
        [jnp.sum(p_cmp[GQA * k:GQA * (k + 1)], axis=0, keepdims=True) for k in range(KVH)], axis=0)
    imp = _dot_split3(p_sum, mcs_ref[...])
    n_blocks = past // SEL_BLOCK + 1
    cur = jnp.full((KVH, 1), past // SEL_BLOCK, jnp.int32)
    sel = _select_blocks(imp, cur, n_blocks)
    sel16 = jnp.concatenate([jnp.broadcast_to(sel[k:k + 1], (GQA, N_SEL_LANES)) for k in range(KVH)], axis=0)

    kv_new = kvnew_ref[0]
    ks_new, vs_new = kv_new[:, 2 * DKV:3 * DKV], kv_new[:, 3 * DKV:4 * DKV]
    s_c = jnp.concatenate([_dot(qs_b, pg[0, 2].astype(BF16)) for pg in pages], axis=1)
    s_n = jnp.sum(qs * ks_new, axis=-1, keepdims=True)
    ok_c = _dot(sel16.astype(BF16), e_ref[...]) > 0.5
    new_blk = past // SEL_BLOCK
    ok_n = sel16[:, new_blk:new_blk + 1] > 0.5
    sm_c = jnp.where(ok_c, s_c, NEG)
    sm_n = jnp.where(ok_n, s_n, NEG)
    m = jnp.maximum(jnp.max(sm_c, axis=-1, keepdims=True), sm_n)
    e_c = jnp.where(ok_c, jnp.exp(sm_c - m), 0.0)
    e_n = jnp.where(ok_n, jnp.exp(sm_n - m), 0.0)
    inv = 1.0 / jnp.maximum(jnp.sum(e_c, axis=-1, keepdims=True) + e_n, 1e-30)
    p_c = (e_c * inv).astype(BF16)
    o_sel_full = (e_n * inv).astype(BF16).astype(F32) * vs_new.astype(BF16).astype(F32)
    for kk, pg in enumerate(pages):
        o_sel_full = o_sel_full + _dot_nt(p_c[:, page * kk:page * (kk + 1)], pg[0, 3].astype(BF16))
    o_sel = _pick_kv_group(o_sel_full)

    kw_new = kwnew_ref[0]
    kwn, vwn = kw_new[:, 0:DKV], kw_new[:, DKV:2 * DKV]
    s_w = _dot(qs_b, win_ref[0, 0].astype(BF16))
    s_wn = jnp.sum(qs * kwn, axis=-1, keepdims=True)
    mw = jnp.maximum(jnp.max(s_w, axis=-1, keepdims=True), s_wn)
    e_w = jnp.exp(s_w - mw)
    e_wn = jnp.exp(s_wn - mw)
    inv_w = 1.0 / (jnp.sum(e_w, axis=-1, keepdims=True) + e_wn)
    o_win = _pick_kv_group(_dot_nt((e_w * inv_w).astype(BF16), win_ref[0, 1].astype(BF16))
                           + (e_wn * inv_w).astype(BF16).astype(F32) * vwn.astype(BF16).astype(F32))

    g = gates_ref[0]
    o_ref[0] = g[:, 0:1] * o_cmp + g[:, 1:2] * o_sel + g[:, 2:3] * o_win

    wb = win_ref.shape[3]
    nb = kwvwt_ref.shape[1]
    mine = lax.broadcasted_iota(jnp.int32, (1, nb), 1) == b_idx
    last = lax.broadcasted_iota(jnp.int32, (1, wb), 1) == wb - 1
    for c in range(2):
        new_col = jnp.sum(jnp.where(mine, kwvwt_ref[DKV * c:DKV * (c + 1), :], 0.0), axis=-1, keepdims=True)
        winout_ref[0, c] = jnp.where(last, new_col, pltpu.roll(win_ref[0, c], wb - 1, 1))


def _nsa_sample_t(cache4, page_table, win4, q_raw, q_rot, gates, kv_new, kw_new, kwvw_t, cmp_k, cmp_v, mcs, e_mat,
                  n_seq):
    nb, n_pages = page_table.shape
    page = cache4.shape[3]
    wb = win4.shape[3]
    seg_pp = page // CMP_STRIDE
    rows = np.arange(page)
    perm = np.zeros((page, page), np.float32)
    perm[rows, (rows % seg_pp) * CMP_STRIDE + rows // seg_pp] = 1.0
    w1k, pek, w1kf, w2k = cmp_k
    w1v, pev, w1vf, w2v = cmp_v
    consts = [kwvw_t, jnp.asarray(perm, BF16), w1k, w1v, pek, pev, w1kf, w1vf, w2k, w2v, mcs, e_mat]
    full = lambda a: pl.BlockSpec(a.shape, lambda b, pt: (0,) * a.ndim)
    page_specs = [pl.BlockSpec((1, 4, DKV, page), functools.partial(
        lambda b, pt, kk: (pt[b * n_seq * n_pages + kk], 0, 0, 0), kk=kk)) for kk in range(n_seq * n_pages)]
    per_b = lambda shape: pl.BlockSpec((n_seq,) + shape, lambda b, pt: (b,) + (0,) * len(shape))
    grid_spec = pltpu.PrefetchScalarGridSpec(
        num_scalar_prefetch=1,
        grid=(nb // n_seq,),
        in_specs=page_specs + [per_b((2, DKV, wb)), per_b((NH, HD)), per_b((NH, HD)), per_b((NH, 3)),
                               per_b((1, 4 * DKV)), per_b((1, 2 * DKV))] + [full(a) for a in consts],
        out_specs=[per_b((NH, HD)), per_b((2, DKV, wb))],
        scratch_shapes=[pltpu.VMEM((2, CMP_STRIDE, n_seq * n_pages * seg_pp, DKV), F32)],
    )
    return pl.pallas_call(
        functools.partial(_nsa_sample_t_body, n_pages, n_seq),
        grid_spec=grid_spec,
        out_shape=[jax.ShapeDtypeStruct((nb, NH, HD), F32), jax.ShapeDtypeStruct((nb, 2, DKV, wb), F32)],
        compiler_params=_cparams(1),
        name="nsa_sample",
    )(page_table.reshape(-1), *([cache4] * (n_seq * n_pages)), win4, q_raw, q_rot, gates, kv_new, kw_new, *consts)


def _rwkv_prep(cols, prev, mu, w0, w2, a0, a2, g2, kk_p, ka):
    xs = cols + (prev - cols) * mu
    r = xs[:, 0:DH]
    k = xs[:, DH:2 * DH]
    v = xs[:, 2 * DH:3 * DH]
    o = 3 * DH
    wd = xs[:, o:o + LORA_W]
    ad = xs[:, o + LORA_W:o + LORA_W + LORA_A]
    gd = xs[:, o + LORA_W + LORA_A:o + LORA_W + LORA_A + LORA_G_PAD]
    w = w0 + _dot(jnp.tanh(wd).astype(BF16), w2)
    w_log = -jax.nn.softplus(-w) - 0.5
    lw = -jnp.exp(w_log)
    a = _sigmoid(a0 + _dot(ad.astype(BF16), a2))
    g = _dot(_sigmoid(gd).astype(BF16), g2)
    kkv = k * kk_p
    k_mod = k * (1.0 + (a - 1.0) * ka)
    return r, k_mod, v, kkv, a, lw, g


def _head_norm(kkv_h):
    return kkv_h / jnp.maximum(jnp.sqrt(jnp.sum(kkv_h * kkv_h, axis=-1, keepdims=True)), 1e-12)


def _rwkv_head_out(y, r_h, k_h, v_h, g_h, rk_h, gnw_h, gnb_h):
    mean = jnp.mean(y, axis=-1, keepdims=True)
    var = jnp.mean(jnp.square(y - mean), axis=-1, keepdims=True)
    yn = (y - mean) * lax.rsqrt(var + GN_EPS) * gnw_h + gnb_h
    bonus = jnp.sum(r_h * k_h * rk_h, axis=-1, keepdims=True) * v_h
    return (yn + bonus) * g_h


def _tri_inverse_all(a_list, blk16, blk32):
    c = a_list[0].shape[0]
    eye = jnp.where(lax.broadcasted_iota(jnp.int32, (c, c), 0) == lax.broadcasted_iota(jnp.int32, (c, c), 1),
                    1.0, 0.0)
    b = lambda x: x.astype(BF16)
    a16 = [b(jnp.where(blk16, a, 0.0)) for a in a_list]
    p = [_dot(x, x) for x in a16]
    t = [_dot(b(eye) - x, b(eye + q)) for x, q in zip(a16, p)]
    for _ in range(2):
        pb = [b(q) for q in p]
        p = [_dot(q, q) for q in pb]
        t = [_dot(b(x), b(eye + q)) for x, q in zip(t, p)]
    off32 = blk32 & jnp.logical_not(blk16)
    for mask in (off32, jnp.logical_not(blk32)):
        tb = [b(x) for x in t]
        m = [_dot(x, b(jnp.where(mask, a, 0.0))) for x, a in zip(tb, a_list)]
        t = [x - _dot(b(y), xb) for x, y, xb in zip(t, m, tb)]
    return t


def _rwkv_prompt_body(cols_ref, mu_ref, w0_ref, w2_ref, a0_ref, a2_ref, g2_ref, kk_ref, ka_ref, rk_ref,
                      gnw_ref, gnb_ref, o_ref, wkv_ref, s_ref, last_ref):
    c = pl.program_id(1)
    ch = RW_CHUNK

    @pl.when(c == 0)
    def _():
        s_ref[...] = jnp.zeros_like(s_ref)
        last_ref[...] = jnp.zeros_like(last_ref)

    cols = cols_ref[...]
    row = lax.broadcasted_iota(jnp.int32, (ch, 1), 0)
    prev = jnp.where(row == 0, last_ref[...], pltpu.roll(cols, 1, 0))
    last_ref[...] = cols[ch - 1:ch, :]
    r, k_mod, v, kkv, a, lw, g = _rwkv_prep(cols, prev, mu_ref[...], w0_ref[...], w2_ref[...], a0_ref[...],
                                            a2_ref[...], g2_ref[...], kk_ref[...], ka_ref[...])
    ti = lax.broadcasted_iota(jnp.int32, (ch, ch), 0)
    si = lax.broadcasted_iota(jnp.int32, (ch, ch), 1)
    ltri = jnp.where(ti >= si, 1.0, 0.0).astype(BF16)
    cl = _cumsum_rows(lw, ltri)
    e_in = jnp.exp(cl)
    e_ex = jnp.exp(cl - lw)
    e_ng = jnp.exp(-cl)
    cl_end = cl[ch - 1:ch, :]
    e_end = jnp.exp(cl_end - cl)
    g_end = jnp.exp(cl_end)
    strict = ti > si
    causal2 = (lax.broadcasted_iota(jnp.int32, (ch, 2 * ch), 0)
               >= lax.broadcasted_iota(jnp.int32, (ch, 2 * ch), 1) % ch)
    blk16 = (ti // 16) == (si // 16)
    blk32 = (ti // 32) == (si // 32)
    rk, gnw, gnb = rk_ref[...], gnw_ref[...], gnb_ref[...]
    b = lambda x: x.astype(BF16)

    heads = range(NH)
    sls = [slice(HD * h, HD * (h + 1)) for h in heads]

    kk_n = [_head_norm(kkv[:, sl]) for sl in sls]
    beta = [kk_n[h] * a[:, sls[h]] for h in heads]
    lhs = [b(jnp.concatenate([kk_n[h] * e_ex[:, sls[h]], r[:, sls[h]] * e_in[:, sls[h]]], axis=0)) for h in heads]
    rhs = [b(jnp.concatenate([beta[h] * e_ng[:, sls[h]], k_mod[:, sls[h]] * e_ng[:, sls[h]]], axis=0))
           for h in heads]
    dec = [b(jnp.concatenate([beta[h] * e_end[:, sls[h]], k_mod[:, sls[h]] * e_end[:, sls[h]]], axis=0))
           for h in heads]
    s0 = [s_ref[h] for h in heads]
    quad = [_dot_nt(lhs[h], rhs[h]) for h in heads]
    s0t = [_dot_nt(lhs[h], b(s0[h])) for h in heads]
    a_b = [jnp.where(strict, q[0:ch, 0:ch], 0.0) for q in quad]
    a_kv = [_dot(b(jnp.where(strict, quad[h][0:ch, ch:2 * ch], 0.0)), b(v[:, sls[h]])) for h in heads]
    b_bk = [b(jnp.where(causal2, q[ch:2 * ch, :], 0.0)) for q in quad]
    t_inv = _tri_inverse_all(a_b, blk16, blk32)
    u = [-_dot(b(t_inv[h]), b(s0t[h][0:ch] + a_kv[h])) for h in heads]
    uv = [b(jnp.concatenate([u[h], v[:, sls[h]]], axis=0)) for h in heads]
    y = [s0t[h][ch:2 * ch] + _dot(b_bk[h], uv[h]) for h in heads]
    s_new = [s0[h] * g_end[:, sls[h]] + _dot_tn(uv[h], dec[h]) for h in heads]
    for h in heads:
        sl = sls[h]
        s_ref[h] = s_new[h]
        o_h = _rwkv_head_out(y[h], r[:, sl], k_mod[:, sl], v[:, sl], g[:, sl], rk[:, sl], gnw[:, sl], gnb[:, sl])
        o_ref[:, sl] = o_h.astype(o_ref.dtype)

    @pl.when(c == pl.num_programs(1) - 1)
    def _():
        wkv_ref[0] = s_ref[...]


def _cumsum_rows(x, ltri):
    hi = x.astype(BF16)
    r1 = x - hi.astype(F32)
    mid = r1.astype(BF16)
    lo = (r1 - mid.astype(F32)).astype(BF16)
    return _dot(ltri, hi) + _dot(ltri, mid) + _dot(ltri, lo)


def _rwkv_prompt(cols, rw, n_batch, t):
    nc = t // RW_CHUNK
    full = lambda a: pl.BlockSpec(a.shape, lambda b, c: (0,) * a.ndim)
    row = lambda b, c: (b * nc + c, 0)
    return pl.pallas_call(
        _rwkv_prompt_body,
        grid=(n_batch, nc),
        in_specs=[pl.BlockSpec((RW_CHUNK, RWKV_PAD), row)] + [full(a) for a in rw],
        out_specs=[pl.BlockSpec((RW_CHUNK, DH), row),
                   pl.BlockSpec((1, NH, HD, HD), lambda b, c: (b, 0, 0, 0))],
        out_shape=[jax.ShapeDtypeStruct((n_batch * t, DH), F32),
                   jax.ShapeDtypeStruct((n_batch, NH, HD, HD), F32)],
        scratch_shapes=[pltpu.VMEM((NH, HD, HD), F32), pltpu.VMEM((1, RWKV_PAD), F32)],
        compiler_params=_cparams(2),
        name="rwkv_prompt",
    )(cols, *rw)


def _rwkv_sample_prep_body(cols_ref, prev_ref, mu_ref, w0_ref, w2_ref, a0_ref, a2_ref, g2_ref, kk_ref, ka_ref,
                           r_ref, k_ref, v_ref, kkn_ref, a_ref, d_ref, g_ref):
    r, k_mod, v, kkv, a, lw, g = _rwkv_prep(cols_ref[...], prev_ref[...], mu_ref[...], w0_ref[...], w2_ref[...],
                                            a0_ref[...], a2_ref[...], g2_ref[...], kk_ref[...], ka_ref[...])
    r_ref[...] = r
    k_ref[...] = k_mod
    v_ref[...] = v
    a_ref[...] = a
    d_ref[...] = jnp.exp(lw)
    g_ref[...] = g
    for h in range(NH):
        sl = slice(HD * h, HD * (h + 1))
        kkn_ref[:, sl] = _head_norm(kkv[:, sl])


def _rwkv_sample_prep(cols, prev, rw_prep):
    n = cols.shape[0]
    args = [cols, prev] + list(rw_prep)
    full = lambda a: pl.BlockSpec(a.shape, lambda i: (0,) * a.ndim)
    out = jax.ShapeDtypeStruct((n, DH), F32)
    return pl.pallas_call(
        _rwkv_sample_prep_body,
        grid=(1,),
        in_specs=[full(a) for a in args],
        out_specs=[pl.BlockSpec((n, DH), lambda i: (0, 0))] * 7,
        out_shape=[out] * 7,
        compiler_params=_cparams(1),
        name="rwkv_sample_prep",
    )(*args)


def _rwkv_sample_step_body(s_ref, r_ref, k_ref, v_ref, kk_ref, a_ref, d_ref, g_ref, rk_ref, gnw_ref, gnb_ref,
                           o_ref, sout_ref):
    s = s_ref[0]
    r, k, v, kk, a, d, g = (x[0] for x in (r_ref, k_ref, v_ref, kk_ref, a_ref, d_ref, g_ref))
    eye = jnp.where(lax.broadcasted_iota(jnp.int32, (HD, HD), 0) == lax.broadcasted_iota(jnp.int32, (HD, HD), 1),
                    1.0, 0.0)
    row = lambda x: x[:, None, :]
    col = lambda x: jnp.sum(eye[None] * x[:, None, :], axis=-1, keepdims=True)
    sa = -jnp.sum(s * row(kk), axis=-1, keepdims=True)
    s_new = s * row(d) + sa * row(kk * a) + col(v) * row(k)
    sout_ref[0] = s_new
    y_col = jnp.sum(s_new * row(r), axis=-1, keepdims=True)
    y = jnp.sum(y_col * eye[None], axis=1)
    o_ref[0] = _rwkv_head_out(y, r, k, v, g, rk_ref[...], gnw_ref[...], gnb_ref[...])


def _rwkv_sample_step(state, vecs, rk, gnw, gnb):
    nb = state.shape[0]
    per_b3 = pl.BlockSpec((1, NH, HD), lambda b: (b, 0, 0))
    per_b4 = pl.BlockSpec((1, NH, HD, HD), lambda b: (b, 0, 0, 0))
    full = pl.BlockSpec((NH, HD), lambda b: (0, 0))
    return pl.pallas_call(
        _rwkv_sample_step_body,
        grid=(nb,),
        in_specs=[per_b4] + [per_b3] * 7 + [full] * 3,
        out_specs=[per_b3, per_b4],
        out_shape=[jax.ShapeDtypeStruct((nb, NH, HD), F32), jax.ShapeDtypeStruct((nb, NH, HD, HD), F32)],
        compiler_params=_cparams(1),
        name="rwkv_sample_step",
    )(state, *vecs, rk, gnw, gnb)


def _rwkv_sample_prep_t_body(cols_ref, kv_ref, kwvw_ref, prev_ref, mu_ref, w0_ref, w2t_ref, a0_ref, a2t_ref,
                             g2t_ref, kk_ref, ka_ref,
                             colst_ref, kvt_ref, kwvwt_ref, r_ref, k_ref, v_ref, kkn_ref, kka_ref, d_ref, g_ref,
                             tr_ref):
    nb = cols_ref.shape[0]
    for j in range(cols_ref.shape[1] // 128):
        tr_ref[128 * j:128 * (j + 1), :] = cols_ref[:, 128 * j:128 * (j + 1)].T
    for j in range(kv_ref.shape[1] // 128):
        kvt_ref[128 * j:128 * (j + 1), :] = kv_ref[:, 128 * j:128 * (j + 1)].T
    for j in range(kwvw_ref.shape[1] // 128):
        kwvwt_ref[128 * j:128 * (j + 1), :] = kwvw_ref[:, 128 * j:128 * (j + 1)].T
    cols = tr_ref[0:RWKV_COLS, :]
    colst_ref[...] = cols
    xs = cols + (prev_ref[...] - cols) * mu_ref[...]
    r = xs[0:DH]
    k = xs[DH:2 * DH]
    v = xs[2 * DH:3 * DH]
    o = 3 * DH
    wd = xs[o:o + LORA_W]
    ad = xs[o + LORA_W:o + LORA_W + LORA_A]
    gd = xs[o + LORA_W + LORA_A:RWKV_COLS]
    w = w0_ref[...] + _dot(w2t_ref[...], jnp.tanh(wd).astype(BF16))
    w_log = -jax.nn.softplus(-w) - 0.5
    a = _sigmoid(a0_ref[...] + _dot(a2t_ref[...], ad.astype(BF16)))
    kkv = (k * kk_ref[...]).reshape(NH, HD, nb)
    norm = jnp.maximum(jnp.sqrt(jnp.sum(kkv * kkv, axis=1, keepdims=True)), 1e-12)
    kkn = (kkv / norm).reshape(DH, nb)
    r_ref[...] = r
    k_ref[...] = k * (1.0 + (a - 1.0) * ka_ref[...])
    v_ref[...] = v
    kkn_ref[...] = kkn
    kka_ref[...] = kkn * a
    d_ref[...] = jnp.exp(-jnp.exp(w_log))
    g_ref[...] = _dot(g2t_ref[...], _sigmoid(gd).astype(BF16))


def _rwkv_sample_prep_t(cols, kv_rows, kwvw, prev_t, params_t):
    nb = cols.shape[0]
    args = [cols, kv_rows, kwvw, prev_t] + list(params_t)
    full = lambda shape: pl.BlockSpec(shape, lambda i: (0,) * len(shape))
    out_rows = [RWKV_COLS, kv_rows.shape[1], kwvw.shape[1]] + [DH] * 7
    return pl.pallas_call(
        _rwkv_sample_prep_t_body,
        grid=(1,),
        in_specs=[full(a.shape) for a in args],
        out_specs=[full((rows, nb)) for rows in out_rows],
        out_shape=[jax.ShapeDtypeStruct((rows, nb), F32) for rows in out_rows],
        scratch_shapes=[pltpu.VMEM((cols.shape[1], nb), F32)],
        compiler_params=_cparams(1),
        name="rwkv_sample_prep",
    )(*args)


def _rwkv_sample_step_t_body(s_ref, r_ref, k_ref, v_ref, kkn_ref, kka_ref, d_ref, g_ref, rk_ref, gnw_ref, gnb_ref,
                             o_ref, sout_ref, y_ref):
    r, k, kkn, kka, d = r_ref[...], k_ref[...], kkn_ref[...], kka_ref[...], d_ref[...]

    def value_row(i, carry):
        s_i = s_ref[0, i]
        sa = -jnp.sum(s_i * kkn, axis=0, keepdims=True)
        s_new = s_i * d + sa * kka + v_ref[pl.ds(i, 1), :] * k
        sout_ref[0, i] = s_new
        y_ref[pl.ds(i, 1), :] = jnp.sum(s_new * r, axis=0, keepdims=True)
        return carry

    lax.fori_loop(0, HD, value_row, 0)
    y = y_ref[...]
    v = v_ref[...]
    mean = jnp.mean(y, axis=0, keepdims=True)
    var = jnp.mean(jnp.square(y - mean), axis=0, keepdims=True)
    yn = (y - mean) * lax.rsqrt(var + GN_EPS) * gnw_ref[...] + gnb_ref[...]
    bonus = jnp.sum(r * k * rk_ref[...], axis=0, keepdims=True) * v
    o_ref[...] = (yn + bonus) * g_ref[...]


def _rwkv_sample_step_t(state4, vecs_t, rk_t, gnw_t, gnb_t):
    nh, hs, _, nb = state4.shape
    head_rows = pl.BlockSpec((hs, nb), lambda h: (h, 0))
    state_spec = pl.BlockSpec((1, hs, hs, nb), lambda h: (h, 0, 0, 0))
    return pl.pallas_call(
        _rwkv_sample_step_t_body,
        grid=(nh,),
        in_specs=[state_spec] + [head_rows] * 10,
        out_specs=[head_rows, state_spec],
        out_shape=[jax.ShapeDtypeStruct((nh * hs, nb), F32), jax.ShapeDtypeStruct(state4.shape, F32)],
        scratch_shapes=[pltpu.VMEM((hs, nb), F32)],
        compiler_params=_cparams(1),
        name="rwkv_sample_step",
    )(state4, *vecs_t, rk_t, gnw_t, gnb_t)


def _outproj_t_body(oa_ref, orwt_ref, wa_ref, wr_ref, h_ref, g_ref, o_ref):
    mixed = _dot(oa_ref[...].astype(BF16), wa_ref[...]) + _dot_tn(orwt_ref[...].astype(BF16), wr_ref[...])
    o_ref[...] = h_ref[...] + _rms(mixed, g_ref[...])


def _outproj_t(o_attn, o_rwkv_t, w_a, w_r, h, g_post):
    n, d = h.shape
    full = lambda a: pl.BlockSpec(a.shape, lambda i: (0,) * a.ndim)
    args = [o_attn, o_rwkv_t, w_a, w_r, h, g_post]
    return pl.pallas_call(
        _outproj_t_body,
        grid=(1,),
        in_specs=[full(a) for a in args],
        out_specs=pl.BlockSpec((n, d), lambda i: (0, 0)),
        out_shape=jax.ShapeDtypeStruct((n, d), F32),
        compiler_params=_cparams(1),
        name="outproj_sample",
    )(*args)


def _outproj_body(oa_ref, orw_ref, wa_ref, wr_ref, h_ref, g_ref, o_ref):
    mixed = _dot(oa_ref[...].astype(BF16), wa_ref[...]) + _dot(orw_ref[...].astype(BF16), wr_ref[...])
    o_ref[...] = h_ref[...] + _rms(mixed, g_ref[...])


def _outproj(o_attn, o_rwkv, w_a, w_r, h, g_post, tm):
    n, d = h.shape
    row = lambda i: (i, 0)
    const = lambda i: (0, 0)
    return pl.pallas_call(
        _outproj_body,
        grid=(n // tm,),
        in_specs=[pl.BlockSpec((tm, DH), row), pl.BlockSpec((tm, DH), row),
                  pl.BlockSpec((DH, d), const), pl.BlockSpec((DH, d), const),
                  pl.BlockSpec((tm, d), row), pl.BlockSpec((1, d), const)],
        out_specs=pl.BlockSpec((tm, d), row),
        out_shape=jax.ShapeDtypeStruct((n, d), F32),
        compiler_params=_cparams(1),
        name="outproj",
    )(o_attn, o_rwkv, w_a, w_r, h, g_post)


def _rope_tables(pos):
    inv = ROPE_THETA ** (-jnp.arange(ROPE_HALF, dtype=F32) / ROPE_HALF)
    ang = pos.astype(F32)[:, None] * inv[None, :]
    cos, sin = jnp.cos(ang), jnp.sin(ang)
    n = pos.shape[0]
    rest = HD - 2 * ROPE_HALF
    c = jnp.concatenate([cos, cos, jnp.ones((n, rest), F32)], axis=1)
    s_lo = jnp.concatenate([-sin, jnp.zeros((n, HD - ROPE_HALF), F32)], axis=1)
    s_hi = jnp.concatenate([jnp.zeros((n, ROPE_HALF), F32), sin, jnp.zeros((n, rest), F32)], axis=1)
    two = lambda x: jnp.concatenate([x, x], axis=1)
    return two(c), two(s_lo), two(s_hi)


def _cmp_to_sel(n_cmp, n_sel, rows):
    i = np.arange(n_cmp)[:, None] * CMP_STRIDE
    j = np.arange(n_sel)[None, :] * SEL_BLOCK
    ov = np.minimum(i + CMP_BLOCK, j + SEL_BLOCK) - np.maximum(i, j)
    m = np.zeros((rows, N_SEL_LANES), np.float32)
    m[:n_cmp, :n_sel] = np.maximum(ov, 0) // CMP_STRIDE
    return jnp.asarray(m, BF16)


def _block_expand(n_keys):
    j = np.arange(N_SEL_LANES)[:, None]
    t = np.arange(n_keys)[None, :]
    return jnp.asarray((t // SEL_BLOCK == j).astype(np.float32), BF16)


def _block_expand_t(n_keys):
    t = np.arange(n_keys)[:, None]
    j = np.arange(N_SEL_LANES)[None, :]
    return jnp.asarray((t // SEL_BLOCK == j).astype(np.float32), BF16)


def _pad_cols(x, width):
    return jnp.pad(x, ((0, 0), (0, width - x.shape[1])))


def _row_tile(n, cap):
    tm = min(n, cap)
    while n % tm:
        tm //= 2
    return tm


def kernel(x_prompt, x_sample, cache_nsa, page_table, state_win, state_wkv, state_shift, norm_f1_pre, norm_f1_post, ffn1_gu, ffn1_dn, norm_mix_pre, w_in, cmp_pe_k, cmp_w1_k, cmp_w2_k, cmp_pe_v, cmp_w1_v, cmp_w2_v, rw_mu, rw_w0, rw_w2, rw_a0, rw_a2, rw_g2, rw_kk, rw_ka, rw_rk, rw_gn_w, rw_gn_b, w_out, norm_mix_post, norm_f2_pre, ffn2_gu, ffn2_dn, norm_f2_post):
    depth = w_in.shape[0]
    assert depth == 1, "single-layer step"
    bp, t, d = x_prompt.shape
    bs, ts, _ = x_sample.shape
    assert ts == 1 and t % Q_BLOCK == 0 and t % RW_CHUNK == 0 and t >= WINDOW
    n_pages = page_table.shape[1]
    page = cache_nsa.shape[2]
    past = n_pages * page
    wb = state_win.shape[2]
    assert wb == min(WINDOW, past) and past % SEL_BLOCK == 0 and page % CMP_STRIDE == 0
    l = 0

    w1g, w1d = ffn1_gu[l].astype(BF16), ffn1_dn[l].astype(BF16)
    w2g, w2d = ffn2_gu[l].astype(BF16), ffn2_dn[l].astype(BF16)
    wi = w_in[l]
    w_nsa = _pad_cols(wi[:, :NSA_COLS], NSA_PAD).astype(BF16)
    w_rw = _pad_cols(wi[:, NSA_COLS:], RWKV_PAD).astype(BF16)
    w_oa, w_or = w_out[l, :DH].astype(BF16), w_out[l, DH:].astype(BF16)
    vec = lambda a: a.reshape(1, -1)
    cmp_k = _cmp_weights(cmp_pe_k[l], cmp_w1_k[l], cmp_w2_k[l])
    cmp_v = _cmp_weights(cmp_pe_v[l], cmp_w1_v[l], cmp_w2_v[l])
    g2_pad = jnp.pad(rw_g2[l], ((0, LORA_G_PAD - LORA_G), (0, 0))).astype(BF16)
    rw_prep = [_pad_cols(vec(rw_mu[l]), RWKV_PAD), vec(rw_w0[l]), rw_w2[l].astype(BF16), vec(rw_a0[l]),
               rw_a2[l].astype(BF16), g2_pad, vec(rw_kk[l]), vec(rw_ka[l])]
    rk_row, gnw_row, gnb_row = vec(rw_rk[l]), vec(rw_gn_w[l]), vec(rw_gn_b[l])

    outs = {}
    for name, x2, n_batch in (("p", x_prompt.reshape(bp * t, d), bp), ("s", x_sample.reshape(bs, d), bs)):
        n = x2.shape[0]
        tm = _row_tile(n, 512)
        is_prompt = name == "p"
        h1, hn = _ffn_half(x2, vec(norm_f1_pre[l]), w1g, w1d, vec(norm_f1_post[l]), vec(norm_mix_pre[l]),
                           tm, 512)
        pos = jnp.tile(jnp.arange(t), bp) if is_prompt else jnp.full((n,), past, jnp.int32)
        rc, rlo, rhi = _rope_tables(pos)
        q_raw, q_rot, kv_rows, kwvw, ks, vs, kw, vw, gates, *kv_t = _inproj_nsa(
            hn, w_nsa, rc, rlo, rhi, tm, t if is_prompt else None)
        cols = _inproj_rwkv(hn, w_rw, tm)
        if is_prompt:
            k_cmp, v_cmp = _compress_prompt(kv_rows, cmp_k, cmp_v, n_batch, t)
            n_seg = t // CMP_STRIDE
            mcs_t = _cmp_to_sel(n_seg - 1, t // SEL_BLOCK, n_seg).T
            o_attn = _attn_prompt(q_raw, q_rot, gates, k_cmp, v_cmp, ks, vs, kw, vw, mcs_t, _block_expand_t(t),
                                  n_batch, t)
            o_rwkv, wkv = _rwkv_prompt(cols, rw_prep + [rk_row, gnw_row, gnb_row], n_batch, t)
            win = kwvw.reshape(n_batch, t, 2 * DKV)[:, t - wb:]
            kv_out = jnp.transpose(kv_t[0].reshape(n_batch, 4, KVH, HD, t), (0, 4, 1, 2, 3))
            shift = cols.reshape(n_batch, t, RWKV_PAD)[:, t - 1:, :RWKV_COLS]
        else:
            cache4 = jnp.transpose(cache_nsa[l], (0, 2, 3, 4, 1)).reshape(cache_nsa.shape[1], 4, DKV, page)
            win4 = jnp.transpose(state_win[l], (0, 2, 3, 4, 1)).reshape(n, 2, DKV, wb)
            wkv4 = jnp.transpose(state_wkv[l], (1, 2, 3, 0))
            prev_t = jnp.transpose(state_shift[l].reshape(n, RWKV_COLS))
            lanes = lambda a: jnp.broadcast_to(a.reshape(-1, 1), (a.size, n))
            params_t = [lanes(rw_mu[l]), lanes(rw_w0[l]), rw_w2[l].T.astype(BF16), lanes(rw_a0[l]),
                        rw_a2[l].T.astype(BF16), rw_g2[l].T.astype(BF16), lanes(rw_kk[l]), lanes(rw_ka[l])]
            cols_t, kv_t, kwvw_t, *vecs_t = _rwkv_sample_prep_t(cols, kv_rows, kwvw, prev_t, params_t)
            n_seg = past // CMP_STRIDE
            mcs = _cmp_to_sel(n_seg - 1, past // SEL_BLOCK + 1, n_seg)
            o_attn, win4_new = _nsa_sample_t(
                cache4, page_table, win4,
                q_raw.astype(F32).reshape(n, NH, HD), q_rot.astype(F32).reshape(n, NH, HD),
                gates[:, :3 * NH].reshape(n, NH, 3), kv_rows.reshape(n, 1, 4 * DKV), kwvw.reshape(n, 1, 2 * DKV),
                kwvw_t, cmp_k, cmp_v, mcs, _block_expand(past), SAMPLE_SEQS_PER_STEP)
            o_attn = o_attn.reshape(n, DH)
            o_rwkv_t, wkv4_new = _rwkv_sample_step_t(wkv4, vecs_t, lanes(rw_rk[l]), lanes(rw_gn_w[l]),
                                                     lanes(rw_gn_b[l]))
            kv_out = jnp.transpose(kv_t.reshape(4, KVH, HD, n), (3, 0, 1, 2))
            win = jnp.transpose(win4_new.reshape(n, 2, KVH, HD, wb), (0, 4, 1, 2, 3))
            wkv = jnp.transpose(wkv4_new, (3, 0, 1, 2))
            shift = jnp.transpose(cols_t)[:, None, :]
            h2 = _outproj_t(o_attn, o_rwkv_t, w_oa, w_or, h1, vec(norm_mix_post[l]))
        if is_prompt:
            h2 = _outproj(o_attn, o_rwkv, w_oa, w_or, h1, vec(norm_mix_post[l]), tm)
        y = _ffn_half(h2, vec(norm_f2_pre[l]), w2g, w2d, vec(norm_f2_post[l]), None, tm, 512)
        outs[name] = (y, kv_out, win, wkv, shift)

    yp, kvp, winp, wkvp, shp = outs["p"]
    ys, kvs, wins, wkvs, shs = outs["s"]
    return (yp.reshape(bp, t, d), ys.reshape(bs, 1, d),
            kvp.reshape(1, bp, t, 4, KVH, HD), kvs.reshape(1, bs, 1, 4, KVH, HD),
            winp.reshape(1, bp, wb, 2, KVH, HD), wins.reshape(1, bs, wb, 2, KVH, HD),
            wkvp[None], wkvs[None], shp[None], shs[None])
```

```python
import functools

import numpy as np
import jax
import jax.numpy as jnp
from jax import lax
from jax.experimental import pallas as pl
from jax.experimental.pallas import tpu as pltpu

F32 = jnp.float32
BF16 = jnp.bfloat16

HD = 64
KVH = 4
GQA = 4
NH = 16
DH = NH * HD
DKV = KVH * HD
ROPE_HALF = 8
ROPE_THETA = 500000.0
CMP_BLOCK = 32
CMP_STRIDE = 16
CMP_HIDDEN = 128
SEL_BLOCK = 64
SEL_TOPN = 16
N_LOCAL_FORCED = 2
WINDOW = 512
Q_BLOCK = 128
LORA_W = 64
LORA_A = 64
LORA_G = 160
RMS_EPS = 1e-6
GN_EPS = 64e-5
NEG = -1e30
FORCE_BONUS = 1e4
LOG2_E = 1.4426950408889634
MASK_BIAS = -1e30
M_FLOOR = -1e20
NSA_COLS = DH + 6 * DKV + 3 * NH
RWKV_COLS = 3 * DH + LORA_W + LORA_A + LORA_G
NSA_PAD = 2688
RWKV_PAD = 3456
LORA_G_PAD = 256
RW_CHUNK = 64
SEL_KEY_CHUNK = 512
N_SEL_LANES = 64
SAMPLE_SEQS_PER_STEP = 2

VMEM_LIMIT_BYTES = 56 * 1024 * 1024


def _cparams(n_axes):
    return pltpu.CompilerParams(dimension_semantics=("arbitrary",) * n_axes,
                                vmem_limit_bytes=VMEM_LIMIT_BYTES)


def _dot(a, b):
    return jnp.dot(a, b, preferred_element_type=F32)


def _dot_nt(a, b):
    return lax.dot_general(a, b, (((1,), (1,)), ((), ())), preferred_element_type=F32)


def _dot_tn(a, b):
    return lax.dot_general(a, b, (((0,), (0,)), ((), ())), preferred_element_type=F32)


def _dot_split3(a, b_bf16):
    hi = a.astype(BF16)
    r1 = a - hi.astype(F32)
    mid = r1.astype(BF16)
    lo = (r1 - mid.astype(F32)).astype(BF16)
    return _dot(hi, b_bf16) + _dot(mid, b_bf16) + _dot(lo, b_bf16)


def _rms(x, g):
    ms = jnp.mean(x * x, axis=-1, keepdims=True)
    return x * lax.rsqrt(ms + RMS_EPS) * g


def _sigmoid(x):
    return jax.nn.sigmoid(x)


def _ffn_body(has_next, x_ref, gpre_ref, wg_ref, wu_ref, wd_ref, gpost_ref, *rest):
    if has_next:
        gnext_ref, o_ref, on_ref, xn_ref, acc_ref = rest
    else:
        o_ref, xn_ref, acc_ref = rest
    j = pl.program_id(1)

    @pl.when(j == 0)
    def _():
        xn_ref[...] = _rms(x_ref[...], gpre_ref[...]).astype(BF16)
        acc_ref[...] = jnp.zeros_like(acc_ref)

    xn = xn_ref[...]
    g = _dot(xn, wg_ref[...])
    u = _dot(xn, wu_ref[...])
    act = ((g * _sigmoid(g)) * u).astype(BF16)
    acc_ref[...] += _dot(act, wd_ref[...])

    @pl.when(j == pl.num_programs(1) - 1)
    def _():
        y = x_ref[...] + 0.5 * _rms(acc_ref[...], gpost_ref[...])
        o_ref[...] = y
        if has_next:
            on_ref[...] = _rms(y, gnext_ref[...]).astype(BF16)


def _ffn_half(x, g_pre, w_gu, w_dn, g_post, g_next, tm, tf):
    n, d = x.shape
    f = w_dn.shape[0]
    nj = f // tf
    has_next = g_next is not None
    row = lambda i, j: (i, 0)
    const = lambda i, j: (0, 0)
    in_specs = [
        pl.BlockSpec((tm, d), row),
        pl.BlockSpec((1, d), const),
        pl.BlockSpec((d, tf), lambda i, j: (0, j)),
        pl.BlockSpec((d, tf), lambda i, j: (0, j + nj)),
        pl.BlockSpec((tf, d), lambda i, j: (j, 0)),
        pl.BlockSpec((1, d), const),
    ]
    args = [x, g_pre, w_gu, w_gu, w_dn, g_post]
    out_shape = [jax.ShapeDtypeStruct((n, d), F32)]
    out_specs = [pl.BlockSpec((tm, d), row)]
    if has_next:
        in_specs.append(pl.BlockSpec((1, d), const))
        args.append(g_next)
        out_shape.append(jax.ShapeDtypeStruct((n, d), BF16))
        out_specs.append(pl.BlockSpec((tm, d), row))
    res = pl.pallas_call(
        functools.partial(_ffn_body, has_next),
        grid=(n // tm, nj),
        in_specs=in_specs,
        out_specs=out_specs,
        out_shape=out_shape,
        scratch_shapes=[pltpu.VMEM((tm, d), BF16), pltpu.VMEM((tm, d), F32)],
        compiler_params=_cparams(2),
        name="ffn_half",
    )(*args)
    return res if has_next else res[0]


def _rope(x, c, s_lo, s_hi):
    w = x.shape[1]
    reps = w // 128
    tile = lambda t: t if reps == 1 else jnp.concatenate([t] * reps, axis=1)
    up = pltpu.roll(x, w - ROPE_HALF, 1)
    dn = pltpu.roll(x, ROPE_HALF, 1)
    return x * tile(c) + up * tile(s_lo) + dn * tile(s_hi)


def _inproj_nsa_body(hn_ref, w_ref, c_ref, slo_ref, shi_ref,
                     qraw_ref, qrot_ref, kv_ref, kwvw_ref, ks_ref, vs_ref, kw_ref, vw_ref, gates_ref, *kvt_ref):
    p = _dot(hn_ref[...], w_ref[...])
    c, s_lo, s_hi = c_ref[...], slo_ref[...], shi_ref[...]
    q = p[:, 0:DH] * (HD ** -0.5 * LOG2_E)
    qraw_ref[...] = q.astype(BF16)
    qrot_ref[...] = _rope(q, c, s_lo, s_hi).astype(BF16)
    o = DH
    ks = _rope(p[:, o + 2 * DKV:o + 3 * DKV], c, s_lo, s_hi)
    vs = p[:, o + 3 * DKV:o + 4 * DKV]
    kw = _rope(p[:, o + 4 * DKV:o + 5 * DKV], c, s_lo, s_hi)
    vw = p[:, o + 5 * DKV:o + 6 * DKV]
    kv_ref[:, 0:2 * DKV] = p[:, o:o + 2 * DKV]
    kv_ref[:, 2 * DKV:3 * DKV] = ks
    kv_ref[:, 3 * DKV:4 * DKV] = vs
    kwvw_ref[:, 0:DKV] = kw
    kwvw_ref[:, DKV:2 * DKV] = vw
    if kvt_ref:
        for j in range(4 * DKV // 128):
            kvt_ref[0][0, 128 * j:128 * (j + 1), :] = kv_ref[:, 128 * j:128 * (j + 1)].T
    vs_t, vw_t = vs.T, vw.T
    for k in range(KVH):
        sl = slice(HD * k, HD * (k + 1))
        ks_ref[k] = ks[:, sl].astype(BF16)
        vs_ref[k] = vs_t[sl, :].astype(BF16)
        kw_ref[k] = kw[:, sl].astype(BF16)
        vw_ref[k] = vw_t[sl, :].astype(BF16)
    gates_ref[...] = _sigmoid(p[:, o + 6 * DKV:o + 6 * DKV + 128])


def _inproj_nsa(hn, w_nsa, rope_c, rope_slo, rope_shi, tm, seq_len=None):
    n, d = hn.shape
    row = lambda i: (i, 0)
    hm = lambda i: (0, i, 0)
    hm_shape = jax.ShapeDtypeStruct((KVH, n, HD), BF16)
    hm_spec = pl.BlockSpec((KVH, tm, HD), hm)
    hmt_shape = jax.ShapeDtypeStruct((KVH, HD, n), BF16)
    hmt_spec = pl.BlockSpec((KVH, HD, tm), lambda i: (0, 0, i))
    out_specs = [pl.BlockSpec((tm, DH), row), pl.BlockSpec((tm, DH), row),
                 pl.BlockSpec((tm, 4 * DKV), row), pl.BlockSpec((tm, 2 * DKV), row),
                 hm_spec, hmt_spec, hm_spec, hmt_spec,
                 pl.BlockSpec((tm, 128), row)]
    out_shape = [jax.ShapeDtypeStruct((n, DH), BF16), jax.ShapeDtypeStruct((n, DH), BF16),
                 jax.ShapeDtypeStruct((n, 4 * DKV), F32), jax.ShapeDtypeStruct((n, 2 * DKV), F32),
                 hm_shape, hmt_shape, hm_shape, hmt_shape,
                 jax.ShapeDtypeStruct((n, 128), F32)]
    if seq_len is not None:
        per_seq = seq_len // tm
        out_specs.append(pl.BlockSpec((1, 4 * DKV, tm), lambda i: (i // per_seq, 0, i % per_seq)))
        out_shape.append(jax.ShapeDtypeStruct((n // seq_len, 4 * DKV, seq_len), F32))
    return pl.pallas_call(
        _inproj_nsa_body,
        grid=(n // tm,),
        in_specs=[pl.BlockSpec((tm, d), row),
                  pl.BlockSpec((d, NSA_PAD), lambda i: (0, 0)),
                  pl.BlockSpec((tm, 128), row), pl.BlockSpec((tm, 128), row), pl.BlockSpec((tm, 128), row)],
        out_specs=out_specs,
        out_shape=out_shape,
        compiler_params=_cparams(1),
        name="inproj_nsa",
    )(hn, w_nsa, rope_c, rope_slo, rope_shi)


def _matmul_body(x_ref, w_ref, o_ref):
    o_ref[...] = _dot(x_ref[...], w_ref[...])


def _inproj_rwkv(hn, w_rw, tm):
    n, d = hn.shape
    c = w_rw.shape[1]
    return pl.pallas_call(
        _matmul_body,
        grid=(n // tm,),
        in_specs=[pl.BlockSpec((tm, d), lambda i: (i, 0)), pl.BlockSpec((d, c), lambda i: (0, 0))],
        out_specs=pl.BlockSpec((tm, c), lambda i: (i, 0)),
        out_shape=jax.ShapeDtypeStruct((n, c), F32),
        compiler_params=_cparams(1),
        name="inproj_rwkv",
    )(hn, w_rw)


def _compress_accumulate(stage_ref, w1c_refs, n_seg):
    accs = [[jnp.zeros((n_seg, 2 * CMP_HIDDEN), F32) for _ in range(KVH)] for _ in range(2)]
    for s in range(CMP_STRIDE):
        for kv in range(2):
            w = w1c_refs[kv][s]
            for pair in range(KVH // 2):
                x = stage_ref[2 * kv + pair, pl.ds(s, n_seg, stride=CMP_STRIDE), :]
                for j in range(2):
                    k = 2 * pair + j
                    accs[kv][k] = accs[kv][k] + _dot(x[:, HD * j:HD * (j + 1)].astype(BF16), w)
    return accs


def _compress_finish(acc, pe0, w2, n_seg):
    a0 = acc[:, :CMP_HIDDEN]
    a1 = pltpu.roll(acc[:, CMP_HIDDEN:], n_seg - 1, 0)
    pre = pe0 + a0 + a1
    return _dot((pre * _sigmoid(pre)).astype(BF16), w2)


def _cmp_prompt_body(kv_ref, w1k_ref, w1v_ref, pek_ref, pev_ref, w1kf_ref, w1vf_ref, w2k_ref, w2v_ref,
                     kc_ref, vc_ref, stage_ref):
    n_seg = kv_ref.shape[0] // CMP_STRIDE
    for grp in range(4):
        stage_ref[grp] = kv_ref[:, 128 * grp:128 * (grp + 1)]
    accs = _compress_accumulate(stage_ref, (w1k_ref, w1v_ref), n_seg)
    for kv, (pe_ref, w1f_ref, w2_ref, out_ref) in enumerate(
            ((pek_ref, w1kf_ref, w2k_ref, kc_ref), (pev_ref, w1vf_ref, w2v_ref, vc_ref))):
        pe0 = _dot(pe_ref[...], w1f_ref[...])
        w2 = w2_ref[...]
        for k in range(KVH):
            c = _compress_finish(accs[kv][k], pe0, w2, n_seg)
            out_ref[0, k] = (c.T if kv == 1 else c).astype(BF16)


def _cmp_weights(pe, w1, w2):
    w1c = jnp.concatenate([w1[:CMP_STRIDE], w1[CMP_STRIDE:]], axis=-1).astype(BF16)
    return w1c, pe.reshape(1, CMP_BLOCK * HD), w1.reshape(CMP_BLOCK * HD, CMP_HIDDEN), w2.astype(BF16)


def _compress_prompt(kv_rows, cmp_k, cmp_v, n_batch, t):
    n_seg = t // CMP_STRIDE
    w1k, pek, w1kf, w2k = cmp_k
    w1v, pev, w1vf, w2v = cmp_v
    full = lambda a: pl.BlockSpec(a.shape, lambda b: (0,) * a.ndim)
    out_shape = jax.ShapeDtypeStruct((n_batch, KVH, n_seg, HD), BF16)
    out_spec = pl.BlockSpec((1, KVH, n_seg, HD), lambda b: (b, 0, 0, 0))
    out_shape_t = jax.ShapeDtypeStruct((n_batch, KVH, HD, n_seg), BF16)
    out_spec_t = pl.BlockSpec((1, KVH, HD, n_seg), lambda b: (b, 0, 0, 0))
    return pl.pallas_call(
        _cmp_prompt_body,
        grid=(n_batch,),
        in_specs=[pl.BlockSpec((t, 2 * DKV), lambda b: (b, 0)),
                  full(w1k), full(w1v), full(pek), full(pev), full(w1kf), full(w1vf), full(w2k), full(w2v)],
        out_specs=[out_spec, out_spec_t],
        out_shape=[out_shape, out_shape_t],
        scratch_shapes=[pltpu.VMEM((4, t, 128), F32)],
        compiler_params=_cparams(1),
        name="nsa_compress_prompt",
    )(kv_rows, w1k, w1v, pek, pev, w1kf, w1vf, w2k, w2v)


def _select_blocks(imp, cur, n_blocks):
    jb = lax.broadcasted_iota(jnp.int32, (1, N_SEL_LANES), 1)
    valid = (jb <= cur) & (jb < n_blocks)
    rel = cur - jb
    forced = (jb == 0) | ((rel >= 0) & (rel < N_LOCAL_FORCED))
    score = jnp.where(valid, imp + jnp.where(forced, FORCE_BONUS, 0.0), NEG)
    rank = jnp.zeros(score.shape, F32)
    for jp in range(min(n_blocks, N_SEL_LANES)):
        col = score[:, jp:jp + 1]
        tie = jnp.where(jb > jp, 1.0, 0.0)
        rank = rank + jnp.where(col > score, 1.0, jnp.where(col == score, tie, 0.0))
    return jnp.where(valid, jnp.where(rank < SEL_TOPN, 1.0, 0.0), 0.0)


def _softmax_rows(s, ok):
    sm = jnp.where(ok, s, NEG)
    m = jnp.max(sm, axis=-1, keepdims=True)
    e = jnp.where(ok, jnp.exp2(sm - m), 0.0)
    l = jnp.sum(e, axis=-1, keepdims=True)
    return e / jnp.maximum(l, 1e-30)


def _softmax_bias(s, bias):
    sb = s + bias
    m = jnp.maximum(jnp.max(sb, axis=-1, keepdims=True), M_FLOOR)
    e = jnp.exp(sb - m)
    return e, 1.0 / jnp.maximum(jnp.sum(e, axis=-1, keepdims=True), 1e-30)


def _select_blocks_t(imp_t, cur, n_blocks, n_live, score_ref):
    jb = lax.broadcasted_iota(jnp.int32, (N_SEL_LANES, 1), 0)
    valid = (jb <= cur) & (jb < n_blocks)
    rel = cur - jb
    forced = (jb == 0) | ((rel >= 0) & (rel < N_LOCAL_FORCED))
    score = jnp.where(valid, imp_t + jnp.where(forced, FORCE_BONUS, 0.0), NEG)
    score_ref[...] = score

    def body(jp, rank):
        other = score_ref[pl.ds(jp, 1), :]
        tie = jnp.where(jb > jp, 1.0, 0.0)
        return rank + jnp.where(other > score, 1.0, jnp.where(other == score, tie, 0.0))

    rank = lax.fori_loop(0, n_live, body, jnp.zeros(score.shape, F32))
    return jnp.where(valid, jnp.where(rank < SEL_TOPN, 1.0, 0.0), 0.0)


def _attn_prompt_body(qraw_ref, qrot_ref, gates_ref, kc_ref, vc_ref, ks_ref, vs_ref, kw_ref, vw_ref,
                      mcst_ref, e_ref, o_ref, score_ref):
    t = ks_ref.shape[1]
    n_seg = kc_ref.shape[2]
    i = pl.program_id(1)
    s0 = i * Q_BLOCK
    qpos = s0 + lax.broadcasted_iota(jnp.int32, (Q_BLOCK, 1), 0)
    qpos_row = s0 + lax.broadcasted_iota(jnp.int32, (1, Q_BLOCK), 1)
    n_idx = lax.broadcasted_iota(jnp.int32, (1, n_seg), 1)
    ok_cmp = (n_idx * CMP_STRIDE + (CMP_BLOCK - 1) <= qpos) & (n_idx < n_seg - 1)
    bias_cmp = jnp.where(ok_cmp, 0.0, MASK_BIAS)
    gates = gates_ref[...]
    mcst = mcst_ref[...]
    n_key_chunks = i // (SEL_KEY_CHUNK // Q_BLOCK) + 1
    n_live_blocks = (s0 + Q_BLOCK) // SEL_BLOCK
    win_start = pl.multiple_of(jnp.maximum(s0 - WINDOW, 0), Q_BLOCK)
    win_len = WINDOW + Q_BLOCK
    wpos = win_start + lax.broadcasted_iota(jnp.int32, (1, win_len), 1)
    dlt = qpos - wpos
    bias_win = jnp.where((dlt >= 0) & (dlt <= WINDOW), 0.0, MASK_BIAS)
    g_rows = [slice(g * Q_BLOCK, (g + 1) * Q_BLOCK) for g in range(GQA)]
    half = Q_BLOCK // 2

    kvhs = range(KVH)
    head_cols = lambda ref, k: jnp.concatenate(
        [ref[:, HD * (GQA * k + g):HD * (GQA * k + g + 1)] for g in range(GQA)], axis=0)
    qr = [head_cols(qraw_ref, k) for k in kvhs]
    qs = [head_cols(qrot_ref, k) for k in kvhs]

    s_cmp = [_dot_nt(qr[k], kc_ref[0, k]) for k in kvhs]
    o_cmp, sel_t = [], []
    for k in kvhs:
        p_heads = []
        for rows in g_rows:
            e_c, inv_c = _softmax_bias(s_cmp[k][rows], bias_cmp)
            p_heads.append(e_c * inv_c)
        o_cmp.append(_dot(jnp.concatenate([p.astype(BF16) for p in p_heads], axis=0), vc_ref[0, k]))
        p_sum = p_heads[0]
        for p in p_heads[1:]:
            p_sum = p_sum + p
        hi = p_sum.astype(BF16)
        r1 = p_sum - hi.astype(F32)
        mid = r1.astype(BF16)
        lo = (r1 - mid.astype(F32)).astype(BF16)
        imp_t = _dot_nt(mcst, hi) + _dot_nt(mcst, mid) + _dot_nt(mcst, lo)
        sel_t.append(_select_blocks_t(imp_t, qpos_row // SEL_BLOCK, t // SEL_BLOCK, n_live_blocks,
                                      score_ref.at[k]).astype(BF16))

    def chunk(c, carry):
        k0 = pl.multiple_of(c * SEL_KEY_CHUNK, SEL_KEY_CHUNK)
        kpos = k0 + lax.broadcasted_iota(jnp.int32, (1, SEL_KEY_CHUNK), 1)
        causal = kpos <= qpos
        e_blk = e_ref[:, pl.ds(k0, SEL_KEY_CHUNK)]
        s = [_dot_nt(qs[k], ks_ref[k, pl.ds(k0, SEL_KEY_CHUNK), :]) for k in kvhs]
        bias = [(jnp.where(causal, _dot_tn(sel_t[k], e_blk), 0.0) - 1.0) * (-MASK_BIAS) for k in kvhs]
        out = []
        for k in kvhs:
            m, l, acc = carry[k]
            m_new, p_b, p_tot = [], [], []
            for rows in g_rows:
                sb = s[k][rows] + bias[k]
                m_g = jnp.maximum(m[rows], jnp.max(sb, axis=-1, keepdims=True))
                p = jnp.exp(sb - m_g)
                m_new.append(m_g)
                p_tot.append(jnp.sum(p, axis=-1, keepdims=True))
                p_b.append(p.astype(BF16))
            m_new = jnp.concatenate(m_new, axis=0)
            alpha = jnp.exp(m - m_new)
            l = alpha * l + jnp.concatenate(p_tot, axis=0)
            acc = alpha * acc + _dot(jnp.concatenate(p_b, axis=0), vs_ref[k, pl.ds(k0, SEL_KEY_CHUNK), :])
            out.append((m_new, l, acc))
        return tuple(out)

    init = tuple((jnp.full((GQA * Q_BLOCK, 1), M_FLOOR, F32), jnp.zeros((GQA * Q_BLOCK, 1), F32),
                  jnp.zeros((GQA * Q_BLOCK, HD), F32)) for _ in kvhs)
    sel_out = lax.fori_loop(0, n_key_chunks, chunk, init)
    o_sel = [acc * (1.0 / jnp.maximum(l, 1e-30)) for _, l, acc in sel_out]

    s_win = [_dot_nt(qs[k], kw_ref[k, pl.ds(win_start, win_len), :]) for k in kvhs]
    for k in kvhs:
        e_b, inv_w = [], []
        for g in range(GQA):
            for hf in range(2):
                rows = slice(g * Q_BLOCK + hf * half, g * Q_BLOCK + (hf + 1) * half)
                e_w, inv = _softmax_bias(s_win[k][rows], bias_win[hf * half:(hf + 1) * half])
                e_b.append(e_w.astype(BF16))
                inv_w.append(inv)
        o_win = (_dot(jnp.concatenate(e_b, axis=0), vw_ref[k, pl.ds(win_start, win_len), :])
                 * jnp.concatenate(inv_w, axis=0))
        for g in range(GQA):
            h = GQA * k + g
            rows = g_rows[g]
            o_h = (gates[:, 3 * h:3 * h + 1] * o_cmp[k][rows] + gates[:, 3 * h + 1:3 * h + 2] * o_sel[k][rows]
                   + gates[:, 3 * h + 2:3 * h + 3] * o_win[rows])
            o_ref[:, HD * h:HD * (h + 1)] = o_h.astype(o_ref.dtype)


def _attn_prompt_t_body(qraw_ref, qrot_ref, gates_ref, kc_ref, vct_ref, ks_ref, vst_ref, kw_ref, vwt_ref,
                        mcst_ref, et_ref, o_ref, score_ref):
    t = ks_ref.shape[1]
    n_seg = kc_ref.shape[2]
    i = pl.program_id(1)
    s0 = i * Q_BLOCK
    qpos = s0 + lax.broadcasted_iota(jnp.int32, (1, Q_BLOCK), 1)
    n_idx = lax.broadcasted_iota(jnp.int32, (n_seg, 1), 0)
    ok_cmp = (n_idx * CMP_STRIDE + (CMP_BLOCK - 1) <= qpos) & (n_idx < n_seg - 1)
    bias_cmp = jnp.where(ok_cmp, 0.0, MASK_BIAS)
    mcst = mcst_ref[...]
    n_key_chunks = i // (SEL_KEY_CHUNK // Q_BLOCK) + 1
    n_live_blocks = (s0 + Q_BLOCK) // SEL_BLOCK
    win_start = pl.multiple_of(jnp.maximum(s0 - WINDOW, 0), Q_BLOCK)
    win_len = WINDOW + Q_BLOCK
    wpos = win_start + lax.broadcasted_iota(jnp.int32, (win_len, 1), 0)
    dlt = qpos - wpos
    bias_win = jnp.where((dlt >= 0) & (dlt <= WINDOW), 0.0, MASK_BIAS)
    g_cols = [slice(g * Q_BLOCK, (g + 1) * Q_BLOCK) for g in range(GQA)]
    kvhs = range(KVH)
    head_rows = lambda ref, k: jnp.concatenate(
        [ref[:, HD * (GQA * k + g):HD * (GQA * k + g + 1)] for g in range(GQA)], axis=0)
    qr = [head_rows(qraw_ref, k) for k in kvhs]
    qs = [head_rows(qrot_ref, k) for k in kvhs]

    def softmax_cols(s_t, bias):
        e_b, inv = [], []
        for cols in g_cols:
            sb = s_t[:, cols] + bias
            m = jnp.maximum(jnp.max(sb, axis=0, keepdims=True), M_FLOOR)
            e = jnp.exp2(sb - m)
            e_b.append(e)
            inv.append(1.0 / jnp.maximum(jnp.sum(e, axis=0, keepdims=True), 1e-30))
        return e_b, inv

    o_cmp, sel_t = [], []
    s_next = _dot_nt(kc_ref[0, 0], qr[0])
    for k in kvhs:
        s_cmp = s_next
        if k + 1 < KVH:
            s_next = _dot_nt(kc_ref[0, k + 1], qr[k + 1])
        e_heads, inv_heads = softmax_cols(s_cmp, bias_cmp)
        p_heads = [e * inv for e, inv in zip(e_heads, inv_heads)]
        o_cmp.append(_dot(vct_ref[0, k], jnp.concatenate([p.astype(BF16) for p in p_heads], axis=1)))
        p_sum = p_heads[0]
        for p in p_heads[1:]:
            p_sum = p_sum + p
        hi = p_sum.astype(BF16)
        r1 = p_sum - hi.astype(F32)
        mid = r1.astype(BF16)
        lo = (r1 - mid.astype(F32)).astype(BF16)
        imp_t = _dot(mcst, hi) + _dot(mcst, mid) + _dot(mcst, lo)
        sel_t.append(_select_blocks_t(imp_t, qpos // SEL_BLOCK, t // SEL_BLOCK, n_live_blocks,
                                      score_ref.at[k]).astype(BF16))

    def chunk(c, carry):
        k0 = pl.multiple_of(c * SEL_KEY_CHUNK, SEL_KEY_CHUNK)
        kpos = k0 + lax.broadcasted_iota(jnp.int32, (SEL_KEY_CHUNK, 1), 0)
        causal = kpos <= qpos
        e_blk = et_ref[pl.ds(k0, SEL_KEY_CHUNK), :]

        def scores(k):
            return (_dot_nt(ks_ref[k, pl.ds(k0, SEL_KEY_CHUNK), :], qs[k]),
                    (jnp.where(causal, _dot(e_blk, sel_t[k]), 0.0) - 1.0) * (-MASK_BIAS))

        out = []
        nxt = scores(0)
        for k in kvhs:
            s_k, bias_k = nxt
            if k + 1 < KVH:
                nxt = scores(k + 1)
            m, l, acc = carry[k]
            m_new, p_b, p_tot = [], [], []
            for cols in g_cols:
                sb = s_k[:, cols] + bias_k
                m_g = jnp.maximum(m[:, cols], jnp.max(sb, axis=0, keepdims=True))
                p = jnp.exp2(sb - m_g)
                m_new.append(m_g)
                p_tot.append(jnp.sum(p, axis=0, keepdims=True))
                p_b.append(p.astype(BF16))
            m_new = jnp.concatenate(m_new, axis=1)
            alpha = jnp.exp2(m - m_new)
            l = alpha * l + jnp.concatenate(p_tot, axis=1)
            acc = alpha * acc + _dot(vst_ref[k, :, pl.ds(k0, SEL_KEY_CHUNK)], jnp.concatenate(p_b, axis=1))
            out.append((m_new, l, acc))
        return tuple(out)

    init = tuple((jnp.full((1, GQA * Q_BLOCK), M_FLOOR, F32), jnp.zeros((1, GQA * Q_BLOCK), F32),
                  jnp.zeros((HD, GQA * Q_BLOCK), F32)) for _ in kvhs)
    sel_out = lax.fori_loop(0, n_key_chunks, chunk, init)
    o_sel = [acc * (1.0 / jnp.maximum(l, 1e-30)) for _, l, acc in sel_out]

    win_scores = lambda k: _dot_nt(kw_ref[k, pl.ds(win_start, win_len), :], qs[k])
    gates_t = gates_ref[...].T
    o_t = []
    s_next = win_scores(0)
    for k in kvhs:
        s_win = s_next
        if k + 1 < KVH:
            s_next = win_scores(k + 1)
        e_heads, inv_heads = softmax_cols(s_win, bias_win)
        o_win = (_dot(vwt_ref[k, :, pl.ds(win_start, win_len)],
                      jnp.concatenate([e.astype(BF16) for e in e_heads], axis=1))
                 * jnp.concatenate(inv_heads, axis=1))
        for g in range(GQA):
            h = GQA * k + g
            cols = g_cols[g]
            o_t.append(gates_t[3 * h:3 * h + 1] * o_cmp[k][:, cols] + gates_t[3 * h + 1:3 * h + 2] * o_sel[k][:, cols]
                       + gates_t[3 * h + 2:3 * h + 3] * o_win[:, cols])
    for j in range(NH // 2):
        pair = jnp.concatenate([o_t[2 * j], o_t[2 * j + 1]], axis=0)
        o_ref[:, 2 * HD * j:2 * HD * (j + 1)] = pair.T.astype(o_ref.dtype)


def _attn_prompt(q_raw, q_rot, gates, k_cmp, v_cmp, ks, vs, kw, vw, mcs_t, e_mat, n_batch, t):
    nqb = t // Q_BLOCK
    n_seg = t // CMP_STRIDE
    qrow = lambda b, i: (b * nqb + i, 0)
    cmp_spec = pl.BlockSpec((1, KVH, n_seg, HD), lambda b, i: (b, 0, 0, 0))
    kv_spec = pl.BlockSpec((KVH, t, HD), lambda b, i: (0, b, 0))
    kvt_spec = pl.BlockSpec((KVH, HD, t), lambda b, i: (0, 0, b))
    cmpt_spec = pl.BlockSpec((1, KVH, HD, n_seg), lambda b, i: (b, 0, 0, 0))
    return pl.pallas_call(
        _attn_prompt_t_body,
        grid=(n_batch, nqb),
        in_specs=[pl.BlockSpec((Q_BLOCK, DH), qrow), pl.BlockSpec((Q_BLOCK, DH), qrow),
                  pl.BlockSpec((Q_BLOCK, 128), qrow),
                  cmp_spec, cmpt_spec, kv_spec, kvt_spec, kv_spec, kvt_spec,
                  pl.BlockSpec(mcs_t.shape, lambda b, i: (0, 0)),
                  pl.BlockSpec(e_mat.shape, lambda b, i: (0, 0))],
        out_specs=pl.BlockSpec((Q_BLOCK, DH), qrow),
        out_shape=jax.ShapeDtypeStruct((n_batch * t, DH), F32),
        scratch_shapes=[pltpu.VMEM((KVH, N_SEL_LANES, Q_BLOCK), F32)],
        compiler_params=_cparams(2),
        name="nsa_attn_prompt",
    )(q_raw, q_rot, gates, k_cmp, v_cmp, ks, vs, kw, vw, mcs_t, e_mat)


def _pick_kv_group(full):
    hk = lax.broadcasted_iota(jnp.int32, (NH, 1), 0) // GQA
    out = jnp.zeros((NH, HD), F32)
    for k in range(KVH):
        out = out + jnp.where(hk == k, full[:, HD * k:HD * (k + 1)], 0.0)
    return out


def _nsa_sample_body(n_pages, pt_ref, *refs):
    pages = refs[:n_pages]
    (win_ref, qraw_ref, qrot_ref, gates_ref, kvnew_ref, kwnew_ref,
     w1k_ref, w1v_ref, pek_ref, pev_ref, w1kf_ref, w1vf_ref, w2k_ref, w2v_ref, mcs_ref, e_ref,
     o_ref, winout_ref, stage_ref) = refs[n_pages:]
    page = pages[0].shape[1]
    past = n_pages * page
    n_seg = past // CMP_STRIDE

    def bdiag(q):
        qt = jnp.concatenate([q] * KVH, axis=1)
        hk = lax.broadcasted_iota(jnp.int32, (NH, DKV), 0) // GQA
        lk = lax.broadcasted_iota(jnp.int32, (NH, DKV), 1) // HD
        return jnp.where(hk == lk, qt, 0.0)

    qr = bdiag(qraw_ref[0])
    qs = bdiag(qrot_ref[0])
    qr_b, qs_b = qr.astype(BF16), qs.astype(BF16)

    for kk, pg in enumerate(pages):
        for grp in range(4):
            stage_ref[grp, page * kk:page * (kk + 1), :] = pg[0, :, 128 * grp:128 * (grp + 1)]
    accs = _compress_accumulate(stage_ref, (w1k_ref, w1v_ref), n_seg)
    cmp_kv = []
    for kv, (pe_ref, w1f_ref, w2_ref) in enumerate(((pek_ref, w1kf_ref, w2k_ref), (pev_ref, w1vf_ref, w2v_ref))):
        pe0 = _dot(pe_ref[...], w1f_ref[...])
        w2 = w2_ref[...]
        cmp_kv.append(jnp.concatenate(
            [_compress_finish(accs[kv][k], pe0, w2, n_seg) for k in range(KVH)], axis=1).astype(BF16))
    k_cmp, v_cmp = cmp_kv

    n_idx = lax.broadcasted_iota(jnp.int32, (1, n_seg), 1)
    ok_cmp = (n_idx * CMP_STRIDE + (CMP_BLOCK - 1) <= past) & (n_idx < n_seg - 1)
    p_cmp = _softmax_rows(_dot_nt(qr_b, k_cmp), ok_cmp)
    o_cmp = _pick_kv_group(_dot(p_cmp.astype(BF16), v_cmp))
    p_sum = jnp.concatenate(
        [jnp.sum(p_cmp[GQA * k:GQA * (k + 1)], axis=0, keepdims=True) for k in range(KVH)], axis=0)
    imp = _dot_split3(p_sum, mcs_ref[...])
    n_blocks = past // SEL_BLOCK + 1
    cur = jnp.full((KVH, 1), past // SEL_BLOCK, jnp.int32)
    sel = _select_blocks(imp, cur, n_blocks)
    sel16 = jnp.concatenate([jnp.broadcast_to(sel[k:k + 1], (GQA, N_SEL_LANES)) for k in range(KVH)], axis=0)

    kv_new = kvnew_ref[0]
    ks_new, vs_new = kv_new[:, 2 * DKV:3 * DKV], kv_new[:, 3 * DKV:4 * DKV]
    k_sel = jnp.concatenate([pg[0, :, 2 * DKV:3 * DKV] for pg in pages], axis=0).astype(BF16)
    v_sel = jnp.concatenate([pg[0, :, 3 * DKV:4 * DKV] for pg in pages], axis=0).astype(BF16)
    s_c = _dot_nt(qs_b, k_sel)
    s_n = jnp.sum(qs * ks_new, axis=-1, keepdims=True)
    ok_c = _dot(sel16.astype(BF16), e_ref[...]) > 0.5
    new_blk = past // SEL_BLOCK
    ok_n = sel16[:, new_blk:new_blk + 1] > 0.5
    sm_c = jnp.where(ok_c, s_c, NEG)
    sm_n = jnp.where(ok_n, s_n, NEG)
    m = jnp.maximum(jnp.max(sm_c, axis=-1, keepdims=True), sm_n)
    e_c = jnp.where(ok_c, jnp.exp2(sm_c - m), 0.0)
    e_n = jnp.where(ok_n, jnp.exp2(sm_n - m), 0.0)
    inv = 1.0 / jnp.maximum(jnp.sum(e_c, axis=-1, keepdims=True) + e_n, 1e-30)
    o_sel = _pick_kv_group(_dot((e_c * inv).astype(BF16), v_sel)
                           + (e_n * inv).astype(BF16).astype(F32) * vs_new.astype(BF16).astype(F32))

    kw_new = kwnew_ref[0]
    kwn, vwn = kw_new[:, 0:DKV], kw_new[:, DKV:2 * DKV]
    s_w = _dot_nt(qs_b, win_ref[0, :, 0:DKV].astype(BF16))
    s_wn = jnp.sum(qs * kwn, axis=-1, keepdims=True)
    mw = jnp.maximum(jnp.max(s_w, axis=-1, keepdims=True), s_wn)
    e_w = jnp.exp2(s_w - mw)
    e_wn = jnp.exp2(s_wn - mw)
    inv_w = 1.0 / (jnp.sum(e_w, axis=-1, keepdims=True) + e_wn)
    o_win = _pick_kv_group(_dot((e_w * inv_w).astype(BF16), win_ref[0, :, DKV:2 * DKV].astype(BF16))
                           + (e_wn * inv_w).astype(BF16).astype(F32) * vwn.astype(BF16).astype(F32))

    g = gates_ref[0]
    o_ref[0] = g[:, 0:1] * o_cmp + g[:, 1:2] * o_sel + g[:, 2:3] * o_win
    wb = win_ref.shape[1]
    winout_ref[0, 0:wb - 1, :] = win_ref[0, 1:wb, :]
    winout_ref[0, wb - 1:wb, :] = kw_new


def _nsa_sample(cache3, page_table, state_win, q_raw, q_rot, gates, kv_new, kw_new, cmp_k, cmp_v, mcs, e_mat):
    nb, n_pages = page_table.shape
    page = cache3.shape[1]
    wb = state_win.shape[1]
    w1k, pek, w1kf, w2k = cmp_k
    w1v, pev, w1vf, w2v = cmp_v
    consts = [w1k, w1v, pek, pev, w1kf, w1vf, w2k, w2v, mcs, e_mat]
    full = lambda a: pl.BlockSpec(a.shape, lambda b, pt: (0,) * a.ndim)
    page_specs = [pl.BlockSpec((1, page, cache3.shape[2]), functools.partial(
        lambda b, pt, kk: (pt[b * n_pages + kk], 0, 0), kk=kk)) for kk in range(n_pages)]
    per_b = lambda shape: pl.BlockSpec((1,) + shape, lambda b, pt: (b, 0, 0))
    grid_spec = pltpu.PrefetchScalarGridSpec(
        num_scalar_prefetch=1,
        grid=(nb,),
        in_specs=page_specs + [per_b((wb, 2 * DKV)), per_b((NH, HD)), per_b((NH, HD)), per_b((NH, 3)),
                               per_b((1, 4 * DKV)), per_b((1, 2 * DKV))] + [full(a) for a in consts],
        out_specs=[per_b((NH, HD)), per_b((wb, 2 * DKV))],
        scratch_shapes=[pltpu.VMEM((4, n_pages * page, 128), F32)],
    )
    return pl.pallas_call(
        functools.partial(_nsa_sample_body, n_pages),
        grid_spec=grid_spec,
        out_shape=[jax.ShapeDtypeStruct((nb, NH, HD), F32), jax.ShapeDtypeStruct((nb, wb, 2 * DKV), F32)],
        compiler_params=_cparams(1),
        name="nsa_sample",
    )(page_table.reshape(-1), *([cache3] * n_pages), state_win, q_raw, q_rot, gates, kv_new, kw_new, *consts)


def _kv_block_diag(q):
    qt = jnp.concatenate([q] * KVH, axis=1)
    hk = lax.broadcasted_iota(jnp.int32, (NH, DKV), 0) // GQA
    lk = lax.broadcasted_iota(jnp.int32, (NH, DKV), 1) // HD
    return jnp.where(hk == lk, qt, 0.0)


def _nsa_sample_t_body(n_pages, n_seq, pt_ref, *refs):
    all_pages = refs[:n_seq * n_pages]
    (win_ref, qraw_ref, qrot_ref, gates_ref, kvnew_ref, kwnew_ref, kwvwt_ref, perm_ref,
     w1k_ref, w1v_ref, pek_ref, pev_ref, w1kf_ref, w1vf_ref, w2k_ref, w2v_ref, mcs_ref, e_ref,
     o_ref, winout_ref, stage_ref) = refs[n_seq * n_pages:]
    page = all_pages[0].shape[3]
    n_seg = n_pages * page // CMP_STRIDE
    seg_pp = page // CMP_STRIDE

    perm = perm_ref[...]
    for kk, pg in enumerate(all_pages):
        for c in range(2):
            xs = _dot_nt(perm, pg[0, c].astype(BF16))
            for s in range(CMP_STRIDE):
                stage_ref[c, s, seg_pp * kk:seg_pp * (kk + 1), :] = xs[seg_pp * s:seg_pp * (s + 1), :]
    accs = [[jnp.zeros((n_seq * n_seg, 2 * CMP_HIDDEN), F32) for _ in range(KVH)] for _ in range(2)]
    for s in range(CMP_STRIDE):
        for kv, w1c_ref in enumerate((w1k_ref, w1v_ref)):
            w = w1c_ref[s]
            x = stage_ref[kv, s].astype(BF16)
            for k in range(KVH):
                accs[kv][k] = accs[kv][k] + _dot(x[:, HD * k:HD * (k + 1)], w)
    for q in range(n_seq):
        one = lambda ref: ref.at[pl.ds(q, 1)]
        _nsa_sample_one(
            all_pages[q * n_pages:(q + 1) * n_pages],
            [[a[n_seg * q:n_seg * (q + 1)] for a in row] for row in accs],
            pl.program_id(0) * n_seq + q,
            one(win_ref), one(qraw_ref), one(qrot_ref), one(gates_ref), one(kvnew_ref), one(kwnew_ref), kwvwt_ref,
            pek_ref, pev_ref, w1kf_ref, w1vf_ref, w2k_ref, w2v_ref, mcs_ref, e_ref, one(o_ref), one(winout_ref))


def _nsa_sample_one(pages, accs, b_idx, win_ref, qraw_ref, qrot_ref, gates_ref, kvnew_ref, kwnew_ref, kwvwt_ref,
                    pek_ref, pev_ref, w1kf_ref, w1vf_ref, w2k_ref, w2v_ref, mcs_ref, e_ref, o_ref, winout_ref):
    page = pages[0].shape[3]
    past = len(pages) * page
    n_seg = past // CMP_STRIDE
    qr = _kv_block_diag(qraw_ref[0])
    qs = _kv_block_diag(qrot_ref[0])
    qr_b, qs_b = qr.astype(BF16), qs.astype(BF16)
    cmp_kv = []
    for kv, (pe_ref, w1f_ref, w2_ref) in enumerate(((pek_ref, w1kf_ref, w2k_ref), (pev_ref, w1vf_ref, w2v_ref))):
        pe0 = _dot(pe_ref[...], w1f_ref[...])
        w2 = w2_ref[...]
        cmp_kv.append(jnp.concatenate(
            [_compress_finish(accs[kv][k], pe0, w2, n_seg) for k in range(KVH)], axis=1).astype(BF16))
    k_cmp, v_cmp = cmp_kv

    n_idx = lax.broadcasted_iota(jnp.int32, (1, n_seg), 1)
    ok_cmp = (n_idx * CMP_STRIDE + (CMP_BLOCK - 1) <= past) & (n_idx < n_seg - 1)
    p_cmp = _softmax_rows(_dot_nt(qr_b, k_cmp), ok_cmp)
    o_cmp = _pick_kv_group(_dot(p_cmp.astype(BF16), v_cmp))
    p_sum = jnp.concatenate(
        [jnp.sum(p_cmp[GQA * k:GQA * (k + 1)], axis=0, keepdims=True) for k in range(KVH)], axis=0)
    imp = _dot_split3(p_sum, mcs_ref[...])
    n_blocks = past // SEL_BLOCK + 1
    cur = jnp.full((KVH, 1), past // SEL_BLOCK, jnp.int32)
    sel = _select_blocks(imp, cur, n_blocks)
    sel16 = jnp.concatenate([jnp.broadcast_to(sel[k:k + 1], (GQA, N_SEL_LANES)) for k in range(KVH)], axis=0)

    kv_new = kvnew_ref[0]
    ks_new, vs_new = kv_new[:, 2 * DKV:3 * DKV], kv_new[:, 3 * DKV:4 * DKV]
    s_c = jnp.concatenate([_dot(qs_b, pg[0, 2].astype(BF16)) for pg in pages], axis=1)
    s_n = jnp.sum(qs * ks_new, axis=-1, keepdims=True)
    ok_c = _dot(sel16.astype(BF16), e_ref[...]) > 0.5
    new_blk = past // SEL_BLOCK
    ok_n = sel16[:, new_blk:new_blk + 1] > 0.5
    sm_c = jnp.where(ok_c, s_c, NEG)
    sm_n = jnp.where(ok_n, s_n, NEG)
    m = jnp.maximum(jnp.max(sm_c, axis=-1, keepdims=True), sm_n)
    e_c = jnp.where(ok_c, jnp.exp2(sm_c - m), 0.0)
    e_n = jnp.where(ok_n, jnp.exp2(sm_n - m), 0.0)
    inv = 1.0 / jnp.maximum(jnp.sum(e_c, axis=-1, keepdims=True) + e_n, 1e-30)
    p_c = (e_c * inv).astype(BF16)
    o_sel_full = (e_n * inv).astype(BF16).astype(F32) * vs_new.astype(BF16).astype(F32)
    for kk, pg in enumerate(pages):
        o_sel_full = o_sel_full + _dot_nt(p_c[:, page * kk:page * (kk + 1)], pg[0, 3].astype(BF16))
    o_sel = _pick_kv_group(o_sel_full)

    kw_new = kwnew_ref[0]
    kwn, vwn = kw_new[:, 0:DKV], kw_new[:, DKV:2 * DKV]
    s_w = _dot(qs_b, win_ref[0, 0].astype(BF16))
    s_wn = jnp.sum(qs * kwn, axis=-1, keepdims=True)
    mw = jnp.maximum(jnp.max(s_w, axis=-1, keepdims=True), s_wn)
    e_w = jnp.exp2(s_w - mw)
    e_wn = jnp.exp2(s_wn - mw)
    inv_w = 1.0 / (jnp.sum(e_w, axis=-1, keepdims=True) + e_wn)
    o_win = _pick_kv_group(_dot_nt((e_w * inv_w).astype(BF16), win_ref[0, 1].astype(BF16))
                           + (e_wn * inv_w).astype(BF16).astype(F32) * vwn.astype(BF16).astype(F32))

    g = gates_ref[0]
    o_ref[0] = g[:, 0:1] * o_cmp + g[:, 1:2] * o_sel + g[:, 2:3] * o_win

    wb = win_ref.shape[3]
    nb = kwvwt_ref.shape[1]
    mine = lax.broadcasted_iota(jnp.int32, (1, nb), 1) == b_idx
    last = lax.broadcasted_iota(jnp.int32, (1, wb), 1) == wb - 1
    for c in range(2):
        new_col = jnp.sum(jnp.where(mine, kwvwt_ref[DKV * c:DKV * (c + 1), :], 0.0), axis=-1, keepdims=True)
        winout_ref[0, c] = jnp.where(last, new_col, pltpu.roll(win_ref[0, c], wb - 1, 1))


def _nsa_sample_t(cache4, page_table, win4, q_raw, q_rot, gates, kv_new, kw_new, kwvw_t, cmp_k, cmp_v, mcs, e_mat,
                  n_seq):
    nb, n_pages = page_table.shape
    page = cache4.shape[3]
    wb = win4.shape[3]
    seg_pp = page // CMP_STRIDE
    rows = np.arange(page)
    perm = np.zeros((page, page), np.float32)
    perm[rows, (rows % seg_pp) * CMP_STRIDE + rows // seg_pp] = 1.0
    w1k, pek, w1kf, w2k = cmp_k
    w1v, pev, w1vf, w2v = cmp_v
    consts = [kwvw_t, jnp.asarray(perm, BF16), w1k, w1v, pek, pev, w1kf, w1vf, w2k, w2v, mcs, e_mat]
    full = lambda a: pl.BlockSpec(a.shape, lambda b, pt: (0,) * a.ndim)
    page_specs = [pl.BlockSpec((1, 4, DKV, page), functools.partial(
        lambda b, pt, kk: (pt[b * n_seq * n_pages + kk], 0, 0, 0), kk=kk)) for kk in range(n_seq * n_pages)]
    per_b = lambda shape: pl.BlockSpec((n_seq,) + shape, lambda b, pt: (b,) + (0,) * len(shape))
    grid_spec = pltpu.PrefetchScalarGridSpec(
        num_scalar_prefetch=1,
        grid=(nb // n_seq,),
        in_specs=page_specs + [per_b((2, DKV, wb)), per_b((NH, HD)), per_b((NH, HD)), per_b((NH, 3)),
                               per_b((1, 4 * DKV)), per_b((1, 2 * DKV))] + [full(a) for a in consts],
        out_specs=[per_b((NH, HD)), per_b((2, DKV, wb))],
        scratch_shapes=[pltpu.VMEM((2, CMP_STRIDE, n_seq * n_pages * seg_pp, DKV), F32)],
    )
    return pl.pallas_call(
        functools.partial(_nsa_sample_t_body, n_pages, n_seq),
        grid_spec=grid_spec,
        out_shape=[jax.ShapeDtypeStruct((nb, NH, HD), F32), jax.ShapeDtypeStruct((nb, 2, DKV, wb), F32)],
        compiler_params=_cparams(1),
        name="nsa_sample",
    )(page_table.reshape(-1), *([cache4] * (n_seq * n_pages)), win4, q_raw, q_rot, gates, kv_new, kw_new, *consts)


def _rwkv_prep(cols, prev, mu, w0, w2, a0, a2, g2, kk_p, ka):
    xs = cols + (prev - cols) * mu
    r = xs[:, 0:DH]
    k = xs[:, DH:2 * DH]
    v = xs[:, 2 * DH:3 * DH]
    o = 3 * DH
    wd = xs[:, o:o + LORA_W]
    ad = xs[:, o + LORA_W:o + LORA_W + LORA_A]
    gd = xs[:, o + LORA_W + LORA_A:o + LORA_W + LORA_A + LORA_G_PAD]
    w = w0 + _dot(jnp.tanh(wd).astype(BF16), w2)
    w_log = -jax.nn.softplus(-w) - 0.5
    lw = -jnp.exp(w_log)
    a = _sigmoid(a0 + _dot(ad.astype(BF16), a2))
    g = _dot(_sigmoid(gd).astype(BF16), g2)
    kkv = k * kk_p
    k_mod = k * (1.0 + (a - 1.0) * ka)
    return r, k_mod, v, kkv, a, lw, g


def _head_norm(kkv_h):
    return kkv_h / jnp.maximum(jnp.sqrt(jnp.sum(kkv_h * kkv_h, axis=-1, keepdims=True)), 1e-12)


def _rwkv_head_out(y, r_h, k_h, v_h, g_h, rk_h, gnw_h, gnb_h):
    mean = jnp.mean(y, axis=-1, keepdims=True)
    var = jnp.mean(jnp.square(y - mean), axis=-1, keepdims=True)
    yn = (y - mean) * lax.rsqrt(var + GN_EPS) * gnw_h + gnb_h
    bonus = jnp.sum(r_h * k_h * rk_h, axis=-1, keepdims=True) * v_h
    return (yn + bonus) * g_h


def _head_ones():
    i = lax.broadcasted_iota(jnp.int32, (4 * HD, 4 * HD), 0) // HD
    j = lax.broadcasted_iota(jnp.int32, (4 * HD, 4 * HD), 1) // HD
    return jnp.where(i == j, 1.0, 0.0).astype(BF16)


def _head_sums(x, head_ones):
    rows, width = x.shape
    groups = width // (4 * HD)
    stacked = jnp.concatenate([x[:, 4 * HD * j:4 * HD * (j + 1)] for j in range(groups)], axis=0)
    sums = _dot_split3(stacked, head_ones)
    return jnp.concatenate([sums[rows * j:rows * (j + 1)] for j in range(groups)], axis=1)


def _tri_inverse_all(a_list, blk16, blk32):
    c = a_list[0].shape[0]
    eye = jnp.where(lax.broadcasted_iota(jnp.int32, (c, c), 0) == lax.broadcasted_iota(jnp.int32, (c, c), 1),
                    1.0, 0.0)
    b = lambda x: x.astype(BF16)
    a16 = [b(jnp.where(blk16, a, 0.0)) for a in a_list]
    p = [_dot(x, x) for x in a16]
    t = [_dot(b(eye) - x, b(eye + q)) for x, q in zip(a16, p)]
    for _ in range(2):
        pb = [b(q) for q in p]
        p = [_dot(q, q) for q in pb]
        t = [_dot(b(x), b(eye + q)) for x, q in zip(t, p)]
    off32 = blk32 & jnp.logical_not(blk16)
    for mask in (off32, jnp.logical_not(blk32)):
        tb = [b(x) for x in t]
        m = [_dot(x, b(jnp.where(mask, a, 0.0))) for x, a in zip(tb, a_list)]
        t = [x - _dot(b(y), xb) for x, y, xb in zip(t, m, tb)]
    return t


def _rwkv_prompt_body(cols_ref, mu_ref, w0_ref, w2_ref, a0_ref, a2_ref, g2_ref, kk_ref, ka_ref, rk_ref,
                      gnw_ref, gnb_ref, o_ref, wkv_ref, s_ref, last_ref):
    c = pl.program_id(1)
    ch = RW_CHUNK

    @pl.when(c == 0)
    def _():
        s_ref[...] = jnp.zeros_like(s_ref)
        last_ref[...] = jnp.zeros_like(last_ref)

    cols = cols_ref[...]
    row = lax.broadcasted_iota(jnp.int32, (ch, 1), 0)
    prev = jnp.where(row == 0, last_ref[...], pltpu.roll(cols, 1, 0))
    last_ref[...] = cols[ch - 1:ch, :]
    r, k_mod, v, kkv, a, lw, g = _rwkv_prep(cols, prev, mu_ref[...], w0_ref[...], w2_ref[...], a0_ref[...],
                                            a2_ref[...], g2_ref[...], kk_ref[...], ka_ref[...])
    ti = lax.broadcasted_iota(jnp.int32, (ch, ch), 0)
    si = lax.broadcasted_iota(jnp.int32, (ch, ch), 1)
    ltri = jnp.where(ti >= si, 1.0, 0.0).astype(BF16)
    cl = _cumsum_rows(lw, ltri)
    e_in = jnp.exp(cl)
    e_ex = jnp.exp(cl - lw)
    e_ng = jnp.exp(-cl)
    cl_end = cl[ch - 1:ch, :]
    e_end = jnp.exp(cl_end - cl)
    g_end = jnp.exp(cl_end)
    strict = ti > si
    causal2 = (lax.broadcasted_iota(jnp.int32, (ch, 2 * ch), 0)
               >= lax.broadcasted_iota(jnp.int32, (ch, 2 * ch), 1) % ch)
    blk16 = (ti // 16) == (si // 16)
    blk32 = (ti // 32) == (si // 32)
    rk, gnw, gnb = rk_ref[...], gnw_ref[...], gnb_ref[...]
    b = lambda x: x.astype(BF16)

    heads = range(NH)
    sls = [slice(HD * h, HD * (h + 1)) for h in heads]

    head_ones = _head_ones()
    kk_n = kkv * jnp.minimum(lax.rsqrt(_head_sums(kkv * kkv, head_ones)), 1e12)
    beta = kk_n * a
    kq, rq = b(kk_n * e_ex), b(r * e_in)
    bd, kd = b(beta * e_ng), b(k_mod * e_ng)
    bdec, kdec = b(beta * e_end), b(k_mod * e_end)
    lhs = [jnp.concatenate([kq[:, sl], rq[:, sl]], axis=0) for sl in sls]
    rhs = [jnp.concatenate([bd[:, sl], kd[:, sl]], axis=0) for sl in sls]
    dec = [jnp.concatenate([bdec[:, sl], kdec[:, sl]], axis=0) for sl in sls]
    s0 = [s_ref[h] for h in heads]
    quad = [_dot_nt(lhs[h], rhs[h]) for h in heads]
    s0t = [_dot_nt(lhs[h], b(s0[h])) for h in heads]
    a_b = [jnp.where(strict, q[0:ch, 0:ch], 0.0) for q in quad]
    a_kv = [_dot(b(jnp.where(strict, quad[h][0:ch, ch:2 * ch], 0.0)), b(v[:, sls[h]])) for h in heads]
    b_bk = [b(jnp.where(causal2, q[ch:2 * ch, :], 0.0)) for q in quad]
    t_inv = _tri_inverse_all(a_b, blk16, blk32)
    u = [-_dot(b(t_inv[h]), b(s0t[h][0:ch] + a_kv[h])) for h in heads]
    uv = [b(jnp.concatenate([u[h], v[:, sls[h]]], axis=0)) for h in heads]
    y = [s0t[h][ch:2 * ch] + _dot(b_bk[h], uv[h]) for h in heads]
    s_new = [s0[h] * g_end[:, sls[h]] + _dot_tn(uv[h], dec[h]) for h in heads]
    for h in heads:
        s_ref[h] = s_new[h]
    y_all = jnp.concatenate(y, axis=1)
    yc = y_all - _head_sums(y_all, head_ones) * (1.0 / HD)
    var = _head_sums(yc * yc, head_ones) * (1.0 / HD)
    yn = yc * lax.rsqrt(var + GN_EPS) * gnw + gnb
    bonus = _head_sums(r * k_mod * rk, head_ones) * v
    o_ref[...] = ((yn + bonus) * g).astype(o_ref.dtype)

    @pl.when(c == pl.num_programs(1) - 1)
    def _():
        wkv_ref[0] = s_ref[...]


def _cumsum_rows(x, ltri):
    hi = x.astype(BF16)
    r1 = x - hi.astype(F32)
    mid = r1.astype(BF16)
    lo = (r1 - mid.astype(F32)).astype(BF16)
    return _dot(ltri, hi) + _dot(ltri, mid) + _dot(ltri, lo)


def _rwkv_prompt(cols, rw, n_batch, t):
    nc = t // RW_CHUNK
    full = lambda a: pl.BlockSpec(a.shape, lambda b, c: (0,) * a.ndim)
    row = lambda b, c: (b * nc + c, 0)
    return pl.pallas_call(
        _rwkv_prompt_body,
        grid=(n_batch, nc),
        in_specs=[pl.BlockSpec((RW_CHUNK, RWKV_PAD), row)] + [full(a) for a in rw],
        out_specs=[pl.BlockSpec((RW_CHUNK, DH), row),
                   pl.BlockSpec((1, NH, HD, HD), lambda b, c: (b, 0, 0, 0))],
        out_shape=[jax.ShapeDtypeStruct((n_batch * t, DH), F32),
                   jax.ShapeDtypeStruct((n_batch, NH, HD, HD), F32)],
        scratch_shapes=[pltpu.VMEM((NH, HD, HD), F32), pltpu.VMEM((1, RWKV_PAD), F32)],
        compiler_params=_cparams(2),
        name="rwkv_prompt",
    )(cols, *rw)


def _rwkv_sample_prep_body(cols_ref, prev_ref, mu_ref, w0_ref, w2_ref, a0_ref, a2_ref, g2_ref, kk_ref, ka_ref,
                           r_ref, k_ref, v_ref, kkn_ref, a_ref, d_ref, g_ref):
    r, k_mod, v, kkv, a, lw, g = _rwkv_prep(cols_ref[...], prev_ref[...], mu_ref[...], w0_ref[...], w2_ref[...],
                                            a0_ref[...], a2_ref[...], g2_ref[...], kk_ref[...], ka_ref[...])
    r_ref[...] = r
    k_ref[...] = k_mod
    v_ref[...] = v
    a_ref[...] = a
    d_ref[...] = jnp.exp(lw)
    g_ref[...] = g
    for h in range(NH):
        sl = slice(HD * h, HD * (h + 1))
        kkn_ref[:, sl] = _head_norm(kkv[:, sl])


def _rwkv_sample_prep(cols, prev, rw_prep):
    n = cols.shape[0]
    args = [cols, prev] + list(rw_prep)
    full = lambda a: pl.BlockSpec(a.shape, lambda i: (0,) * a.ndim)
    out = jax.ShapeDtypeStruct((n, DH), F32)
    return pl.pallas_call(
        _rwkv_sample_prep_body,
        grid=(1,),
        in_specs=[full(a) for a in args],
        out_specs=[pl.BlockSpec((n, DH), lambda i: (0, 0))] * 7,
        out_shape=[out] * 7,
        compiler_params=_cparams(1),
        name="rwkv_sample_prep",
    )(*args)


def _rwkv_sample_step_body(s_ref, r_ref, k_ref, v_ref, kk_ref, a_ref, d_ref, g_ref, rk_ref, gnw_ref, gnb_ref,
                           o_ref, sout_ref):
    s = s_ref[0]
    r, k, v, kk, a, d, g = (x[0] for x in (r_ref, k_ref, v_ref, kk_ref, a_ref, d_ref, g_ref))
    eye = jnp.where(lax.broadcasted_iota(jnp.int32, (HD, HD), 0) == lax.broadcasted_iota(jnp.int32, (HD, HD), 1),
                    1.0, 0.0)
    row = lambda x: x[:, None, :]
    col = lambda x: jnp.sum(eye[None] * x[:, None, :], axis=-1, keepdims=True)
    sa = -jnp.sum(s * row(kk), axis=-1, keepdims=True)
    s_new = s * row(d) + sa * row(kk * a) + col(v) * row(k)
    sout_ref[0] = s_new
    y_col = jnp.sum(s_new * row(r), axis=-1, keepdims=True)
    y = jnp.sum(y_col * eye[None], axis=1)
    o_ref[0] = _rwkv_head_out(y, r, k, v, g, rk_ref[...], gnw_ref[...], gnb_ref[...])


def _rwkv_sample_step(state, vecs, rk, gnw, gnb):
    nb = state.shape[0]
    per_b3 = pl.BlockSpec((1, NH, HD), lambda b: (b, 0, 0))
    per_b4 = pl.BlockSpec((1, NH, HD, HD), lambda b: (b, 0, 0, 0))
    full = pl.BlockSpec((NH, HD), lambda b: (0, 0))
    return pl.pallas_call(
        _rwkv_sample_step_body,
        grid=(nb,),
        in_specs=[per_b4] + [per_b3] * 7 + [full] * 3,
        out_specs=[per_b3, per_b4],
        out_shape=[jax.ShapeDtypeStruct((nb, NH, HD), F32), jax.ShapeDtypeStruct((nb, NH, HD, HD), F32)],
        compiler_params=_cparams(1),
        name="rwkv_sample_step",
    )(state, *vecs, rk, gnw, gnb)


def _rwkv_sample_prep_t_body(cols_ref, kv_ref, kwvw_ref, prev_ref, mu_ref, w0_ref, w2t_ref, a0_ref, a2t_ref,
                             g2t_ref, kk_ref, ka_ref,
                             colst_ref, kvt_ref, kwvwt_ref, r_ref, k_ref, v_ref, kkn_ref, kka_ref, d_ref, g_ref,
                             tr_ref):
    nb = cols_ref.shape[0]
    for j in range(cols_ref.shape[1] // 128):
        tr_ref[128 * j:128 * (j + 1), :] = cols_ref[:, 128 * j:128 * (j + 1)].T
    for j in range(kv_ref.shape[1] // 128):
        kvt_ref[128 * j:128 * (j + 1), :] = kv_ref[:, 128 * j:128 * (j + 1)].T
    for j in range(kwvw_ref.shape[1] // 128):
        kwvwt_ref[128 * j:128 * (j + 1), :] = kwvw_ref[:, 128 * j:128 * (j + 1)].T
    cols = tr_ref[0:RWKV_COLS, :]
    colst_ref[...] = cols
    xs = cols + (prev_ref[...] - cols) * mu_ref[...]
    r = xs[0:DH]
    k = xs[DH:2 * DH]
    v = xs[2 * DH:3 * DH]
    o = 3 * DH
    wd = xs[o:o + LORA_W]
    ad = xs[o + LORA_W:o + LORA_W + LORA_A]
    gd = xs[o + LORA_W + LORA_A:RWKV_COLS]
    w = w0_ref[...] + _dot(w2t_ref[...], jnp.tanh(wd).astype(BF16))
    w_log = -jax.nn.softplus(-w) - 0.5
    a = _sigmoid(a0_ref[...] + _dot(a2t_ref[...], ad.astype(BF16)))
    kkv = (k * kk_ref[...]).reshape(NH, HD, nb)
    norm = jnp.maximum(jnp.sqrt(jnp.sum(kkv * kkv, axis=1, keepdims=True)), 1e-12)
    kkn = (kkv / norm).reshape(DH, nb)
    r_ref[...] = r
    k_ref[...] = k * (1.0 + (a - 1.0) * ka_ref[...])
    v_ref[...] = v
    kkn_ref[...] = kkn
    kka_ref[...] = kkn * a
    d_ref[...] = jnp.exp(-jnp.exp(w_log))
    g_ref[...] = _dot(g2t_ref[...], _sigmoid(gd).astype(BF16))


def _rwkv_sample_prep_t(cols, kv_rows, kwvw, prev_t, params_t):
    nb = cols.shape[0]
    args = [cols, kv_rows, kwvw, prev_t] + list(params_t)
    full = lambda shape: pl.BlockSpec(shape, lambda i: (0,) * len(shape))
    out_rows = [RWKV_COLS, kv_rows.shape[1], kwvw.shape[1]] + [DH] * 7
    return pl.pallas_call(
        _rwkv_sample_prep_t_body,
        grid=(1,),
        in_specs=[full(a.shape) for a in args],
        out_specs=[full((rows, nb)) for rows in out_rows],
        out_shape=[jax.ShapeDtypeStruct((rows, nb), F32) for rows in out_rows],
        scratch_shapes=[pltpu.VMEM((cols.shape[1], nb), F32)],
        compiler_params=_cparams(1),
        name="rwkv_sample_prep",
    )(*args)


def _rwkv_sample_step_t_body(s_ref, r_ref, k_ref, v_ref, kkn_ref, kka_ref, d_ref, g_ref, rk_ref, gnw_ref, gnb_ref,
                             o_ref, sout_ref, y_ref):
    r, k, kkn, kka, d = r_ref[...], k_ref[...], kkn_ref[...], kka_ref[...], d_ref[...]

    def value_row(i, carry):
        s_i = s_ref[0, i]
        sa = -jnp.sum(s_i * kkn, axis=0, keepdims=True)
        s_new = s_i * d + sa * kka + v_ref[pl.ds(i, 1), :] * k
        sout_ref[0, i] = s_new
        y_ref[pl.ds(i, 1), :] = jnp.sum(s_new * r, axis=0, keepdims=True)
        return carry

    lax.fori_loop(0, HD, value_row, 0)
    y = y_ref[...]
    v = v_ref[...]
    mean = jnp.mean(y, axis=0, keepdims=True)
    var = jnp.mean(jnp.square(y - mean), axis=0, keepdims=True)
    yn = (y - mean) * lax.rsqrt(var + GN_EPS) * gnw_ref[...] + gnb_ref[...]
    bonus = jnp.sum(r * k * rk_ref[...], axis=0, keepdims=True) * v
    o_ref[...] = (yn + bonus) * g_ref[...]


def _rwkv_sample_step_t(state4, vecs_t, rk_t, gnw_t, gnb_t):
    nh, hs, _, nb = state4.shape
    head_rows = pl.BlockSpec((hs, nb), lambda h: (h, 0))
    state_spec = pl.BlockSpec((1, hs, hs, nb), lambda h: (h, 0, 0, 0))
    return pl.pallas_call(
        _rwkv_sample_step_t_body,
        grid=(nh,),
        in_specs=[state_spec] + [head_rows] * 10,
        out_specs=[head_rows, state_spec],
        out_shape=[jax.ShapeDtypeStruct((nh * hs, nb), F32), jax.ShapeDtypeStruct(state4.shape, F32)],
        scratch_shapes=[pltpu.VMEM((hs, nb), F32)],
        compiler_params=_cparams(1),
        name="rwkv_sample_step",
    )(state4, *vecs_t, rk_t, gnw_t, gnb_t)


def _outproj_t_body(oa_ref, orwt_ref, wa_ref, wr_ref, h_ref, g_ref, o_ref):
    mixed = _dot(oa_ref[...].astype(BF16), wa_ref[...]) + _dot_tn(orwt_ref[...].astype(BF16), wr_ref[...])
    o_ref[...] = h_ref[...] + _rms(mixed, g_ref[...])


def _outproj_t(o_attn, o_rwkv_t, w_a, w_r, h, g_post):
    n, d = h.shape
    full = lambda a: pl.BlockSpec(a.shape, lambda i: (0,) * a.ndim)
    args = [o_attn, o_rwkv_t, w_a, w_r, h, g_post]
    return pl.pallas_call(
        _outproj_t_body,
        grid=(1,),
        in_specs=[full(a) for a in args],
        out_specs=pl.BlockSpec((n, d), lambda i: (0, 0)),
        out_shape=jax.ShapeDtypeStruct((n, d), F32),
        compiler_params=_cparams(1),
        name="outproj_sample",
    )(*args)


def _outproj_body(oa_ref, orw_ref, wa_ref, wr_ref, h_ref, g_ref, o_ref):
    mixed = _dot(oa_ref[...].astype(BF16), wa_ref[...]) + _dot(orw_ref[...].astype(BF16), wr_ref[...])
    o_ref[...] = h_ref[...] + _rms(mixed, g_ref[...])


def _outproj(o_attn, o_rwkv, w_a, w_r, h, g_post, tm):
    n, d = h.shape
    row = lambda i: (i, 0)
    const = lambda i: (0, 0)
    return pl.pallas_call(
        _outproj_body,
        grid=(n // tm,),
        in_specs=[pl.BlockSpec((tm, DH), row), pl.BlockSpec((tm, DH), row),
                  pl.BlockSpec((DH, d), const), pl.BlockSpec((DH, d), const),
                  pl.BlockSpec((tm, d), row), pl.BlockSpec((1, d), const)],
        out_specs=pl.BlockSpec((tm, d), row),
        out_shape=jax.ShapeDtypeStruct((n, d), F32),
        compiler_params=_cparams(1),
        name="outproj",
    )(o_attn, o_rwkv, w_a, w_r, h, g_post)


def _rope_tables(pos):
    inv = ROPE_THETA ** (-jnp.arange(ROPE_HALF, dtype=F32) / ROPE_HALF)
    ang = pos.astype(F32)[:, None] * inv[None, :]
    cos, sin = jnp.cos(ang), jnp.sin(ang)
    n = pos.shape[0]
    rest = HD - 2 * ROPE_HALF
    c = jnp.concatenate([cos, cos, jnp.ones((n, rest), F32)], axis=1)
    s_lo = jnp.concatenate([-sin, jnp.zeros((n, HD - ROPE_HALF), F32)], axis=1)
    s_hi = jnp.concatenate([jnp.zeros((n, ROPE_HALF), F32), sin, jnp.zeros((n, rest), F32)], axis=1)
    two = lambda x: jnp.concatenate([x, x], axis=1)
    return two(c), two(s_lo), two(s_hi)


def _cmp_to_sel(n_cmp, n_sel, rows):
    i = np.arange(n_cmp)[:, None] * CMP_STRIDE
    j = np.arange(n_sel)[None, :] * SEL_BLOCK
    ov = np.minimum(i + CMP_BLOCK, j + SEL_BLOCK) - np.maximum(i, j)
    m = np.zeros((rows, N_SEL_LANES), np.float32)
    m[:n_cmp, :n_sel] = np.maximum(ov, 0) // CMP_STRIDE
    return jnp.asarray(m, BF16)


def _block_expand(n_keys):
    j = np.arange(N_SEL_LANES)[:, None]
    t = np.arange(n_keys)[None, :]
    return jnp.asarray((t // SEL_BLOCK == j).astype(np.float32), BF16)


def _block_expand_t(n_keys):
    t = np.arange(n_keys)[:, None]
    j = np.arange(N_SEL_LANES)[None, :]
    return jnp.asarray((t // SEL_BLOCK == j).astype(np.float32), BF16)


def _pad_cols(x, width):
    return jnp.pad(x, ((0, 0), (0, width - x.shape[1])))


def _row_tile(n, cap):
    tm = min(n, cap)
    while n % tm:
        tm //= 2
    return tm


def kernel(x_prompt, x_sample, cache_nsa, page_table, state_win, state_wkv, state_shift, norm_f1_pre, norm_f1_post, ffn1_gu, ffn1_dn, norm_mix_pre, w_in, cmp_pe_k, cmp_w1_k, cmp_w2_k, cmp_pe_v, cmp_w1_v, cmp_w2_v, rw_mu, rw_w0, rw_w2, rw_a0, rw_a2, rw_g2, rw_kk, rw_ka, rw_rk, rw_gn_w, rw_gn_b, w_out, norm_mix_post, norm_f2_pre, ffn2_gu, ffn2_dn, norm_f2_post):
    depth = w_in.shape[0]
    assert depth == 1, "single-layer step"
    bp, t, d = x_prompt.shape
    bs, ts, _ = x_sample.shape
    assert ts == 1 and t % Q_BLOCK == 0 and t % RW_CHUNK == 0 and t >= WINDOW
    n_pages = page_table.shape[1]
    page = cache_nsa.shape[2]
    past = n_pages * page
    wb = state_win.shape[2]
    assert wb == min(WINDOW, past) and past % SEL_BLOCK == 0 and page % CMP_STRIDE == 0
    l = 0

    w1g, w1d = ffn1_gu[l].astype(BF16), ffn1_dn[l].astype(BF16)
    w2g, w2d = ffn2_gu[l].astype(BF16), ffn2_dn[l].astype(BF16)
    wi = w_in[l]
    w_nsa = _pad_cols(wi[:, :NSA_COLS], NSA_PAD).astype(BF16)
    w_rw = _pad_cols(wi[:, NSA_COLS:], RWKV_PAD).astype(BF16)
    w_oa, w_or = w_out[l, :DH].astype(BF16), w_out[l, DH:].astype(BF16)
    vec = lambda a: a.reshape(1, -1)
    cmp_k = _cmp_weights(cmp_pe_k[l], cmp_w1_k[l], cmp_w2_k[l])
    cmp_v = _cmp_weights(cmp_pe_v[l], cmp_w1_v[l], cmp_w2_v[l])
    g2_pad = jnp.pad(rw_g2[l], ((0, LORA_G_PAD - LORA_G), (0, 0))).astype(BF16)
    rw_prep = [_pad_cols(vec(rw_mu[l]), RWKV_PAD), vec(rw_w0[l]), rw_w2[l].astype(BF16), vec(rw_a0[l]),
               rw_a2[l].astype(BF16), g2_pad, vec(rw_kk[l]), vec(rw_ka[l])]
    rk_row, gnw_row, gnb_row = vec(rw_rk[l]), vec(rw_gn_w[l]), vec(rw_gn_b[l])

    outs = {}
    for name, x2, n_batch in (("p", x_prompt.reshape(bp * t, d), bp), ("s", x_sample.reshape(bs, d), bs)):
        n = x2.shape[0]
        tm = _row_tile(n, 512)
        is_prompt = name == "p"
        h1, hn = _ffn_half(x2, vec(norm_f1_pre[l]), w1g, w1d, vec(norm_f1_post[l]), vec(norm_mix_pre[l]),
                           tm, 512)
        pos = jnp.tile(jnp.arange(t), bp) if is_prompt else jnp.full((n,), past, jnp.int32)
        rc, rlo, rhi = _rope_tables(pos)
        q_raw, q_rot, kv_rows, kwvw, ks, vs, kw, vw, gates, *kv_t = _inproj_nsa(
            hn, w_nsa, rc, rlo, rhi, tm, t if is_prompt else None)
        cols = _inproj_rwkv(hn, w_rw, tm)
        if is_prompt:
            k_cmp, v_cmp = _compress_prompt(kv_rows, cmp_k, cmp_v, n_batch, t)
            n_seg = t // CMP_STRIDE
            mcs_t = _cmp_to_sel(n_seg - 1, t // SEL_BLOCK, n_seg).T
            o_attn = _attn_prompt(q_raw, q_rot, gates, k_cmp, v_cmp, ks, vs, kw, vw, mcs_t, _block_expand_t(t),
                                  n_batch, t)
            o_rwkv, wkv = _rwkv_prompt(cols, rw_prep + [rk_row, gnw_row, gnb_row], n_batch, t)
            win = kwvw.reshape(n_batch, t, 2 * DKV)[:, t - wb:]
            kv_out = jnp.transpose(kv_t[0].reshape(n_batch, 4, KVH, HD, t), (0, 4, 1, 2, 3))
            shift = cols.reshape(n_batch, t, RWKV_PAD)[:, t - 1:, :RWKV_COLS]
        else:
            cache4 = jnp.transpose(cache_nsa[l], (0, 2, 3, 4, 1)).reshape(cache_nsa.shape[1], 4, DKV, page)
            win4 = jnp.transpose(state_win[l], (0, 2, 3, 4, 1)).reshape(n, 2, DKV, wb)
            wkv4 = jnp.transpose(state_wkv[l], (1, 2, 3, 0))
            prev_t = jnp.transpose(state_shift[l].reshape(n, RWKV_COLS))
            lanes = lambda a: jnp.broadcast_to(a.reshape(-1, 1), (a.size, n))
            params_t = [lanes(rw_mu[l]), lanes(rw_w0[l]), rw_w2[l].T.astype(BF16), lanes(rw_a0[l]),
                        rw_a2[l].T.astype(BF16), rw_g2[l].T.astype(BF16), lanes(rw_kk[l]), lanes(rw_ka[l])]
            cols_t, kv_t, kwvw_t, *vecs_t = _rwkv_sample_prep_t(cols, kv_rows, kwvw, prev_t, params_t)
            n_seg = past // CMP_STRIDE
            mcs = _cmp_to_sel(n_seg - 1, past // SEL_BLOCK + 1, n_seg)
            o_attn, win4_new = _nsa_sample_t(
                cache4, page_table, win4,
                q_raw.astype(F32).reshape(n, NH, HD), q_rot.astype(F32).reshape(n, NH, HD),
                gates[:, :3 * NH].reshape(n, NH, 3), kv_rows.reshape(n, 1, 4 * DKV), kwvw.reshape(n, 1, 2 * DKV),
                kwvw_t, cmp_k, cmp_v, mcs, _block_expand(past), SAMPLE_SEQS_PER_STEP)
            o_attn = o_attn.reshape(n, DH)
            o_rwkv_t, wkv4_new = _rwkv_sample_step_t(wkv4, vecs_t, lanes(rw_rk[l]), lanes(rw_gn_w[l]),
                                                     lanes(rw_gn_b[l]))
            kv_out = jnp.transpose(kv_t.reshape(4, KVH, HD, n), (3, 0, 1, 2))
            win = jnp.transpose(win4_new.reshape(n, 2, KVH, HD, wb), (0, 4, 1, 2, 3))
            wkv = jnp.transpose(wkv4_new, (3, 0, 1, 2))
            shift = jnp.transpose(cols_t)[:, None, :]
            h2 = _outproj_t(o_attn, o_rwkv_t, w_oa, w_or, h1, vec(norm_mix_post[l]))
        if is_prompt:
            h2 = _outproj(o_attn, o_rwkv, w_oa, w_or, h1, vec(norm_mix_post[l]), tm)
        y = _ffn_half(h2, vec(norm_f2_pre[l]), w2g, w2d, vec(norm_f2_post[l]), None, tm, 512)
        outs[name] = (y, kv_out, win, wkv, shift)

    yp, kvp, winp, wkvp, shp = outs["p"]
    ys, kvs, wins, wkvs, shs = outs["s"]
    return (yp.reshape(bp, t, d), ys.reshape(bs, 1, d),
            kvp.reshape(1, bp, t, 4, KVH, HD), kvs.reshape(1, bs, 1, 4, KVH, HD),
            winp.reshape(1, bp, wb, 2, KVH, HD), wins.reshape(1, bs, wb, 2, KVH, HD),
            wkvp[None], wkvs[None], shp[None], shs[None])
```

```python
import functools

import numpy as np
import jax
import jax.numpy as jnp
from jax import lax
from jax.experimental import pallas as pl
from jax.experimental.pallas import tpu as pltpu

F32 = jnp.float32
BF16 = jnp.bfloat16

HD = 64
KVH = 4
GQA = 4
NH = 16
DH = NH * HD
DKV = KVH * HD
ROPE_HALF = 8
ROPE_THETA = 500000.0
CMP_BLOCK = 32
CMP_STRIDE = 16
CMP_HIDDEN = 128
SEL_BLOCK = 64
SEL_TOPN = 16
N_LOCAL_FORCED = 2
WINDOW = 512
Q_BLOCK = 128
LORA_W = 64
LORA_A = 64
LORA_G = 160
RMS_EPS = 1e-6
GN_EPS = 64e-5
NEG = -1e30
FORCE_BONUS = 1e4
LOG2_E = 1.4426950408889634
MASK_BIAS = -1e30
M_FLOOR = -1e20
NSA_COLS = DH + 6 * DKV + 3 * NH
RWKV_COLS = 3 * DH + LORA_W + LORA_A + LORA_G
NSA_PAD = 2688
RWKV_PAD = 3456
LORA_G_PAD = 256
RW_GROUP = 4
RW_CHUNK = 64
SEL_KEY_CHUNK = 512
N_SEL_LANES = 64
SAMPLE_SEQS_PER_STEP = 2

VMEM_LIMIT_BYTES = 56 * 1024 * 1024


def _cparams(n_axes):
    return pltpu.CompilerParams(dimension_semantics=("arbitrary",) * n_axes,
                                vmem_limit_bytes=VMEM_LIMIT_BYTES)


def _dot(a, b):
    return jnp.dot(a, b, preferred_element_type=F32)


def _dot_nt(a, b):
    return lax.dot_general(a, b, (((1,), (1,)), ((), ())), preferred_element_type=F32)


def _dot_tn(a, b):
    return lax.dot_general(a, b, (((0,), (0,)), ((), ())), preferred_element_type=F32)


def _dot_split3(a, b_bf16):
    hi = a.astype(BF16)
    r1 = a - hi.astype(F32)
    mid = r1.astype(BF16)
    lo = (r1 - mid.astype(F32)).astype(BF16)
    return _dot(hi, b_bf16) + _dot(mid, b_bf16) + _dot(lo, b_bf16)


def _rms(x, g):
    ms = jnp.mean(x * x, axis=-1, keepdims=True)
    return x * lax.rsqrt(ms + RMS_EPS) * g


def _sigmoid(x):
    return jax.nn.sigmoid(x)


def _ffn_body(has_next, x_ref, gpre_ref, wg_ref, wu_ref, wd_ref, gpost_ref, *rest):
    if has_next:
        gnext_ref, o_ref, on_ref, xn_ref, acc_ref = rest
    else:
        o_ref, xn_ref, acc_ref = rest
    j = pl.program_id(1)

    @pl.when(j == 0)
    def _():
        xn_ref[...] = _rms(x_ref[...], gpre_ref[...]).astype(BF16)
        acc_ref[...] = jnp.zeros_like(acc_ref)

    xn = xn_ref[...]
    g = _dot(xn, wg_ref[...])
    u = _dot(xn, wu_ref[...])
    act = ((g * _sigmoid(g)) * u).astype(BF16)
    acc_ref[...] += _dot(act, wd_ref[...])

    @pl.when(j == pl.num_programs(1) - 1)
    def _():
        y = x_ref[...] + 0.5 * _rms(acc_ref[...], gpost_ref[...])
        o_ref[...] = y
        if has_next:
            on_ref[...] = _rms(y, gnext_ref[...]).astype(BF16)


def _ffn_half(x, g_pre, w_gu, w_dn, g_post, g_next, tm, tf):
    n, d = x.shape
    f = w_dn.shape[0]
    nj = f // tf
    has_next = g_next is not None
    row = lambda i, j: (i, 0)
    const = lambda i, j: (0, 0)
    in_specs = [
        pl.BlockSpec((tm, d), row),
        pl.BlockSpec((1, d), const),
        pl.BlockSpec((d, tf), lambda i, j: (0, j)),
        pl.BlockSpec((d, tf), lambda i, j: (0, j + nj)),
        pl.BlockSpec((tf, d), lambda i, j: (j, 0)),
        pl.BlockSpec((1, d), const),
    ]
    args = [x, g_pre, w_gu, w_gu, w_dn, g_post]
    out_shape = [jax.ShapeDtypeStruct((n, d), F32)]
    out_specs = [pl.BlockSpec((tm, d), row)]
    if has_next:
        in_specs.append(pl.BlockSpec((1, d), const))
        args.append(g_next)
        out_shape.append(jax.ShapeDtypeStruct((n, d), BF16))
        out_specs.append(pl.BlockSpec((tm, d), row))
    res = pl.pallas_call(
        functools.partial(_ffn_body, has_next),
        grid=(n // tm, nj),
        in_specs=in_specs,
        out_specs=out_specs,
        out_shape=out_shape,
        scratch_shapes=[pltpu.VMEM((tm, d), BF16), pltpu.VMEM((tm, d), F32)],
        compiler_params=_cparams(2),
        name="ffn_half",
    )(*args)
    return res if has_next else res[0]


def _rope(x, c, s_lo, s_hi):
    w = x.shape[1]
    reps = w // 128
    tile = lambda t: t if reps == 1 else jnp.concatenate([t] * reps, axis=1)
    up = pltpu.roll(x, w - ROPE_HALF, 1)
    dn = pltpu.roll(x, ROPE_HALF, 1)
    return x * tile(c) + up * tile(s_lo) + dn * tile(s_hi)


def _inproj_nsa_body(hn_ref, w_ref, c_ref, slo_ref, shi_ref,
                     qraw_ref, qrot_ref, kv_ref, kwvw_ref, ks_ref, vs_ref, kw_ref, vw_ref, gates_ref, *kvt_ref):
    p = _dot(hn_ref[...], w_ref[...])
    c, s_lo, s_hi = c_ref[...], slo_ref[...], shi_ref[...]
    q = p[:, 0:DH] * (HD ** -0.5 * LOG2_E)
    qraw_ref[...] = q.astype(BF16)
    qrot_ref[...] = _rope(q, c, s_lo, s_hi).astype(BF16)
    o = DH
    ks = _rope(p[:, o + 2 * DKV:o + 3 * DKV], c, s_lo, s_hi)
    vs = p[:, o + 3 * DKV:o + 4 * DKV]
    kw = _rope(p[:, o + 4 * DKV:o + 5 * DKV], c, s_lo, s_hi)
    vw = p[:, o + 5 * DKV:o + 6 * DKV]
    kv_ref[:, 0:2 * DKV] = p[:, o:o + 2 * DKV]
    kv_ref[:, 2 * DKV:3 * DKV] = ks
    kv_ref[:, 3 * DKV:4 * DKV] = vs
    kwvw_ref[:, 0:DKV] = kw
    kwvw_ref[:, DKV:2 * DKV] = vw
    if kvt_ref:
        for j in range(4 * DKV // 128):
            kvt_ref[0][0, 128 * j:128 * (j + 1), :] = kv_ref[:, 128 * j:128 * (j + 1)].T
    vs_t, vw_t = vs.T, vw.T
    for k in range(KVH):
        sl = slice(HD * k, HD * (k + 1))
        ks_ref[k] = ks[:, sl].astype(BF16)
        vs_ref[k] = vs_t[sl, :].astype(BF16)
        kw_ref[k] = kw[:, sl].astype(BF16)
        vw_ref[k] = vw_t[sl, :].astype(BF16)
    gates_ref[...] = _sigmoid(p[:, o + 6 * DKV:o + 6 * DKV + 128])


def _inproj_nsa(hn, w_nsa, rope_c, rope_slo, rope_shi, tm, seq_len=None):
    n, d = hn.shape
    row = lambda i: (i, 0)
    hm = lambda i: (0, i, 0)
    hm_shape = jax.ShapeDtypeStruct((KVH, n, HD), BF16)
    hm_spec = pl.BlockSpec((KVH, tm, HD), hm)
    hmt_shape = jax.ShapeDtypeStruct((KVH, HD, n), BF16)
    hmt_spec = pl.BlockSpec((KVH, HD, tm), lambda i: (0, 0, i))
    out_specs = [pl.BlockSpec((tm, DH), row), pl.BlockSpec((tm, DH), row),
                 pl.BlockSpec((tm, 4 * DKV), row), pl.BlockSpec((tm, 2 * DKV), row),
                 hm_spec, hmt_spec, hm_spec, hmt_spec,
                 pl.BlockSpec((tm, 128), row)]
    out_shape = [jax.ShapeDtypeStruct((n, DH), BF16), jax.ShapeDtypeStruct((n, DH), BF16),
                 jax.ShapeDtypeStruct((n, 4 * DKV), F32), jax.ShapeDtypeStruct((n, 2 * DKV), F32),
                 hm_shape, hmt_shape, hm_shape, hmt_shape,
                 jax.ShapeDtypeStruct((n, 128), F32)]
    if seq_len is not None:
        per_seq = seq_len // tm
        out_specs.append(pl.BlockSpec((1, 4 * DKV, tm), lambda i: (i // per_seq, 0, i % per_seq)))
        out_shape.append(jax.ShapeDtypeStruct((n // seq_len, 4 * DKV, seq_len), F32))
    return pl.pallas_call(
        _inproj_nsa_body,
        grid=(n // tm,),
        in_specs=[pl.BlockSpec((tm, d), row),
                  pl.BlockSpec((d, NSA_PAD), lambda i: (0, 0)),
                  pl.BlockSpec((tm, 128), row), pl.BlockSpec((tm, 128), row), pl.BlockSpec((tm, 128), row)],
        out_specs=out_specs,
        out_shape=out_shape,
        compiler_params=_cparams(1),
        name="inproj_nsa",
    )(hn, w_nsa, rope_c, rope_slo, rope_shi)


def _matmul_body(x_ref, w_ref, o_ref):
    o_ref[...] = _dot(x_ref[...], w_ref[...])


def _inproj_rwkv(hn, w_rw, tm):
    n, d = hn.shape
    c = w_rw.shape[1]
    return pl.pallas_call(
        _matmul_body,
        grid=(n // tm,),
        in_specs=[pl.BlockSpec((tm, d), lambda i: (i, 0)), pl.BlockSpec((d, c), lambda i: (0, 0))],
        out_specs=pl.BlockSpec((tm, c), lambda i: (i, 0)),
        out_shape=jax.ShapeDtypeStruct((n, c), F32),
        compiler_params=_cparams(1),
        name="inproj_rwkv",
    )(hn, w_rw)


def _compress_accumulate(stage_ref, w1c_refs, n_seg):
    accs = [[jnp.zeros((n_seg, 2 * CMP_HIDDEN), F32) for _ in range(KVH)] for _ in range(2)]
    for s in range(CMP_STRIDE):
        for kv in range(2):
            w = w1c_refs[kv][s]
            for pair in range(KVH // 2):
                x = stage_ref[2 * kv + pair, pl.ds(s, n_seg, stride=CMP_STRIDE), :]
                for j in range(2):
                    k = 2 * pair + j
                    accs[kv][k] = accs[kv][k] + _dot(x[:, HD * j:HD * (j + 1)].astype(BF16), w)
    return accs


def _compress_finish(acc, pe0, w2, n_seg):
    a0 = acc[:, :CMP_HIDDEN]
    a1 = pltpu.roll(acc[:, CMP_HIDDEN:], n_seg - 1, 0)
    pre = pe0 + a0 + a1
    return _dot((pre * _sigmoid(pre)).astype(BF16), w2)


def _cmp_prompt_body(kv_ref, w1k_ref, w1v_ref, pek_ref, pev_ref, w1kf_ref, w1vf_ref, w2k_ref, w2v_ref,
                     kc_ref, vc_ref, stage_ref):
    n_seg = kv_ref.shape[0] // CMP_STRIDE
    for grp in range(4):
        stage_ref[grp] = kv_ref[:, 128 * grp:128 * (grp + 1)]
    accs = _compress_accumulate(stage_ref, (w1k_ref, w1v_ref), n_seg)
    for kv, (pe_ref, w1f_ref, w2_ref, out_ref) in enumerate(
            ((pek_ref, w1kf_ref, w2k_ref, kc_ref), (pev_ref, w1vf_ref, w2v_ref, vc_ref))):
        pe0 = _dot(pe_ref[...], w1f_ref[...])
        w2 = w2_ref[...]
        for k in range(KVH):
            c = _compress_finish(accs[kv][k], pe0, w2, n_seg)
            out_ref[0, k] = (c.T if kv == 1 else c).astype(BF16)


def _cmp_weights(pe, w1, w2):
    w1c = jnp.concatenate([w1[:CMP_STRIDE], w1[CMP_STRIDE:]], axis=-1).astype(BF16)
    return w1c, pe.reshape(1, CMP_BLOCK * HD), w1.reshape(CMP_BLOCK * HD, CMP_HIDDEN), w2.astype(BF16)


def _compress_prompt(kv_rows, cmp_k, cmp_v, n_batch, t):
    n_seg = t // CMP_STRIDE
    w1k, pek, w1kf, w2k = cmp_k
    w1v, pev, w1vf, w2v = cmp_v
    full = lambda a: pl.BlockSpec(a.shape, lambda b: (0,) * a.ndim)
    out_shape = jax.ShapeDtypeStruct((n_batch, KVH, n_seg, HD), BF16)
    out_spec = pl.BlockSpec((1, KVH, n_seg, HD), lambda b: (b, 0, 0, 0))
    out_shape_t = jax.ShapeDtypeStruct((n_batch, KVH, HD, n_seg), BF16)
    out_spec_t = pl.BlockSpec((1, KVH, HD, n_seg), lambda b: (b, 0, 0, 0))
    return pl.pallas_call(
        _cmp_prompt_body,
        grid=(n_batch,),
        in_specs=[pl.BlockSpec((t, 2 * DKV), lambda b: (b, 0)),
                  full(w1k), full(w1v), full(pek), full(pev), full(w1kf), full(w1vf), full(w2k), full(w2v)],
        out_specs=[out_spec, out_spec_t],
        out_shape=[out_shape, out_shape_t],
        scratch_shapes=[pltpu.VMEM((4, t, 128), F32)],
        compiler_params=_cparams(1),
        name="nsa_compress_prompt",
    )(kv_rows, w1k, w1v, pek, pev, w1kf, w1vf, w2k, w2v)


def _select_blocks(imp, cur, n_blocks):
    jb = lax.broadcasted_iota(jnp.int32, (1, N_SEL_LANES), 1)
    valid = (jb <= cur) & (jb < n_blocks)
    rel = cur - jb
    forced = (jb == 0) | ((rel >= 0) & (rel < N_LOCAL_FORCED))
    score = jnp.where(valid, imp + jnp.where(forced, FORCE_BONUS, 0.0), NEG)
    rank = jnp.zeros(score.shape, F32)
    for jp in range(min(n_blocks, N_SEL_LANES)):
        col = score[:, jp:jp + 1]
        tie = jnp.where(jb > jp, 1.0, 0.0)
        rank = rank + jnp.where(col > score, 1.0, jnp.where(col == score, tie, 0.0))
    return jnp.where(valid, jnp.where(rank < SEL_TOPN, 1.0, 0.0), 0.0)


def _softmax_rows(s, ok):
    sm = jnp.where(ok, s, NEG)
    m = jnp.max(sm, axis=-1, keepdims=True)
    e = jnp.where(ok, jnp.exp2(sm - m), 0.0)
    l = jnp.sum(e, axis=-1, keepdims=True)
    return e / jnp.maximum(l, 1e-30)


def _softmax_bias(s, bias):
    sb = s + bias
    m = jnp.maximum(jnp.max(sb, axis=-1, keepdims=True), M_FLOOR)
    e = jnp.exp(sb - m)
    return e, 1.0 / jnp.maximum(jnp.sum(e, axis=-1, keepdims=True), 1e-30)


def _select_blocks_t(imp_t, cur, n_blocks, n_live, score_ref):
    jb = lax.broadcasted_iota(jnp.int32, (N_SEL_LANES, 1), 0)
    valid = (jb <= cur) & (jb < n_blocks)
    rel = cur - jb
    forced = (jb == 0) | ((rel >= 0) & (rel < N_LOCAL_FORCED))
    score = jnp.where(valid, imp_t + jnp.where(forced, FORCE_BONUS, 0.0), NEG)
    score_ref[...] = score

    def body(jp, rank):
        other = score_ref[pl.ds(jp, 1), :]
        tie = jnp.where(jb > jp, 1.0, 0.0)
        return rank + jnp.where(other > score, 1.0, jnp.where(other == score, tie, 0.0))

    rank = lax.fori_loop(0, n_live, body, jnp.zeros(score.shape, F32))
    return jnp.where(valid, jnp.where(rank < SEL_TOPN, 1.0, 0.0), 0.0)


def _attn_prompt_body(qraw_ref, qrot_ref, gates_ref, kc_ref, vc_ref, ks_ref, vs_ref, kw_ref, vw_ref,
                      mcst_ref, e_ref, o_ref, score_ref):
    t = ks_ref.shape[1]
    n_seg = kc_ref.shape[2]
    i = pl.program_id(1)
    s0 = i * Q_BLOCK
    qpos = s0 + lax.broadcasted_iota(jnp.int32, (Q_BLOCK, 1), 0)
    qpos_row = s0 + lax.broadcasted_iota(jnp.int32, (1, Q_BLOCK), 1)
    n_idx = lax.broadcasted_iota(jnp.int32, (1, n_seg), 1)
    ok_cmp = (n_idx * CMP_STRIDE + (CMP_BLOCK - 1) <= qpos) & (n_idx < n_seg - 1)
    bias_cmp = jnp.where(ok_cmp, 0.0, MASK_BIAS)
    gates = gates_ref[...]
    mcst = mcst_ref[...]
    n_key_chunks = i // (SEL_KEY_CHUNK // Q_BLOCK) + 1
    n_live_blocks = (s0 + Q_BLOCK) // SEL_BLOCK
    win_start = pl.multiple_of(jnp.maximum(s0 - WINDOW, 0), Q_BLOCK)
    win_len = WINDOW + Q_BLOCK
    wpos = win_start + lax.broadcasted_iota(jnp.int32, (1, win_len), 1)
    dlt = qpos - wpos
    bias_win = jnp.where((dlt >= 0) & (dlt <= WINDOW), 0.0, MASK_BIAS)
    g_rows = [slice(g * Q_BLOCK, (g + 1) * Q_BLOCK) for g in range(GQA)]
    half = Q_BLOCK // 2

    kvhs = range(KVH)
    head_cols = lambda ref, k: jnp.concatenate(
        [ref[:, HD * (GQA * k + g):HD * (GQA * k + g + 1)] for g in range(GQA)], axis=0)
    qr = [head_cols(qraw_ref, k) for k in kvhs]
    qs = [head_cols(qrot_ref, k) for k in kvhs]

    s_cmp = [_dot_nt(qr[k], kc_ref[0, k]) for k in kvhs]
    o_cmp, sel_t = [], []
    for k in kvhs:
        p_heads = []
        for rows in g_rows:
            e_c, inv_c = _softmax_bias(s_cmp[k][rows], bias_cmp)
            p_heads.append(e_c * inv_c)
        o_cmp.append(_dot(jnp.concatenate([p.astype(BF16) for p in p_heads], axis=0), vc_ref[0, k]))
        p_sum = p_heads[0]
        for p in p_heads[1:]:
            p_sum = p_sum + p
        hi = p_sum.astype(BF16)
        r1 = p_sum - hi.astype(F32)
        mid = r1.astype(BF16)
        lo = (r1 - mid.astype(F32)).astype(BF16)
        imp_t = _dot_nt(mcst, hi) + _dot_nt(mcst, mid) + _dot_nt(mcst, lo)
        sel_t.append(_select_blocks_t(imp_t, qpos_row // SEL_BLOCK, t // SEL_BLOCK, n_live_blocks,
                                      score_ref.at[k]).astype(BF16))

    def chunk(c, carry):
        k0 = pl.multiple_of(c * SEL_KEY_CHUNK, SEL_KEY_CHUNK)
        kpos = k0 + lax.broadcasted_iota(jnp.int32, (1, SEL_KEY_CHUNK), 1)
        causal = kpos <= qpos
        e_blk = e_ref[:, pl.ds(k0, SEL_KEY_CHUNK)]
        s = [_dot_nt(qs[k], ks_ref[k, pl.ds(k0, SEL_KEY_CHUNK), :]) for k in kvhs]
        bias = [(jnp.where(causal, _dot_tn(sel_t[k], e_blk), 0.0) - 1.0) * (-MASK_BIAS) for k in kvhs]
        out = []
        for k in kvhs:
            m, l, acc = carry[k]
            m_new, p_b, p_tot = [], [], []
            for rows in g_rows:
                sb = s[k][rows] + bias[k]
                m_g = jnp.maximum(m[rows], jnp.max(sb, axis=-1, keepdims=True))
                p = jnp.exp(sb - m_g)
                m_new.append(m_g)
                p_tot.append(jnp.sum(p, axis=-1, keepdims=True))
                p_b.append(p.astype(BF16))
            m_new = jnp.concatenate(m_new, axis=0)
            alpha = jnp.exp(m - m_new)
            l = alpha * l + jnp.concatenate(p_tot, axis=0)
            acc = alpha * acc + _dot(jnp.concatenate(p_b, axis=0), vs_ref[k, pl.ds(k0, SEL_KEY_CHUNK), :])
            out.append((m_new, l, acc))
        return tuple(out)

    init = tuple((jnp.full((GQA * Q_BLOCK, 1), M_FLOOR, F32), jnp.zeros((GQA * Q_BLOCK, 1), F32),
                  jnp.zeros((GQA * Q_BLOCK, HD), F32)) for _ in kvhs)
    sel_out = lax.fori_loop(0, n_key_chunks, chunk, init)
    o_sel = [acc * (1.0 / jnp.maximum(l, 1e-30)) for _, l, acc in sel_out]

    s_win = [_dot_nt(qs[k], kw_ref[k, pl.ds(win_start, win_len), :]) for k in kvhs]
    for k in kvhs:
        e_b, inv_w = [], []
        for g in range(GQA):
            for hf in range(2):
                rows = slice(g * Q_BLOCK + hf * half, g * Q_BLOCK + (hf + 1) * half)
                e_w, inv = _softmax_bias(s_win[k][rows], bias_win[hf * half:(hf + 1) * half])
                e_b.append(e_w.astype(BF16))
                inv_w.append(inv)
        o_win = (_dot(jnp.concatenate(e_b, axis=0), vw_ref[k, pl.ds(win_start, win_len), :])
                 * jnp.concatenate(inv_w, axis=0))
        for g in range(GQA):
            h = GQA * k + g
            rows = g_rows[g]
            o_h = (gates[:, 3 * h:3 * h + 1] * o_cmp[k][rows] + gates[:, 3 * h + 1:3 * h + 2] * o_sel[k][rows]
                   + gates[:, 3 * h + 2:3 * h + 3] * o_win[rows])
            o_ref[:, HD * h:HD * (h + 1)] = o_h.astype(o_ref.dtype)


def _attn_prompt_t_body(qraw_ref, qrot_ref, gates_ref, kc_ref, vct_ref, ks_ref, vst_ref, kw_ref, vwt_ref,
                        mcst_ref, et_ref, o_ref, score_ref):
    t = et_ref.shape[0]
    n_bat = qraw_ref.shape[0]
    n_seg = kc_ref.shape[2]
    i = pl.program_id(0)
    s0 = i * Q_BLOCK
    qpos = s0 + lax.broadcasted_iota(jnp.int32, (1, Q_BLOCK), 1)
    n_idx = lax.broadcasted_iota(jnp.int32, (n_seg, 1), 0)
    ok_cmp = (n_idx * CMP_STRIDE + (CMP_BLOCK - 1) <= qpos) & (n_idx < n_seg - 1)
    bias_cmp = jnp.where(ok_cmp, 0.0, MASK_BIAS)
    mcst = mcst_ref[...]
    n_key_chunks = i // (SEL_KEY_CHUNK // Q_BLOCK) + 1
    n_live_blocks = (s0 + Q_BLOCK) // SEL_BLOCK
    win_start = pl.multiple_of(jnp.maximum(s0 - WINDOW, 0), Q_BLOCK)
    win_len = WINDOW + Q_BLOCK
    wpos = win_start + lax.broadcasted_iota(jnp.int32, (win_len, 1), 0)
    dlt = qpos - wpos
    bias_win = jnp.where((dlt >= 0) & (dlt <= WINDOW), 0.0, MASK_BIAS)
    g_cols = [slice(g * Q_BLOCK, (g + 1) * Q_BLOCK) for g in range(GQA)]
    units = [(bb, k) for bb in range(n_bat) for k in range(KVH)]
    n_units = len(units)
    head_rows = lambda ref, bb, k: jnp.concatenate(
        [ref[bb, :, HD * (GQA * k + g):HD * (GQA * k + g + 1)] for g in range(GQA)], axis=0)
    qr = [head_rows(qraw_ref, bb, k) for bb, k in units]
    qs = [head_rows(qrot_ref, bb, k) for bb, k in units]

    def softmax_cols(s_t, bias):
        e_b, inv = [], []
        for cols in g_cols:
            sb = s_t[:, cols] + bias
            m = jnp.maximum(jnp.max(sb, axis=0, keepdims=True), M_FLOOR)
            e = jnp.exp2(sb - m)
            e_b.append(e)
            inv.append(1.0 / jnp.maximum(jnp.sum(e, axis=0, keepdims=True), 1e-30))
        return e_b, inv

    o_cmp, sel_t = [], []
    cmp_scores = lambda u: _dot_nt(kc_ref[units[u][0], units[u][1]], qr[u])
    s_next = cmp_scores(0)
    for u, (bb, k) in enumerate(units):
        s_cmp = s_next
        if u + 1 < n_units:
            s_next = cmp_scores(u + 1)
        e_heads, inv_heads = softmax_cols(s_cmp, bias_cmp)
        p_heads = [e * inv for e, inv in zip(e_heads, inv_heads)]
        o_cmp.append(_dot(vct_ref[bb, k], jnp.concatenate([p.astype(BF16) for p in p_heads], axis=1)))
        p_sum = p_heads[0]
        for p in p_heads[1:]:
            p_sum = p_sum + p
        hi = p_sum.astype(BF16)
        r1 = p_sum - hi.astype(F32)
        mid = r1.astype(BF16)
        lo = (r1 - mid.astype(F32)).astype(BF16)
        imp_t = _dot(mcst, hi) + _dot(mcst, mid) + _dot(mcst, lo)
        sel_t.append(_select_blocks_t(imp_t, qpos // SEL_BLOCK, t // SEL_BLOCK, n_live_blocks,
                                      score_ref.at[u]).astype(BF16))

    def chunk(c, carry):
        k0 = pl.multiple_of(c * SEL_KEY_CHUNK, SEL_KEY_CHUNK)
        kpos = k0 + lax.broadcasted_iota(jnp.int32, (SEL_KEY_CHUNK, 1), 0)
        causal = kpos <= qpos
        e_blk = et_ref[pl.ds(k0, SEL_KEY_CHUNK), :]

        def scores(u):
            bb, k = units[u]
            rows = pl.ds(pl.multiple_of(bb * t + k0, SEL_KEY_CHUNK), SEL_KEY_CHUNK)
            return (_dot_nt(ks_ref[k, rows, :], qs[u]),
                    (jnp.where(causal, _dot(e_blk, sel_t[u]), 0.0) - 1.0) * (-MASK_BIAS))

        out = []
        nxt = scores(0)
        for u, (bb, k) in enumerate(units):
            s_k, bias_k = nxt
            if u + 1 < n_units:
                nxt = scores(u + 1)
            m, l, acc = carry[u]
            m_new, p_b, p_tot = [], [], []
            for cols in g_cols:
                sb = s_k[:, cols] + bias_k
                m_g = jnp.maximum(m[:, cols], jnp.max(sb, axis=0, keepdims=True))
                p = jnp.exp2(sb - m_g)
                m_new.append(m_g)
                p_tot.append(jnp.sum(p, axis=0, keepdims=True))
                p_b.append(p.astype(BF16))
            m_new = jnp.concatenate(m_new, axis=1)
            alpha = jnp.exp2(m - m_new)
            l = alpha * l + jnp.concatenate(p_tot, axis=1)
            cols_v = pl.ds(pl.multiple_of(bb * t + k0, SEL_KEY_CHUNK), SEL_KEY_CHUNK)
            acc = alpha * acc + _dot(vst_ref[k, :, cols_v], jnp.concatenate(p_b, axis=1))
            out.append((m_new, l, acc))
        return tuple(out)

    init = tuple((jnp.full((1, GQA * Q_BLOCK), M_FLOOR, F32), jnp.zeros((1, GQA * Q_BLOCK), F32),
                  jnp.zeros((HD, GQA * Q_BLOCK), F32)) for _ in units)
    sel_out = lax.fori_loop(0, n_key_chunks, chunk, init)
    o_sel = [acc * (1.0 / jnp.maximum(l, 1e-30)) for _, l, acc in sel_out]

    win_rows = lambda bb: pl.ds(pl.multiple_of(bb * t + win_start, Q_BLOCK), win_len)
    win_scores = lambda u: _dot_nt(kw_ref[units[u][1], win_rows(units[u][0]), :], qs[u])
    gates_all = [gates_ref[bb].T for bb in range(n_bat)]
    o_t = []
    s_next = win_scores(0)
    for u, (bb, k) in enumerate(units):
        gates_t = gates_all[bb]
        s_win = s_next
        if u + 1 < n_units:
            s_next = win_scores(u + 1)
        e_heads, inv_heads = softmax_cols(s_win, bias_win)
        o_win = (_dot(vwt_ref[k, :, win_rows(bb)],
                      jnp.concatenate([e.astype(BF16) for e in e_heads], axis=1))
                 * jnp.concatenate(inv_heads, axis=1))
        for g in range(GQA):
            h = GQA * k + g
            cols = g_cols[g]
            o_t.append(gates_t[3 * h:3 * h + 1] * o_cmp[u][:, cols] + gates_t[3 * h + 1:3 * h + 2] * o_sel[u][:, cols]
                       + gates_t[3 * h + 2:3 * h + 3] * o_win[:, cols])
    for bb in range(n_bat):
        for j in range(NH // 2):
            pair = jnp.concatenate([o_t[NH * bb + 2 * j], o_t[NH * bb + 2 * j + 1]], axis=0)
            o_ref[bb, :, 2 * HD * j:2 * HD * (j + 1)] = pair.T.astype(o_ref.dtype)


def _attn_prompt(q_raw, q_rot, gates, k_cmp, v_cmp, ks, vs, kw, vw, mcs_t, e_mat, n_batch, t):
    nqb = t // Q_BLOCK
    qblock = lambda width: pl.BlockSpec((n_batch, Q_BLOCK, width), lambda i: (0, i, 0))
    full = lambda a: pl.BlockSpec(a.shape, lambda i: (0,) * a.ndim)
    seq3 = lambda a: a.reshape(n_batch, t, a.shape[1])
    args = [seq3(q_raw), seq3(q_rot), seq3(gates), k_cmp, v_cmp, ks, vs, kw, vw, mcs_t, e_mat]
    o = pl.pallas_call(
        _attn_prompt_t_body,
        grid=(nqb,),
        in_specs=[qblock(DH), qblock(DH), qblock(128)] + [full(a) for a in args[3:]],
        out_specs=qblock(DH),
        out_shape=jax.ShapeDtypeStruct((n_batch, t, DH), F32),
        scratch_shapes=[pltpu.VMEM((n_batch * KVH, N_SEL_LANES, Q_BLOCK), F32)],
        compiler_params=_cparams(1),
        name="nsa_attn_prompt",
    )(*args)
    return o.reshape(n_batch * t, DH)


def _pick_kv_group(full):
    hk = lax.broadcasted_iota(jnp.int32, (NH, 1), 0) // GQA
    out = jnp.zeros((NH, HD), F32)
    for k in range(KVH):
        out = out + jnp.where(hk == k, full[:, HD * k:HD * (k + 1)], 0.0)
    return out


def _nsa_sample_body(n_pages, pt_ref, *refs):
    pages = refs[:n_pages]
    (win_ref, qraw_ref, qrot_ref, gates_ref, kvnew_ref, kwnew_ref,
     w1k_ref, w1v_ref, pek_ref, pev_ref, w1kf_ref, w1vf_ref, w2k_ref, w2v_ref, mcs_ref, e_ref,
     o_ref, winout_ref, stage_ref) = refs[n_pages:]
    page = pages[0].shape[1]
    past = n_pages * page
    n_seg = past // CMP_STRIDE

    def bdiag(q):
        qt = jnp.concatenate([q] * KVH, axis=1)
        hk = lax.broadcasted_iota(jnp.int32, (NH, DKV), 0) // GQA
        lk = lax.broadcasted_iota(jnp.int32, (NH, DKV), 1) // HD
        return jnp.where(hk == lk, qt, 0.0)

    qr = bdiag(qraw_ref[0])
    qs = bdiag(qrot_ref[0])
    qr_b, qs_b = qr.astype(BF16), qs.astype(BF16)

    for kk, pg in enumerate(pages):
        for grp in range(4):
            stage_ref[grp, page * kk:page * (kk + 1), :] = pg[0, :, 128 * grp:128 * (grp + 1)]
    accs = _compress_accumulate(stage_ref, (w1k_ref, w1v_ref), n_seg)
    cmp_kv = []
    for kv, (pe_ref, w1f_ref, w2_ref) in enumerate(((pek_ref, w1kf_ref, w2k_ref), (pev_ref, w1vf_ref, w2v_ref))):
        pe0 = _dot(pe_ref[...], w1f_ref[...])
        w2 = w2_ref[...]
        cmp_kv.append(jnp.concatenate(
            [_compress_finish(accs[kv][k], pe0, w2, n_seg) for k in range(KVH)], axis=1).astype(BF16))
    k_cmp, v_cmp = cmp_kv

    n_idx = lax.broadcasted_iota(jnp.int32, (1, n_seg), 1)
    ok_cmp = (n_idx * CMP_STRIDE + (CMP_BLOCK - 1) <= past) & (n_idx < n_seg - 1)
    p_cmp = _softmax_rows(_dot_nt(qr_b, k_cmp), ok_cmp)
    o_cmp = _pick_kv_group(_dot(p_cmp.astype(BF16), v_cmp))
    p_sum = jnp.concatenate(
        [jnp.sum(p_cmp[GQA * k:GQA * (k + 1)], axis=0, keepdims=True) for k in range(KVH)], axis=0)
    imp = _dot_split3(p_sum, mcs_ref[...])
    n_blocks = past // SEL_BLOCK + 1
    cur = jnp.full((KVH, 1), past // SEL_BLOCK, jnp.int32)
    sel = _select_blocks(imp, cur, n_blocks)
    sel16 = jnp.concatenate([jnp.broadcast_to(sel[k:k + 1], (GQA, N_SEL_LANES)) for k in range(KVH)], axis=0)

    kv_new = kvnew_ref[0]
    ks_new, vs_new = kv_new[:, 2 * DKV:3 * DKV], kv_new[:, 3 * DKV:4 * DKV]
    k_sel = jnp.concatenate([pg[0, :, 2 * DKV:3 * DKV] for pg in pages], axis=0).astype(BF16)
    v_sel = jnp.concatenate([pg[0, :, 3 * DKV:4 * DKV] for pg in pages], axis=0).astype(BF16)
    s_c = _dot_nt(qs_b, k_sel)
    s_n = jnp.sum(qs * ks_new, axis=-1, keepdims=True)
    ok_c = _dot(sel16.astype(BF16), e_ref[...]) > 0.5
    new_blk = past // SEL_BLOCK
    ok_n = sel16[:, new_blk:new_blk + 1] > 0.5
    sm_c = jnp.where(ok_c, s_c, NEG)
    sm_n = jnp.where(ok_n, s_n, NEG)
    m = jnp.maximum(jnp.max(sm_c, axis=-1, keepdims=True), sm_n)
    e_c = jnp.where(ok_c, jnp.exp2(sm_c - m), 0.0)
    e_n = jnp.where(ok_n, jnp.exp2(sm_n - m), 0.0)
    inv = 1.0 / jnp.maximum(jnp.sum(e_c, axis=-1, keepdims=True) + e_n, 1e-30)
    o_sel = _pick_kv_group(_dot((e_c * inv).astype(BF16), v_sel)
                           + (e_n * inv).astype(BF16).astype(F32) * vs_new.astype(BF16).astype(F32))

    kw_new = kwnew_ref[0]
    kwn, vwn = kw_new[:, 0:DKV], kw_new[:, DKV:2 * DKV]
    s_w = _dot_nt(qs_b, win_ref[0, :, 0:DKV].astype(BF16))
    s_wn = jnp.sum(qs * kwn, axis=-1, keepdims=True)
    mw = jnp.maximum(jnp.max(s_w, axis=-1, keepdims=True), s_wn)
    e_w = jnp.exp2(s_w - mw)
    e_wn = jnp.exp2(s_wn - mw)
    inv_w = 1.0 / (jnp.sum(e_w, axis=-1, keepdims=True) + e_wn)
    o_win = _pick_kv_group(_dot((e_w * inv_w).astype(BF16), win_ref[0, :, DKV:2 * DKV].astype(BF16))
                           + (e_wn * inv_w).astype(BF16).astype(F32) * vwn.astype(BF16).astype(F32))

    g = gates_ref[0]
    o_ref[0] = g[:, 0:1] * o_cmp + g[:, 1:2] * o_sel + g[:, 2:3] * o_win
    wb = win_ref.shape[1]
    winout_ref[0, 0:wb - 1, :] = win_ref[0, 1:wb, :]
    winout_ref[0, wb - 1:wb, :] = kw_new


def _nsa_sample(cache3, page_table, state_win, q_raw, q_rot, gates, kv_new, kw_new, cmp_k, cmp_v, mcs, e_mat):
    nb, n_pages = page_table.shape
    page = cache3.shape[1]
    wb = state_win.shape[1]
    w1k, pek, w1kf, w2k = cmp_k
    w1v, pev, w1vf, w2v = cmp_v
    consts = [w1k, w1v, pek, pev, w1kf, w1vf, w2k, w2v, mcs, e_mat]
    full = lambda a: pl.BlockSpec(a.shape, lambda b, pt: (0,) * a.ndim)
    page_specs = [pl.BlockSpec((1, page, cache3.shape[2]), functools.partial(
        lambda b, pt, kk: (pt[b * n_pages + kk], 0, 0), kk=kk)) for kk in range(n_pages)]
    per_b = lambda shape: pl.BlockSpec((1,) + shape, lambda b, pt: (b, 0, 0))
    grid_spec = pltpu.PrefetchScalarGridSpec(
        num_scalar_prefetch=1,
        grid=(nb,),
        in_specs=page_specs + [per_b((wb, 2 * DKV)), per_b((NH, HD)), per_b((NH, HD)), per_b((NH, 3)),
                               per_b((1, 4 * DKV)), per_b((1, 2 * DKV))] + [full(a) for a in consts],
        out_specs=[per_b((NH, HD)), per_b((wb, 2 * DKV))],
        scratch_shapes=[pltpu.VMEM((4, n_pages * page, 128), F32)],
    )
    return pl.pallas_call(
        functools.partial(_nsa_sample_body, n_pages),
        grid_spec=grid_spec,
        out_shape=[jax.ShapeDtypeStruct((nb, NH, HD), F32), jax.ShapeDtypeStruct((nb, wb, 2 * DKV), F32)],
        compiler_params=_cparams(1),
        name="nsa_sample",
    )(page_table.reshape(-1), *([cache3] * n_pages), state_win, q_raw, q_rot, gates, kv_new, kw_new, *consts)


def _kv_block_diag(q):
    qt = jnp.concatenate([q] * KVH, axis=1)
    hk = lax.broadcasted_iota(jnp.int32, (NH, DKV), 0) // GQA
    lk = lax.broadcasted_iota(jnp.int32, (NH, DKV), 1) // HD
    return jnp.where(hk == lk, qt, 0.0)


def _nsa_sample_t_body(n_pages, n_seq, pt_ref, *refs):
    all_pages = refs[:n_seq * n_pages]
    (win_ref, qraw_ref, qrot_ref, gates_ref, kvnew_ref, kwnew_ref, kwvwt_ref, perm_ref,
     w1k_ref, w1v_ref, pek_ref, pev_ref, w1kf_ref, w1vf_ref, w2k_ref, w2v_ref, mcs_ref, e_ref,
     o_ref, winout_ref, stage_ref) = refs[n_seq * n_pages:]
    page = all_pages[0].shape[3]
    n_seg = n_pages * page // CMP_STRIDE
    seg_pp = page // CMP_STRIDE

    perm = perm_ref[...]
    for kk, pg in enumerate(all_pages):
        for c in range(2):
            xs = _dot_nt(perm, pg[0, c].astype(BF16))
            for s in range(CMP_STRIDE):
                stage_ref[c, s, seg_pp * kk:seg_pp * (kk + 1), :] = xs[seg_pp * s:seg_pp * (s + 1), :]
    accs = [[jnp.zeros((n_seq * n_seg, 2 * CMP_HIDDEN), F32) for _ in range(KVH)] for _ in range(2)]
    for s in range(CMP_STRIDE):
        for kv, w1c_ref in enumerate((w1k_ref, w1v_ref)):
            w = w1c_ref[s]
            x = stage_ref[kv, s].astype(BF16)
            for k in range(KVH):
                accs[kv][k] = accs[kv][k] + _dot(x[:, HD * k:HD * (k + 1)], w)
    for q in range(n_seq):
        one = lambda ref: ref.at[pl.ds(q, 1)]
        _nsa_sample_one(
            all_pages[q * n_pages:(q + 1) * n_pages],
            [[a[n_seg * q:n_seg * (q + 1)] for a in row] for row in accs],
            pl.program_id(0) * n_seq + q,
            one(win_ref), one(qraw_ref), one(qrot_ref), one(gates_ref), one(kvnew_ref), one(kwnew_ref), kwvwt_ref,
            pek_ref, pev_ref, w1kf_ref, w1vf_ref, w2k_ref, w2v_ref, mcs_ref, e_ref, one(o_ref), one(winout_ref))


def _nsa_sample_one(pages, accs, b_idx, win_ref, qraw_ref, qrot_ref, gates_ref, kvnew_ref, kwnew_ref, kwvwt_ref,
                    pek_ref, pev_ref, w1kf_ref, w1vf_ref, w2k_ref, w2v_ref, mcs_ref, e_ref, o_ref, winout_ref):
    page = pages[0].shape[3]
    past = len(pages) * page
    n_seg = past // CMP_STRIDE
    qr = _kv_block_diag(qraw_ref[0])
    qs = _kv_block_diag(qrot_ref[0])
    qr_b, qs_b = qr.astype(BF16), qs.astype(BF16)
    cmp_kv = []
    for kv, (pe_ref, w1f_ref, w2_ref) in enumerate(((pek_ref, w1kf_ref, w2k_ref), (pev_ref, w1vf_ref, w2v_ref))):
        pe0 = _dot(pe_ref[...], w1f_ref[...])
        w2 = w2_ref[...]
        cmp_kv.append(jnp.concatenate(
            [_compress_finish(accs[kv][k], pe0, w2, n_seg) for k in range(KVH)], axis=1).astype(BF16))
    k_cmp, v_cmp = cmp_kv

    n_idx = lax.broadcasted_iota(jnp.int32, (1, n_seg), 1)
    ok_cmp = (n_idx * CMP_STRIDE + (CMP_BLOCK - 1) <= past) & (n_idx < n_seg - 1)
    p_cmp = _softmax_rows(_dot_nt(qr_b, k_cmp), ok_cmp)
    o_cmp = _pick_kv_group(_dot(p_cmp.astype(BF16), v_cmp))
    p_sum = jnp.concatenate(
        [jnp.sum(p_cmp[GQA * k:GQA * (k + 1)], axis=0, keepdims=True) for k in range(KVH)], axis=0)
    imp = _dot_split3(p_sum, mcs_ref[...])
    n_blocks = past // SEL_BLOCK + 1
    cur = jnp.full((KVH, 1), past // SEL_BLOCK, jnp.int32)
    sel = _select_blocks(imp, cur, n_blocks)
    sel16 = jnp.concatenate([jnp.broadcast_to(sel[k:k + 1], (GQA, N_SEL_LANES)) for k in range(KVH)], axis=0)

    kv_new = kvnew_ref[0]
    ks_new, vs_new = kv_new[:, 2 * DKV:3 * DKV], kv_new[:, 3 * DKV:4 * DKV]
    s_c = jnp.concatenate([_dot(qs_b, pg[0, 2].astype(BF16)) for pg in pages], axis=1)
    s_n = jnp.sum(qs * ks_new, axis=-1, keepdims=True)
    ok_c = _dot(sel16.astype(BF16), e_ref[...]) > 0.5
    new_blk = past // SEL_BLOCK
    ok_n = sel16[:, new_blk:new_blk + 1] > 0.5
    sm_c = jnp.where(ok_c, s_c, NEG)
    sm_n = jnp.where(ok_n, s_n, NEG)
    m = jnp.maximum(jnp.max(sm_c, axis=-1, keepdims=True), sm_n)
    e_c = jnp.where(ok_c, jnp.exp2(sm_c - m), 0.0)
    e_n = jnp.where(ok_n, jnp.exp2(sm_n - m), 0.0)
    inv = 1.0 / jnp.maximum(jnp.sum(e_c, axis=-1, keepdims=True) + e_n, 1e-30)
    p_c = (e_c * inv).astype(BF16)
    o_sel_full = (e_n * inv).astype(BF16).astype(F32) * vs_new.astype(BF16).astype(F32)
    for kk, pg in enumerate(pages):
        o_sel_full = o_sel_full + _dot_nt(p_c[:, page * kk:page * (kk + 1)], pg[0, 3].astype(BF16))
    o_sel = _pick_kv_group(o_sel_full)

    kw_new = kwnew_ref[0]
    kwn, vwn = kw_new[:, 0:DKV], kw_new[:, DKV:2 * DKV]
    s_w = _dot(qs_b, win_ref[0, 0].astype(BF16))
    s_wn = jnp.sum(qs * kwn, axis=-1, keepdims=True)
    mw = jnp.maximum(jnp.max(s_w, axis=-1, keepdims=True), s_wn)
    e_w = jnp.exp2(s_w - mw)
    e_wn = jnp.exp2(s_wn - mw)
    inv_w = 1.0 / (jnp.sum(e_w, axis=-1, keepdims=True) + e_wn)
    o_win = _pick_kv_group(_dot_nt((e_w * inv_w).astype(BF16), win_ref[0, 1].astype(BF16))
                           + (e_wn * inv_w).astype(BF16).astype(F32) * vwn.astype(BF16).astype(F32))

    g = gates_ref[0]
    o_ref[0] = g[:, 0:1] * o_cmp + g[:, 1:2] * o_sel + g[:, 2:3] * o_win

    wb = win_ref.shape[3]
    nb = kwvwt_ref.shape[1]
    mine = lax.broadcasted_iota(jnp.int32, (1, nb), 1) == b_idx
    last = lax.broadcasted_iota(jnp.int32, (1, wb), 1) == wb - 1
    for c in range(2):
        new_col = jnp.sum(jnp.where(mine, kwvwt_ref[DKV * c:DKV * (c + 1), :], 0.0), axis=-1, keepdims=True)
        winout_ref[0, c] = jnp.where(last, new_col, pltpu.roll(win_ref[0, c], wb - 1, 1))


def _nsa_sample_t(cache4, page_table, win4, q_raw, q_rot, gates, kv_new, kw_new, kwvw_t, cmp_k, cmp_v, mcs, e_mat,
                  n_seq):
    nb, n_pages = page_table.shape
    page = cache4.shape[3]
    wb = win4.shape[3]
    seg_pp = page // CMP_STRIDE
    rows = np.arange(page)
    perm = np.zeros((page, page), np.float32)
    perm[rows, (rows % seg_pp) * CMP_STRIDE + rows // seg_pp] = 1.0
    w1k, pek, w1kf, w2k = cmp_k
    w1v, pev, w1vf, w2v = cmp_v
    consts = [kwvw_t, jnp.asarray(perm, BF16), w1k, w1v, pek, pev, w1kf, w1vf, w2k, w2v, mcs, e_mat]
    full = lambda a: pl.BlockSpec(a.shape, lambda b, pt: (0,) * a.ndim)
    page_specs = [pl.BlockSpec((1, 4, DKV, page), functools.partial(
        lambda b, pt, kk: (pt[b * n_seq * n_pages + kk], 0, 0, 0), kk=kk)) for kk in range(n_seq * n_pages)]
    per_b = lambda shape: pl.BlockSpec((n_seq,) + shape, lambda b, pt: (b,) + (0,) * len(shape))
    grid_spec = pltpu.PrefetchScalarGridSpec(
        num_scalar_prefetch=1,
        grid=(nb // n_seq,),
        in_specs=page_specs + [per_b((2, DKV, wb)), per_b((NH, HD)), per_b((NH, HD)), per_b((NH, 3)),
                               per_b((1, 4 * DKV)), per_b((1, 2 * DKV))] + [full(a) for a in consts],
        out_specs=[per_b((NH, HD)), per_b((2, DKV, wb))],
        scratch_shapes=[pltpu.VMEM((2, CMP_STRIDE, n_seq * n_pages * seg_pp, DKV), F32)],
    )
    return pl.pallas_call(
        functools.partial(_nsa_sample_t_body, n_pages, n_seq),
        grid_spec=grid_spec,
        out_shape=[jax.ShapeDtypeStruct((nb, NH, HD), F32), jax.ShapeDtypeStruct((nb, 2, DKV, wb), F32)],
        compiler_params=_cparams(1),
        name="nsa_sample",
    )(page_table.reshape(-1), *([cache4] * (n_seq * n_pages)), win4, q_raw, q_rot, gates, kv_new, kw_new, *consts)


def _rwkv_prep(cols, prev, mu, w0, w2, a0, a2, g2, kk_p, ka):
    xs = cols + (prev - cols) * mu
    r = xs[:, 0:DH]
    k = xs[:, DH:2 * DH]
    v = xs[:, 2 * DH:3 * DH]
    o = 3 * DH
    wd = xs[:, o:o + LORA_W]
    ad = xs[:, o + LORA_W:o + LORA_W + LORA_A]
    gd = xs[:, o + LORA_W + LORA_A:o + LORA_W + LORA_A + LORA_G_PAD]
    w = w0 + _dot(jnp.tanh(wd).astype(BF16), w2)
    w_log = -jax.nn.softplus(-w) - 0.5
    lw = -jnp.exp(w_log)
    a = _sigmoid(a0 + _dot(ad.astype(BF16), a2))
    g = _dot(_sigmoid(gd).astype(BF16), g2)
    kkv = k * kk_p
    k_mod = k * (1.0 + (a - 1.0) * ka)
    return r, k_mod, v, kkv, a, lw, g


def _head_norm(kkv_h):
    return kkv_h / jnp.maximum(jnp.sqrt(jnp.sum(kkv_h * kkv_h, axis=-1, keepdims=True)), 1e-12)


def _rwkv_head_out(y, r_h, k_h, v_h, g_h, rk_h, gnw_h, gnb_h):
    mean = jnp.mean(y, axis=-1, keepdims=True)
    var = jnp.mean(jnp.square(y - mean), axis=-1, keepdims=True)
    yn = (y - mean) * lax.rsqrt(var + GN_EPS) * gnw_h + gnb_h
    bonus = jnp.sum(r_h * k_h * rk_h, axis=-1, keepdims=True) * v_h
    return (yn + bonus) * g_h


def _head_ones():
    i = lax.broadcasted_iota(jnp.int32, (4 * HD, 4 * HD), 0) // HD
    j = lax.broadcasted_iota(jnp.int32, (4 * HD, 4 * HD), 1) // HD
    return jnp.where(i == j, 1.0, 0.0).astype(BF16)


def _head_sums(x, head_ones):
    rows, width = x.shape
    groups = width // (4 * HD)
    stacked = jnp.concatenate([x[:, 4 * HD * j:4 * HD * (j + 1)] for j in range(groups)], axis=0)
    sums = _dot_split3(stacked, head_ones)
    return jnp.concatenate([sums[rows * j:rows * (j + 1)] for j in range(groups)], axis=1)


def _block_diag(x):
    lane_head = lax.broadcasted_iota(jnp.int32, (1, x.shape[1]), 1) // HD
    return jnp.concatenate([jnp.where(lane_head == h, x, jnp.zeros_like(x)) for h in range(RW_GROUP)], axis=0)


def _tri_inverse_all(a_list, eye, blk16, blk32):
    b = lambda x: x.astype(BF16)
    bd = lambda x: _block_diag(b(x))
    a16 = [jnp.where(blk16, a, 0.0) for a in a_list]
    p = [_dot(b(x), bd(x)) for x in a16]
    t = [_dot(b(eye - x), bd(eye + q)) for x, q in zip(a16, p)]
    for _ in range(2):
        p = [_dot(b(q), bd(q)) for q in p]
        t = [_dot(b(x), bd(eye + q)) for x, q in zip(t, p)]
    off32 = blk32 & jnp.logical_not(blk16)
    for mask in (off32, jnp.logical_not(blk32)):
        m = [_dot(b(x), bd(jnp.where(mask, a, 0.0))) for x, a in zip(t, a_list)]
        t = [x - _dot(b(y), bd(x)) for x, y in zip(t, m)]
    return t


def _rwkv_prompt_body(cols_ref, mu_ref, w0_ref, w2_ref, a0_ref, a2_ref, g2_ref, kk_ref, ka_ref, rk_ref,
                      gnw_ref, gnb_ref, o_ref, wkv_ref, s_ref, last_ref):
    c = pl.program_id(0)
    ch = RW_CHUNK
    n_seq = cols_ref.shape[0]
    seq_rows = [slice(ch * q, ch * (q + 1)) for q in range(n_seq)]

    @pl.when(c == 0)
    def _():
        s_ref[...] = jnp.zeros_like(s_ref)
        last_ref[...] = jnp.zeros_like(last_ref)

    cols = jnp.concatenate([cols_ref[q] for q in range(n_seq)], axis=0)
    row = lax.broadcasted_iota(jnp.int32, (ch, 1), 0)
    prev = jnp.concatenate([jnp.where(row == 0, last_ref[q], pltpu.roll(cols[seq_rows[q]], 1, 0))
                            for q in range(n_seq)], axis=0)
    for q in range(n_seq):
        last_ref[q] = cols[ch * (q + 1) - 1:ch * (q + 1), :]
    r, k_mod, v, kkv, a, lw, g = _rwkv_prep(cols, prev, mu_ref[...], w0_ref[...], w2_ref[...], a0_ref[...],
                                            a2_ref[...], g2_ref[...], kk_ref[...], ka_ref[...])
    ti = lax.broadcasted_iota(jnp.int32, (n_seq * ch, n_seq * ch), 0)
    si = lax.broadcasted_iota(jnp.int32, (n_seq * ch, n_seq * ch), 1)
    ltri = jnp.where((ti >= si) & (ti // ch == si // ch), 1.0, 0.0).astype(BF16)
    cl = _cumsum_rows(lw, ltri)
    e_in = jnp.exp(cl)
    e_ex = jnp.exp(cl - lw)
    e_ng = jnp.exp(-cl)
    cl_last = [cl[ch * (q + 1) - 1:ch * (q + 1), :] for q in range(n_seq)]
    e_end = jnp.exp(jnp.concatenate([jnp.broadcast_to(x, (ch, DH)) for x in cl_last], axis=0) - cl)
    g_end = [jnp.exp(x) for x in cl_last]
    rk, gnw, gnb = rk_ref[...], gnw_ref[...], gnb_ref[...]
    b = lambda x: x.astype(BF16)
    gw = RW_GROUP * HD
    n_grp = NH // RW_GROUP
    slabs = [(seq_rows[q], slice(gw * j, gw * (j + 1)), q) for q in range(n_seq) for j in range(n_grp)]

    t_row = lax.broadcasted_iota(jnp.int32, (ch, gw), 0)
    s_col = lax.broadcasted_iota(jnp.int32, (ch, gw), 1) % HD
    strict = t_row > s_col
    causal = t_row >= s_col
    eye = jnp.where(t_row == s_col, 1.0, 0.0)
    blk16 = (t_row // 16) == (s_col // 16)
    blk32 = (t_row // 32) == (s_col // 32)
    head_ones = _head_ones()
    kk_n = kkv * jnp.minimum(lax.rsqrt(_head_sums(kkv * kkv, head_ones)), 1e12)
    beta = kk_n * a
    kq, rq = b(kk_n * e_ex), b(r * e_in)
    bd, kd = b(beta * e_ng), b(k_mod * e_ng)
    bdec, kdec = b(beta * e_end), b(k_mod * e_end)
    v_b = b(v)
    ns = range(len(slabs))
    lhs = [jnp.concatenate([kq[rs, gs], rq[rs, gs]], axis=0) for rs, gs, _ in slabs]
    s0 = [s_ref[i] for i in ns]
    quad = [_dot_nt(lhs[i], jnp.concatenate([_block_diag(bd[rs, gs]), _block_diag(kd[rs, gs])], axis=0))
            for i, (rs, gs, _) in enumerate(slabs)]
    s0t = [_dot_nt(lhs[i], _block_diag(b(s0[i]))) for i in ns]
    a_b = [jnp.where(strict, q[0:ch, 0:gw], 0.0) for q in quad]
    a_kv = [_dot(b(jnp.where(strict, quad[i][0:ch, gw:2 * gw], 0.0)), _block_diag(v_b[rs, gs]))
            for i, (rs, gs, _) in enumerate(slabs)]
    t_inv = _tri_inverse_all(a_b, eye, blk16, blk32)
    u = [-_dot(b(t_inv[i]), _block_diag(b(s0t[i][0:ch] + a_kv[i]))) for i in ns]
    y = [s0t[i][ch:2 * ch]
         + _dot(b(jnp.where(causal, quad[i][ch:2 * ch, 0:gw], 0.0)), _block_diag(b(u[i])))
         + _dot(b(jnp.where(causal, quad[i][ch:2 * ch, gw:2 * gw], 0.0)), _block_diag(v_b[rs, gs]))
         for i, (rs, gs, _) in enumerate(slabs)]
    lane_head = lax.broadcasted_iota(jnp.int32, (1, gw), 1) // HD
    for i, (rs, gs, q) in enumerate(slabs):
        cross = _dot_tn(jnp.concatenate([b(u[i]), v_b[rs, gs]], axis=0),
                        jnp.concatenate([bdec[rs, gs], kdec[rs, gs]], axis=0))
        s_add = jnp.zeros((HD, gw), F32)
        for h in range(RW_GROUP):
            s_add = s_add + jnp.where(lane_head == h, cross[HD * h:HD * (h + 1)], 0.0)
        s_ref[i] = s0[i] * g_end[q][:, gs] + s_add
    y_all = jnp.concatenate([jnp.concatenate(y[n_grp * q:n_grp * (q + 1)], axis=1) for q in range(n_seq)],
                            axis=0)
    yc = y_all - _head_sums(y_all, head_ones) * (1.0 / HD)
    var = _head_sums(yc * yc, head_ones) * (1.0 / HD)
    yn = yc * lax.rsqrt(var + GN_EPS) * gnw + gnb
    bonus = _head_sums(r * k_mod * rk, head_ones) * v
    out = ((yn + bonus) * g).astype(o_ref.dtype)
    for q in range(n_seq):
        o_ref[q] = out[seq_rows[q]]

    @pl.when(c == pl.num_programs(0) - 1)
    def _():
        for q in range(n_seq):
            for h in range(NH):
                lanes = slice(HD * (h % RW_GROUP), HD * (h % RW_GROUP + 1))
                wkv_ref[q, h] = s_ref[n_grp * q + h // RW_GROUP][:, lanes]


def _cumsum_rows(x, ltri):
    hi = x.astype(BF16)
    r1 = x - hi.astype(F32)
    mid = r1.astype(BF16)
    lo = (r1 - mid.astype(F32)).astype(BF16)
    return _dot(ltri, hi) + _dot(ltri, mid) + _dot(ltri, lo)


def _rwkv_prompt(cols, rw, n_batch, t):
    nc = t // RW_CHUNK
    full = lambda a: pl.BlockSpec(a.shape, lambda c: (0,) * a.ndim)
    chunk = lambda c: (0, c, 0)
    o, wkv = pl.pallas_call(
        _rwkv_prompt_body,
        grid=(nc,),
        in_specs=[pl.BlockSpec((n_batch, RW_CHUNK, RWKV_PAD), chunk)] + [full(a) for a in rw],
        out_specs=[pl.BlockSpec((n_batch, RW_CHUNK, DH), chunk),
                   pl.BlockSpec((n_batch, NH, HD, HD), lambda c: (0, 0, 0, 0))],
        out_shape=[jax.ShapeDtypeStruct((n_batch, t, DH), F32),
                   jax.ShapeDtypeStruct((n_batch, NH, HD, HD), F32)],
        scratch_shapes=[pltpu.VMEM((n_batch * NH // RW_GROUP, HD, RW_GROUP * HD), F32),
                        pltpu.VMEM((n_batch, 1, RWKV_PAD), F32)],
        compiler_params=_cparams(1),
        name="rwkv_prompt",
    )(cols.reshape(n_batch, t, RWKV_PAD), *rw)
    return o.reshape(n_batch * t, DH), wkv


def _rwkv_sample_prep_body(cols_ref, prev_ref, mu_ref, w0_ref, w2_ref, a0_ref, a2_ref, g2_ref, kk_ref, ka_ref,
                           r_ref, k_ref, v_ref, kkn_ref, a_ref, d_ref, g_ref):
    r, k_mod, v, kkv, a, lw, g = _rwkv_prep(cols_ref[...], prev_ref[...], mu_ref[...], w0_ref[...], w2_ref[...],
                                            a0_ref[...], a2_ref[...], g2_ref[...], kk_ref[...], ka_ref[...])
    r_ref[...] = r
    k_ref[...] = k_mod
    v_ref[...] = v
    a_ref[...] = a
    d_ref[...] = jnp.exp(lw)
    g_ref[...] = g
    for h in range(NH):
        sl = slice(HD * h, HD * (h + 1))
        kkn_ref[:, sl] = _head_norm(kkv[:, sl])


def _rwkv_sample_prep(cols, prev, rw_prep):
    n = cols.shape[0]
    args = [cols, prev] + list(rw_prep)
    full = lambda a: pl.BlockSpec(a.shape, lambda i: (0,) * a.ndim)
    out = jax.ShapeDtypeStruct((n, DH), F32)
    return pl.pallas_call(
        _rwkv_sample_prep_body,
        grid=(1,),
        in_specs=[full(a) for a in args],
        out_specs=[pl.BlockSpec((n, DH), lambda i: (0, 0))] * 7,
        out_shape=[out] * 7,
        compiler_params=_cparams(1),
        name="rwkv_sample_prep",
    )(*args)


def _rwkv_sample_step_body(s_ref, r_ref, k_ref, v_ref, kk_ref, a_ref, d_ref, g_ref, rk_ref, gnw_ref, gnb_ref,
                           o_ref, sout_ref):
    s = s_ref[0]
    r, k, v, kk, a, d, g = (x[0] for x in (r_ref, k_ref, v_ref, kk_ref, a_ref, d_ref, g_ref))
    eye = jnp.where(lax.broadcasted_iota(jnp.int32, (HD, HD), 0) == lax.broadcasted_iota(jnp.int32, (HD, HD), 1),
                    1.0, 0.0)
    row = lambda x: x[:, None, :]
    col = lambda x: jnp.sum(eye[None] * x[:, None, :], axis=-1, keepdims=True)
    sa = -jnp.sum(s * row(kk), axis=-1, keepdims=True)
    s_new = s * row(d) + sa * row(kk * a) + col(v) * row(k)
    sout_ref[0] = s_new
    y_col = jnp.sum(s_new * row(r), axis=-1, keepdims=True)
    y = jnp.sum(y_col * eye[None], axis=1)
    o_ref[0] = _rwkv_head_out(y, r, k, v, g, rk_ref[...], gnw_ref[...], gnb_ref[...])


def _rwkv_sample_step(state, vecs, rk, gnw, gnb):
    nb = state.shape[0]
    per_b3 = pl.BlockSpec((1, NH, HD), lambda b: (b, 0, 0))
    per_b4 = pl.BlockSpec((1, NH, HD, HD), lambda b: (b, 0, 0, 0))
    full = pl.BlockSpec((NH, HD), lambda b: (0, 0))
    return pl.pallas_call(
        _rwkv_sample_step_body,
        grid=(nb,),
        in_specs=[per_b4] + [per_b3] * 7 + [full] * 3,
        out_specs=[per_b3, per_b4],
        out_shape=[jax.ShapeDtypeStruct((nb, NH, HD), F32), jax.ShapeDtypeStruct((nb, NH, HD, HD), F32)],
        compiler_params=_cparams(1),
        name="rwkv_sample_step",
    )(state, *vecs, rk, gnw, gnb)


def _rwkv_sample_prep_t_body(cols_ref, kv_ref, kwvw_ref, prev_ref, mu_ref, w0_ref, w2t_ref, a0_ref, a2t_ref,
                             g2t_ref, kk_ref, ka_ref,
                             colst_ref, kvt_ref, kwvwt_ref, r_ref, k_ref, v_ref, kkn_ref, kka_ref, d_ref, g_ref,
                             tr_ref):
    nb = cols_ref.shape[0]
    for j in range(cols_ref.shape[1] // 128):
        tr_ref[128 * j:128 * (j + 1), :] = cols_ref[:, 128 * j:128 * (j + 1)].T
    for j in range(kv_ref.shape[1] // 128):
        kvt_ref[128 * j:128 * (j + 1), :] = kv_ref[:, 128 * j:128 * (j + 1)].T
    for j in range(kwvw_ref.shape[1] // 128):
        kwvwt_ref[128 * j:128 * (j + 1), :] = kwvw_ref[:, 128 * j:128 * (j + 1)].T
    cols = tr_ref[0:RWKV_COLS, :]
    colst_ref[...] = cols
    xs = cols + (prev_ref[...] - cols) * mu_ref[...]
    r = xs[0:DH]
    k = xs[DH:2 * DH]
    v = xs[2 * DH:3 * DH]
    o = 3 * DH
    wd = xs[o:o + LORA_W]
    ad = xs[o + LORA_W:o + LORA_W + LORA_A]
    gd = xs[o + LORA_W + LORA_A:RWKV_COLS]
    w = w0_ref[...] + _dot(w2t_ref[...], jnp.tanh(wd).astype(BF16))
    w_log = -jax.nn.softplus(-w) - 0.5
    a = _sigmoid(a0_ref[...] + _dot(a2t_ref[...], ad.astype(BF16)))
    kkv = (k * kk_ref[...]).reshape(NH, HD, nb)
    norm = jnp.maximum(jnp.sqrt(jnp.sum(kkv * kkv, axis=1, keepdims=True)), 1e-12)
    kkn = (kkv / norm).reshape(DH, nb)
    r_ref[...] = r
    k_ref[...] = k * (1.0 + (a - 1.0) * ka_ref[...])
    v_ref[...] = v
    kkn_ref[...] = kkn
    kka_ref[...] = kkn * a
    d_ref[...] = jnp.exp(-jnp.exp(w_log))
    g_ref[...] = _dot(g2t_ref[...], _sigmoid(gd).astype(BF16))


def _rwkv_sample_prep_t(cols, kv_rows, kwvw, prev_t, params_t):
    nb = cols.shape[0]
    args = [cols, kv_rows, kwvw, prev_t] + list(params_t)
    full = lambda shape: pl.BlockSpec(shape, lambda i: (0,) * len(shape))
    out_rows = [RWKV_COLS, kv_rows.shape[1], kwvw.shape[1]] + [DH] * 7
    return pl.pallas_call(
        _rwkv_sample_prep_t_body,
        grid=(1,),
        in_specs=[full(a.shape) for a in args],
        out_specs=[full((rows, nb)) for rows in out_rows],
        out_shape=[jax.ShapeDtypeStruct((rows, nb), F32) for rows in out_rows],
        scratch_shapes=[pltpu.VMEM((cols.shape[1], nb), F32)],
        compiler_params=_cparams(1),
        name="rwkv_sample_prep",
    )(*args)


def _rwkv_sample_step_t_body(s_ref, r_ref, k_ref, v_ref, kkn_ref, kka_ref, d_ref, g_ref, rk_ref, gnw_ref, gnb_ref,
                             o_ref, sout_ref, y_ref):
    r, k, kkn, kka, d = r_ref[...], k_ref[...], kkn_ref[...], kka_ref[...], d_ref[...]

    def value_row(i, carry):
        s_i = s_ref[0, i]
        sa = -jnp.sum(s_i * kkn, axis=0, keepdims=True)
        s_new = s_i * d + sa * kka + v_ref[pl.ds(i, 1), :] * k
        sout_ref[0, i] = s_new
        y_ref[pl.ds(i, 1), :] = jnp.sum(s_new * r, axis=0, keepdims=True)
        return carry

    lax.fori_loop(0, HD, value_row, 0)
    y = y_ref[...]
    v = v_ref[...]
    mean = jnp.mean(y, axis=0, keepdims=True)
    var = jnp.mean(jnp.square(y - mean), axis=0, keepdims=True)
    yn = (y - mean) * lax.rsqrt(var + GN_EPS) * gnw_ref[...] + gnb_ref[...]
    bonus = jnp.sum(r * k * rk_ref[...], axis=0, keepdims=True) * v
    o_ref[...] = (yn + bonus) * g_ref[...]


def _rwkv_sample_step_t(state4, vecs_t, rk_t, gnw_t, gnb_t):
    nh, hs, _, nb = state4.shape
    head_rows = pl.BlockSpec((hs, nb), lambda h: (h, 0))
    state_spec = pl.BlockSpec((1, hs, hs, nb), lambda h: (h, 0, 0, 0))
    return pl.pallas_call(
        _rwkv_sample_step_t_body,
        grid=(nh,),
        in_specs=[state_spec] + [head_rows] * 10,
        out_specs=[head_rows, state_spec],
        out_shape=[jax.ShapeDtypeStruct((nh * hs, nb), F32), jax.ShapeDtypeStruct(state4.shape, F32)],
        scratch_shapes=[pltpu.VMEM((hs, nb), F32)],
        compiler_params=_cparams(1),
        name="rwkv_sample_step",
    )(state4, *vecs_t, rk_t, gnw_t, gnb_t)


def _outproj_t_body(oa_ref, orwt_ref, wa_ref, wr_ref, h_ref, g_ref, o_ref):
    mixed = _dot(oa_ref[...].astype(BF16), wa_ref[...]) + _dot_tn(orwt_ref[...].astype(BF16), wr_ref[...])
    o_ref[...] = h_ref[...] + _rms(mixed, g_ref[...])


def _outproj_t(o_attn, o_rwkv_t, w_a, w_r, h, g_post):
    n, d = h.shape
    full = lambda a: pl.BlockSpec(a.shape, lambda i: (0,) * a.ndim)
    args = [o_attn, o_rwkv_t, w_a, w_r, h, g_post]
    return pl.pallas_call(
        _outproj_t_body,
        grid=(1,),
        in_specs=[full(a) for a in args],
        out_specs=pl.BlockSpec((n, d), lambda i: (0, 0)),
        out_shape=jax.ShapeDtypeStruct((n, d), F32),
        compiler_params=_cparams(1),
        name="outproj_sample",
    )(*args)


def _outproj_body(oa_ref, orw_ref, wa_ref, wr_ref, h_ref, g_ref, o_ref):
    mixed = _dot(oa_ref[...].astype(BF16), wa_ref[...]) + _dot(orw_ref[...].astype(BF16), wr_ref[...])
    o_ref[...] = h_ref[...] + _rms(mixed, g_ref[...])


def _outproj(o_attn, o_rwkv, w_a, w_r, h, g_post, tm):
    n, d = h.shape
    row = lambda i: (i, 0)
    const = lambda i: (0, 0)
    return pl.pallas_call(
        _outproj_body,
        grid=(n // tm,),
        in_specs=[pl.BlockSpec((tm, DH), row), pl.BlockSpec((tm, DH), row),
                  pl.BlockSpec((DH, d), const), pl.BlockSpec((DH, d), const),
                  pl.BlockSpec((tm, d), row), pl.BlockSpec((1, d), const)],
        out_specs=pl.BlockSpec((tm, d), row),
        out_shape=jax.ShapeDtypeStruct((n, d), F32),
        compiler_params=_cparams(1),
        name="outproj",
    )(o_attn, o_rwkv, w_a, w_r, h, g_post)


def _rope_tables(pos):
    inv = ROPE_THETA ** (-jnp.arange(ROPE_HALF, dtype=F32) / ROPE_HALF)
    ang = pos.astype(F32)[:, None] * inv[None, :]
    cos, sin = jnp.cos(ang), jnp.sin(ang)
    n = pos.shape[0]
    rest = HD - 2 * ROPE_HALF
    c = jnp.concatenate([cos, cos, jnp.ones((n, rest), F32)], axis=1)
    s_lo = jnp.concatenate([-sin, jnp.zeros((n, HD - ROPE_HALF), F32)], axis=1)
    s_hi = jnp.concatenate([jnp.zeros((n, ROPE_HALF), F32), sin, jnp.zeros((n, rest), F32)], axis=1)
    two = lambda x: jnp.concatenate([x, x], axis=1)
    return two(c), two(s_lo), two(s_hi)


def _cmp_to_sel(n_cmp, n_sel, rows):
    i = np.arange(n_cmp)[:, None] * CMP_STRIDE
    j = np.arange(n_sel)[None, :] * SEL_BLOCK
    ov = np.minimum(i + CMP_BLOCK, j + SEL_BLOCK) - np.maximum(i, j)
    m = np.zeros((rows, N_SEL_LANES), np.float32)
    m[:n_cmp, :n_sel] = np.maximum(ov, 0) // CMP_STRIDE
    return jnp.asarray(m, BF16)


def _block_expand(n_keys):
    j = np.arange(N_SEL_LANES)[:, None]
    t = np.arange(n_keys)[None, :]
    return jnp.asarray((t // SEL_BLOCK == j).astype(np.float32), BF16)


def _block_expand_t(n_keys):
    t = np.arange(n_keys)[:, None]
    j = np.arange(N_SEL_LANES)[None, :]
    return jnp.asarray((t // SEL_BLOCK == j).astype(np.float32), BF16)


def _pad_cols(x, width):
    return jnp.pad(x, ((0, 0), (0, width - x.shape[1])))


def _row_tile(n, cap):
    tm = min(n, cap)
    while n % tm:
        tm //= 2
    return tm


def kernel(x_prompt, x_sample, cache_nsa, page_table, state_win, state_wkv, state_shift, norm_f1_pre, norm_f1_post, ffn1_gu, ffn1_dn, norm_mix_pre, w_in, cmp_pe_k, cmp_w1_k, cmp_w2_k, cmp_pe_v, cmp_w1_v, cmp_w2_v, rw_mu, rw_w0, rw_w2, rw_a0, rw_a2, rw_g2, rw_kk, rw_ka, rw_rk, rw_gn_w, rw_gn_b, w_out, norm_mix_post, norm_f2_pre, ffn2_gu, ffn2_dn, norm_f2_post):
    depth = w_in.shape[0]
    assert depth == 1, "single-layer step"
    bp, t, d = x_prompt.shape
    bs, ts, _ = x_sample.shape
    assert ts == 1 and t % Q_BLOCK == 0 and t % RW_CHUNK == 0 and t >= WINDOW
    n_pages = page_table.shape[1]
    page = cache_nsa.shape[2]
    past = n_pages * page
    wb = state_win.shape[2]
    assert wb == min(WINDOW, past) and past % SEL_BLOCK == 0 and page % CMP_STRIDE == 0
    l = 0

    w1g, w1d = ffn1_gu[l].astype(BF16), ffn1_dn[l].astype(BF16)
    w2g, w2d = ffn2_gu[l].astype(BF16), ffn2_dn[l].astype(BF16)
    wi = w_in[l]
    w_nsa = _pad_cols(wi[:, :NSA_COLS], NSA_PAD).astype(BF16)
    w_rw = _pad_cols(wi[:, NSA_COLS:], RWKV_PAD).astype(BF16)
    w_oa, w_or = w_out[l, :DH].astype(BF16), w_out[l, DH:].astype(BF16)
    vec = lambda a: a.reshape(1, -1)
    cmp_k = _cmp_weights(cmp_pe_k[l], cmp_w1_k[l], cmp_w2_k[l])
    cmp_v = _cmp_weights(cmp_pe_v[l], cmp_w1_v[l], cmp_w2_v[l])
    g2_pad = jnp.pad(rw_g2[l], ((0, LORA_G_PAD - LORA_G), (0, 0))).astype(BF16)
    rw_prep = [_pad_cols(vec(rw_mu[l]), RWKV_PAD), vec(rw_w0[l]), rw_w2[l].astype(BF16), vec(rw_a0[l]),
               rw_a2[l].astype(BF16), g2_pad, vec(rw_kk[l]), vec(rw_ka[l])]
    rk_row, gnw_row, gnb_row = vec(rw_rk[l]), vec(rw_gn_w[l]), vec(rw_gn_b[l])

    outs = {}
    for name, x2, n_batch in (("p", x_prompt.reshape(bp * t, d), bp), ("s", x_sample.reshape(bs, d), bs)):
        n = x2.shape[0]
        tm = _row_tile(n, 512)
        is_prompt = name == "p"
        h1, hn = _ffn_half(x2, vec(norm_f1_pre[l]), w1g, w1d, vec(norm_f1_post[l]), vec(norm_mix_pre[l]),
                           tm, 512)
        pos = jnp.tile(jnp.arange(t), bp) if is_prompt else jnp.full((n,), past, jnp.int32)
        rc, rlo, rhi = _rope_tables(pos)
        q_raw, q_rot, kv_rows, kwvw, ks, vs, kw, vw, gates, *kv_t = _inproj_nsa(
            hn, w_nsa, rc, rlo, rhi, tm, t if is_prompt else None)
        cols = _inproj_rwkv(hn, w_rw, tm)
        if is_prompt:
            k_cmp, v_cmp = _compress_prompt(kv_rows, cmp_k, cmp_v, n_batch, t)
            n_seg = t // CMP_STRIDE
            mcs_t = _cmp_to_sel(n_seg - 1, t // SEL_BLOCK, n_seg).T
            o_attn = _attn_prompt(q_raw, q_rot, gates, k_cmp, v_cmp, ks, vs, kw, vw, mcs_t, _block_expand_t(t),
                                  n_batch, t)
            o_rwkv, wkv = _rwkv_prompt(cols, rw_prep + [rk_row, gnw_row, gnb_row], n_batch, t)
            win = kwvw.reshape(n_batch, t, 2 * DKV)[:, t - wb:]
            kv_out = jnp.transpose(kv_t[0].reshape(n_batch, 4, KVH, HD, t), (0, 4, 1, 2, 3))
            shift = cols.reshape(n_batch, t, RWKV_PAD)[:, t - 1:, :RWKV_COLS]
        else:
            cache4 = jnp.transpose(cache_nsa[l], (0, 2, 3, 4, 1)).reshape(cache_nsa.shape[1], 4, DKV, page)
            win4 = jnp.transpose(state_win[l], (0, 2, 3, 4, 1)).reshape(n, 2, DKV, wb)
            wkv4 = jnp.transpose(state_wkv[l], (1, 2, 3, 0))
            prev_t = jnp.transpose(state_shift[l].reshape(n, RWKV_COLS))
            lanes = lambda a: jnp.broadcast_to(a.reshape(-1, 1), (a.size, n))
            params_t = [lanes(rw_mu[l]), lanes(rw_w0[l]), rw_w2[l].T.astype(BF16), lanes(rw_a0[l]),
                        rw_a2[l].T.astype(BF16), rw_g2[l].T.astype(BF16), lanes(rw_kk[l]), lanes(rw_ka[l])]
            cols_t, kv_t, kwvw_t, *vecs_t = _rwkv_sample_prep_t(cols, kv_rows, kwvw, prev_t, params_t)
            n_seg = past // CMP_STRIDE
            mcs = _cmp_to_sel(n_seg - 1, past // SEL_BLOCK + 1, n_seg)
            o_attn, win4_new = _nsa_sample_t(
                cache4, page_table, win4,
                q_raw.astype(F32).reshape(n, NH, HD), q_rot.astype(F32).reshape(n, NH, HD),
                gates[:, :3 * NH].reshape(n, NH, 3), kv_rows.reshape(n, 1, 4 * DKV), kwvw.reshape(n, 1, 2 * DKV),
                kwvw_t, cmp_k, cmp_v, mcs, _block_expand(past), SAMPLE_SEQS_PER_STEP)
            o_attn = o_attn.reshape(n, DH)
            o_rwkv_t, wkv4_new = _rwkv_sample_step_t(wkv4, vecs_t, lanes(rw_rk[l]), lanes(rw_gn_w[l]),
                                                     lanes(rw_gn_b[l]))
            kv_out = jnp.transpose(kv_t.reshape(4, KVH, HD, n), (3, 0, 1, 2))
            win = jnp.transpose(win4_new.reshape(n, 2, KVH, HD, wb), (0, 4, 1, 2, 3))
            wkv = jnp.transpose(wkv4_new, (3, 0, 1, 2))
            shift = jnp.transpose(cols_t)[:, None, :]
            h2 = _outproj_t(o_attn, o_rwkv_t, w_oa, w_or, h1, vec(norm_mix_post[l]))
        if is_prompt:
            h2 = _outproj(o_attn, o_rwkv, w_oa, w_or, h1, vec(norm_mix_post[l]), tm)
        y = _ffn_half(h2, vec(norm_f2_pre[l]), w2g, w2d, vec(norm_f2_post[l]), None, tm, 512)
        outs[name] = (y, kv_out, win, wkv, shift)

    yp, kvp, winp, wkvp, shp = outs["p"]
    ys, kvs, wins, wkvs, shs = outs["s"]
    return (yp.reshape(bp, t, d), ys.reshape(bs, 1, d),
            kvp.reshape(1, bp, t, 4, KVH, HD), kvs.reshape(1, bs, 1, 4, KVH, HD),
            winp.reshape(1, bp, wb, 2, KVH, HD), wins.reshape(1, bs, wb, 2, KVH, HD),
            wkvp[None], wkvs[None], shp[None], shs[None])
```

```python
import functools

import numpy as np
import jax
import jax.numpy as jnp
from jax import lax
from jax.experimental import pallas as pl
from jax.experimental.pallas import tpu as pltpu

F32 = jnp.float32
BF16 = jnp.bfloat16

HD = 64
KVH = 4
GQA = 4
NH = 16
DH = NH * HD
DKV = KVH * HD
ROPE_HALF = 8
ROPE_THETA = 500000.0
CMP_BLOCK = 32
CMP_STRIDE = 16
CMP_HIDDEN = 128
SEL_BLOCK = 64
SEL_TOPN = 16
N_LOCAL_FORCED = 2
WINDOW = 512
Q_BLOCK = 128
LORA_W = 64
LORA_A = 64
LORA_G = 160
RMS_EPS = 1e-6
GN_EPS = 64e-5
NEG = -1e30
FORCE_BONUS = 1e4
LOG2_E = 1.4426950408889634
MASK_BIAS = -1e30
M_FLOOR = -1e20
NSA_COLS = DH + 6 * DKV + 3 * NH
RWKV_COLS = 3 * DH + LORA_W + LORA_A + LORA_G
NSA_PAD = 2688
RWKV_PAD = 3456
LORA_G_PAD = 256
RW_GROUP = 4
RW_CHUNK = 64
SEL_KEY_CHUNK = 512
N_SEL_LANES = 64
SAMPLE_SEQS_PER_STEP = 2

VMEM_LIMIT_BYTES = 56 * 1024 * 1024


def _cparams(n_axes):
    return pltpu.CompilerParams(dimension_semantics=("arbitrary",) * n_axes,
                                vmem_limit_bytes=VMEM_LIMIT_BYTES)


def _dot(a, b):
    return jnp.dot(a, b, preferred_element_type=F32)


def _dot_nt(a, b):
    return lax.dot_general(a, b, (((1,), (1,)), ((), ())), preferred_element_type=F32)


def _dot_tn(a, b):
    return lax.dot_general(a, b, (((0,), (0,)), ((), ())), preferred_element_type=F32)


def _dot_split3(a, b_bf16):
    hi = a.astype(BF16)
    r1 = a - hi.astype(F32)
    mid = r1.astype(BF16)
    lo = (r1 - mid.astype(F32)).astype(BF16)
    return _dot(hi, b_bf16) + _dot(mid, b_bf16) + _dot(lo, b_bf16)


def _rms(x, g):
    ms = jnp.mean(x * x, axis=-1, keepdims=True)
    return x * lax.rsqrt(ms + RMS_EPS) * g


def _sigmoid(x):
    return jax.nn.sigmoid(x)


def _ffn_body(has_next, x_ref, gpre_ref, wg_ref, wu_ref, wd_ref, gpost_ref, *rest):
    if has_next:
        gnext_ref, o_ref, on_ref, xn_ref, acc_ref = rest
    else:
        o_ref, xn_ref, acc_ref = rest
    j = pl.program_id(1)

    @pl.when(j == 0)
    def _():
        xn_ref[...] = _rms(x_ref[...], gpre_ref[...]).astype(BF16)
        acc_ref[...] = jnp.zeros_like(acc_ref)

    xn = xn_ref[...]
    g = _dot(xn, wg_ref[...])
    u = _dot(xn, wu_ref[...])
    act = ((g * _sigmoid(g)) * u).astype(BF16)
    acc_ref[...] += _dot(act, wd_ref[...])

    @pl.when(j == pl.num_programs(1) - 1)
    def _():
        y = x_ref[...] + 0.5 * _rms(acc_ref[...], gpost_ref[...])
        o_ref[...] = y
        if has_next:
            on_ref[...] = _rms(y, gnext_ref[...]).astype(BF16)


def _ffn_half(x, g_pre, w_gu, w_dn, g_post, g_next, tm, tf):
    n, d = x.shape
    f = w_dn.shape[0]
    nj = f // tf
    has_next = g_next is not None
    row = lambda i, j: (i, 0)
    const = lambda i, j: (0, 0)
    in_specs = [
        pl.BlockSpec((tm, d), row),
        pl.BlockSpec((1, d), const),
        pl.BlockSpec((d, tf), lambda i, j: (0, j)),
        pl.BlockSpec((d, tf), lambda i, j: (0, j + nj)),
        pl.BlockSpec((tf, d), lambda i, j: (j, 0)),
        pl.BlockSpec((1, d), const),
    ]
    args = [x, g_pre, w_gu, w_gu, w_dn, g_post]
    out_shape = [jax.ShapeDtypeStruct((n, d), F32)]
    out_specs = [pl.BlockSpec((tm, d), row)]
    if has_next:
        in_specs.append(pl.BlockSpec((1, d), const))
        args.append(g_next)
        out_shape.append(jax.ShapeDtypeStruct((n, d), BF16))
        out_specs.append(pl.BlockSpec((tm, d), row))
    res = pl.pallas_call(
        functools.partial(_ffn_body, has_next),
        grid=(n // tm, nj),
        in_specs=in_specs,
        out_specs=out_specs,
        out_shape=out_shape,
        scratch_shapes=[pltpu.VMEM((tm, d), BF16), pltpu.VMEM((tm, d), F32)],
        compiler_params=_cparams(2),
        name="ffn_half",
    )(*args)
    return res if has_next else res[0]


def _rope(x, c, s_lo, s_hi):
    w = x.shape[1]
    reps = w // 128
    tile = lambda t: t if reps == 1 else jnp.concatenate([t] * reps, axis=1)
    up = pltpu.roll(x, w - ROPE_HALF, 1)
    dn = pltpu.roll(x, ROPE_HALF, 1)
    return x * tile(c) + up * tile(s_lo) + dn * tile(s_hi)


def _inproj_nsa_body(hn_ref, w_ref, c_ref, slo_ref, shi_ref,
                     qraw_ref, qrot_ref, kv_ref, kwvw_ref, ks_ref, vs_ref, kw_ref, vw_ref, gates_ref, *kvt_ref):
    p = _dot(hn_ref[...], w_ref[...])
    c, s_lo, s_hi = c_ref[...], slo_ref[...], shi_ref[...]
    q = p[:, 0:DH] * (HD ** -0.5 * LOG2_E)
    qraw_ref[...] = q.astype(BF16)
    qrot_ref[...] = _rope(q, c, s_lo, s_hi).astype(BF16)
    o = DH
    ks = _rope(p[:, o + 2 * DKV:o + 3 * DKV], c, s_lo, s_hi)
    vs = p[:, o + 3 * DKV:o + 4 * DKV]
    kw = _rope(p[:, o + 4 * DKV:o + 5 * DKV], c, s_lo, s_hi)
    vw = p[:, o + 5 * DKV:o + 6 * DKV]
    kv_ref[:, 0:2 * DKV] = p[:, o:o + 2 * DKV]
    kv_ref[:, 2 * DKV:3 * DKV] = ks
    kv_ref[:, 3 * DKV:4 * DKV] = vs
    kwvw_ref[:, 0:DKV] = kw
    kwvw_ref[:, DKV:2 * DKV] = vw
    if kvt_ref:
        for j in range(4 * DKV // 128):
            kvt_ref[0][0, 128 * j:128 * (j + 1), :] = kv_ref[:, 128 * j:128 * (j + 1)].T
    vs_t, vw_t = vs.T, vw.T
    for k in range(KVH):
        sl = slice(HD * k, HD * (k + 1))
        ks_ref[k] = ks[:, sl].astype(BF16)
        vs_ref[k] = vs_t[sl, :].astype(BF16)
        kw_ref[k] = kw[:, sl].astype(BF16)
        vw_ref[k] = vw_t[sl, :].astype(BF16)
    gates_ref[...] = _sigmoid(p[:, o + 6 * DKV:o + 6 * DKV + 128])


def _inproj_nsa(hn, w_nsa, rope_c, rope_slo, rope_shi, tm, seq_len=None):
    n, d = hn.shape
    row = lambda i: (i, 0)
    hm = lambda i: (0, i, 0)
    tab_blocks = rope_c.shape[0] // tm
    tab = lambda i: (i % tab_blocks, 0)
    hm_shape = jax.ShapeDtypeStruct((KVH, n, HD), BF16)
    hm_spec = pl.BlockSpec((KVH, tm, HD), hm)
    hmt_shape = jax.ShapeDtypeStruct((KVH, HD, n), BF16)
    hmt_spec = pl.BlockSpec((KVH, HD, tm), lambda i: (0, 0, i))
    out_specs = [pl.BlockSpec((tm, DH), row), pl.BlockSpec((tm, DH), row),
                 pl.BlockSpec((tm, 4 * DKV), row), pl.BlockSpec((tm, 2 * DKV), row),
                 hm_spec, hmt_spec, hm_spec, hmt_spec,
                 pl.BlockSpec((tm, 128), row)]
    out_shape = [jax.ShapeDtypeStruct((n, DH), BF16), jax.ShapeDtypeStruct((n, DH), BF16),
                 jax.ShapeDtypeStruct((n, 4 * DKV), F32), jax.ShapeDtypeStruct((n, 2 * DKV), F32),
                 hm_shape, hmt_shape, hm_shape, hmt_shape,
                 jax.ShapeDtypeStruct((n, 128), F32)]
    if seq_len is not None:
        per_seq = seq_len // tm
        out_specs.append(pl.BlockSpec((1, 4 * DKV, tm), lambda i: (i // per_seq, 0, i % per_seq)))
        out_shape.append(jax.ShapeDtypeStruct((n // seq_len, 4 * DKV, seq_len), F32))
    return pl.pallas_call(
        _inproj_nsa_body,
        grid=(n // tm,),
        in_specs=[pl.BlockSpec((tm, d), row),
                  pl.BlockSpec((d, NSA_PAD), lambda i: (0, 0)),
                  pl.BlockSpec((tm, 128), tab), pl.BlockSpec((tm, 128), tab), pl.BlockSpec((tm, 128), tab)],
        out_specs=out_specs,
        out_shape=out_shape,
        compiler_params=_cparams(1),
        name="inproj_nsa",
    )(hn, w_nsa, rope_c, rope_slo, rope_shi)


def _matmul_body(x_ref, w_ref, o_ref):
    o_ref[...] = _dot(x_ref[...], w_ref[...])


def _inproj_rwkv(hn, w_rw, tm):
    n, d = hn.shape
    c = w_rw.shape[1]
    return pl.pallas_call(
        _matmul_body,
        grid=(n // tm,),
        in_specs=[pl.BlockSpec((tm, d), lambda i: (i, 0)), pl.BlockSpec((d, c), lambda i: (0, 0))],
        out_specs=pl.BlockSpec((tm, c), lambda i: (i, 0)),
        out_shape=jax.ShapeDtypeStruct((n, c), F32),
        compiler_params=_cparams(1),
        name="inproj_rwkv",
    )(hn, w_rw)


def _compress_accumulate(stage_ref, w1c_refs, n_seg):
    accs = [[jnp.zeros((n_seg, 2 * CMP_HIDDEN), F32) for _ in range(KVH)] for _ in range(2)]
    for s in range(CMP_STRIDE):
        for kv in range(2):
            w = w1c_refs[kv][s]
            for pair in range(KVH // 2):
                x = stage_ref[2 * kv + pair, pl.ds(s, n_seg, stride=CMP_STRIDE), :]
                for j in range(2):
                    k = 2 * pair + j
                    accs[kv][k] = accs[kv][k] + _dot(x[:, HD * j:HD * (j + 1)].astype(BF16), w)
    return accs


def _compress_finish(acc, pe0, w2, n_seg):
    a0 = acc[:, :CMP_HIDDEN]
    a1 = pltpu.roll(acc[:, CMP_HIDDEN:], n_seg - 1, 0)
    pre = pe0 + a0 + a1
    return _dot((pre * _sigmoid(pre)).astype(BF16), w2)


def _cmp_prompt_body(kv_ref, w1k_ref, w1v_ref, pek_ref, pev_ref, w1kf_ref, w1vf_ref, w2k_ref, w2v_ref,
                     kc_ref, vc_ref, stage_ref):
    n_seg = kv_ref.shape[0] // CMP_STRIDE
    for grp in range(4):
        stage_ref[grp] = kv_ref[:, 128 * grp:128 * (grp + 1)]
    accs = _compress_accumulate(stage_ref, (w1k_ref, w1v_ref), n_seg)
    for kv, (pe_ref, w1f_ref, w2_ref, out_ref) in enumerate(
            ((pek_ref, w1kf_ref, w2k_ref, kc_ref), (pev_ref, w1vf_ref, w2v_ref, vc_ref))):
        pe0 = _dot(pe_ref[...], w1f_ref[...])
        w2 = w2_ref[...]
        for k in range(KVH):
            c = _compress_finish(accs[kv][k], pe0, w2, n_seg)
            out_ref[0, k] = (c.T if kv == 1 else c).astype(BF16)


def _cmp_weights(pe, w1, w2):
    w1c = jnp.concatenate([w1[:CMP_STRIDE], w1[CMP_STRIDE:]], axis=-1).astype(BF16)
    return w1c, pe.reshape(1, CMP_BLOCK * HD), w1.reshape(CMP_BLOCK * HD, CMP_HIDDEN), w2.astype(BF16)


def _compress_prompt(kv_rows, cmp_k, cmp_v, n_batch, t):
    n_seg = t // CMP_STRIDE
    w1k, pek, w1kf, w2k = cmp_k
    w1v, pev, w1vf, w2v = cmp_v
    full = lambda a: pl.BlockSpec(a.shape, lambda b: (0,) * a.ndim)
    out_shape = jax.ShapeDtypeStruct((n_batch, KVH, n_seg, HD), BF16)
    out_spec = pl.BlockSpec((1, KVH, n_seg, HD), lambda b: (b, 0, 0, 0))
    out_shape_t = jax.ShapeDtypeStruct((n_batch, KVH, HD, n_seg), BF16)
    out_spec_t = pl.BlockSpec((1, KVH, HD, n_seg), lambda b: (b, 0, 0, 0))
    return pl.pallas_call(
        _cmp_prompt_body,
        grid=(n_batch,),
        in_specs=[pl.BlockSpec((t, 2 * DKV), lambda b: (b, 0)),
                  full(w1k), full(w1v), full(pek), full(pev), full(w1kf), full(w1vf), full(w2k), full(w2v)],
        out_specs=[out_spec, out_spec_t],
        out_shape=[out_shape, out_shape_t],
        scratch_shapes=[pltpu.VMEM((4, t, 128), F32)],
        compiler_params=_cparams(1),
        name="nsa_compress_prompt",
    )(kv_rows, w1k, w1v, pek, pev, w1kf, w1vf, w2k, w2v)


def _select_blocks(imp, cur, n_blocks):
    jb = lax.broadcasted_iota(jnp.int32, (1, N_SEL_LANES), 1)
    valid = (jb <= cur) & (jb < n_blocks)
    rel = cur - jb
    forced = (jb == 0) | ((rel >= 0) & (rel < N_LOCAL_FORCED))
    score = jnp.where(valid, imp + jnp.where(forced, FORCE_BONUS, 0.0), NEG)
    rank = jnp.zeros(score.shape, F32)
    for jp in range(min(n_blocks, N_SEL_LANES)):
        col = score[:, jp:jp + 1]
        tie = jnp.where(jb > jp, 1.0, 0.0)
        rank = rank + jnp.where(col > score, 1.0, jnp.where(col == score, tie, 0.0))
    return jnp.where(valid, jnp.where(rank < SEL_TOPN, 1.0, 0.0), 0.0)


def _softmax_rows(s, ok):
    sm = jnp.where(ok, s, NEG)
    m = jnp.max(sm, axis=-1, keepdims=True)
    e = jnp.where(ok, jnp.exp2(sm - m), 0.0)
    l = jnp.sum(e, axis=-1, keepdims=True)
    return e / jnp.maximum(l, 1e-30)


def _softmax_bias(s, bias):
    sb = s + bias
    m = jnp.maximum(jnp.max(sb, axis=-1, keepdims=True), M_FLOOR)
    e = jnp.exp(sb - m)
    return e, 1.0 / jnp.maximum(jnp.sum(e, axis=-1, keepdims=True), 1e-30)


def _select_blocks_t(imp_t, cur, n_blocks, n_live, score_ref):
    jb = lax.broadcasted_iota(jnp.int32, (N_SEL_LANES, 1), 0)
    valid = (jb <= cur) & (jb < n_blocks)
    rel = cur - jb
    forced = (jb == 0) | ((rel >= 0) & (rel < N_LOCAL_FORCED))
    score = jnp.where(valid, imp_t + jnp.where(forced, FORCE_BONUS, 0.0), NEG)
    score_ref[...] = score

    def body(jp, rank):
        other = score_ref[pl.ds(jp, 1), :]
        tie = jnp.where(jb > jp, 1.0, 0.0)
        return rank + jnp.where(other > score, 1.0, jnp.where(other == score, tie, 0.0))

    rank = lax.fori_loop(0, n_live, body, jnp.zeros(score.shape, F32))
    return jnp.where(valid, jnp.where(rank < SEL_TOPN, 1.0, 0.0), 0.0)


def _attn_prompt_body(qraw_ref, qrot_ref, gates_ref, kc_ref, vc_ref, ks_ref, vs_ref, kw_ref, vw_ref,
                      mcst_ref, e_ref, o_ref, score_ref):
    t = ks_ref.shape[1]
    n_seg = kc_ref.shape[2]
    i = pl.program_id(1)
    s0 = i * Q_BLOCK
    qpos = s0 + lax.broadcasted_iota(jnp.int32, (Q_BLOCK, 1), 0)
    qpos_row = s0 + lax.broadcasted_iota(jnp.int32, (1, Q_BLOCK), 1)
    n_idx = lax.broadcasted_iota(jnp.int32, (1, n_seg), 1)
    ok_cmp = (n_idx * CMP_STRIDE + (CMP_BLOCK - 1) <= qpos) & (n_idx < n_seg - 1)
    bias_cmp = jnp.where(ok_cmp, 0.0, MASK_BIAS)
    gates = gates_ref[...]
    mcst = mcst_ref[...]
    n_key_chunks = i // (SEL_KEY_CHUNK // Q_BLOCK) + 1
    n_live_blocks = (s0 + Q_BLOCK) // SEL_BLOCK
    win_start = pl.multiple_of(jnp.maximum(s0 - WINDOW, 0), Q_BLOCK)
    win_len = WINDOW + Q_BLOCK
    wpos = win_start + lax.broadcasted_iota(jnp.int32, (1, win_len), 1)
    dlt = qpos - wpos
    bias_win = jnp.where((dlt >= 0) & (dlt <= WINDOW), 0.0, MASK_BIAS)
    g_rows = [slice(g * Q_BLOCK, (g + 1) * Q_BLOCK) for g in range(GQA)]
    half = Q_BLOCK // 2

    kvhs = range(KVH)
    head_cols = lambda ref, k: jnp.concatenate(
        [ref[:, HD * (GQA * k + g):HD * (GQA * k + g + 1)] for g in range(GQA)], axis=0)
    qr = [head_cols(qraw_ref, k) for k in kvhs]
    qs = [head_cols(qrot_ref, k) for k in kvhs]

    s_cmp = [_dot_nt(qr[k], kc_ref[0, k]) for k in kvhs]
    o_cmp, sel_t = [], []
    for k in kvhs:
        p_heads = []
        for rows in g_rows:
            e_c, inv_c = _softmax_bias(s_cmp[k][rows], bias_cmp)
            p_heads.append(e_c * inv_c)
        o_cmp.append(_dot(jnp.concatenate([p.astype(BF16) for p in p_heads], axis=0), vc_ref[0, k]))
        p_sum = p_heads[0]
        for p in p_heads[1:]:
            p_sum = p_sum + p
        hi = p_sum.astype(BF16)
        r1 = p_sum - hi.astype(F32)
        mid = r1.astype(BF16)
        lo = (r1 - mid.astype(F32)).astype(BF16)
        imp_t = _dot_nt(mcst, hi) + _dot_nt(mcst, mid) + _dot_nt(mcst, lo)
        sel_t.append(_select_blocks_t(imp_t, qpos_row // SEL_BLOCK, t // SEL_BLOCK, n_live_blocks,
                                      score_ref.at[k]).astype(BF16))

    def chunk(c, carry):
        k0 = pl.multiple_of(c * SEL_KEY_CHUNK, SEL_KEY_CHUNK)
        kpos = k0 + lax.broadcasted_iota(jnp.int32, (1, SEL_KEY_CHUNK), 1)
        causal = kpos <= qpos
        e_blk = e_ref[:, pl.ds(k0, SEL_KEY_CHUNK)]
        s = [_dot_nt(qs[k], ks_ref[k, pl.ds(k0, SEL_KEY_CHUNK), :]) for k in kvhs]
        bias = [(jnp.where(causal, _dot_tn(sel_t[k], e_blk), 0.0) - 1.0) * (-MASK_BIAS) for k in kvhs]
        out = []
        for k in kvhs:
            m, l, acc = carry[k]
            m_new, p_b, p_tot = [], [], []
            for rows in g_rows:
                sb = s[k][rows] + bias[k]
                m_g = jnp.maximum(m[rows], jnp.max(sb, axis=-1, keepdims=True))
                p = jnp.exp(sb - m_g)
                m_new.append(m_g)
                p_tot.append(jnp.sum(p, axis=-1, keepdims=True))
                p_b.append(p.astype(BF16))
            m_new = jnp.concatenate(m_new, axis=0)
            alpha = jnp.exp(m - m_new)
            l = alpha * l + jnp.concatenate(p_tot, axis=0)
            acc = alpha * acc + _dot(jnp.concatenate(p_b, axis=0), vs_ref[k, pl.ds(k0, SEL_KEY_CHUNK), :])
            out.append((m_new, l, acc))
        return tuple(out)

    init = tuple((jnp.full((GQA * Q_BLOCK, 1), M_FLOOR, F32), jnp.zeros((GQA * Q_BLOCK, 1), F32),
                  jnp.zeros((GQA * Q_BLOCK, HD), F32)) for _ in kvhs)
    sel_out = lax.fori_loop(0, n_key_chunks, chunk, init)
    o_sel = [acc * (1.0 / jnp.maximum(l, 1e-30)) for _, l, acc in sel_out]

    s_win = [_dot_nt(qs[k], kw_ref[k, pl.ds(win_start, win_len), :]) for k in kvhs]
    for k in kvhs:
        e_b, inv_w = [], []
        for g in range(GQA):
            for hf in range(2):
                rows = slice(g * Q_BLOCK + hf * half, g * Q_BLOCK + (hf + 1) * half)
                e_w, inv = _softmax_bias(s_win[k][rows], bias_win[hf * half:(hf + 1) * half])
                e_b.append(e_w.astype(BF16))
                inv_w.append(inv)
        o_win = (_dot(jnp.concatenate(e_b, axis=0), vw_ref[k, pl.ds(win_start, win_len), :])
                 * jnp.concatenate(inv_w, axis=0))
        for g in range(GQA):
            h = GQA * k + g
            rows = g_rows[g]
            o_h = (gates[:, 3 * h:3 * h + 1] * o_cmp[k][rows] + gates[:, 3 * h + 1:3 * h + 2] * o_sel[k][rows]
                   + gates[:, 3 * h + 2:3 * h + 3] * o_win[rows])
            o_ref[:, HD * h:HD * (h + 1)] = o_h.astype(o_ref.dtype)


def _attn_prompt_t_body(qraw_ref, qrot_ref, gates_ref, kc_ref, vct_ref, ks_ref, vst_ref, kw_ref, vwt_ref,
                        mcst_ref, et_ref, o_ref, score_ref):
    t = et_ref.shape[0]
    n_bat = qraw_ref.shape[0]
    n_seg = kc_ref.shape[2]
    i = pl.program_id(0)
    s0 = i * Q_BLOCK
    qpos = s0 + lax.broadcasted_iota(jnp.int32, (1, Q_BLOCK), 1)
    n_idx = lax.broadcasted_iota(jnp.int32, (n_seg, 1), 0)
    ok_cmp = (n_idx * CMP_STRIDE + (CMP_BLOCK - 1) <= qpos) & (n_idx < n_seg - 1)
    bias_cmp = jnp.where(ok_cmp, 0.0, MASK_BIAS)
    mcst = mcst_ref[...]
    n_key_chunks = i // (SEL_KEY_CHUNK // Q_BLOCK) + 1
    n_live_blocks = (s0 + Q_BLOCK) // SEL_BLOCK
    win_start = pl.multiple_of(jnp.maximum(s0 - WINDOW, 0), Q_BLOCK)
    win_len = WINDOW + Q_BLOCK
    wpos = win_start + lax.broadcasted_iota(jnp.int32, (win_len, 1), 0)
    dlt = qpos - wpos
    bias_win = jnp.where((dlt >= 0) & (dlt <= WINDOW), 0.0, MASK_BIAS)
    g_cols = [slice(g * Q_BLOCK, (g + 1) * Q_BLOCK) for g in range(GQA)]
    units = [(bb, k) for bb in range(n_bat) for k in range(KVH)]
    n_units = len(units)
    head_rows = lambda ref, bb, k: jnp.concatenate(
        [ref[bb, :, HD * (GQA * k + g):HD * (GQA * k + g + 1)] for g in range(GQA)], axis=0)
    qr = [head_rows(qraw_ref, bb, k) for bb, k in units]
    qs = [head_rows(qrot_ref, bb, k) for bb, k in units]

    def softmax_cols(s_t, bias):
        e_b, inv = [], []
        for cols in g_cols:
            sb = s_t[:, cols] + bias
            m = jnp.maximum(jnp.max(sb, axis=0, keepdims=True), M_FLOOR)
            e = jnp.exp2(sb - m)
            e_b.append(e)
            inv.append(1.0 / jnp.maximum(jnp.sum(e, axis=0, keepdims=True), 1e-30))
        return e_b, inv

    o_cmp, sel_t = [], []
    cmp_scores = lambda u: _dot_nt(kc_ref[units[u][0], units[u][1]], qr[u])
    s_next = cmp_scores(0)
    for u, (bb, k) in enumerate(units):
        s_cmp = s_next
        if u + 1 < n_units:
            s_next = cmp_scores(u + 1)
        e_heads, inv_heads = softmax_cols(s_cmp, bias_cmp)
        p_heads = [e * inv for e, inv in zip(e_heads, inv_heads)]
        o_cmp.append(_dot(vct_ref[bb, k], jnp.concatenate([p.astype(BF16) for p in p_heads], axis=1)))
        p_sum = p_heads[0]
        for p in p_heads[1:]:
            p_sum = p_sum + p
        hi = p_sum.astype(BF16)
        r1 = p_sum - hi.astype(F32)
        mid = r1.astype(BF16)
        lo = (r1 - mid.astype(F32)).astype(BF16)
        imp_t = _dot(mcst, hi) + _dot(mcst, mid) + _dot(mcst, lo)
        sel_t.append(_select_blocks_t(imp_t, qpos // SEL_BLOCK, t // SEL_BLOCK, n_live_blocks,
                                      score_ref.at[u]).astype(BF16))

    def chunk(c, carry):
        k0 = pl.multiple_of(c * SEL_KEY_CHUNK, SEL_KEY_CHUNK)
        kpos = k0 + lax.broadcasted_iota(jnp.int32, (SEL_KEY_CHUNK, 1), 0)
        causal = kpos <= qpos
        e_blk = et_ref[pl.ds(k0, SEL_KEY_CHUNK), :]

        def scores(u):
            bb, k = units[u]
            rows = pl.ds(pl.multiple_of(bb * t + k0, SEL_KEY_CHUNK), SEL_KEY_CHUNK)
            return (_dot_nt(ks_ref[k, rows, :], qs[u]),
                    (jnp.where(causal, _dot(e_blk, sel_t[u]), 0.0) - 1.0) * (-MASK_BIAS))

        out = []
        nxt = scores(0)
        for u, (bb, k) in enumerate(units):
            s_k, bias_k = nxt
            if u + 1 < n_units:
                nxt = scores(u + 1)
            m, l, acc = carry[u]
            m_new, p_b, p_tot = [], [], []
            for cols in g_cols:
                sb = s_k[:, cols] + bias_k
                m_g = jnp.maximum(m[:, cols], jnp.max(sb, axis=0, keepdims=True))
                p = jnp.exp2(sb - m_g)
                m_new.append(m_g)
                p_tot.append(jnp.sum(p, axis=0, keepdims=True))
                p_b.append(p.astype(BF16))
            m_new = jnp.concatenate(m_new, axis=1)
            alpha = jnp.exp2(m - m_new)
            l = alpha * l + jnp.concatenate(p_tot, axis=1)
            cols_v = pl.ds(pl.multiple_of(bb * t + k0, SEL_KEY_CHUNK), SEL_KEY_CHUNK)
            acc = alpha * acc + _dot(vst_ref[k, :, cols_v], jnp.concatenate(p_b, axis=1))
            out.append((m_new, l, acc))
        return tuple(out)

    init = tuple((jnp.full((1, GQA * Q_BLOCK), M_FLOOR, F32), jnp.zeros((1, GQA * Q_BLOCK), F32),
                  jnp.zeros((HD, GQA * Q_BLOCK), F32)) for _ in units)
    sel_out = lax.fori_loop(0, n_key_chunks, chunk, init)
    o_sel = [acc * (1.0 / jnp.maximum(l, 1e-30)) for _, l, acc in sel_out]

    win_rows = lambda bb: pl.ds(pl.multiple_of(bb * t + win_start, Q_BLOCK), win_len)
    win_scores = lambda u: _dot_nt(kw_ref[units[u][1], win_rows(units[u][0]), :], qs[u])
    gates_all = [gates_ref[bb].T for bb in range(n_bat)]
    o_t = []
    s_next = win_scores(0)
    for u, (bb, k) in enumerate(units):
        gates_t = gates_all[bb]
        s_win = s_next
        if u + 1 < n_units:
            s_next = win_scores(u + 1)
        e_heads, inv_heads = softmax_cols(s_win, bias_win)
        o_win = (_dot(vwt_ref[k, :, win_rows(bb)],
                      jnp.concatenate([e.astype(BF16) for e in e_heads], axis=1))
                 * jnp.concatenate(inv_heads, axis=1))
        for g in range(GQA):
            h = GQA * k + g
            cols = g_cols[g]
            o_t.append(gates_t[3 * h:3 * h + 1] * o_cmp[u][:, cols] + gates_t[3 * h + 1:3 * h + 2] * o_sel[u][:, cols]
                       + gates_t[3 * h + 2:3 * h + 3] * o_win[:, cols])
    for bb in range(n_bat):
        for j in range(NH // 2):
            pair = jnp.concatenate([o_t[NH * bb + 2 * j], o_t[NH * bb + 2 * j + 1]], axis=0)
            o_ref[bb, :, 2 * HD * j:2 * HD * (j + 1)] = pair.T.astype(o_ref.dtype)


def _attn_prompt(q_raw, q_rot, gates, k_cmp, v_cmp, ks, vs, kw, vw, mcs_t, e_mat, n_batch, t):
    nqb = t // Q_BLOCK
    qblock = lambda width: pl.BlockSpec((n_batch, Q_BLOCK, width), lambda i: (0, i, 0))
    full = lambda a: pl.BlockSpec(a.shape, lambda i: (0,) * a.ndim)
    seq3 = lambda a: a.reshape(n_batch, t, a.shape[1])
    args = [seq3(q_raw), seq3(q_rot), seq3(gates), k_cmp, v_cmp, ks, vs, kw, vw, mcs_t, e_mat]
    o = pl.pallas_call(
        _attn_prompt_t_body,
        grid=(nqb,),
        in_specs=[qblock(DH), qblock(DH), qblock(128)] + [full(a) for a in args[3:]],
        out_specs=qblock(DH),
        out_shape=jax.ShapeDtypeStruct((n_batch, t, DH), F32),
        scratch_shapes=[pltpu.VMEM((n_batch * KVH, N_SEL_LANES, Q_BLOCK), F32)],
        compiler_params=_cparams(1),
        name="nsa_attn_prompt",
    )(*args)
    return o.reshape(n_batch * t, DH)


def _pick_kv_group(full):
    hk = lax.broadcasted_iota(jnp.int32, (NH, 1), 0) // GQA
    out = jnp.zeros((NH, HD), F32)
    for k in range(KVH):
        out = out + jnp.where(hk == k, full[:, HD * k:HD * (k + 1)], 0.0)
    return out


def _nsa_sample_body(n_pages, pt_ref, *refs):
    pages = refs[:n_pages]
    (win_ref, qraw_ref, qrot_ref, gates_ref, kvnew_ref, kwnew_ref,
     w1k_ref, w1v_ref, pek_ref, pev_ref, w1kf_ref, w1vf_ref, w2k_ref, w2v_ref, mcs_ref, e_ref,
     o_ref, winout_ref, stage_ref) = refs[n_pages:]
    page = pages[0].shape[1]
    past = n_pages * page
    n_seg = past // CMP_STRIDE

    def bdiag(q):
        qt = jnp.concatenate([q] * KVH, axis=1)
        hk = lax.broadcasted_iota(jnp.int32, (NH, DKV), 0) // GQA
        lk = lax.broadcasted_iota(jnp.int32, (NH, DKV), 1) // HD
        return jnp.where(hk == lk, qt, 0.0)

    qr = bdiag(qraw_ref[0])
    qs = bdiag(qrot_ref[0])
    qr_b, qs_b = qr.astype(BF16), qs.astype(BF16)

    for kk, pg in enumerate(pages):
        for grp in range(4):
            stage_ref[grp, page * kk:page * (kk + 1), :] = pg[0, :, 128 * grp:128 * (grp + 1)]
    accs = _compress_accumulate(stage_ref, (w1k_ref, w1v_ref), n_seg)
    cmp_kv = []
    for kv, (pe_ref, w1f_ref, w2_ref) in enumerate(((pek_ref, w1kf_ref, w2k_ref), (pev_ref, w1vf_ref, w2v_ref))):
        pe0 = _dot(pe_ref[...], w1f_ref[...])
        w2 = w2_ref[...]
        cmp_kv.append(jnp.concatenate(
            [_compress_finish(accs[kv][k], pe0, w2, n_seg) for k in range(KVH)], axis=1).astype(BF16))
    k_cmp, v_cmp = cmp_kv

    n_idx = lax.broadcasted_iota(jnp.int32, (1, n_seg), 1)
    ok_cmp = (n_idx * CMP_STRIDE + (CMP_BLOCK - 1) <= past) & (n_idx < n_seg - 1)
    p_cmp = _softmax_rows(_dot_nt(qr_b, k_cmp), ok_cmp)
    o_cmp = _pick_kv_group(_dot(p_cmp.astype(BF16), v_cmp))
    p_sum = jnp.concatenate(
        [jnp.sum(p_cmp[GQA * k:GQA * (k + 1)], axis=0, keepdims=True) for k in range(KVH)], axis=0)
    imp = _dot_split3(p_sum, mcs_ref[...])
    n_blocks = past // SEL_BLOCK + 1
    cur = jnp.full((KVH, 1), past // SEL_BLOCK, jnp.int32)
    sel = _select_blocks(imp, cur, n_blocks)
    sel16 = jnp.concatenate([jnp.broadcast_to(sel[k:k + 1], (GQA, N_SEL_LANES)) for k in range(KVH)], axis=0)

    kv_new = kvnew_ref[0]
    ks_new, vs_new = kv_new[:, 2 * DKV:3 * DKV], kv_new[:, 3 * DKV:4 * DKV]
    k_sel = jnp.concatenate([pg[0, :, 2 * DKV:3 * DKV] for pg in pages], axis=0).astype(BF16)
    v_sel = jnp.concatenate([pg[0, :, 3 * DKV:4 * DKV] for pg in pages], axis=0).astype(BF16)
    s_c = _dot_nt(qs_b, k_sel)
    s_n = jnp.sum(qs * ks_new, axis=-1, keepdims=True)
    ok_c = _dot(sel16.astype(BF16), e_ref[...]) > 0.5
    new_blk = past // SEL_BLOCK
    ok_n = sel16[:, new_blk:new_blk + 1] > 0.5
    sm_c = jnp.where(ok_c, s_c, NEG)
    sm_n = jnp.where(ok_n, s_n, NEG)
    m = jnp.maximum(jnp.max(sm_c, axis=-1, keepdims=True), sm_n)
    e_c = jnp.where(ok_c, jnp.exp2(sm_c - m), 0.0)
    e_n = jnp.where(ok_n, jnp.exp2(sm_n - m), 0.0)
    inv = 1.0 / jnp.maximum(jnp.sum(e_c, axis=-1, keepdims=True) + e_n, 1e-30)
    o_sel = _pick_kv_group(_dot((e_c * inv).astype(BF16), v_sel)
                           + (e_n * inv).astype(BF16).astype(F32) * vs_new.astype(BF16).astype(F32))

    kw_new = kwnew_ref[0]
    kwn, vwn = kw_new[:, 0:DKV], kw_new[:, DKV:2 * DKV]
    s_w = _dot_nt(qs_b, win_ref[0, :, 0:DKV].astype(BF16))
    s_wn = jnp.sum(qs * kwn, axis=-1, keepdims=True)
    mw = jnp.maximum(jnp.max(s_w, axis=-1, keepdims=True), s_wn)
    e_w = jnp.exp2(s_w - mw)
    e_wn = jnp.exp2(s_wn - mw)
    inv_w = 1.0 / (jnp.sum(e_w, axis=-1, keepdims=True) + e_wn)
    o_win = _pick_kv_group(_dot((e_w * inv_w).astype(BF16), win_ref[0, :, DKV:2 * DKV].astype(BF16))
                           + (e_wn * inv_w).astype(BF16).astype(F32) * vwn.astype(BF16).astype(F32))

    g = gates_ref[0]
    o_ref[0] = g[:, 0:1] * o_cmp + g[:, 1:2] * o_sel + g[:, 2:3] * o_win
    wb = win_ref.shape[1]
    winout_ref[0, 0:wb - 1, :] = win_ref[0, 1:wb, :]
    winout_ref[0, wb - 1:wb, :] = kw_new


def _nsa_sample(cache3, page_table, state_win, q_raw, q_rot, gates, kv_new, kw_new, cmp_k, cmp_v, mcs, e_mat):
    nb, n_pages = page_table.shape
    page = cache3.shape[1]
    wb = state_win.shape[1]
    w1k, pek, w1kf, w2k = cmp_k
    w1v, pev, w1vf, w2v = cmp_v
    consts = [w1k, w1v, pek, pev, w1kf, w1vf, w2k, w2v, mcs, e_mat]
    full = lambda a: pl.BlockSpec(a.shape, lambda b, pt: (0,) * a.ndim)
    page_specs = [pl.BlockSpec((1, page, cache3.shape[2]), functools.partial(
        lambda b, pt, kk: (pt[b * n_pages + kk], 0, 0), kk=kk)) for kk in range(n_pages)]
    per_b = lambda shape: pl.BlockSpec((1,) + shape, lambda b, pt: (b, 0, 0))
    grid_spec = pltpu.PrefetchScalarGridSpec(
        num_scalar_prefetch=1,
        grid=(nb,),
        in_specs=page_specs + [per_b((wb, 2 * DKV)), per_b((NH, HD)), per_b((NH, HD)), per_b((NH, 3)),
                               per_b((1, 4 * DKV)), per_b((1, 2 * DKV))] + [full(a) for a in consts],
        out_specs=[per_b((NH, HD)), per_b((wb, 2 * DKV))],
        scratch_shapes=[pltpu.VMEM((4, n_pages * page, 128), F32)],
    )
    return pl.pallas_call(
        functools.partial(_nsa_sample_body, n_pages),
        grid_spec=grid_spec,
        out_shape=[jax.ShapeDtypeStruct((nb, NH, HD), F32), jax.ShapeDtypeStruct((nb, wb, 2 * DKV), F32)],
        compiler_params=_cparams(1),
        name="nsa_sample",
    )(page_table.reshape(-1), *([cache3] * n_pages), state_win, q_raw, q_rot, gates, kv_new, kw_new, *consts)


def _kv_block_diag(q):
    qt = jnp.concatenate([q] * KVH, axis=1)
    hk = lax.broadcasted_iota(jnp.int32, (NH, DKV), 0) // GQA
    lk = lax.broadcasted_iota(jnp.int32, (NH, DKV), 1) // HD
    return jnp.where(hk == lk, qt, 0.0)


def _nsa_sample_t_body(n_pages, n_seq, pt_ref, *refs):
    all_pages = refs[:n_seq * n_pages]
    (win_ref, qraw_ref, qrot_ref, gates_ref, kvnew_ref, kwnew_ref, kwvwt_ref, perm_ref,
     w1k_ref, w1v_ref, pek_ref, pev_ref, w1kf_ref, w1vf_ref, w2k_ref, w2v_ref, mcs_ref, e_ref,
     o_ref, winout_ref, stage_ref) = refs[n_seq * n_pages:]
    page = all_pages[0].shape[3]
    n_seg = n_pages * page // CMP_STRIDE
    seg_pp = page // CMP_STRIDE

    perm = perm_ref[...]
    for kk, pg in enumerate(all_pages):
        for c in range(2):
            xs = _dot_nt(perm, pg[0, c].astype(BF16))
            for s in range(CMP_STRIDE):
                for k in range(KVH):
                    stage_ref[c, k, seg_pp * kk:seg_pp * (kk + 1), HD * s:HD * (s + 1)] = (
                        xs[seg_pp * s:seg_pp * (s + 1), HD * k:HD * (k + 1)])
    accs = [[_dot(stage_ref[kv, k].astype(BF16), w1_ref[...]) for k in range(KVH)]
            for kv, w1_ref in enumerate((w1k_ref, w1v_ref))]
    for q in range(n_seq):
        one = lambda ref: ref.at[pl.ds(q, 1)]
        _nsa_sample_one(
            all_pages[q * n_pages:(q + 1) * n_pages],
            [[a[n_seg * q:n_seg * (q + 1)] for a in row] for row in accs],
            pl.program_id(0) * n_seq + q,
            one(win_ref), one(qraw_ref), one(qrot_ref), one(gates_ref), one(kvnew_ref), one(kwnew_ref), kwvwt_ref,
            pek_ref, pev_ref, w1kf_ref, w1vf_ref, w2k_ref, w2v_ref, mcs_ref, e_ref, one(o_ref), one(winout_ref))


def _nsa_sample_one(pages, accs, b_idx, win_ref, qraw_ref, qrot_ref, gates_ref, kvnew_ref, kwnew_ref, kwvwt_ref,
                    pek_ref, pev_ref, w1kf_ref, w1vf_ref, w2k_ref, w2v_ref, mcs_ref, e_ref, o_ref, winout_ref):
    page = pages[0].shape[3]
    past = len(pages) * page
    n_seg = past // CMP_STRIDE
    qr = _kv_block_diag(qraw_ref[0])
    qs = _kv_block_diag(qrot_ref[0])
    qr_b, qs_b = qr.astype(BF16), qs.astype(BF16)
    cmp_kv = []
    for kv, (pe_ref, w1f_ref, w2_ref) in enumerate(((pek_ref, w1kf_ref, w2k_ref), (pev_ref, w1vf_ref, w2v_ref))):
        pe0 = _dot(pe_ref[...], w1f_ref[...])
        w2 = w2_ref[...]
        cmp_kv.append(jnp.concatenate(
            [_compress_finish(accs[kv][k], pe0, w2, n_seg) for k in range(KVH)], axis=1).astype(BF16))
    k_cmp, v_cmp = cmp_kv

    n_idx = lax.broadcasted_iota(jnp.int32, (1, n_seg), 1)
    ok_cmp = (n_idx * CMP_STRIDE + (CMP_BLOCK - 1) <= past) & (n_idx < n_seg - 1)
    p_cmp = _softmax_rows(_dot_nt(qr_b, k_cmp), ok_cmp)
    o_cmp = _pick_kv_group(_dot(p_cmp.astype(BF16), v_cmp))
    p_sum = jnp.concatenate(
        [jnp.sum(p_cmp[GQA * k:GQA * (k + 1)], axis=0, keepdims=True) for k in range(KVH)], axis=0)
    imp = _dot_split3(p_sum, mcs_ref[...])
    n_blocks = past // SEL_BLOCK + 1
    cur = jnp.full((KVH, 1), past // SEL_BLOCK, jnp.int32)
    sel = _select_blocks(imp, cur, n_blocks)
    sel16 = jnp.concatenate([jnp.broadcast_to(sel[k:k + 1], (GQA, N_SEL_LANES)) for k in range(KVH)], axis=0)

    kv_new = kvnew_ref[0]
    ks_new, vs_new = kv_new[:, 2 * DKV:3 * DKV], kv_new[:, 3 * DKV:4 * DKV]
    s_c = jnp.concatenate([_dot(qs_b, pg[0, 2].astype(BF16)) for pg in pages], axis=1)
    s_n = jnp.sum(qs * ks_new, axis=-1, keepdims=True)
    ok_c = _dot(sel16.astype(BF16), e_ref[...]) > 0.5
    new_blk = past // SEL_BLOCK
    ok_n = sel16[:, new_blk:new_blk + 1] > 0.5
    sm_c = jnp.where(ok_c, s_c, NEG)
    sm_n = jnp.where(ok_n, s_n, NEG)
    m = jnp.maximum(jnp.max(sm_c, axis=-1, keepdims=True), sm_n)
    e_c = jnp.where(ok_c, jnp.exp2(sm_c - m), 0.0)
    e_n = jnp.where(ok_n, jnp.exp2(sm_n - m), 0.0)
    inv = 1.0 / jnp.maximum(jnp.sum(e_c, axis=-1, keepdims=True) + e_n, 1e-30)
    p_c = (e_c * inv).astype(BF16)
    o_sel_full = (e_n * inv).astype(BF16).astype(F32) * vs_new.astype(BF16).astype(F32)
    for kk, pg in enumerate(pages):
        o_sel_full = o_sel_full + _dot_nt(p_c[:, page * kk:page * (kk + 1)], pg[0, 3].astype(BF16))
    o_sel = _pick_kv_group(o_sel_full)

    kw_new = kwnew_ref[0]
    kwn, vwn = kw_new[:, 0:DKV], kw_new[:, DKV:2 * DKV]
    s_w = _dot(qs_b, win_ref[0, 0].astype(BF16))
    s_wn = jnp.sum(qs * kwn, axis=-1, keepdims=True)
    mw = jnp.maximum(jnp.max(s_w, axis=-1, keepdims=True), s_wn)
    e_w = jnp.exp2(s_w - mw)
    e_wn = jnp.exp2(s_wn - mw)
    inv_w = 1.0 / (jnp.sum(e_w, axis=-1, keepdims=True) + e_wn)
    o_win = _pick_kv_group(_dot_nt((e_w * inv_w).astype(BF16), win_ref[0, 1].astype(BF16))
                           + (e_wn * inv_w).astype(BF16).astype(F32) * vwn.astype(BF16).astype(F32))

    g = gates_ref[0]
    o_ref[0] = g[:, 0:1] * o_cmp + g[:, 1:2] * o_sel + g[:, 2:3] * o_win

    wb = win_ref.shape[3]
    nb = kwvwt_ref.shape[1]
    mine = lax.broadcasted_iota(jnp.int32, (1, nb), 1) == b_idx
    last = lax.broadcasted_iota(jnp.int32, (1, wb), 1) == wb - 1
    for c in range(2):
        new_col = jnp.sum(jnp.where(mine, kwvwt_ref[DKV * c:DKV * (c + 1), :], 0.0), axis=-1, keepdims=True)
        winout_ref[0, c] = jnp.where(last, new_col, pltpu.roll(win_ref[0, c], wb - 1, 1))


def _nsa_sample_t(cache4, page_table, win4, q_raw, q_rot, gates, kv_new, kw_new, kwvw_t, cmp_k, cmp_v, mcs, e_mat,
                  n_seq):
    nb, n_pages = page_table.shape
    page = cache4.shape[3]
    wb = win4.shape[3]
    seg_pp = page // CMP_STRIDE
    rows = np.arange(page)
    perm = np.zeros((page, page), np.float32)
    perm[rows, (rows % seg_pp) * CMP_STRIDE + rows // seg_pp] = 1.0
    w1k, pek, w1kf, w2k = cmp_k
    w1v, pev, w1vf, w2v = cmp_v
    flat = lambda w: w.reshape(CMP_STRIDE * HD, 2 * CMP_HIDDEN)
    consts = [kwvw_t, jnp.asarray(perm, BF16), flat(w1k), flat(w1v), pek, pev, w1kf, w1vf, w2k, w2v, mcs, e_mat]
    full = lambda a: pl.BlockSpec(a.shape, lambda b, pt: (0,) * a.ndim)
    page_specs = [pl.BlockSpec((1, 4, DKV, page), functools.partial(
        lambda b, pt, kk: (pt[b * n_seq * n_pages + kk], 0, 0, 0), kk=kk)) for kk in range(n_seq * n_pages)]
    per_b = lambda shape: pl.BlockSpec((n_seq,) + shape, lambda b, pt: (b,) + (0,) * len(shape))
    grid_spec = pltpu.PrefetchScalarGridSpec(
        num_scalar_prefetch=1,
        grid=(nb // n_seq,),
        in_specs=page_specs + [per_b((2, DKV, wb)), per_b((NH, HD)), per_b((NH, HD)), per_b((NH, 3)),
                               per_b((1, 4 * DKV)), per_b((1, 2 * DKV))] + [full(a) for a in consts],
        out_specs=[per_b((NH, HD)), per_b((2, DKV, wb))],
        scratch_shapes=[pltpu.VMEM((2, KVH, n_seq * n_pages * seg_pp, CMP_STRIDE * HD), F32)],
    )
    return pl.pallas_call(
        functools.partial(_nsa_sample_t_body, n_pages, n_seq),
        grid_spec=grid_spec,
        out_shape=[jax.ShapeDtypeStruct((nb, NH, HD), F32), jax.ShapeDtypeStruct((nb, 2, DKV, wb), F32)],
        compiler_params=_cparams(1),
        name="nsa_sample",
    )(page_table.reshape(-1), *([cache4] * (n_seq * n_pages)), win4, q_raw, q_rot, gates, kv_new, kw_new, *consts)


def _rwkv_prep(cols, prev, mu, w0, w2, a0, a2, g2, kk_p, ka):
    xs = cols + (prev - cols) * mu
    r = xs[:, 0:DH]
    k = xs[:, DH:2 * DH]
    v = xs[:, 2 * DH:3 * DH]
    o = 3 * DH
    wd = xs[:, o:o + LORA_W]
    ad = xs[:, o + LORA_W:o + LORA_W + LORA_A]
    gd = xs[:, o + LORA_W + LORA_A:o + LORA_W + LORA_A + LORA_G_PAD]
    w = w0 + _dot(jnp.tanh(wd).astype(BF16), w2)
    w_log = -jax.nn.softplus(-w) - 0.5
    lw = -jnp.exp(w_log)
    a = _sigmoid(a0 + _dot(ad.astype(BF16), a2))
    g = _dot(_sigmoid(gd).astype(BF16), g2)
    kkv = k * kk_p
    k_mod = k * (1.0 + (a - 1.0) * ka)
    return r, k_mod, v, kkv, a, lw, g


def _head_norm(kkv_h):
    return kkv_h / jnp.maximum(jnp.sqrt(jnp.sum(kkv_h * kkv_h, axis=-1, keepdims=True)), 1e-12)


def _rwkv_head_out(y, r_h, k_h, v_h, g_h, rk_h, gnw_h, gnb_h):
    mean = jnp.mean(y, axis=-1, keepdims=True)
    var = jnp.mean(jnp.square(y - mean), axis=-1, keepdims=True)
    yn = (y - mean) * lax.rsqrt(var + GN_EPS) * gnw_h + gnb_h
    bonus = jnp.sum(r_h * k_h * rk_h, axis=-1, keepdims=True) * v_h
    return (yn + bonus) * g_h


def _head_ones():
    i = lax.broadcasted_iota(jnp.int32, (4 * HD, 4 * HD), 0) // HD
    j = lax.broadcasted_iota(jnp.int32, (4 * HD, 4 * HD), 1) // HD
    return jnp.where(i == j, 1.0, 0.0).astype(BF16)


def _head_sums(x, head_ones):
    rows, width = x.shape
    groups = width // (4 * HD)
    stacked = jnp.concatenate([x[:, 4 * HD * j:4 * HD * (j + 1)] for j in range(groups)], axis=0)
    sums = _dot_split3(stacked, head_ones)
    return jnp.concatenate([sums[rows * j:rows * (j + 1)] for j in range(groups)], axis=1)


def _block_diag(x):
    lane_head = lax.broadcasted_iota(jnp.int32, (1, x.shape[1]), 1) // HD
    return jnp.concatenate([jnp.where(lane_head == h, x, jnp.zeros_like(x)) for h in range(RW_GROUP)], axis=0)


def _tri_inverse_all(a_list, eye, blk16, blk32):
    b = lambda x: x.astype(BF16)
    bd = lambda x: _block_diag(b(x))
    a16 = [jnp.where(blk16, a, 0.0) for a in a_list]
    p = [_dot(b(x), bd(x)) for x in a16]
    t = [_dot(b(eye - x), bd(eye + q)) for x, q in zip(a16, p)]
    for _ in range(2):
        p = [_dot(b(q), bd(q)) for q in p]
        t = [_dot(b(x), bd(eye + q)) for x, q in zip(t, p)]
    off32 = blk32 & jnp.logical_not(blk16)
    for mask in (off32, jnp.logical_not(blk32)):
        m = [_dot(b(x), bd(jnp.where(mask, a, 0.0))) for x, a in zip(t, a_list)]
        t = [x - _dot(b(y), bd(x)) for x, y in zip(t, m)]
    return t


def _rwkv_prompt_body(cols_ref, mu_ref, w0_ref, w2_ref, a0_ref, a2_ref, g2_ref, kk_ref, ka_ref, rk_ref,
                      gnw_ref, gnb_ref, o_ref, wkv_ref, s_ref, last_ref):
    c = pl.program_id(0)
    ch = RW_CHUNK
    n_seq = cols_ref.shape[0]
    seq_rows = [slice(ch * q, ch * (q + 1)) for q in range(n_seq)]

    @pl.when(c == 0)
    def _():
        s_ref[...] = jnp.zeros_like(s_ref)
        last_ref[...] = jnp.zeros_like(last_ref)

    cols = jnp.concatenate([cols_ref[q] for q in range(n_seq)], axis=0)
    row = lax.broadcasted_iota(jnp.int32, (ch, 1), 0)
    prev = jnp.concatenate([jnp.where(row == 0, last_ref[q], pltpu.roll(cols[seq_rows[q]], 1, 0))
                            for q in range(n_seq)], axis=0)
    for q in range(n_seq):
        last_ref[q] = cols[ch * (q + 1) - 1:ch * (q + 1), :]
    r, k_mod, v, kkv, a, lw, g = _rwkv_prep(cols, prev, mu_ref[...], w0_ref[...], w2_ref[...], a0_ref[...],
                                            a2_ref[...], g2_ref[...], kk_ref[...], ka_ref[...])
    ti = lax.broadcasted_iota(jnp.int32, (n_seq * ch, n_seq * ch), 0)
    si = lax.broadcasted_iota(jnp.int32, (n_seq * ch, n_seq * ch), 1)
    ltri = jnp.where((ti >= si) & (ti // ch == si // ch), 1.0, 0.0).astype(BF16)
    cl = _cumsum_rows(lw, ltri)
    e_in = jnp.exp(cl)
    e_ex = jnp.exp(cl - lw)
    e_ng = jnp.exp(-cl)
    cl_last = [cl[ch * (q + 1) - 1:ch * (q + 1), :] for q in range(n_seq)]
    e_end = jnp.exp(jnp.concatenate([jnp.broadcast_to(x, (ch, DH)) for x in cl_last], axis=0) - cl)
    g_end = [jnp.exp(x) for x in cl_last]
    rk, gnw, gnb = rk_ref[...], gnw_ref[...], gnb_ref[...]
    b = lambda x: x.astype(BF16)
    gw = RW_GROUP * HD
    n_grp = NH // RW_GROUP
    slabs = [(seq_rows[q], slice(gw * j, gw * (j + 1)), q) for q in range(n_seq) for j in range(n_grp)]

    t_row = lax.broadcasted_iota(jnp.int32, (ch, gw), 0)
    s_col = lax.broadcasted_iota(jnp.int32, (ch, gw), 1) % HD
    strict = t_row > s_col
    causal = t_row >= s_col
    eye = jnp.where(t_row == s_col, 1.0, 0.0)
    blk16 = (t_row // 16) == (s_col // 16)
    blk32 = (t_row // 32) == (s_col // 32)
    head_ones = _head_ones()
    kk_n = kkv * jnp.minimum(lax.rsqrt(_head_sums(kkv * kkv, head_ones)), 1e12)
    beta = kk_n * a
    kq, rq = b(kk_n * e_ex), b(r * e_in)
    bd, kd = b(beta * e_ng), b(k_mod * e_ng)
    bdec, kdec = b(beta * e_end), b(k_mod * e_end)
    v_b = b(v)
    ns = range(len(slabs))
    lhs = [jnp.concatenate([kq[rs, gs], rq[rs, gs]], axis=0) for rs, gs, _ in slabs]
    s0 = [s_ref[i] for i in ns]
    quad = [_dot_nt(lhs[i], jnp.concatenate([_block_diag(bd[rs, gs]), _block_diag(kd[rs, gs])], axis=0))
            for i, (rs, gs, _) in enumerate(slabs)]
    s0t = [_dot_nt(lhs[i], _block_diag(b(s0[i]))) for i in ns]
    a_b = [jnp.where(strict, q[0:ch, 0:gw], 0.0) for q in quad]
    a_kv = [_dot(b(jnp.where(strict, quad[i][0:ch, gw:2 * gw], 0.0)), _block_diag(v_b[rs, gs]))
            for i, (rs, gs, _) in enumerate(slabs)]
    t_inv = _tri_inverse_all(a_b, eye, blk16, blk32)
    u = [-_dot(b(t_inv[i]), _block_diag(b(s0t[i][0:ch] + a_kv[i]))) for i in ns]
    y = [s0t[i][ch:2 * ch]
         + _dot(b(jnp.where(causal, quad[i][ch:2 * ch, 0:gw], 0.0)), _block_diag(b(u[i])))
         + _dot(b(jnp.where(causal, quad[i][ch:2 * ch, gw:2 * gw], 0.0)), _block_diag(v_b[rs, gs]))
         for i, (rs, gs, _) in enumerate(slabs)]
    lane_head = lax.broadcasted_iota(jnp.int32, (1, gw), 1) // HD
    for i, (rs, gs, q) in enumerate(slabs):
        cross = _dot_tn(jnp.concatenate([b(u[i]), v_b[rs, gs]], axis=0),
                        jnp.concatenate([bdec[rs, gs], kdec[rs, gs]], axis=0))
        s_add = jnp.zeros((HD, gw), F32)
        for h in range(RW_GROUP):
            s_add = s_add + jnp.where(lane_head == h, cross[HD * h:HD * (h + 1)], 0.0)
        s_ref[i] = s0[i] * g_end[q][:, gs] + s_add
    y_all = jnp.concatenate([jnp.concatenate(y[n_grp * q:n_grp * (q + 1)], axis=1) for q in range(n_seq)],
                            axis=0)
    yc = y_all - _head_sums(y_all, head_ones) * (1.0 / HD)
    var = _head_sums(yc * yc, head_ones) * (1.0 / HD)
    yn = yc * lax.rsqrt(var + GN_EPS) * gnw + gnb
    bonus = _head_sums(r * k_mod * rk, head_ones) * v
    out = ((yn + bonus) * g).astype(o_ref.dtype)
    for q in range(n_seq):
        o_ref[q] = out[seq_rows[q]]

    @pl.when(c == pl.num_programs(0) - 1)
    def _():
        for q in range(n_seq):
            for h in range(NH):
                lanes = slice(HD * (h % RW_GROUP), HD * (h % RW_GROUP + 1))
                wkv_ref[q, h] = s_ref[n_grp * q + h // RW_GROUP][:, lanes]


def _cumsum_rows(x, ltri):
    hi = x.astype(BF16)
    r1 = x - hi.astype(F32)
    mid = r1.astype(BF16)
    lo = (r1 - mid.astype(F32)).astype(BF16)
    return _dot(ltri, hi) + _dot(ltri, mid) + _dot(ltri, lo)


def _rwkv_prompt(cols, rw, n_batch, t):
    nc = t // RW_CHUNK
    full = lambda a: pl.BlockSpec(a.shape, lambda c: (0,) * a.ndim)
    chunk = lambda c: (0, c, 0)
    o, wkv = pl.pallas_call(
        _rwkv_prompt_body,
        grid=(nc,),
        in_specs=[pl.BlockSpec((n_batch, RW_CHUNK, RWKV_PAD), chunk)] + [full(a) for a in rw],
        out_specs=[pl.BlockSpec((n_batch, RW_CHUNK, DH), chunk),
                   pl.BlockSpec((n_batch, NH, HD, HD), lambda c: (0, 0, 0, 0))],
        out_shape=[jax.ShapeDtypeStruct((n_batch, t, DH), F32),
                   jax.ShapeDtypeStruct((n_batch, NH, HD, HD), F32)],
        scratch_shapes=[pltpu.VMEM((n_batch * NH // RW_GROUP, HD, RW_GROUP * HD), F32),
                        pltpu.VMEM((n_batch, 1, RWKV_PAD), F32)],
        compiler_params=_cparams(1),
        name="rwkv_prompt",
    )(cols.reshape(n_batch, t, RWKV_PAD), *rw)
    return o.reshape(n_batch * t, DH), wkv


def _rwkv_sample_prep_body(cols_ref, prev_ref, mu_ref, w0_ref, w2_ref, a0_ref, a2_ref, g2_ref, kk_ref, ka_ref,
                           r_ref, k_ref, v_ref, kkn_ref, a_ref, d_ref, g_ref):
    r, k_mod, v, kkv, a, lw, g = _rwkv_prep(cols_ref[...], prev_ref[...], mu_ref[...], w0_ref[...], w2_ref[...],
                                            a0_ref[...], a2_ref[...], g2_ref[...], kk_ref[...], ka_ref[...])
    r_ref[...] = r
    k_ref[...] = k_mod
    v_ref[...] = v
    a_ref[...] = a
    d_ref[...] = jnp.exp(lw)
    g_ref[...] = g
    for h in range(NH):
        sl = slice(HD * h, HD * (h + 1))
        kkn_ref[:, sl] = _head_norm(kkv[:, sl])


def _rwkv_sample_prep(cols, prev, rw_prep):
    n = cols.shape[0]
    args = [cols, prev] + list(rw_prep)
    full = lambda a: pl.BlockSpec(a.shape, lambda i: (0,) * a.ndim)
    out = jax.ShapeDtypeStruct((n, DH), F32)
    return pl.pallas_call(
        _rwkv_sample_prep_body,
        grid=(1,),
        in_specs=[full(a) for a in args],
        out_specs=[pl.BlockSpec((n, DH), lambda i: (0, 0))] * 7,
        out_shape=[out] * 7,
        compiler_params=_cparams(1),
        name="rwkv_sample_prep",
    )(*args)


def _rwkv_sample_step_body(s_ref, r_ref, k_ref, v_ref, kk_ref, a_ref, d_ref, g_ref, rk_ref, gnw_ref, gnb_ref,
                           o_ref, sout_ref):
    s = s_ref[0]
    r, k, v, kk, a, d, g = (x[0] for x in (r_ref, k_ref, v_ref, kk_ref, a_ref, d_ref, g_ref))
    eye = jnp.where(lax.broadcasted_iota(jnp.int32, (HD, HD), 0) == lax.broadcasted_iota(jnp.int32, (HD, HD), 1),
                    1.0, 0.0)
    row = lambda x: x[:, None, :]
    col = lambda x: jnp.sum(eye[None] * x[:, None, :], axis=-1, keepdims=True)
    sa = -jnp.sum(s * row(kk), axis=-1, keepdims=True)
    s_new = s * row(d) + sa * row(kk * a) + col(v) * row(k)
    sout_ref[0] = s_new
    y_col = jnp.sum(s_new * row(r), axis=-1, keepdims=True)
    y = jnp.sum(y_col * eye[None], axis=1)
    o_ref[0] = _rwkv_head_out(y, r, k, v, g, rk_ref[...], gnw_ref[...], gnb_ref[...])


def _rwkv_sample_step(state, vecs, rk, gnw, gnb):
    nb = state.shape[0]
    per_b3 = pl.BlockSpec((1, NH, HD), lambda b: (b, 0, 0))
    per_b4 = pl.BlockSpec((1, NH, HD, HD), lambda b: (b, 0, 0, 0))
    full = pl.BlockSpec((NH, HD), lambda b: (0, 0))
    return pl.pallas_call(
        _rwkv_sample_step_body,
        grid=(nb,),
        in_specs=[per_b4] + [per_b3] * 7 + [full] * 3,
        out_specs=[per_b3, per_b4],
        out_shape=[jax.ShapeDtypeStruct((nb, NH, HD), F32), jax.ShapeDtypeStruct((nb, NH, HD, HD), F32)],
        compiler_params=_cparams(1),
        name="rwkv_sample_step",
    )(state, *vecs, rk, gnw, gnb)


def _rwkv_sample_prep_t_body(cols_ref, kv_ref, kwvw_ref, prev_ref, mu_ref, w0_ref, w2t_ref, a0_ref, a2t_ref,
                             g2t_ref, kk_ref, ka_ref,
                             colst_ref, kvt_ref, kwvwt_ref, r_ref, k_ref, v_ref, kkn_ref, kka_ref, d_ref, g_ref,
                             tr_ref):
    nb = cols_ref.shape[0]
    for j in range(cols_ref.shape[1] // 128):
        tr_ref[128 * j:128 * (j + 1), :] = cols_ref[:, 128 * j:128 * (j + 1)].T
    for j in range(kv_ref.shape[1] // 128):
        kvt_ref[128 * j:128 * (j + 1), :] = kv_ref[:, 128 * j:128 * (j + 1)].T
    for j in range(kwvw_ref.shape[1] // 128):
        kwvwt_ref[128 * j:128 * (j + 1), :] = kwvw_ref[:, 128 * j:128 * (j + 1)].T
    cols = tr_ref[0:RWKV_COLS, :]
    colst_ref[...] = cols
    xs = cols + (prev_ref[...] - cols) * mu_ref[...]
    r = xs[0:DH]
    k = xs[DH:2 * DH]
    v = xs[2 * DH:3 * DH]
    o = 3 * DH
    wd = xs[o:o + LORA_W]
    ad = xs[o + LORA_W:o + LORA_W + LORA_A]
    gd = xs[o + LORA_W + LORA_A:RWKV_COLS]
    w = w0_ref[...] + _dot(w2t_ref[...], jnp.tanh(wd).astype(BF16))
    w_log = -jax.nn.softplus(-w) - 0.5
    a = _sigmoid(a0_ref[...] + _dot(a2t_ref[...], ad.astype(BF16)))
    kkv = (k * kk_ref[...]).reshape(NH, HD, nb)
    norm = jnp.maximum(jnp.sqrt(jnp.sum(kkv * kkv, axis=1, keepdims=True)), 1e-12)
    kkn = (kkv / norm).reshape(DH, nb)
    r_ref[...] = r
    k_ref[...] = k * (1.0 + (a - 1.0) * ka_ref[...])
    v_ref[...] = v
    kkn_ref[...] = kkn
    kka_ref[...] = kkn * a
    d_ref[...] = jnp.exp(-jnp.exp(w_log))
    g_ref[...] = _dot(g2t_ref[...], _sigmoid(gd).astype(BF16))


def _rwkv_sample_prep_t(cols, kv_rows, kwvw, prev_t, params_t):
    nb = cols.shape[0]
    args = [cols, kv_rows, kwvw, prev_t] + list(params_t)
    full = lambda shape: pl.BlockSpec(shape, lambda i: (0,) * len(shape))
    out_rows = [RWKV_COLS, kv_rows.shape[1], kwvw.shape[1]] + [DH] * 7
    return pl.pallas_call(
        _rwkv_sample_prep_t_body,
        grid=(1,),
        in_specs=[full(a.shape) for a in args],
        out_specs=[full((rows, nb)) for rows in out_rows],
        out_shape=[jax.ShapeDtypeStruct((rows, nb), F32) for rows in out_rows],
        scratch_shapes=[pltpu.VMEM((cols.shape[1], nb), F32)],
        compiler_params=_cparams(1),
        name="rwkv_sample_prep",
    )(*args)


def _rwkv_sample_step_t_body(s_ref, r_ref, k_ref, v_ref, kkn_ref, kka_ref, d_ref, g_ref, rk_ref, gnw_ref, gnb_ref,
                             o_ref, sout_ref, y_ref):
    r, k, kkn, kka, d = r_ref[...], k_ref[...], kkn_ref[...], kka_ref[...], d_ref[...]

    def value_row(i, carry):
        s_i = s_ref[0, i]
        sa = -jnp.sum(s_i * kkn, axis=0, keepdims=True)
        s_new = s_i * d + sa * kka + v_ref[pl.ds(i, 1), :] * k
        sout_ref[0, i] = s_new
        y_ref[pl.ds(i, 1), :] = jnp.sum(s_new * r, axis=0, keepdims=True)
        return carry

    lax.fori_loop(0, HD, value_row, 0)
    y = y_ref[...]
    v = v_ref[...]
    mean = jnp.mean(y, axis=0, keepdims=True)
    var = jnp.mean(jnp.square(y - mean), axis=0, keepdims=True)
    yn = (y - mean) * lax.rsqrt(var + GN_EPS) * gnw_ref[...] + gnb_ref[...]
    bonus = jnp.sum(r * k * rk_ref[...], axis=0, keepdims=True) * v
    o_ref[...] = (yn + bonus) * g_ref[...]


def _rwkv_sample_step_t(state4, vecs_t, rk_t, gnw_t, gnb_t):
    nh, hs, _, nb = state4.shape
    head_rows = pl.BlockSpec((hs, nb), lambda h: (h, 0))
    state_spec = pl.BlockSpec((1, hs, hs, nb), lambda h: (h, 0, 0, 0))
    return pl.pallas_call(
        _rwkv_sample_step_t_body,
        grid=(nh,),
        in_specs=[state_spec] + [head_rows] * 10,
        out_specs=[head_rows, state_spec],
        out_shape=[jax.ShapeDtypeStruct((nh * hs, nb), F32), jax.ShapeDtypeStruct(state4.shape, F32)],
        scratch_shapes=[pltpu.VMEM((hs, nb), F32)],
        compiler_params=_cparams(1),
        name="rwkv_sample_step",
    )(state4, *vecs_t, rk_t, gnw_t, gnb_t)


def _outproj_t_body(oa_ref, orwt_ref, wa_ref, wr_ref, h_ref, g_ref, o_ref):
    mixed = _dot(oa_ref[...].astype(BF16), wa_ref[...]) + _dot_tn(orwt_ref[...].astype(BF16), wr_ref[...])
    o_ref[...] = h_ref[...] + _rms(mixed, g_ref[...])


def _outproj_t(o_attn, o_rwkv_t, w_a, w_r, h, g_post):
    n, d = h.shape
    full = lambda a: pl.BlockSpec(a.shape, lambda i: (0,) * a.ndim)
    args = [o_attn, o_rwkv_t, w_a, w_r, h, g_post]
    return pl.pallas_call(
        _outproj_t_body,
        grid=(1,),
        in_specs=[full(a) for a in args],
        out_specs=pl.BlockSpec((n, d), lambda i: (0, 0)),
        out_shape=jax.ShapeDtypeStruct((n, d), F32),
        compiler_params=_cparams(1),
        name="outproj_sample",
    )(*args)


def _outproj_body(oa_ref, orw_ref, wa_ref, wr_ref, h_ref, g_ref, o_ref):
    mixed = _dot(oa_ref[...].astype(BF16), wa_ref[...]) + _dot(orw_ref[...].astype(BF16), wr_ref[...])
    o_ref[...] = h_ref[...] + _rms(mixed, g_ref[...])


def _outproj(o_attn, o_rwkv, w_a, w_r, h, g_post, tm):
    n, d = h.shape
    row = lambda i: (i, 0)
    const = lambda i: (0, 0)
    return pl.pallas_call(
        _outproj_body,
        grid=(n // tm,),
        in_specs=[pl.BlockSpec((tm, DH), row), pl.BlockSpec((tm, DH), row),
                  pl.BlockSpec((DH, d), const), pl.BlockSpec((DH, d), const),
                  pl.BlockSpec((tm, d), row), pl.BlockSpec((1, d), const)],
        out_specs=pl.BlockSpec((tm, d), row),
        out_shape=jax.ShapeDtypeStruct((n, d), F32),
        compiler_params=_cparams(1),
        name="outproj",
    )(o_attn, o_rwkv, w_a, w_r, h, g_post)


def _rope_tables(pos):
    inv = ROPE_THETA ** (-jnp.arange(ROPE_HALF, dtype=F32) / ROPE_HALF)
    ang = pos.astype(F32)[:, None] * inv[None, :]
    cos, sin = jnp.cos(ang), jnp.sin(ang)
    n = pos.shape[0]
    rest = HD - 2 * ROPE_HALF
    c = jnp.concatenate([cos, cos, jnp.ones((n, rest), F32)], axis=1)
    s_lo = jnp.concatenate([-sin, jnp.zeros((n, HD - ROPE_HALF), F32)], axis=1)
    s_hi = jnp.concatenate([jnp.zeros((n, ROPE_HALF), F32), sin, jnp.zeros((n, rest), F32)], axis=1)
    two = lambda x: jnp.concatenate([x, x], axis=1)
    return two(c), two(s_lo), two(s_hi)


def _cmp_to_sel(n_cmp, n_sel, rows):
    i = np.arange(n_cmp)[:, None] * CMP_STRIDE
    j = np.arange(n_sel)[None, :] * SEL_BLOCK
    ov = np.minimum(i + CMP_BLOCK, j + SEL_BLOCK) - np.maximum(i, j)
    m = np.zeros((rows, N_SEL_LANES), np.float32)
    m[:n_cmp, :n_sel] = np.maximum(ov, 0) // CMP_STRIDE
    return jnp.asarray(m, BF16)


def _block_expand(n_keys):
    j = np.arange(N_SEL_LANES)[:, None]
    t = np.arange(n_keys)[None, :]
    return jnp.asarray((t // SEL_BLOCK == j).astype(np.float32), BF16)


def _block_expand_t(n_keys):
    t = np.arange(n_keys)[:, None]
    j = np.arange(N_SEL_LANES)[None, :]
    return jnp.asarray((t // SEL_BLOCK == j).astype(np.float32), BF16)


def _pad_cols(x, width):
    return jnp.pad(x, ((0, 0), (0, width - x.shape[1])))


def _row_tile(n, cap):
    tm = min(n, cap)
    while n % tm:
        tm //= 2
    return tm


def kernel(x_prompt, x_sample, cache_nsa, page_table, state_win, state_wkv, state_shift, norm_f1_pre, norm_f1_post, ffn1_gu, ffn1_dn, norm_mix_pre, w_in, cmp_pe_k, cmp_w1_k, cmp_w2_k, cmp_pe_v, cmp_w1_v, cmp_w2_v, rw_mu, rw_w0, rw_w2, rw_a0, rw_a2, rw_g2, rw_kk, rw_ka, rw_rk, rw_gn_w, rw_gn_b, w_out, norm_mix_post, norm_f2_pre, ffn2_gu, ffn2_dn, norm_f2_post):
    depth = w_in.shape[0]
    assert depth == 1, "single-layer step"
    bp, t, d = x_prompt.shape
    bs, ts, _ = x_sample.shape
    assert ts == 1 and t % Q_BLOCK == 0 and t % RW_CHUNK == 0 and t >= WINDOW
    n_pages = page_table.shape[1]
    page = cache_nsa.shape[2]
    past = n_pages * page
    wb = state_win.shape[2]
    assert wb == min(WINDOW, past) and past % SEL_BLOCK == 0 and page % CMP_STRIDE == 0
    l = 0

    w1g, w1d = ffn1_gu[l].astype(BF16), ffn1_dn[l].astype(BF16)
    w2g, w2d = ffn2_gu[l].astype(BF16), ffn2_dn[l].astype(BF16)
    wi = w_in[l]
    w_nsa = _pad_cols(wi[:, :NSA_COLS], NSA_PAD).astype(BF16)
    w_rw = _pad_cols(wi[:, NSA_COLS:], RWKV_PAD).astype(BF16)
    w_oa, w_or = w_out[l, :DH].astype(BF16), w_out[l, DH:].astype(BF16)
    vec = lambda a: a.reshape(1, -1)
    cmp_k = _cmp_weights(cmp_pe_k[l], cmp_w1_k[l], cmp_w2_k[l])
    cmp_v = _cmp_weights(cmp_pe_v[l], cmp_w1_v[l], cmp_w2_v[l])
    g2_pad = jnp.pad(rw_g2[l], ((0, LORA_G_PAD - LORA_G), (0, 0))).astype(BF16)
    rw_prep = [_pad_cols(vec(rw_mu[l]), RWKV_PAD), vec(rw_w0[l]), rw_w2[l].astype(BF16), vec(rw_a0[l]),
               rw_a2[l].astype(BF16), g2_pad, vec(rw_kk[l]), vec(rw_ka[l])]
    rk_row, gnw_row, gnb_row = vec(rw_rk[l]), vec(rw_gn_w[l]), vec(rw_gn_b[l])

    outs = {}
    for name, x2, n_batch in (("p", x_prompt.reshape(bp * t, d), bp), ("s", x_sample.reshape(bs, d), bs)):
        n = x2.shape[0]
        tm = _row_tile(n, 512)
        is_prompt = name == "p"
        h1, hn = _ffn_half(x2, vec(norm_f1_pre[l]), w1g, w1d, vec(norm_f1_post[l]), vec(norm_mix_pre[l]),
                           tm, 512)
        pos = jnp.arange(t) if is_prompt else jnp.full((n,), past, jnp.int32)
        rc, rlo, rhi = _rope_tables(pos)
        q_raw, q_rot, kv_rows, kwvw, ks, vs, kw, vw, gates, *kv_t = _inproj_nsa(
            hn, w_nsa, rc, rlo, rhi, tm, t if is_prompt else None)
        cols = _inproj_rwkv(hn, w_rw, tm)
        if is_prompt:
            k_cmp, v_cmp = _compress_prompt(kv_rows, cmp_k, cmp_v, n_batch, t)
            n_seg = t // CMP_STRIDE
            mcs_t = _cmp_to_sel(n_seg - 1, t // SEL_BLOCK, n_seg).T
            o_attn = _attn_prompt(q_raw, q_rot, gates, k_cmp, v_cmp, ks, vs, kw, vw, mcs_t, _block_expand_t(t),
                                  n_batch, t)
            o_rwkv, wkv = _rwkv_prompt(cols, rw_prep + [rk_row, gnw_row, gnb_row], n_batch, t)
            win = kwvw.reshape(n_batch, t, 2 * DKV)[:, t - wb:]
            kv_out = jnp.transpose(kv_t[0].reshape(n_batch, 4, KVH, HD, t), (0, 4, 1, 2, 3))
            shift = cols.reshape(n_batch, t, RWKV_PAD)[:, t - 1:, :RWKV_COLS]
        else:
            cache4 = jnp.transpose(cache_nsa[l], (0, 2, 3, 4, 1)).reshape(cache_nsa.shape[1], 4, DKV, page)
            win4 = jnp.transpose(state_win[l], (0, 2, 3, 4, 1)).reshape(n, 2, DKV, wb)
            wkv4 = jnp.transpose(state_wkv[l], (1, 2, 3, 0))
            prev_t = jnp.transpose(state_shift[l].reshape(n, RWKV_COLS))
            lanes = lambda a: jnp.broadcast_to(a.reshape(-1, 1), (a.size, n))
            params_t = [lanes(rw_mu[l]), lanes(rw_w0[l]), rw_w2[l].T.astype(BF16), lanes(rw_a0[l]),
                        rw_a2[l].T.astype(BF16), rw_g2[l].T.astype(BF16), lanes(rw_kk[l]), lanes(rw_ka[l])]
            cols_t, kv_t, kwvw_t, *vecs_t = _rwkv_sample_prep_t(cols, kv_rows, kwvw, prev_t, params_t)
            n_seg = past // CMP_STRIDE
            mcs = _cmp_to_sel(n_seg - 1, past // SEL_BLOCK + 1, n_seg)
            o_attn, win4_new = _nsa_sample_t(
                cache4, page_table, win4,
                q_raw.astype(F32).reshape(n, NH, HD), q_rot.astype(F32).reshape(n, NH, HD),
                gates[:, :3 * NH].reshape(n, NH, 3), kv_rows.reshape(n, 1, 4 * DKV), kwvw.reshape(n, 1, 2 * DKV),
                kwvw_t, cmp_k, cmp_v, mcs, _block_expand(past), SAMPLE_SEQS_PER_STEP)
            o_attn = o_attn.reshape(n, DH)
            o_rwkv_t, wkv4_new = _rwkv_sample_step_t(wkv4, vecs_t, lanes(rw_rk[l]), lanes(rw_gn_w[l]),
                                                     lanes(rw_gn_b[l]))
            kv_out = jnp.transpose(kv_t.reshape(4, KVH, HD, n), (3, 0, 1, 2))
            win = jnp.transpose(win4_new.reshape(n, 2, KVH, HD, wb), (0, 4, 1, 2, 3))
            wkv = jnp.transpose(wkv4_new, (3, 0, 1, 2))
            shift = jnp.transpose(cols_t)[:, None, :]
            h2 = _outproj_t(o_attn, o_rwkv_t, w_oa, w_or, h1, vec(norm_mix_post[l]))
        if is_prompt:
            h2 = _outproj(o_attn, o_rwkv, w_oa, w_or, h1, vec(norm_mix_post[l]), tm)
        y = _ffn_half(h2, vec(norm_f2_pre[l]), w2g, w2d, vec(norm_f2_post[l]), None, tm, 512)
        outs[name] = (y, kv_out, win, wkv, shift)

    yp, kvp, winp, wkvp, shp = outs["p"]
    ys, kvs, wins, wkvs, shs = outs["s"]
    return (yp.reshape(bp, t, d), ys.reshape(bs, 1, d),
            kvp.reshape(1, bp, t, 4, KVH, HD), kvs.reshape(1, bs, 1, 4, KVH, HD),
            winp.reshape(1, bp, wb, 2, KVH, HD), wins.reshape(1, bs, wb, 2, KVH, HD),
            wkvp[None], wkvs[None], shp[None], shs[None])
```

```python
import functools

import numpy as np
import jax
import jax.numpy as jnp
from jax import lax
from jax.experimental import pallas as pl
from jax.experimental.pallas import tpu as pltpu

F32 = jnp.float32
BF16 = jnp.bfloat16

HD = 64
KVH = 4
GQA = 4
NH = 16
DH = NH * HD
DKV = KVH * HD
ROPE_HALF = 8
ROPE_THETA = 500000.0
CMP_BLOCK = 32
CMP_STRIDE = 16
CMP_HIDDEN = 128
SEL_BLOCK = 64
SEL_TOPN = 16
N_LOCAL_FORCED = 2
WINDOW = 512
Q_BLOCK = 128
LORA_W = 64
LORA_A = 64
LORA_G = 160
RMS_EPS = 1e-6
GN_EPS = 64e-5
NEG = -1e30
FORCE_BONUS = 1e4
LOG2_E = 1.4426950408889634
MASK_BIAS = -1e30
M_FLOOR = -1e20
NSA_COLS = DH + 6 * DKV + 3 * NH
RWKV_COLS = 3 * DH + LORA_W + LORA_A + LORA_G
NSA_PAD = 2688
RWKV_PAD = 3456
LORA_G_PAD = 256
RW_GROUP = 4
RW_CHUNK = 64
SEL_KEY_CHUNK = 512
N_SEL_LANES = 64
SAMPLE_SEQS_PER_STEP = 2

VMEM_LIMIT_BYTES = 56 * 1024 * 1024


def _cparams(n_axes):
    return pltpu.CompilerParams(dimension_semantics=("arbitrary",) * n_axes,
                                vmem_limit_bytes=VMEM_LIMIT_BYTES)


def _dot(a, b):
    return jnp.dot(a, b, preferred_element_type=F32)


def _dot_nt(a, b):
    return lax.dot_general(a, b, (((1,), (1,)), ((), ())), preferred_element_type=F32)


def _dot_tn(a, b):
    return lax.dot_general(a, b, (((0,), (0,)), ((), ())), preferred_element_type=F32)


def _dot_split3(a, b_bf16):
    hi = a.astype(BF16)
    r1 = a - hi.astype(F32)
    mid = r1.astype(BF16)
    lo = (r1 - mid.astype(F32)).astype(BF16)
    return _dot(hi, b_bf16) + _dot(mid, b_bf16) + _dot(lo, b_bf16)


def _rms(x, g):
    ms = jnp.mean(x * x, axis=-1, keepdims=True)
    return x * lax.rsqrt(ms + RMS_EPS) * g


def _sigmoid(x):
    return jax.nn.sigmoid(x)


def _ffn_body(has_next, x_ref, gpre_ref, wg_ref, wu_ref, wd_ref, gpost_ref, *rest):
    if has_next:
        gnext_ref, o_ref, on_ref, xn_ref, acc_ref = rest
    else:
        o_ref, xn_ref, acc_ref = rest
    j = pl.program_id(1)

    @pl.when(j == 0)
    def _():
        xn_ref[...] = _rms(x_ref[...], gpre_ref[...]).astype(BF16)
        acc_ref[...] = jnp.zeros_like(acc_ref)

    xn = xn_ref[...]
    g = _dot(xn, wg_ref[...])
    u = _dot(xn, wu_ref[...])
    act = ((g * _sigmoid(g)) * u).astype(BF16)
    acc_ref[...] += _dot(act, wd_ref[...])

    @pl.when(j == pl.num_programs(1) - 1)
    def _():
        y = x_ref[...] + 0.5 * _rms(acc_ref[...], gpost_ref[...])
        o_ref[...] = y
        if has_next:
            on_ref[...] = _rms(y, gnext_ref[...]).astype(BF16)


def _ffn_half(x, g_pre, w_gu, w_dn, g_post, g_next, tm, tf):
    n, d = x.shape
    f = w_dn.shape[0]
    nj = f // tf
    has_next = g_next is not None
    row = lambda i, j: (i, 0)
    const = lambda i, j: (0, 0)
    in_specs = [
        pl.BlockSpec((tm, d), row),
        pl.BlockSpec((1, d), const),
        pl.BlockSpec((d, tf), lambda i, j: (0, j)),
        pl.BlockSpec((d, tf), lambda i, j: (0, j + nj)),
        pl.BlockSpec((tf, d), lambda i, j: (j, 0)),
        pl.BlockSpec((1, d), const),
    ]
    args = [x, g_pre, w_gu, w_gu, w_dn, g_post]
    out_shape = [jax.ShapeDtypeStruct((n, d), F32)]
    out_specs = [pl.BlockSpec((tm, d), row)]
    if has_next:
        in_specs.append(pl.BlockSpec((1, d), const))
        args.append(g_next)
        out_shape.append(jax.ShapeDtypeStruct((n, d), BF16))
        out_specs.append(pl.BlockSpec((tm, d), row))
    res = pl.pallas_call(
        functools.partial(_ffn_body, has_next),
        grid=(n // tm, nj),
        in_specs=in_specs,
        out_specs=out_specs,
        out_shape=out_shape,
        scratch_shapes=[pltpu.VMEM((tm, d), BF16), pltpu.VMEM((tm, d), F32)],
        compiler_params=_cparams(2),
        name="ffn_half",
    )(*args)
    return res if has_next else res[0]


def _rope(x, c, s_lo, s_hi):
    w = x.shape[1]
    reps = w // 128
    tile = lambda t: t if reps == 1 else jnp.concatenate([t] * reps, axis=1)
    up = pltpu.roll(x, w - ROPE_HALF, 1)
    dn = pltpu.roll(x, ROPE_HALF, 1)
    return x * tile(c) + up * tile(s_lo) + dn * tile(s_hi)


def _inproj_nsa_body(seq_len, hn_ref, w_ref, c_ref, slo_ref, shi_ref,
                     qraw_ref, qrot_ref, kv_ref, kwvw_ref, ks_ref, vs_ref, kw_ref, vw_ref, gates_ref, *kvt_ref):
    p = _dot(hn_ref[...], w_ref[...])
    tm = p.shape[0]
    c, s_lo, s_hi = c_ref[...], slo_ref[...], shi_ref[...]
    q = p[:, 0:DH] * (HD ** -0.5 * LOG2_E)
    qraw_ref[...] = q.astype(BF16)
    qrot_ref[...] = _rope(q, c, s_lo, s_hi).astype(BF16)
    o = DH
    ks = _rope(p[:, o + 2 * DKV:o + 3 * DKV], c, s_lo, s_hi)
    vs = p[:, o + 3 * DKV:o + 4 * DKV]
    kw = _rope(p[:, o + 4 * DKV:o + 5 * DKV], c, s_lo, s_hi)
    vw = p[:, o + 5 * DKV:o + 6 * DKV]
    kv_ref[:, 0:2 * DKV] = p[:, o:o + 2 * DKV]
    kv_ref[:, 2 * DKV:3 * DKV] = ks
    kv_ref[:, 3 * DKV:4 * DKV] = vs
    kwvw_ref[:, 0:DKV] = kw
    kwvw_ref[:, DKV:2 * DKV] = vw
    if kvt_ref:
        for j in range(4 * DKV // 128):
            kvt_ref[0][0, 128 * j:128 * (j + 1), :] = kv_ref[:, 128 * j:128 * (j + 1)].T
    vs_t, vw_t = vs.T, vw.T
    if seq_len is None:
        blk_onehot = jnp.zeros((tm, N_SEL_LANES), F32)
    else:
        pos = (pl.program_id(0) * tm + lax.broadcasted_iota(jnp.int32, (tm, 1), 0)) % seq_len
        blk_onehot = jnp.where(lax.broadcasted_iota(jnp.int32, (1, N_SEL_LANES), 1) == pos // SEL_BLOCK, 1.0, 0.0)
    for k in range(KVH):
        sl = slice(HD * k, HD * (k + 1))
        ks_ref[k] = jnp.concatenate([ks[:, sl], blk_onehot], axis=1).astype(BF16)
        vs_ref[k] = vs_t[sl, :].astype(BF16)
        kw_ref[k] = kw[:, sl].astype(BF16)
        vw_ref[k] = vw_t[sl, :].astype(BF16)
    gates_ref[...] = _sigmoid(p[:, o + 6 * DKV:o + 6 * DKV + 128])


def _inproj_nsa(hn, w_nsa, rope_c, rope_slo, rope_shi, tm, seq_len=None):
    n, d = hn.shape
    row = lambda i: (i, 0)
    hm = lambda i: (0, i, 0)
    tab_blocks = rope_c.shape[0] // tm
    tab = lambda i: (i % tab_blocks, 0)
    hm_shape = jax.ShapeDtypeStruct((KVH, n, HD), BF16)
    hm_spec = pl.BlockSpec((KVH, tm, HD), hm)
    hmt_shape = jax.ShapeDtypeStruct((KVH, HD, n), BF16)
    hmt_spec = pl.BlockSpec((KVH, HD, tm), lambda i: (0, 0, i))
    ksel_shape = jax.ShapeDtypeStruct((KVH, n, HD + N_SEL_LANES), BF16)
    ksel_spec = pl.BlockSpec((KVH, tm, HD + N_SEL_LANES), hm)
    out_specs = [pl.BlockSpec((tm, DH), row), pl.BlockSpec((tm, DH), row),
                 pl.BlockSpec((tm, 4 * DKV), row), pl.BlockSpec((tm, 2 * DKV), row),
                 ksel_spec, hmt_spec, hm_spec, hmt_spec,
                 pl.BlockSpec((tm, 128), row)]
    out_shape = [jax.ShapeDtypeStruct((n, DH), BF16), jax.ShapeDtypeStruct((n, DH), BF16),
                 jax.ShapeDtypeStruct((n, 4 * DKV), F32), jax.ShapeDtypeStruct((n, 2 * DKV), F32),
                 ksel_shape, hmt_shape, hm_shape, hmt_shape,
                 jax.ShapeDtypeStruct((n, 128), F32)]
    if seq_len is not None:
        per_seq = seq_len // tm
        out_specs.append(pl.BlockSpec((1, 4 * DKV, tm), lambda i: (i // per_seq, 0, i % per_seq)))
        out_shape.append(jax.ShapeDtypeStruct((n // seq_len, 4 * DKV, seq_len), F32))
    return pl.pallas_call(
        functools.partial(_inproj_nsa_body, seq_len),
        grid=(n // tm,),
        in_specs=[pl.BlockSpec((tm, d), row),
                  pl.BlockSpec((d, NSA_PAD), lambda i: (0, 0)),
                  pl.BlockSpec((tm, 128), tab), pl.BlockSpec((tm, 128), tab), pl.BlockSpec((tm, 128), tab)],
        out_specs=out_specs,
        out_shape=out_shape,
        compiler_params=_cparams(1),
        name="inproj_nsa",
    )(hn, w_nsa, rope_c, rope_slo, rope_shi)


def _matmul_body(x_ref, w_ref, o_ref):
    o_ref[...] = _dot(x_ref[...], w_ref[...])


def _inproj_rwkv(hn, w_rw, tm):
    n, d = hn.shape
    c = w_rw.shape[1]
    return pl.pallas_call(
        _matmul_body,
        grid=(n // tm,),
        in_specs=[pl.BlockSpec((tm, d), lambda i: (i, 0)), pl.BlockSpec((d, c), lambda i: (0, 0))],
        out_specs=pl.BlockSpec((tm, c), lambda i: (i, 0)),
        out_shape=jax.ShapeDtypeStruct((n, c), F32),
        compiler_params=_cparams(1),
        name="inproj_rwkv",
    )(hn, w_rw)


def _compress_accumulate(stage_ref, w1c_refs, n_seg):
    accs = [[jnp.zeros((n_seg, 2 * CMP_HIDDEN), F32) for _ in range(KVH)] for _ in range(2)]
    for s in range(CMP_STRIDE):
        for kv in range(2):
            w = w1c_refs[kv][s]
            for pair in range(KVH // 2):
                x = stage_ref[2 * kv + pair, pl.ds(s, n_seg, stride=CMP_STRIDE), :]
                for j in range(2):
                    k = 2 * pair + j
                    accs[kv][k] = accs[kv][k] + _dot(x[:, HD * j:HD * (j + 1)].astype(BF16), w)
    return accs


def _compress_finish(acc, pe0, w2, n_seg):
    a0 = acc[:, :CMP_HIDDEN]
    a1 = pltpu.roll(acc[:, CMP_HIDDEN:], n_seg - 1, 0)
    pre = pe0 + a0 + a1
    return _dot((pre * _sigmoid(pre)).astype(BF16), w2)


def _cmp_prompt_body(kv_ref, w1k_ref, w1v_ref, pek_ref, pev_ref, w1kf_ref, w1vf_ref, w2k_ref, w2v_ref,
                     kc_ref, vc_ref, stage_ref):
    n_seg = kv_ref.shape[0] // CMP_STRIDE
    for grp in range(4):
        stage_ref[grp] = kv_ref[:, 128 * grp:128 * (grp + 1)]
    accs = _compress_accumulate(stage_ref, (w1k_ref, w1v_ref), n_seg)
    for kv, (pe_ref, w1f_ref, w2_ref, out_ref) in enumerate(
            ((pek_ref, w1kf_ref, w2k_ref, kc_ref), (pev_ref, w1vf_ref, w2v_ref, vc_ref))):
        pe0 = _dot(pe_ref[...], w1f_ref[...])
        w2 = w2_ref[...]
        for k in range(KVH):
            c = _compress_finish(accs[kv][k], pe0, w2, n_seg)
            out_ref[0, k] = (c.T if kv == 1 else c).astype(BF16)


def _cmp_weights(pe, w1, w2):
    w1c = jnp.concatenate([w1[:CMP_STRIDE], w1[CMP_STRIDE:]], axis=-1).astype(BF16)
    return w1c, pe.reshape(1, CMP_BLOCK * HD), w1.reshape(CMP_BLOCK * HD, CMP_HIDDEN), w2.astype(BF16)


def _compress_prompt(kv_rows, cmp_k, cmp_v, n_batch, t):
    n_seg = t // CMP_STRIDE
    w1k, pek, w1kf, w2k = cmp_k
    w1v, pev, w1vf, w2v = cmp_v
    full = lambda a: pl.BlockSpec(a.shape, lambda b: (0,) * a.ndim)
    out_shape = jax.ShapeDtypeStruct((n_batch, KVH, n_seg, HD), BF16)
    out_spec = pl.BlockSpec((1, KVH, n_seg, HD), lambda b: (b, 0, 0, 0))
    out_shape_t = jax.ShapeDtypeStruct((n_batch, KVH, HD, n_seg), BF16)
    out_spec_t = pl.BlockSpec((1, KVH, HD, n_seg), lambda b: (b, 0, 0, 0))
    return pl.pallas_call(
        _cmp_prompt_body,
        grid=(n_batch,),
        in_specs=[pl.BlockSpec((t, 2 * DKV), lambda b: (b, 0)),
                  full(w1k), full(w1v), full(pek), full(pev), full(w1kf), full(w1vf), full(w2k), full(w2v)],
        out_specs=[out_spec, out_spec_t],
        out_shape=[out_shape, out_shape_t],
        scratch_shapes=[pltpu.VMEM((4, t, 128), F32)],
        compiler_params=_cparams(1),
        name="nsa_compress_prompt",
    )(kv_rows, w1k, w1v, pek, pev, w1kf, w1vf, w2k, w2v)


def _select_blocks(imp, cur, n_blocks):
    jb = lax.broadcasted_iota(jnp.int32, (1, N_SEL_LANES), 1)
    valid = (jb <= cur) & (jb < n_blocks)
    rel = cur - jb
    forced = (jb == 0) | ((rel >= 0) & (rel < N_LOCAL_FORCED))
    score = jnp.where(valid, imp + jnp.where(forced, FORCE_BONUS, 0.0), NEG)
    rank = jnp.zeros(score.shape, F32)
    for jp in range(min(n_blocks, N_SEL_LANES)):
        col = score[:, jp:jp + 1]
        tie = jnp.where(jb > jp, 1.0, 0.0)
        rank = rank + jnp.where(col > score, 1.0, jnp.where(col == score, tie, 0.0))
    return jnp.where(valid, jnp.where(rank < SEL_TOPN, 1.0, 0.0), 0.0)


def _softmax_rows(s, ok):
    sm = jnp.where(ok, s, NEG)
    m = jnp.max(sm, axis=-1, keepdims=True)
    e = jnp.where(ok, jnp.exp2(sm - m), 0.0)
    l = jnp.sum(e, axis=-1, keepdims=True)
    return e / jnp.maximum(l, 1e-30)


def _softmax_bias(s, bias):
    sb = s + bias
    m = jnp.maximum(jnp.max(sb, axis=-1, keepdims=True), M_FLOOR)
    e = jnp.exp(sb - m)
    return e, 1.0 / jnp.maximum(jnp.sum(e, axis=-1, keepdims=True), 1e-30)


def _select_blocks_t(imp_t, cur, n_blocks, n_live, score_ref):
    jb = lax.broadcasted_iota(jnp.int32, (N_SEL_LANES, 1), 0)
    valid = (jb <= cur) & (jb < n_blocks)
    rel = cur - jb
    forced = (jb == 0) | ((rel >= 0) & (rel < N_LOCAL_FORCED))
    score = jnp.where(valid, imp_t + jnp.where(forced, FORCE_BONUS, 0.0), NEG)
    score_ref[...] = score

    def body(jp, rank):
        other = score_ref[pl.ds(jp, 1), :]
        tie = jnp.where(jb > jp, 1.0, 0.0)
        return rank + jnp.where(other > score, 1.0, jnp.where(other == score, tie, 0.0))

    rank = lax.fori_loop(0, n_live, body, jnp.zeros(score.shape, F32))
    return jnp.where(valid, jnp.where(rank < SEL_TOPN, 1.0, 0.0), 0.0)


def _attn_prompt_body(qraw_ref, qrot_ref, gates_ref, kc_ref, vc_ref, ks_ref, vs_ref, kw_ref, vw_ref,
                      mcst_ref, e_ref, o_ref, score_ref):
    t = ks_ref.shape[1]
    n_seg = kc_ref.shape[2]
    i = pl.program_id(1)
    s0 = i * Q_BLOCK
    qpos = s0 + lax.broadcasted_iota(jnp.int32, (Q_BLOCK, 1), 0)
    qpos_row = s0 + lax.broadcasted_iota(jnp.int32, (1, Q_BLOCK), 1)
    n_idx = lax.broadcasted_iota(jnp.int32, (1, n_seg), 1)
    ok_cmp = (n_idx * CMP_STRIDE + (CMP_BLOCK - 1) <= qpos) & (n_idx < n_seg - 1)
    bias_cmp = jnp.where(ok_cmp, 0.0, MASK_BIAS)
    gates = gates_ref[...]
    mcst = mcst_ref[...]
    n_key_chunks = i // (SEL_KEY_CHUNK // Q_BLOCK) + 1
    n_live_blocks = (s0 + Q_BLOCK) // SEL_BLOCK
    win_start = pl.multiple_of(jnp.maximum(s0 - WINDOW, 0), Q_BLOCK)
    win_len = WINDOW + Q_BLOCK
    wpos = win_start + lax.broadcasted_iota(jnp.int32, (1, win_len), 1)
    dlt = qpos - wpos
    bias_win = jnp.where((dlt >= 0) & (dlt <= WINDOW), 0.0, MASK_BIAS)
    g_rows = [slice(g * Q_BLOCK, (g + 1) * Q_BLOCK) for g in range(GQA)]
    half = Q_BLOCK // 2

    kvhs = range(KVH)
    head_cols = lambda ref, k: jnp.concatenate(
        [ref[:, HD * (GQA * k + g):HD * (GQA * k + g + 1)] for g in range(GQA)], axis=0)
    qr = [head_cols(qraw_ref, k) for k in kvhs]
    qs = [head_cols(qrot_ref, k) for k in kvhs]

    s_cmp = [_dot_nt(qr[k], kc_ref[0, k]) for k in kvhs]
    o_cmp, sel_t = [], []
    for k in kvhs:
        p_heads = []
        for rows in g_rows:
            e_c, inv_c = _softmax_bias(s_cmp[k][rows], bias_cmp)
            p_heads.append(e_c * inv_c)
        o_cmp.append(_dot(jnp.concatenate([p.astype(BF16) for p in p_heads], axis=0), vc_ref[0, k]))
        p_sum = p_heads[0]
        for p in p_heads[1:]:
            p_sum = p_sum + p
        hi = p_sum.astype(BF16)
        r1 = p_sum - hi.astype(F32)
        mid = r1.astype(BF16)
        lo = (r1 - mid.astype(F32)).astype(BF16)
        imp_t = _dot_nt(mcst, hi) + _dot_nt(mcst, mid) + _dot_nt(mcst, lo)
        sel_t.append(_select_blocks_t(imp_t, qpos_row // SEL_BLOCK, t // SEL_BLOCK, n_live_blocks,
                                      score_ref.at[k]).astype(BF16))

    def chunk(c, carry):
        k0 = pl.multiple_of(c * SEL_KEY_CHUNK, SEL_KEY_CHUNK)
        kpos = k0 + lax.broadcasted_iota(jnp.int32, (1, SEL_KEY_CHUNK), 1)
        causal = kpos <= qpos
        e_blk = e_ref[:, pl.ds(k0, SEL_KEY_CHUNK)]
        s = [_dot_nt(qs[k], ks_ref[k, pl.ds(k0, SEL_KEY_CHUNK), :]) for k in kvhs]
        bias = [(jnp.where(causal, _dot_tn(sel_t[k], e_blk), 0.0) - 1.0) * (-MASK_BIAS) for k in kvhs]
        out = []
        for k in kvhs:
            m, l, acc = carry[k]
            m_new, p_b, p_tot = [], [], []
            for rows in g_rows:
                sb = s[k][rows] + bias[k]
                m_g = jnp.maximum(m[rows], jnp.max(sb, axis=-1, keepdims=True))
                p = jnp.exp(sb - m_g)
                m_new.append(m_g)
                p_tot.append(jnp.sum(p, axis=-1, keepdims=True))
                p_b.append(p.astype(BF16))
            m_new = jnp.concatenate(m_new, axis=0)
            alpha = jnp.exp(m - m_new)
            l = alpha * l + jnp.concatenate(p_tot, axis=0)
            acc = alpha * acc + _dot(jnp.concatenate(p_b, axis=0), vs_ref[k, pl.ds(k0, SEL_KEY_CHUNK), :])
            out.append((m_new, l, acc))
        return tuple(out)

    init = tuple((jnp.full((GQA * Q_BLOCK, 1), M_FLOOR, F32), jnp.zeros((GQA * Q_BLOCK, 1), F32),
                  jnp.zeros((GQA * Q_BLOCK, HD), F32)) for _ in kvhs)
    sel_out = lax.fori_loop(0, n_key_chunks, chunk, init)
    o_sel = [acc * (1.0 / jnp.maximum(l, 1e-30)) for _, l, acc in sel_out]

    s_win = [_dot_nt(qs[k], kw_ref[k, pl.ds(win_start, win_len), :]) for k in kvhs]
    for k in kvhs:
        e_b, inv_w = [], []
        for g in range(GQA):
            for hf in range(2):
                rows = slice(g * Q_BLOCK + hf * half, g * Q_BLOCK + (hf + 1) * half)
                e_w, inv = _softmax_bias(s_win[k][rows], bias_win[hf * half:(hf + 1) * half])
                e_b.append(e_w.astype(BF16))
                inv_w.append(inv)
        o_win = (_dot(jnp.concatenate(e_b, axis=0), vw_ref[k, pl.ds(win_start, win_len), :])
                 * jnp.concatenate(inv_w, axis=0))
        for g in range(GQA):
            h = GQA * k + g
            rows = g_rows[g]
            o_h = (gates[:, 3 * h:3 * h + 1] * o_cmp[k][rows] + gates[:, 3 * h + 1:3 * h + 2] * o_sel[k][rows]
                   + gates[:, 3 * h + 2:3 * h + 3] * o_win[rows])
            o_ref[:, HD * h:HD * (h + 1)] = o_h.astype(o_ref.dtype)


def _attn_prompt_t_body(qraw_ref, qrot_ref, gates_ref, kc_ref, vct_ref, ks_ref, vst_ref, kw_ref, vwt_ref,
                        mcst_ref, o_ref, score_ref):
    n_bat = qraw_ref.shape[0]
    t = ks_ref.shape[1] // n_bat
    n_seg = kc_ref.shape[2]
    i = pl.program_id(0)
    s0 = i * Q_BLOCK
    qpos = s0 + lax.broadcasted_iota(jnp.int32, (1, Q_BLOCK), 1)
    n_idx = lax.broadcasted_iota(jnp.int32, (n_seg, 1), 0)
    ok_cmp = (n_idx * CMP_STRIDE + (CMP_BLOCK - 1) <= qpos) & (n_idx < n_seg - 1)
    bias_cmp = jnp.where(ok_cmp, 0.0, MASK_BIAS)
    mcst = mcst_ref[...]
    n_full_chunks = i // (SEL_KEY_CHUNK // Q_BLOCK)
    n_live_blocks = (s0 + Q_BLOCK) // SEL_BLOCK
    win_start = pl.multiple_of(jnp.maximum(s0 - WINDOW, 0), Q_BLOCK)
    win_len = WINDOW + Q_BLOCK
    wpos = win_start + lax.broadcasted_iota(jnp.int32, (win_len, 1), 0)
    dlt = qpos - wpos
    bias_win = jnp.where((dlt >= 0) & (dlt <= WINDOW), 0.0, MASK_BIAS)
    g_cols = [slice(g * Q_BLOCK, (g + 1) * Q_BLOCK) for g in range(GQA)]
    units = [(bb, k) for bb in range(n_bat) for k in range(KVH)]
    n_units = len(units)
    head_rows = lambda ref, bb, k: jnp.concatenate(
        [ref[bb, :, HD * (GQA * k + g):HD * (GQA * k + g + 1)] for g in range(GQA)], axis=0)
    qr = [head_rows(qraw_ref, bb, k) for bb, k in units]
    qs = [head_rows(qrot_ref, bb, k) for bb, k in units]

    def softmax_cols(s_t, bias):
        e_b, inv = [], []
        for cols in g_cols:
            sb = s_t[:, cols] + bias
            m = jnp.maximum(jnp.max(sb, axis=0, keepdims=True), M_FLOOR)
            e = jnp.exp2(sb - m)
            e_b.append(e)
            inv.append(1.0 / jnp.maximum(jnp.sum(e, axis=0, keepdims=True), 1e-30))
        return e_b, inv

    o_cmp, q_sel = [], []
    cmp_scores = lambda u: _dot_nt(kc_ref[units[u][0], units[u][1]], qr[u])
    s_next = cmp_scores(0)
    for u, (bb, k) in enumerate(units):
        s_cmp = s_next
        if u + 1 < n_units:
            s_next = cmp_scores(u + 1)
        e_heads, inv_heads = softmax_cols(s_cmp, bias_cmp)
        p_heads = [e * inv for e, inv in zip(e_heads, inv_heads)]
        o_cmp.append(_dot(vct_ref[bb, k], jnp.concatenate([p.astype(BF16) for p in p_heads], axis=1)))
        p_sum = p_heads[0]
        for p in p_heads[1:]:
            p_sum = p_sum + p
        hi = p_sum.astype(BF16)
        r1 = p_sum - hi.astype(F32)
        mid = r1.astype(BF16)
        lo = (r1 - mid.astype(F32)).astype(BF16)
        imp_t = _dot(mcst, hi) + _dot(mcst, mid) + _dot(mcst, lo)
        sel_u = _select_blocks_t(imp_t, qpos // SEL_BLOCK, t // SEL_BLOCK, n_live_blocks, score_ref.at[u])
        sel_q = jnp.concatenate([sel_u, jnp.zeros_like(sel_u)], axis=0).T[:, 0:N_SEL_LANES]
        sel_bias = ((sel_q - 1.0) * (-MASK_BIAS)).astype(BF16)
        q_sel.append(jnp.concatenate([qs[u], jnp.concatenate([sel_bias] * GQA, axis=0)], axis=1))

    def chunk(c, carry, diagonal):
        k0 = pl.multiple_of(c * SEL_KEY_CHUNK, SEL_KEY_CHUNK)
        if diagonal:
            kpos = k0 + lax.broadcasted_iota(jnp.int32, (SEL_KEY_CHUNK, 1), 0)
            causal_bias = jnp.where(kpos <= qpos, 0.0, MASK_BIAS)

        def scores(u):
            bb, k = units[u]
            rows = pl.ds(pl.multiple_of(bb * t + k0, SEL_KEY_CHUNK), SEL_KEY_CHUNK)
            return _dot_nt(ks_ref[k, rows, :], q_sel[u])

        out = []
        nxt = scores(0)
        for u, (bb, k) in enumerate(units):
            s_k = nxt
            if u + 1 < n_units:
                nxt = scores(u + 1)
            m, l, acc = carry[u]
            m_new, p_b, p_tot = [], [], []
            for cols in g_cols:
                sb = s_k[:, cols] + causal_bias if diagonal else s_k[:, cols]
                m_g = jnp.maximum(m[:, cols], jnp.max(sb, axis=0, keepdims=True))
                p = jnp.exp2(sb - m_g)
                m_new.append(m_g)
                p_tot.append(jnp.sum(p, axis=0, keepdims=True))
                p_b.append(p.astype(BF16))
            m_new = jnp.concatenate(m_new, axis=1)
            alpha = jnp.exp2(m - m_new)
            l = alpha * l + jnp.concatenate(p_tot, axis=1)
            cols_v = pl.ds(pl.multiple_of(bb * t + k0, SEL_KEY_CHUNK), SEL_KEY_CHUNK)
            acc = alpha * acc + _dot(vst_ref[k, :, cols_v], jnp.concatenate(p_b, axis=1))
            out.append((m_new, l, acc))
        return tuple(out)

    init = tuple((jnp.full((1, GQA * Q_BLOCK), M_FLOOR, F32), jnp.zeros((1, GQA * Q_BLOCK), F32),
                  jnp.zeros((HD, GQA * Q_BLOCK), F32)) for _ in units)
    before = lax.fori_loop(0, n_full_chunks, lambda c, carry: chunk(c, carry, False), init)
    sel_out = chunk(n_full_chunks, before, True)
    o_sel = [acc * (1.0 / jnp.maximum(l, 1e-30)) for _, l, acc in sel_out]

    win_rows = lambda bb: pl.ds(pl.multiple_of(bb * t + win_start, Q_BLOCK), win_len)
    win_scores = lambda u: _dot_nt(kw_ref[units[u][1], win_rows(units[u][0]), :], qs[u])
    gates_all = [gates_ref[bb].T for bb in range(n_bat)]
    o_t = []
    s_next = win_scores(0)
    for u, (bb, k) in enumerate(units):
        gates_t = gates_all[bb]
        s_win = s_next
        if u + 1 < n_units:
            s_next = win_scores(u + 1)
        e_heads, inv_heads = softmax_cols(s_win, bias_win)
        o_win = (_dot(vwt_ref[k, :, win_rows(bb)],
                      jnp.concatenate([e.astype(BF16) for e in e_heads], axis=1))
                 * jnp.concatenate(inv_heads, axis=1))
        for g in range(GQA):
            h = GQA * k + g
            cols = g_cols[g]
            o_t.append(gates_t[3 * h:3 * h + 1] * o_cmp[u][:, cols] + gates_t[3 * h + 1:3 * h + 2] * o_sel[u][:, cols]
                       + gates_t[3 * h + 2:3 * h + 3] * o_win[:, cols])
    for bb in range(n_bat):
        for j in range(NH // 2):
            pair = jnp.concatenate([o_t[NH * bb + 2 * j], o_t[NH * bb + 2 * j + 1]], axis=0)
            o_ref[bb, :, 2 * HD * j:2 * HD * (j + 1)] = pair.T.astype(o_ref.dtype)


def _attn_prompt(q_raw, q_rot, gates, k_cmp, v_cmp, ks, vs, kw, vw, mcs_t, n_batch, t):
    nqb = t // Q_BLOCK
    qblock = lambda width: pl.BlockSpec((n_batch, Q_BLOCK, width), lambda i: (0, i, 0))
    full = lambda a: pl.BlockSpec(a.shape, lambda i: (0,) * a.ndim)
    seq3 = lambda a: a.reshape(n_batch, t, a.shape[1])
    args = [seq3(q_raw), seq3(q_rot), seq3(gates), k_cmp, v_cmp, ks, vs, kw, vw, mcs_t]
    o = pl.pallas_call(
        _attn_prompt_t_body,
        grid=(nqb,),
        in_specs=[qblock(DH), qblock(DH), qblock(128)] + [full(a) for a in args[3:]],
        out_specs=qblock(DH),
        out_shape=jax.ShapeDtypeStruct((n_batch, t, DH), F32),
        scratch_shapes=[pltpu.VMEM((n_batch * KVH, N_SEL_LANES, Q_BLOCK), F32)],
        compiler_params=_cparams(1),
        name="nsa_attn_prompt",
    )(*args)
    return o.reshape(n_batch * t, DH)


def _pick_kv_group(full):
    hk = lax.broadcasted_iota(jnp.int32, (NH, 1), 0) // GQA
    out = jnp.zeros((NH, HD), F32)
    for k in range(KVH):
        out = out + jnp.where(hk == k, full[:, HD * k:HD * (k + 1)], 0.0)
    return out


def _nsa_sample_body(n_pages, pt_ref, *refs):
    pages = refs[:n_pages]
    (win_ref, qraw_ref, qrot_ref, gates_ref, kvnew_ref, kwnew_ref,
     w1k_ref, w1v_ref, pek_ref, pev_ref, w1kf_ref, w1vf_ref, w2k_ref, w2v_ref, mcs_ref, e_ref,
     o_ref, winout_ref, stage_ref) = refs[n_pages:]
    page = pages[0].shape[1]
    past = n_pages * page
    n_seg = past // CMP_STRIDE

    def bdiag(q):
        qt = jnp.concatenate([q] * KVH, axis=1)
        hk = lax.broadcasted_iota(jnp.int32, (NH, DKV), 0) // GQA
        lk = lax.broadcasted_iota(jnp.int32, (NH, DKV), 1) // HD
        return jnp.where(hk == lk, qt, 0.0)

    qr = bdiag(qraw_ref[0])
    qs = bdiag(qrot_ref[0])
    qr_b, qs_b = qr.astype(BF16), qs.astype(BF16)

    for kk, pg in enumerate(pages):
        for grp in range(4):
            stage_ref[grp, page * kk:page * (kk + 1), :] = pg[0, :, 128 * grp:128 * (grp + 1)]
    accs = _compress_accumulate(stage_ref, (w1k_ref, w1v_ref), n_seg)
    cmp_kv = []
    for kv, (pe_ref, w1f_ref, w2_ref) in enumerate(((pek_ref, w1kf_ref, w2k_ref), (pev_ref, w1vf_ref, w2v_ref))):
        pe0 = _dot(pe_ref[...], w1f_ref[...])
        w2 = w2_ref[...]
        cmp_kv.append(jnp.concatenate(
            [_compress_finish(accs[kv][k], pe0, w2, n_seg) for k in range(KVH)], axis=1).astype(BF16))
    k_cmp, v_cmp = cmp_kv

    n_idx = lax.broadcasted_iota(jnp.int32, (1, n_seg), 1)
    ok_cmp = (n_idx * CMP_STRIDE + (CMP_BLOCK - 1) <= past) & (n_idx < n_seg - 1)
    p_cmp = _softmax_rows(_dot_nt(qr_b, k_cmp), ok_cmp)
    o_cmp = _pick_kv_group(_dot(p_cmp.astype(BF16), v_cmp))
    p_sum = jnp.concatenate(
        [jnp.sum(p_cmp[GQA * k:GQA * (k + 1)], axis=0, keepdims=True) for k in range(KVH)], axis=0)
    imp = _dot_split3(p_sum, mcs_ref[...])
    n_blocks = past // SEL_BLOCK + 1
    cur = jnp.full((KVH, 1), past // SEL_BLOCK, jnp.int32)
    sel = _select_blocks(imp, cur, n_blocks)
    sel16 = jnp.concatenate([jnp.broadcast_to(sel[k:k + 1], (GQA, N_SEL_LANES)) for k in range(KVH)], axis=0)

    kv_new = kvnew_ref[0]
    ks_new, vs_new = kv_new[:, 2 * DKV:3 * DKV], kv_new[:, 3 * DKV:4 * DKV]
    k_sel = jnp.concatenate([pg[0, :, 2 * DKV:3 * DKV] for pg in pages], axis=0).astype(BF16)
    v_sel = jnp.concatenate([pg[0, :, 3 * DKV:4 * DKV] for pg in pages], axis=0).astype(BF16)
    s_c = _dot_nt(qs_b, k_sel)
    s_n = jnp.sum(qs * ks_new, axis=-1, keepdims=True)
    ok_c = _dot(sel16.astype(BF16), e_ref[...]) > 0.5
    new_blk = past // SEL_BLOCK
    ok_n = sel16[:, new_blk:new_blk + 1] > 0.5
    sm_c = jnp.where(ok_c, s_c, NEG)
    sm_n = jnp.where(ok_n, s_n, NEG)
    m = jnp.maximum(jnp.max(sm_c, axis=-1, keepdims=True), sm_n)
    e_c = jnp.where(ok_c, jnp.exp2(sm_c - m), 0.0)
    e_n = jnp.where(ok_n, jnp.exp2(sm_n - m), 0.0)
    inv = 1.0 / jnp.maximum(jnp.sum(e_c, axis=-1, keepdims=True) + e_n, 1e-30)
    o_sel = _pick_kv_group(_dot((e_c * inv).astype(BF16), v_sel)
                           + (e_n * inv).astype(BF16).astype(F32) * vs_new.astype(BF16).astype(F32))

    kw_new = kwnew_ref[0]
    kwn, vwn = kw_new[:, 0:DKV], kw_new[:, DKV:2 * DKV]
    s_w = _dot_nt(qs_b, win_ref[0, :, 0:DKV].astype(BF16))
    s_wn = jnp.sum(qs * kwn, axis=-1, keepdims=True)
    mw = jnp.maximum(jnp.max(s_w, axis=-1, keepdims=True), s_wn)
    e_w = jnp.exp2(s_w - mw)
    e_wn = jnp.exp2(s_wn - mw)
    inv_w = 1.0 / (jnp.sum(e_w, axis=-1, keepdims=True) + e_wn)
    o_win = _pick_kv_group(_dot((e_w * inv_w).astype(BF16), win_ref[0, :, DKV:2 * DKV].astype(BF16))
                           + (e_wn * inv_w).astype(BF16).astype(F32) * vwn.astype(BF16).astype(F32))

    g = gates_ref[0]
    o_ref[0] = g[:, 0:1] * o_cmp + g[:, 1:2] * o_sel + g[:, 2:3] * o_win
    wb = win_ref.shape[1]
    winout_ref[0, 0:wb - 1, :] = win_ref[0, 1:wb, :]
    winout_ref[0, wb - 1:wb, :] = kw_new


def _nsa_sample(cache3, page_table, state_win, q_raw, q_rot, gates, kv_new, kw_new, cmp_k, cmp_v, mcs, e_mat):
    nb, n_pages = page_table.shape
    page = cache3.shape[1]
    wb = state_win.shape[1]
    w1k, pek, w1kf, w2k = cmp_k
    w1v, pev, w1vf, w2v = cmp_v
    consts = [w1k, w1v, pek, pev, w1kf, w1vf, w2k, w2v, mcs, e_mat]
    full = lambda a: pl.BlockSpec(a.shape, lambda b, pt: (0,) * a.ndim)
    page_specs = [pl.BlockSpec((1, page, cache3.shape[2]), functools.partial(
        lambda b, pt, kk: (pt[b * n_pages + kk], 0, 0), kk=kk)) for kk in range(n_pages)]
    per_b = lambda shape: pl.BlockSpec((1,) + shape, lambda b, pt: (b, 0, 0))
    grid_spec = pltpu.PrefetchScalarGridSpec(
        num_scalar_prefetch=1,
        grid=(nb,),
        in_specs=page_specs + [per_b((wb, 2 * DKV)), per_b((NH, HD)), per_b((NH, HD)), per_b((NH, 3)),
                               per_b((1, 4 * DKV)), per_b((1, 2 * DKV))] + [full(a) for a in consts],
        out_specs=[per_b((NH, HD)), per_b((wb, 2 * DKV))],
        scratch_shapes=[pltpu.VMEM((4, n_pages * page, 128), F32)],
    )
    return pl.pallas_call(
        functools.partial(_nsa_sample_body, n_pages),
        grid_spec=grid_spec,
        out_shape=[jax.ShapeDtypeStruct((nb, NH, HD), F32), jax.ShapeDtypeStruct((nb, wb, 2 * DKV), F32)],
        compiler_params=_cparams(1),
        name="nsa_sample",
    )(page_table.reshape(-1), *([cache3] * n_pages), state_win, q_raw, q_rot, gates, kv_new, kw_new, *consts)


def _kv_block_diag(q):
    qt = jnp.concatenate([q] * KVH, axis=1)
    hk = lax.broadcasted_iota(jnp.int32, (NH, DKV), 0) // GQA
    lk = lax.broadcasted_iota(jnp.int32, (NH, DKV), 1) // HD
    return jnp.where(hk == lk, qt, 0.0)


def _nsa_sample_t_body(n_pages, n_seq, pt_ref, *refs):
    all_pages = refs[:n_seq * n_pages]
    (win_ref, qraw_ref, qrot_ref, gates_ref, kvnew_ref, kwnew_ref, kwvwt_ref, perm_ref,
     w1k_ref, w1v_ref, pek_ref, pev_ref, w1kf_ref, w1vf_ref, w2k_ref, w2v_ref, mcs_ref, e_ref,
     o_ref, winout_ref, stage_ref) = refs[n_seq * n_pages:]
    page = all_pages[0].shape[3]
    n_seg = n_pages * page // CMP_STRIDE
    seg_pp = page // CMP_STRIDE

    perm = perm_ref[...]
    for kk, pg in enumerate(all_pages):
        for c in range(2):
            xs = _dot_nt(perm, pg[0, c].astype(BF16))
            for s in range(CMP_STRIDE):
                for k in range(KVH):
                    stage_ref[c, k, seg_pp * kk:seg_pp * (kk + 1), HD * s:HD * (s + 1)] = (
                        xs[seg_pp * s:seg_pp * (s + 1), HD * k:HD * (k + 1)])
    accs = [[_dot(stage_ref[kv, k].astype(BF16), w1_ref[...]) for k in range(KVH)]
            for kv, w1_ref in enumerate((w1k_ref, w1v_ref))]
    for q in range(n_seq):
        one = lambda ref: ref.at[pl.ds(q, 1)]
        _nsa_sample_one(
            all_pages[q * n_pages:(q + 1) * n_pages],
            [[a[n_seg * q:n_seg * (q + 1)] for a in row] for row in accs],
            pl.program_id(0) * n_seq + q,
            one(win_ref), one(qraw_ref), one(qrot_ref), one(gates_ref), one(kvnew_ref), one(kwnew_ref), kwvwt_ref,
            pek_ref, pev_ref, w1kf_ref, w1vf_ref, w2k_ref, w2v_ref, mcs_ref, e_ref, one(o_ref), one(winout_ref))


def _nsa_sample_one(pages, accs, b_idx, win_ref, qraw_ref, qrot_ref, gates_ref, kvnew_ref, kwnew_ref, kwvwt_ref,
                    pek_ref, pev_ref, w1kf_ref, w1vf_ref, w2k_ref, w2v_ref, mcs_ref, e_ref, o_ref, winout_ref):
    page = pages[0].shape[3]
    past = len(pages) * page
    n_seg = past // CMP_STRIDE
    qr = _kv_block_diag(qraw_ref[0])
    qs = _kv_block_diag(qrot_ref[0])
    qr_b, qs_b = qr.astype(BF16), qs.astype(BF16)
    cmp_kv = []
    for kv, (pe_ref, w1f_ref, w2_ref) in enumerate(((pek_ref, w1kf_ref, w2k_ref), (pev_ref, w1vf_ref, w2v_ref))):
        pe0 = _dot(pe_ref[...], w1f_ref[...])
        w2 = w2_ref[...]
        cmp_kv.append(jnp.concatenate(
            [_compress_finish(accs[kv][k], pe0, w2, n_seg) for k in range(KVH)], axis=1).astype(BF16))
    k_cmp, v_cmp = cmp_kv

    n_idx = lax.broadcasted_iota(jnp.int32, (1, n_seg), 1)
    ok_cmp = (n_idx * CMP_STRIDE + (CMP_BLOCK - 1) <= past) & (n_idx < n_seg - 1)
    p_cmp = _softmax_rows(_dot_nt(qr_b, k_cmp), ok_cmp)
    o_cmp = _pick_kv_group(_dot(p_cmp.astype(BF16), v_cmp))
    p_sum = jnp.concatenate(
        [jnp.sum(p_cmp[GQA * k:GQA * (k + 1)], axis=0, keepdims=True) for k in range(KVH)], axis=0)
    imp = _dot_split3(p_sum, mcs_ref[...])
    n_blocks = past // SEL_BLOCK + 1
    cur = jnp.full((KVH, 1), past // SEL_BLOCK, jnp.int32)
    sel = _select_blocks(imp, cur, n_blocks)
    sel16 = jnp.concatenate([jnp.broadcast_to(sel[k:k + 1], (GQA, N_SEL_LANES)) for k in range(KVH)], axis=0)

    kv_new = kvnew_ref[0]
    ks_new, vs_new = kv_new[:, 2 * DKV:3 * DKV], kv_new[:, 3 * DKV:4 * DKV]
    s_c = jnp.concatenate([_dot(qs_b, pg[0, 2].astype(BF16)) for pg in pages], axis=1)
    s_n = jnp.sum(qs * ks_new, axis=-1, keepdims=True)
    ok_c = _dot(sel16.astype(BF16), e_ref[...]) > 0.5
    new_blk = past // SEL_BLOCK
    ok_n = sel16[:, new_blk:new_blk + 1] > 0.5
    sm_c = jnp.where(ok_c, s_c, NEG)
    sm_n = jnp.where(ok_n, s_n, NEG)
    m = jnp.maximum(jnp.max(sm_c, axis=-1, keepdims=True), sm_n)
    e_c = jnp.where(ok_c, jnp.exp2(sm_c - m), 0.0)
    e_n = jnp.where(ok_n, jnp.exp2(sm_n - m), 0.0)
    inv = 1.0 / jnp.maximum(jnp.sum(e_c, axis=-1, keepdims=True) + e_n, 1e-30)
    p_c = (e_c * inv).astype(BF16)
    o_sel_full = (e_n * inv).astype(BF16).astype(F32) * vs_new.astype(BF16).astype(F32)
    for kk, pg in enumerate(pages):
        o_sel_full = o_sel_full + _dot_nt(p_c[:, page * kk:page * (kk + 1)], pg[0, 3].astype(BF16))
    o_sel = _pick_kv_group(o_sel_full)

    kw_new = kwnew_ref[0]
    kwn, vwn = kw_new[:, 0:DKV], kw_new[:, DKV:2 * DKV]
    s_w = _dot(qs_b, win_ref[0, 0].astype(BF16))
    s_wn = jnp.sum(qs * kwn, axis=-1, keepdims=True)
    mw = jnp.maximum(jnp.max(s_w, axis=-1, keepdims=True), s_wn)
    e_w = jnp.exp2(s_w - mw)
    e_wn = jnp.exp2(s_wn - mw)
    inv_w = 1.0 / (jnp.sum(e_w, axis=-1, keepdims=True) + e_wn)
    o_win = _pick_kv_group(_dot_nt((e_w * inv_w).astype(BF16), win_ref[0, 1].astype(BF16))
                           + (e_wn * inv_w).astype(BF16).astype(F32) * vwn.astype(BF16).astype(F32))

    g = gates_ref[0]
    o_ref[0] = g[:, 0:1] * o_cmp + g[:, 1:2] * o_sel + g[:, 2:3] * o_win

    wb = win_ref.shape[3]
    nb = kwvwt_ref.shape[1]
    mine = lax.broadcasted_iota(jnp.int32, (1, nb), 1) == b_idx
    last = lax.broadcasted_iota(jnp.int32, (1, wb), 1) == wb - 1
    for c in range(2):
        new_col = jnp.sum(jnp.where(mine, kwvwt_ref[DKV * c:DKV * (c + 1), :], 0.0), axis=-1, keepdims=True)
        winout_ref[0, c] = jnp.where(last, new_col, pltpu.roll(win_ref[0, c], wb - 1, 1))


def _nsa_sample_t(cache4, page_table, win4, q_raw, q_rot, gates, kv_new, kw_new, kwvw_t, cmp_k, cmp_v, mcs, e_mat,
                  n_seq):
    nb, n_pages = page_table.shape
    page = cache4.shape[3]
    wb = win4.shape[3]
    seg_pp = page // CMP_STRIDE
    rows = np.arange(page)
    perm = np.zeros((page, page), np.float32)
    perm[rows, (rows % seg_pp) * CMP_STRIDE + rows // seg_pp] = 1.0
    w1k, pek, w1kf, w2k = cmp_k
    w1v, pev, w1vf, w2v = cmp_v
    flat = lambda w: w.reshape(CMP_STRIDE * HD, 2 * CMP_HIDDEN)
    consts = [kwvw_t, jnp.asarray(perm, BF16), flat(w1k), flat(w1v), pek, pev, w1kf, w1vf, w2k, w2v, mcs, e_mat]
    full = lambda a: pl.BlockSpec(a.shape, lambda b, pt: (0,) * a.ndim)
    page_specs = [pl.BlockSpec((1, 4, DKV, page), functools.partial(
        lambda b, pt, kk: (pt[b * n_seq * n_pages + kk], 0, 0, 0), kk=kk)) for kk in range(n_seq * n_pages)]
    per_b = lambda shape: pl.BlockSpec((n_seq,) + shape, lambda b, pt: (b,) + (0,) * len(shape))
    grid_spec = pltpu.PrefetchScalarGridSpec(
        num_scalar_prefetch=1,
        grid=(nb // n_seq,),
        in_specs=page_specs + [per_b((2, DKV, wb)), per_b((NH, HD)), per_b((NH, HD)), per_b((NH, 3)),
                               per_b((1, 4 * DKV)), per_b((1, 2 * DKV))] + [full(a) for a in consts],
        out_specs=[per_b((NH, HD)), per_b((2, DKV, wb))],
        scratch_shapes=[pltpu.VMEM((2, KVH, n_seq * n_pages * seg_pp, CMP_STRIDE * HD), F32)],
    )
    return pl.pallas_call(
        functools.partial(_nsa_sample_t_body, n_pages, n_seq),
        grid_spec=grid_spec,
        out_shape=[jax.ShapeDtypeStruct((nb, NH, HD), F32), jax.ShapeDtypeStruct((nb, 2, DKV, wb), F32)],
        compiler_params=_cparams(1),
        name="nsa_sample",
    )(page_table.reshape(-1), *([cache4] * (n_seq * n_pages)), win4, q_raw, q_rot, gates, kv_new, kw_new, *consts)


def _rwkv_prep(cols, prev, mu, w0, w2, a0, a2, g2, kk_p, ka):
    xs = cols + (prev - cols) * mu
    r = xs[:, 0:DH]
    k = xs[:, DH:2 * DH]
    v = xs[:, 2 * DH:3 * DH]
    o = 3 * DH
    wd = xs[:, o:o + LORA_W]
    ad = xs[:, o + LORA_W:o + LORA_W + LORA_A]
    gd = xs[:, o + LORA_W + LORA_A:o + LORA_W + LORA_A + LORA_G_PAD]
    w = w0 + _dot(jnp.tanh(wd).astype(BF16), w2)
    w_log = -jax.nn.softplus(-w) - 0.5
    lw = -jnp.exp(w_log)
    a = _sigmoid(a0 + _dot(ad.astype(BF16), a2))
    g = _dot(_sigmoid(gd).astype(BF16), g2)
    kkv = k * kk_p
    k_mod = k * (1.0 + (a - 1.0) * ka)
    return r, k_mod, v, kkv, a, lw, g


def _head_norm(kkv_h):
    return kkv_h / jnp.maximum(jnp.sqrt(jnp.sum(kkv_h * kkv_h, axis=-1, keepdims=True)), 1e-12)


def _rwkv_head_out(y, r_h, k_h, v_h, g_h, rk_h, gnw_h, gnb_h):
    mean = jnp.mean(y, axis=-1, keepdims=True)
    var = jnp.mean(jnp.square(y - mean), axis=-1, keepdims=True)
    yn = (y - mean) * lax.rsqrt(var + GN_EPS) * gnw_h + gnb_h
    bonus = jnp.sum(r_h * k_h * rk_h, axis=-1, keepdims=True) * v_h
    return (yn + bonus) * g_h


def _head_ones():
    i = lax.broadcasted_iota(jnp.int32, (4 * HD, 4 * HD), 0) // HD
    j = lax.broadcasted_iota(jnp.int32, (4 * HD, 4 * HD), 1) // HD
    return jnp.where(i == j, 1.0, 0.0).astype(BF16)


def _head_sums(x, head_ones):
    rows, width = x.shape
    groups = width // (4 * HD)
    stacked = jnp.concatenate([x[:, 4 * HD * j:4 * HD * (j + 1)] for j in range(groups)], axis=0)
    sums = _dot_split3(stacked, head_ones)
    return jnp.concatenate([sums[rows * j:rows * (j + 1)] for j in range(groups)], axis=1)


def _block_diag(x):
    lane_head = lax.broadcasted_iota(jnp.int32, (1, x.shape[1]), 1) // HD
    return jnp.concatenate([jnp.where(lane_head == h, x, jnp.zeros_like(x)) for h in range(RW_GROUP)], axis=0)


def _tri_inverse_all(a_list, eye, blk16, blk32):
    b = lambda x: x.astype(BF16)
    bd = lambda x: _block_diag(b(x))
    a16 = [jnp.where(blk16, a, 0.0) for a in a_list]
    p = [_dot(b(x), bd(x)) for x in a16]
    t = [_dot(b(eye - x), bd(eye + q)) for x, q in zip(a16, p)]
    for _ in range(2):
        p = [_dot(b(q), bd(q)) for q in p]
        t = [_dot(b(x), bd(eye + q)) for x, q in zip(t, p)]
    off32 = blk32 & jnp.logical_not(blk16)
    for mask in (off32, jnp.logical_not(blk32)):
        m = [_dot(b(x), bd(jnp.where(mask, a, 0.0))) for x, a in zip(t, a_list)]
        t = [x - _dot(b(y), bd(x)) for x, y in zip(t, m)]
    return t


def _rwkv_prompt_body(cols_ref, mu_ref, w0_ref, w2_ref, a0_ref, a2_ref, g2_ref, kk_ref, ka_ref, rk_ref,
                      gnw_ref, gnb_ref, o_ref, wkv_ref, s_ref, last_ref):
    c = pl.program_id(0)
    ch = RW_CHUNK
    n_seq = cols_ref.shape[0]
    seq_rows = [slice(ch * q, ch * (q + 1)) for q in range(n_seq)]

    @pl.when(c == 0)
    def _():
        s_ref[...] = jnp.zeros_like(s_ref)
        last_ref[...] = jnp.zeros_like(last_ref)

    cols = jnp.concatenate([cols_ref[q] for q in range(n_seq)], axis=0)
    row = lax.broadcasted_iota(jnp.int32, (ch, 1), 0)
    prev = jnp.concatenate([jnp.where(row == 0, last_ref[q], pltpu.roll(cols[seq_rows[q]], 1, 0))
                            for q in range(n_seq)], axis=0)
    for q in range(n_seq):
        last_ref[q] = cols[ch * (q + 1) - 1:ch * (q + 1), :]
    r, k_mod, v, kkv, a, lw, g = _rwkv_prep(cols, prev, mu_ref[...], w0_ref[...], w2_ref[...], a0_ref[...],
                                            a2_ref[...], g2_ref[...], kk_ref[...], ka_ref[...])
    ti = lax.broadcasted_iota(jnp.int32, (n_seq * ch, n_seq * ch), 0)
    si = lax.broadcasted_iota(jnp.int32, (n_seq * ch, n_seq * ch), 1)
    ltri = jnp.where((ti >= si) & (ti // ch == si // ch), 1.0, 0.0).astype(BF16)
    cl = _cumsum_rows(lw, ltri)
    e_in = jnp.exp(cl)
    e_ex = jnp.exp(cl - lw)
    e_ng = jnp.exp(-cl)
    cl_last = [cl[ch * (q + 1) - 1:ch * (q + 1), :] for q in range(n_seq)]
    e_end = jnp.exp(jnp.concatenate([jnp.broadcast_to(x, (ch, DH)) for x in cl_last], axis=0) - cl)
    g_end = [jnp.exp(x) for x in cl_last]
    rk, gnw, gnb = rk_ref[...], gnw_ref[...], gnb_ref[...]
    b = lambda x: x.astype(BF16)
    gw = RW_GROUP * HD
    n_grp = NH // RW_GROUP
    slabs = [(seq_rows[q], slice(gw * j, gw * (j + 1)), q) for q in range(n_seq) for j in range(n_grp)]

    t_row = lax.broadcasted_iota(jnp.int32, (ch, gw), 0)
    s_col = lax.broadcasted_iota(jnp.int32, (ch, gw), 1) % HD
    strict = t_row > s_col
    causal = t_row >= s_col
    eye = jnp.where(t_row == s_col, 1.0, 0.0)
    blk16 = (t_row // 16) == (s_col // 16)
    blk32 = (t_row // 32) == (s_col // 32)
    head_ones = _head_ones()
    kk_n = kkv * jnp.minimum(lax.rsqrt(_head_sums(kkv * kkv, head_ones)), 1e12)
    beta = kk_n * a
    kq, rq = b(kk_n * e_ex), b(r * e_in)
    bd, kd = b(beta * e_ng), b(k_mod * e_ng)
    bdec, kdec = b(beta * e_end), b(k_mod * e_end)
    v_b = b(v)
    ns = range(len(slabs))
    lhs = [jnp.concatenate([kq[rs, gs], rq[rs, gs]], axis=0) for rs, gs, _ in slabs]
    s0 = [s_ref[i] for i in ns]
    quad = [_dot_nt(lhs[i], jnp.concatenate([_block_diag(bd[rs, gs]), _block_diag(kd[rs, gs])], axis=0))
            for i, (rs, gs, _) in enumerate(slabs)]
    s0t = [_dot_nt(lhs[i], _block_diag(b(s0[i]))) for i in ns]
    a_b = [jnp.where(strict, q[0:ch, 0:gw], 0.0) for q in quad]
    a_kv = [_dot(b(jnp.where(strict, quad[i][0:ch, gw:2 * gw], 0.0)), _block_diag(v_b[rs, gs]))
            for i, (rs, gs, _) in enumerate(slabs)]
    t_inv = _tri_inverse_all(a_b, eye, blk16, blk32)
    u = [-_dot(b(t_inv[i]), _block_diag(b(s0t[i][0:ch] + a_kv[i]))) for i in ns]
    y = [s0t[i][ch:2 * ch]
         + _dot(b(jnp.where(causal, quad[i][ch:2 * ch, 0:gw], 0.0)), _block_diag(b(u[i])))
         + _dot(b(jnp.where(causal, quad[i][ch:2 * ch, gw:2 * gw], 0.0)), _block_diag(v_b[rs, gs]))
         for i, (rs, gs, _) in enumerate(slabs)]
    lane_head = lax.broadcasted_iota(jnp.int32, (1, gw), 1) // HD
    for i, (rs, gs, q) in enumerate(slabs):
        cross = _dot_tn(jnp.concatenate([b(u[i]), v_b[rs, gs]], axis=0),
                        jnp.concatenate([bdec[rs, gs], kdec[rs, gs]], axis=0))
        s_add = jnp.zeros((HD, gw), F32)
        for h in range(RW_GROUP):
            s_add = s_add + jnp.where(lane_head == h, cross[HD * h:HD * (h + 1)], 0.0)
        s_ref[i] = s0[i] * g_end[q][:, gs] + s_add
    y_all = jnp.concatenate([jnp.concatenate(y[n_grp * q:n_grp * (q + 1)], axis=1) for q in range(n_seq)],
                            axis=0)
    yc = y_all - _head_sums(y_all, head_ones) * (1.0 / HD)
    var = _head_sums(yc * yc, head_ones) * (1.0 / HD)
    yn = yc * lax.rsqrt(var + GN_EPS) * gnw + gnb
    bonus = _head_sums(r * k_mod * rk, head_ones) * v
    out = ((yn + bonus) * g).astype(o_ref.dtype)
    for q in range(n_seq):
        o_ref[q] = out[seq_rows[q]]

    @pl.when(c == pl.num_programs(0) - 1)
    def _():
        for q in range(n_seq):
            for h in range(NH):
                lanes = slice(HD * (h % RW_GROUP), HD * (h % RW_GROUP + 1))
                wkv_ref[q, h] = s_ref[n_grp * q + h // RW_GROUP][:, lanes]


def _cumsum_rows(x, ltri):
    hi = x.astype(BF16)
    r1 = x - hi.astype(F32)
    mid = r1.astype(BF16)
    lo = (r1 - mid.astype(F32)).astype(BF16)
    return _dot(ltri, hi) + _dot(ltri, mid) + _dot(ltri, lo)


def _rwkv_prompt(cols, rw, n_batch, t):
    nc = t // RW_CHUNK
    full = lambda a: pl.BlockSpec(a.shape, lambda c: (0,) * a.ndim)
    chunk = lambda c: (0, c, 0)
    o, wkv = pl.pallas_call(
        _rwkv_prompt_body,
        grid=(nc,),
        in_specs=[pl.BlockSpec((n_batch, RW_CHUNK, RWKV_PAD), chunk)] + [full(a) for a in rw],
        out_specs=[pl.BlockSpec((n_batch, RW_CHUNK, DH), chunk),
                   pl.BlockSpec((n_batch, NH, HD, HD), lambda c: (0, 0, 0, 0))],
        out_shape=[jax.ShapeDtypeStruct((n_batch, t, DH), F32),
                   jax.ShapeDtypeStruct((n_batch, NH, HD, HD), F32)],
        scratch_shapes=[pltpu.VMEM((n_batch * NH // RW_GROUP, HD, RW_GROUP * HD), F32),
                        pltpu.VMEM((n_batch, 1, RWKV_PAD), F32)],
        compiler_params=_cparams(1),
        name="rwkv_prompt",
    )(cols.reshape(n_batch, t, RWKV_PAD), *rw)
    return o.reshape(n_batch * t, DH), wkv


def _rwkv_sample_prep_body(cols_ref, prev_ref, mu_ref, w0_ref, w2_ref, a0_ref, a2_ref, g2_ref, kk_ref, ka_ref,
                           r_ref, k_ref, v_ref, kkn_ref, a_ref, d_ref, g_ref):
    r, k_mod, v, kkv, a, lw, g = _rwkv_prep(cols_ref[...], prev_ref[...], mu_ref[...], w0_ref[...], w2_ref[...],
                                            a0_ref[...], a2_ref[...], g2_ref[...], kk_ref[...], ka_ref[...])
    r_ref[...] = r
    k_ref[...] = k_mod
    v_ref[...] = v
    a_ref[...] = a
    d_ref[...] = jnp.exp(lw)
    g_ref[...] = g
    for h in range(NH):
        sl = slice(HD * h, HD * (h + 1))
        kkn_ref[:, sl] = _head_norm(kkv[:, sl])


def _rwkv_sample_prep(cols, prev, rw_prep):
    n = cols.shape[0]
    args = [cols, prev] + list(rw_prep)
    full = lambda a: pl.BlockSpec(a.shape, lambda i: (0,) * a.ndim)
    out = jax.ShapeDtypeStruct((n, DH), F32)
    return pl.pallas_call(
        _rwkv_sample_prep_body,
        grid=(1,),
        in_specs=[full(a) for a in args],
        out_specs=[pl.BlockSpec((n, DH), lambda i: (0, 0))] * 7,
        out_shape=[out] * 7,
        compiler_params=_cparams(1),
        name="rwkv_sample_prep",
    )(*args)


def _rwkv_sample_step_body(s_ref, r_ref, k_ref, v_ref, kk_ref, a_ref, d_ref, g_ref, rk_ref, gnw_ref, gnb_ref,
                           o_ref, sout_ref):
    s = s_ref[0]
    r, k, v, kk, a, d, g = (x[0] for x in (r_ref, k_ref, v_ref, kk_ref, a_ref, d_ref, g_ref))
    eye = jnp.where(lax.broadcasted_iota(jnp.int32, (HD, HD), 0) == lax.broadcasted_iota(jnp.int32, (HD, HD), 1),
                    1.0, 0.0)
    row = lambda x: x[:, None, :]
    col = lambda x: jnp.sum(eye[None] * x[:, None, :], axis=-1, keepdims=True)
    sa = -jnp.sum(s * row(kk), axis=-1, keepdims=True)
    s_new = s * row(d) + sa * row(kk * a) + col(v) * row(k)
    sout_ref[0] = s_new
    y_col = jnp.sum(s_new * row(r), axis=-1, keepdims=True)
    y = jnp.sum(y_col * eye[None], axis=1)
    o_ref[0] = _rwkv_head_out(y, r, k, v, g, rk_ref[...], gnw_ref[...], gnb_ref[...])


def _rwkv_sample_step(state, vecs, rk, gnw, gnb):
    nb = state.shape[0]
    per_b3 = pl.BlockSpec((1, NH, HD), lambda b: (b, 0, 0))
    per_b4 = pl.BlockSpec((1, NH, HD, HD), lambda b: (b, 0, 0, 0))
    full = pl.BlockSpec((NH, HD), lambda b: (0, 0))
    return pl.pallas_call(
        _rwkv_sample_step_body,
        grid=(nb,),
        in_specs=[per_b4] + [per_b3] * 7 + [full] * 3,
        out_specs=[per_b3, per_b4],
        out_shape=[jax.ShapeDtypeStruct((nb, NH, HD), F32), jax.ShapeDtypeStruct((nb, NH, HD, HD), F32)],
        compiler_params=_cparams(1),
        name="rwkv_sample_step",
    )(state, *vecs, rk, gnw, gnb)


def _rwkv_sample_prep_t_body(cols_ref, kv_ref, kwvw_ref, prev_ref, mu_ref, w0_ref, w2t_ref, a0_ref, a2t_ref,
                             g2t_ref, kk_ref, ka_ref,
                             colst_ref, kvt_ref, kwvwt_ref, r_ref, k_ref, v_ref, kkn_ref, kka_ref, d_ref, g_ref,
                             tr_ref):
    nb = cols_ref.shape[0]
    for j in range(cols_ref.shape[1] // 128):
        tr_ref[128 * j:128 * (j + 1), :] = cols_ref[:, 128 * j:128 * (j + 1)].T
    for j in range(kv_ref.shape[1] // 128):
        kvt_ref[128 * j:128 * (j + 1), :] = kv_ref[:, 128 * j:128 * (j + 1)].T
    for j in range(kwvw_ref.shape[1] // 128):
        kwvwt_ref[128 * j:128 * (j + 1), :] = kwvw_ref[:, 128 * j:128 * (j + 1)].T
    cols = tr_ref[0:RWKV_COLS, :]
    colst_ref[...] = cols
    xs = cols + (prev_ref[...] - cols) * mu_ref[...]
    r = xs[0:DH]
    k = xs[DH:2 * DH]
    v = xs[2 * DH:3 * DH]
    o = 3 * DH
    wd = xs[o:o + LORA_W]
    ad = xs[o + LORA_W:o + LORA_W + LORA_A]
    gd = xs[o + LORA_W + LORA_A:RWKV_COLS]
    w = w0_ref[...] + _dot(w2t_ref[...], jnp.tanh(wd).astype(BF16))
    w_log = -jax.nn.softplus(-w) - 0.5
    a = _sigmoid(a0_ref[...] + _dot(a2t_ref[...], ad.astype(BF16)))
    kkv = (k * kk_ref[...]).reshape(NH, HD, nb)
    norm = jnp.maximum(jnp.sqrt(jnp.sum(kkv * kkv, axis=1, keepdims=True)), 1e-12)
    kkn = (kkv / norm).reshape(DH, nb)
    r_ref[...] = r
    k_ref[...] = k * (1.0 + (a - 1.0) * ka_ref[...])
    v_ref[...] = v
    kkn_ref[...] = kkn
    kka_ref[...] = kkn * a
    d_ref[...] = jnp.exp(-jnp.exp(w_log))
    g_ref[...] = _dot(g2t_ref[...], _sigmoid(gd).astype(BF16))


def _rwkv_sample_prep_t(cols, kv_rows, kwvw, prev_t, params_t):
    nb = cols.shape[0]
    args = [cols, kv_rows, kwvw, prev_t] + list(params_t)
    full = lambda shape: pl.BlockSpec(shape, lambda i: (0,) * len(shape))
    out_rows = [RWKV_COLS, kv_rows.shape[1], kwvw.shape[1]] + [DH] * 7
    return pl.pallas_call(
        _rwkv_sample_prep_t_body,
        grid=(1,),
        in_specs=[full(a.shape) for a in args],
        out_specs=[full((rows, nb)) for rows in out_rows],
        out_shape=[jax.ShapeDtypeStruct((rows, nb), F32) for rows in out_rows],
        scratch_shapes=[pltpu.VMEM((cols.shape[1], nb), F32)],
        compiler_params=_cparams(1),
        name="rwkv_sample_prep",
    )(*args)


def _rwkv_sample_step_t_body(s_ref, r_ref, k_ref, v_ref, kkn_ref, kka_ref, d_ref, g_ref, rk_ref, gnw_ref, gnb_ref,
                             o_ref, sout_ref, y_ref):
    r, k, kkn, kka, d = r_ref[...], k_ref[...], kkn_ref[...], kka_ref[...], d_ref[...]

    def value_row(i, carry):
        s_i = s_ref[0, i]
        sa = -jnp.sum(s_i * kkn, axis=0, keepdims=True)
        s_new = s_i * d + sa * kka + v_ref[pl.ds(i, 1), :] * k
        sout_ref[0, i] = s_new
        y_ref[pl.ds(i, 1), :] = jnp.sum(s_new * r, axis=0, keepdims=True)
        return carry

    lax.fori_loop(0, HD, value_row, 0)
    y = y_ref[...]
    v = v_ref[...]
    mean = jnp.mean(y, axis=0, keepdims=True)
    var = jnp.mean(jnp.square(y - mean), axis=0, keepdims=True)
    yn = (y - mean) * lax.rsqrt(var + GN_EPS) * gnw_ref[...] + gnb_ref[...]
    bonus = jnp.sum(r * k * rk_ref[...], axis=0, keepdims=True) * v
    o_ref[...] = (yn + bonus) * g_ref[...]


def _rwkv_sample_step_t(state4, vecs_t, rk_t, gnw_t, gnb_t):
    nh, hs, _, nb = state4.shape
    head_rows = pl.BlockSpec((hs, nb), lambda h: (h, 0))
    state_spec = pl.BlockSpec((1, hs, hs, nb), lambda h: (h, 0, 0, 0))
    return pl.pallas_call(
        _rwkv_sample_step_t_body,
        grid=(nh,),
        in_specs=[state_spec] + [head_rows] * 10,
        out_specs=[head_rows, state_spec],
        out_shape=[jax.ShapeDtypeStruct((nh * hs, nb), F32), jax.ShapeDtypeStruct(state4.shape, F32)],
        scratch_shapes=[pltpu.VMEM((hs, nb), F32)],
        compiler_params=_cparams(1),
        name="rwkv_sample_step",
    )(state4, *vecs_t, rk_t, gnw_t, gnb_t)


def _outproj_t_body(oa_ref, orwt_ref, wa_ref, wr_ref, h_ref, g_ref, o_ref):
    mixed = _dot(oa_ref[...].astype(BF16), wa_ref[...]) + _dot_tn(orwt_ref[...].astype(BF16), wr_ref[...])
    o_ref[...] = h_ref[...] + _rms(mixed, g_ref[...])


def _outproj_t(o_attn, o_rwkv_t, w_a, w_r, h, g_post):
    n, d = h.shape
    full = lambda a: pl.BlockSpec(a.shape, lambda i: (0,) * a.ndim)
    args = [o_attn, o_rwkv_t, w_a, w_r, h, g_post]
    return pl.pallas_call(
        _outproj_t_body,
        grid=(1,),
        in_specs=[full(a) for a in args],
        out_specs=pl.BlockSpec((n, d), lambda i: (0, 0)),
        out_shape=jax.ShapeDtypeStruct((n, d), F32),
        compiler_params=_cparams(1),
        name="outproj_sample",
    )(*args)


def _outproj_body(oa_ref, orw_ref, wa_ref, wr_ref, h_ref, g_ref, o_ref):
    mixed = _dot(oa_ref[...].astype(BF16), wa_ref[...]) + _dot(orw_ref[...].astype(BF16), wr_ref[...])
    o_ref[...] = h_ref[...] + _rms(mixed, g_ref[...])


def _outproj(o_attn, o_rwkv, w_a, w_r, h, g_post, tm):
    n, d = h.shape
    row = lambda i: (i, 0)
    const = lambda i: (0, 0)
    return pl.pallas_call(
        _outproj_body,
        grid=(n // tm,),
        in_specs=[pl.BlockSpec((tm, DH), row), pl.BlockSpec((tm, DH), row),
                  pl.BlockSpec((DH, d), const), pl.BlockSpec((DH, d), const),
                  pl.BlockSpec((tm, d), row), pl.BlockSpec((1, d), const)],
        out_specs=pl.BlockSpec((tm, d), row),
        out_shape=jax.ShapeDtypeStruct((n, d), F32),
        compiler_params=_cparams(1),
        name="outproj",
    )(o_attn, o_rwkv, w_a, w_r, h, g_post)


def _rope_tables(pos):
    inv = ROPE_THETA ** (-jnp.arange(ROPE_HALF, dtype=F32) / ROPE_HALF)
    ang = pos.astype(F32)[:, None] * inv[None, :]
    cos, sin = jnp.cos(ang), jnp.sin(ang)
    n = pos.shape[0]
    rest = HD - 2 * ROPE_HALF
    c = jnp.concatenate([cos, cos, jnp.ones((n, rest), F32)], axis=1)
    s_lo = jnp.concatenate([-sin, jnp.zeros((n, HD - ROPE_HALF), F32)], axis=1)
    s_hi = jnp.concatenate([jnp.zeros((n, ROPE_HALF), F32), sin, jnp.zeros((n, rest), F32)], axis=1)
    two = lambda x: jnp.concatenate([x, x], axis=1)
    return two(c), two(s_lo), two(s_hi)


def _cmp_to_sel(n_cmp, n_sel, rows):
    i = np.arange(n_cmp)[:, None] * CMP_STRIDE
    j = np.arange(n_sel)[None, :] * SEL_BLOCK
    ov = np.minimum(i + CMP_BLOCK, j + SEL_BLOCK) - np.maximum(i, j)
    m = np.zeros((rows, N_SEL_LANES), np.float32)
    m[:n_cmp, :n_sel] = np.maximum(ov, 0) // CMP_STRIDE
    return jnp.asarray(m, BF16)


def _block_expand(n_keys):
    j = np.arange(N_SEL_LANES)[:, None]
    t = np.arange(n_keys)[None, :]
    return jnp.asarray((t // SEL_BLOCK == j).astype(np.float32), BF16)


def _block_expand_t(n_keys):
    t = np.arange(n_keys)[:, None]
    j = np.arange(N_SEL_LANES)[None, :]
    return jnp.asarray((t // SEL_BLOCK == j).astype(np.float32), BF16)


def _pad_cols(x, width):
    return jnp.pad(x, ((0, 0), (0, width - x.shape[1])))


def _row_tile(n, cap):
    tm = min(n, cap)
    while n % tm:
        tm //= 2
    return tm


def kernel(x_prompt, x_sample, cache_nsa, page_table, state_win, state_wkv, state_shift, norm_f1_pre, norm_f1_post, ffn1_gu, ffn1_dn, norm_mix_pre, w_in, cmp_pe_k, cmp_w1_k, cmp_w2_k, cmp_pe_v, cmp_w1_v, cmp_w2_v, rw_mu, rw_w0, rw_w2, rw_a0, rw_a2, rw_g2, rw_kk, rw_ka, rw_rk, rw_gn_w, rw_gn_b, w_out, norm_mix_post, norm_f2_pre, ffn2_gu, ffn2_dn, norm_f2_post):
    depth = w_in.shape[0]
    assert depth == 1, "single-layer step"
    bp, t, d = x_prompt.shape
    bs, ts, _ = x_sample.shape
    assert ts == 1 and t % SEL_KEY_CHUNK == 0 and t % RW_CHUNK == 0 and WINDOW + Q_BLOCK <= t <= SEL_BLOCK * N_SEL_LANES
    n_pages = page_table.shape[1]
    page = cache_nsa.shape[2]
    past = n_pages * page
    wb = state_win.shape[2]
    assert wb == min(WINDOW, past) and past % SEL_BLOCK == 0 and page % CMP_STRIDE == 0
    l = 0

    w1g, w1d = ffn1_gu[l].astype(BF16), ffn1_dn[l].astype(BF16)
    w2g, w2d = ffn2_gu[l].astype(BF16), ffn2_dn[l].astype(BF16)
    wi = w_in[l]
    w_nsa = _pad_cols(wi[:, :NSA_COLS], NSA_PAD).astype(BF16)
    w_rw = _pad_cols(wi[:, NSA_COLS:], RWKV_PAD).astype(BF16)
    w_oa, w_or = w_out[l, :DH].astype(BF16), w_out[l, DH:].astype(BF16)
    vec = lambda a: a.reshape(1, -1)
    cmp_k = _cmp_weights(cmp_pe_k[l], cmp_w1_k[l], cmp_w2_k[l])
    cmp_v = _cmp_weights(cmp_pe_v[l], cmp_w1_v[l], cmp_w2_v[l])
    g2_pad = jnp.pad(rw_g2[l], ((0, LORA_G_PAD - LORA_G), (0, 0))).astype(BF16)
    rw_prep = [_pad_cols(vec(rw_mu[l]), RWKV_PAD), vec(rw_w0[l]), rw_w2[l].astype(BF16), vec(rw_a0[l]),
               rw_a2[l].astype(BF16), g2_pad, vec(rw_kk[l]), vec(rw_ka[l])]
    rk_row, gnw_row, gnb_row = vec(rw_rk[l]), vec(rw_gn_w[l]), vec(rw_gn_b[l])

    outs = {}
    for name, x2, n_batch in (("p", x_prompt.reshape(bp * t, d), bp), ("s", x_sample.reshape(bs, d), bs)):
        n = x2.shape[0]
        tm = _row_tile(n, 512)
        is_prompt = name == "p"
        h1, hn = _ffn_half(x2, vec(norm_f1_pre[l]), w1g, w1d, vec(norm_f1_post[l]), vec(norm_mix_pre[l]),
                           tm, 512)
        pos = jnp.arange(t) if is_prompt else jnp.full((n,), past, jnp.int32)
        rc, rlo, rhi = _rope_tables(pos)
        q_raw, q_rot, kv_rows, kwvw, ks, vs, kw, vw, gates, *kv_t = _inproj_nsa(
            hn, w_nsa, rc, rlo, rhi, tm, t if is_prompt else None)
        cols = _inproj_rwkv(hn, w_rw, tm)
        if is_prompt:
            k_cmp, v_cmp = _compress_prompt(kv_rows, cmp_k, cmp_v, n_batch, t)
            n_seg = t // CMP_STRIDE
            mcs_t = _cmp_to_sel(n_seg - 1, t // SEL_BLOCK, n_seg).T
            o_attn = _attn_prompt(q_raw, q_rot, gates, k_cmp, v_cmp, ks, vs, kw, vw, mcs_t, n_batch, t)
            o_rwkv, wkv = _rwkv_prompt(cols, rw_prep + [rk_row, gnw_row, gnb_row], n_batch, t)
            win = kwvw.reshape(n_batch, t, 2 * DKV)[:, t - wb:]
            kv_out = jnp.transpose(kv_t[0].reshape(n_batch, 4, KVH, HD, t), (0, 4, 1, 2, 3))
            shift = cols.reshape(n_batch, t, RWKV_PAD)[:, t - 1:, :RWKV_COLS]
        else:
            cache4 = jnp.transpose(cache_nsa[l], (0, 2, 3, 4, 1)).reshape(cache_nsa.shape[1], 4, DKV, page)
            win4 = jnp.transpose(state_win[l], (0, 2, 3, 4, 1)).reshape(n, 2, DKV, wb)
            wkv4 = jnp.transpose(state_wkv[l], (1, 2, 3, 0))
            prev_t = jnp.transpose(state_shift[l].reshape(n, RWKV_COLS))
            lanes = lambda a: jnp.broadcast_to(a.reshape(-1, 1), (a.size, n))
            params_t = [lanes(rw_mu[l]), lanes(rw_w0[l]), rw_w2[l].T.astype(BF16), lanes(rw_a0[l]),
                        rw_a2[l].T.astype(BF16), rw_g2[l].T.astype(BF16), lanes(rw_kk[l]), lanes(rw_ka[l])]
            cols_t, kv_t, kwvw_t, *vecs_t = _rwkv_sample_prep_t(cols, kv_rows, kwvw, prev_t, params_t)
            n_seg = past // CMP_STRIDE
            mcs = _cmp_to_sel(n_seg - 1, past // SEL_BLOCK + 1, n_seg)
            o_attn, win4_new = _nsa_sample_t(
                cache4, page_table, win4,
                q_raw.astype(F32).reshape(n, NH, HD), q_rot.astype(F32).reshape(n, NH, HD),
                gates[:, :3 * NH].reshape(n, NH, 3), kv_rows.reshape(n, 1, 4 * DKV), kwvw.reshape(n, 1, 2 * DKV),
                kwvw_t, cmp_k, cmp_v, mcs, _block_expand(past), SAMPLE_SEQS_PER_STEP)
            o_attn = o_attn.reshape(n, DH)
            o_rwkv_t, wkv4_new = _rwkv_sample_step_t(wkv4, vecs_t, lanes(rw_rk[l]), lanes(rw_gn_w[l]),
                                                     lanes(rw_gn_b[l]))
            kv_out = jnp.transpose(kv_t.reshape(4, KVH, HD, n), (3, 0, 1, 2))
            win = jnp.transpose(win4_new.reshape(n, 2, KVH, HD, wb), (0, 4, 1, 2, 3))
            wkv = jnp.transpose(wkv4_new, (3, 0, 1, 2))
            shift = jnp.transpose(cols_t)[:, None, :]
            h2 = _outproj_t(o_attn, o_rwkv_t, w_oa, w_or, h1, vec(norm_mix_post[l]))
        if is_prompt:
            h2 = _outproj(o_attn, o_rwkv, w_oa, w_or, h1, vec(norm_mix_post[l]), tm)
        y = _ffn_half(h2, vec(norm_f2_pre[l]), w2g, w2d, vec(norm_f2_post[l]), None, tm, 512)
        outs[name] = (y, kv_out, win, wkv, shift)

    yp, kvp, winp, wkvp, shp = outs["p"]
    ys, kvs, wins, wkvs, shs = outs["s"]
    return (yp.reshape(bp, t, d), ys.reshape(bs, 1, d),
            kvp.reshape(1, bp, t, 4, KVH, HD), kvs.reshape(1, bs, 1, 4, KVH, HD),
            winp.reshape(1, bp, wb, 2, KVH, HD), wins.reshape(1, bs, wb, 2, KVH, HD),
            wkvp[None], wkvs[None], shp[None], shs[None])
```

```python
import functools

import numpy as np
import jax
import jax.numpy as jnp
from jax import lax
from jax.experimental import pallas as pl
from jax.experimental.pallas import tpu as pltpu

F32 = jnp.float32
BF16 = jnp.bfloat16

HD = 64
KVH = 4
GQA = 4
NH = 16
DH = NH * HD
DKV = KVH * HD
ROPE_HALF = 8
ROPE_THETA = 500000.0
CMP_BLOCK = 32
CMP_STRIDE = 16
CMP_HIDDEN = 128
SEL_BLOCK = 64
SEL_TOPN = 16
N_LOCAL_FORCED = 2
WINDOW = 512
Q_BLOCK = 128
LORA_W = 64
LORA_A = 64
LORA_G = 160
RMS_EPS = 1e-6
GN_EPS = 64e-5
NEG = -1e30
FORCE_BONUS = 1e4
LOG2_E = 1.4426950408889634
MASK_BIAS = -1e30
M_FLOOR = -1e20
NSA_COLS = DH + 6 * DKV + 3 * NH
RWKV_COLS = 3 * DH + LORA_W + LORA_A + LORA_G
NSA_PAD = 2688
RWKV_PAD = 3456
LORA_G_PAD = 256
RW_GROUP = 4
RW_CHUNK = 64
SEL_KEY_CHUNK = 512
N_SEL_LANES = 64
SAMPLE_SEQS_PER_STEP = 2

VMEM_LIMIT_BYTES = 56 * 1024 * 1024


def _cparams(n_axes):
    return pltpu.CompilerParams(dimension_semantics=("arbitrary",) * n_axes,
                                vmem_limit_bytes=VMEM_LIMIT_BYTES)


def _dot(a, b):
    return jnp.dot(a, b, preferred_element_type=F32)


def _dot_nt(a, b):
    return lax.dot_general(a, b, (((1,), (1,)), ((), ())), preferred_element_type=F32)


def _dot_tn(a, b):
    return lax.dot_general(a, b, (((0,), (0,)), ((), ())), preferred_element_type=F32)


def _dot_split3(a, b_bf16):
    hi = a.astype(BF16)
    r1 = a - hi.astype(F32)
    mid = r1.astype(BF16)
    lo = (r1 - mid.astype(F32)).astype(BF16)
    return _dot(hi, b_bf16) + _dot(mid, b_bf16) + _dot(lo, b_bf16)


def _rms(x, g):
    ms = jnp.mean(x * x, axis=-1, keepdims=True)
    return x * lax.rsqrt(ms + RMS_EPS) * g


def _sigmoid(x):
    return jax.nn.sigmoid(x)


def _ffn_body(has_next, x_ref, gpre_ref, wg_ref, wu_ref, wd_ref, gpost_ref, *rest):
    if has_next:
        gnext_ref, o_ref, on_ref, xn_ref, acc_ref = rest
    else:
        o_ref, xn_ref, acc_ref = rest
    j = pl.program_id(1)

    @pl.when(j == 0)
    def _():
        xn_ref[...] = _rms(x_ref[...], gpre_ref[...]).astype(BF16)
        acc_ref[...] = jnp.zeros_like(acc_ref)

    xn = xn_ref[...]
    g = _dot(xn, wg_ref[...])
    u = _dot(xn, wu_ref[...])
    act = ((g * _sigmoid(g)) * u).astype(BF16)
    acc_ref[...] += _dot(act, wd_ref[...])

    @pl.when(j == pl.num_programs(1) - 1)
    def _():
        y = x_ref[...] + 0.5 * _rms(acc_ref[...], gpost_ref[...])
        o_ref[...] = y
        if has_next:
            on_ref[...] = _rms(y, gnext_ref[...]).astype(BF16)


def _ffn_half(x, g_pre, w_gu, w_dn, g_post, g_next, tm, tf):
    n, d = x.shape
    f = w_dn.shape[0]
    nj = f // tf
    has_next = g_next is not None
    row = lambda i, j: (i, 0)
    const = lambda i, j: (0, 0)
    in_specs = [
        pl.BlockSpec((tm, d), row),
        pl.BlockSpec((1, d), const),
        pl.BlockSpec((d, tf), lambda i, j: (0, j)),
        pl.BlockSpec((d, tf), lambda i, j: (0, j + nj)),
        pl.BlockSpec((tf, d), lambda i, j: (j, 0)),
        pl.BlockSpec((1, d), const),
    ]
    args = [x, g_pre, w_gu, w_gu, w_dn, g_post]
    out_shape = [jax.ShapeDtypeStruct((n, d), F32)]
    out_specs = [pl.BlockSpec((tm, d), row)]
    if has_next:
        in_specs.append(pl.BlockSpec((1, d), const))
        args.append(g_next)
        out_shape.append(jax.ShapeDtypeStruct((n, d), BF16))
        out_specs.append(pl.BlockSpec((tm, d), row))
    res = pl.pallas_call(
        functools.partial(_ffn_body, has_next),
        grid=(n // tm, nj),
        in_specs=in_specs,
        out_specs=out_specs,
        out_shape=out_shape,
        scratch_shapes=[pltpu.VMEM((tm, d), BF16), pltpu.VMEM((tm, d), F32)],
        compiler_params=_cparams(2),
        name="ffn_half",
    )(*args)
    return res if has_next else res[0]


def _rope(x, c, s_lo, s_hi):
    w = x.shape[1]
    reps = w // 128
    tile = lambda t: t if reps == 1 else jnp.concatenate([t] * reps, axis=1)
    up = pltpu.roll(x, w - ROPE_HALF, 1)
    dn = pltpu.roll(x, ROPE_HALF, 1)
    return x * tile(c) + up * tile(s_lo) + dn * tile(s_hi)


def _inproj_nsa_body(seq_len, hn_ref, w_ref, c_ref, slo_ref, shi_ref,
                     qraw_ref, qrot_ref, kv_ref, kwvw_ref, ks_ref, vs_ref, kw_ref, vw_ref, gates_ref, *kvt_ref):
    p = _dot(hn_ref[...], w_ref[...])
    tm = p.shape[0]
    c, s_lo, s_hi = c_ref[...], slo_ref[...], shi_ref[...]
    q = p[:, 0:DH] * (HD ** -0.5 * LOG2_E)
    qraw_ref[...] = q.astype(BF16)
    qrot_ref[...] = _rope(q, c, s_lo, s_hi).astype(BF16)
    o = DH
    ks = _rope(p[:, o + 2 * DKV:o + 3 * DKV], c, s_lo, s_hi)
    vs = p[:, o + 3 * DKV:o + 4 * DKV]
    kw = _rope(p[:, o + 4 * DKV:o + 5 * DKV], c, s_lo, s_hi)
    vw = p[:, o + 5 * DKV:o + 6 * DKV]
    kv_ref[:, 0:2 * DKV] = p[:, o:o + 2 * DKV]
    kv_ref[:, 2 * DKV:3 * DKV] = ks
    kv_ref[:, 3 * DKV:4 * DKV] = vs
    kwvw_ref[:, 0:DKV] = kw
    kwvw_ref[:, DKV:2 * DKV] = vw
    if kvt_ref:
        for j in range(4 * DKV // 128):
            kvt_ref[0][0, 128 * j:128 * (j + 1), :] = kv_ref[:, 128 * j:128 * (j + 1)].T
    vs_t, vw_t = vs.T, vw.T
    if seq_len is None:
        blk_onehot = jnp.zeros((tm, N_SEL_LANES), F32)
    else:
        pos = (pl.program_id(0) * tm + lax.broadcasted_iota(jnp.int32, (tm, 1), 0)) % seq_len
        blk_onehot = jnp.where(lax.broadcasted_iota(jnp.int32, (1, N_SEL_LANES), 1) == pos // SEL_BLOCK, 1.0, 0.0)
    for k in range(KVH):
        sl = slice(HD * k, HD * (k + 1))
        ks_ref[k] = jnp.concatenate([ks[:, sl], blk_onehot], axis=1).astype(BF16)
        vs_ref[k] = vs_t[sl, :].astype(BF16)
        kw_ref[k] = kw[:, sl].astype(BF16)
        vw_ref[k] = vw_t[sl, :].astype(BF16)
    gates_ref[...] = _sigmoid(p[:, o + 6 * DKV:o + 6 * DKV + 128])


def _inproj_nsa(hn, w_nsa, rope_c, rope_slo, rope_shi, tm, seq_len=None):
    n, d = hn.shape
    row = lambda i: (i, 0)
    hm = lambda i: (0, i, 0)
    tab_blocks = rope_c.shape[0] // tm
    tab = lambda i: (i % tab_blocks, 0)
    hm_shape = jax.ShapeDtypeStruct((KVH, n, HD), BF16)
    hm_spec = pl.BlockSpec((KVH, tm, HD), hm)
    hmt_shape = jax.ShapeDtypeStruct((KVH, HD, n), BF16)
    hmt_spec = pl.BlockSpec((KVH, HD, tm), lambda i: (0, 0, i))
    ksel_shape = jax.ShapeDtypeStruct((KVH, n, HD + N_SEL_LANES), BF16)
    ksel_spec = pl.BlockSpec((KVH, tm, HD + N_SEL_LANES), hm)
    out_specs = [pl.BlockSpec((tm, DH), row), pl.BlockSpec((tm, DH), row),
                 pl.BlockSpec((tm, 4 * DKV), row), pl.BlockSpec((tm, 2 * DKV), row),
                 ksel_spec, hmt_spec, hm_spec, hmt_spec,
                 pl.BlockSpec((tm, 128), row)]
    out_shape = [jax.ShapeDtypeStruct((n, DH), BF16), jax.ShapeDtypeStruct((n, DH), BF16),
                 jax.ShapeDtypeStruct((n, 4 * DKV), F32), jax.ShapeDtypeStruct((n, 2 * DKV), F32),
                 ksel_shape, hmt_shape, hm_shape, hmt_shape,
                 jax.ShapeDtypeStruct((n, 128), F32)]
    if seq_len is not None:
        per_seq = seq_len // tm
        out_specs.append(pl.BlockSpec((1, 4 * DKV, tm), lambda i: (i // per_seq, 0, i % per_seq)))
        out_shape.append(jax.ShapeDtypeStruct((n // seq_len, 4 * DKV, seq_len), F32))
    return pl.pallas_call(
        functools.partial(_inproj_nsa_body, seq_len),
        grid=(n // tm,),
        in_specs=[pl.BlockSpec((tm, d), row),
                  pl.BlockSpec((d, NSA_PAD), lambda i: (0, 0)),
                  pl.BlockSpec((tm, 128), tab), pl.BlockSpec((tm, 128), tab), pl.BlockSpec((tm, 128), tab)],
        out_specs=out_specs,
        out_shape=out_shape,
        compiler_params=_cparams(1),
        name="inproj_nsa",
    )(hn, w_nsa, rope_c, rope_slo, rope_shi)


def _matmul_body(x_ref, w_ref, o_ref):
    o_ref[...] = _dot(x_ref[...], w_ref[...])


def _inproj_rwkv(hn, w_rw, tm):
    n, d = hn.shape
    c = w_rw.shape[1]
    return pl.pallas_call(
        _matmul_body,
        grid=(n // tm,),
        in_specs=[pl.BlockSpec((tm, d), lambda i: (i, 0)), pl.BlockSpec((d, c), lambda i: (0, 0))],
        out_specs=pl.BlockSpec((tm, c), lambda i: (i, 0)),
        out_shape=jax.ShapeDtypeStruct((n, c), F32),
        compiler_params=_cparams(1),
        name="inproj_rwkv",
    )(hn, w_rw)


def _compress_accumulate(stage_ref, w1c_refs, n_seg):
    accs = [[jnp.zeros((n_seg, 2 * CMP_HIDDEN), F32) for _ in range(KVH)] for _ in range(2)]
    for s in range(CMP_STRIDE):
        for kv in range(2):
            w = w1c_refs[kv][s]
            for pair in range(KVH // 2):
                x = stage_ref[2 * kv + pair, pl.ds(s, n_seg, stride=CMP_STRIDE), :]
                for j in range(2):
                    k = 2 * pair + j
                    accs[kv][k] = accs[kv][k] + _dot(x[:, HD * j:HD * (j + 1)].astype(BF16), w)
    return accs


def _compress_finish(acc, pe0, w2, n_seg):
    a0 = acc[:, :CMP_HIDDEN]
    a1 = pltpu.roll(acc[:, CMP_HIDDEN:], n_seg - 1, 0)
    pre = pe0 + a0 + a1
    return _dot((pre * _sigmoid(pre)).astype(BF16), w2)


def _cmp_prompt_body(kv_ref, w1k_ref, w1v_ref, pek_ref, pev_ref, w1kf_ref, w1vf_ref, w2k_ref, w2v_ref,
                     kc_ref, vc_ref, stage_ref):
    n_seg = kv_ref.shape[0] // CMP_STRIDE
    for grp in range(4):
        stage_ref[grp] = kv_ref[:, 128 * grp:128 * (grp + 1)]
    accs = _compress_accumulate(stage_ref, (w1k_ref, w1v_ref), n_seg)
    for kv, (pe_ref, w1f_ref, w2_ref, out_ref) in enumerate(
            ((pek_ref, w1kf_ref, w2k_ref, kc_ref), (pev_ref, w1vf_ref, w2v_ref, vc_ref))):
        pe0 = _dot(pe_ref[...], w1f_ref[...])
        w2 = w2_ref[...]
        for k in range(KVH):
            c = _compress_finish(accs[kv][k], pe0, w2, n_seg)
            out_ref[0, k] = (c.T if kv == 1 else c).astype(BF16)


def _cmp_weights(pe, w1, w2):
    w1c = jnp.concatenate([w1[:CMP_STRIDE], w1[CMP_STRIDE:]], axis=-1).astype(BF16)
    return w1c, pe.reshape(1, CMP_BLOCK * HD), w1.reshape(CMP_BLOCK * HD, CMP_HIDDEN), w2.astype(BF16)


def _compress_prompt(kv_rows, cmp_k, cmp_v, n_batch, t):
    n_seg = t // CMP_STRIDE
    w1k, pek, w1kf, w2k = cmp_k
    w1v, pev, w1vf, w2v = cmp_v
    full = lambda a: pl.BlockSpec(a.shape, lambda b: (0,) * a.ndim)
    out_shape = jax.ShapeDtypeStruct((n_batch, KVH, n_seg, HD), BF16)
    out_spec = pl.BlockSpec((1, KVH, n_seg, HD), lambda b: (b, 0, 0, 0))
    out_shape_t = jax.ShapeDtypeStruct((n_batch, KVH, HD, n_seg), BF16)
    out_spec_t = pl.BlockSpec((1, KVH, HD, n_seg), lambda b: (b, 0, 0, 0))
    return pl.pallas_call(
        _cmp_prompt_body,
        grid=(n_batch,),
        in_specs=[pl.BlockSpec((t, 2 * DKV), lambda b: (b, 0)),
                  full(w1k), full(w1v), full(pek), full(pev), full(w1kf), full(w1vf), full(w2k), full(w2v)],
        out_specs=[out_spec, out_spec_t],
        out_shape=[out_shape, out_shape_t],
        scratch_shapes=[pltpu.VMEM((4, t, 128), F32)],
        compiler_params=_cparams(1),
        name="nsa_compress_prompt",
    )(kv_rows, w1k, w1v, pek, pev, w1kf, w1vf, w2k, w2v)


def _select_blocks(imp, cur, n_blocks):
    jb = lax.broadcasted_iota(jnp.int32, (1, N_SEL_LANES), 1)
    valid = (jb <= cur) & (jb < n_blocks)
    rel = cur - jb
    forced = (jb == 0) | ((rel >= 0) & (rel < N_LOCAL_FORCED))
    score = jnp.where(valid, imp + jnp.where(forced, FORCE_BONUS, 0.0), NEG)
    rank = jnp.zeros(score.shape, F32)
    for jp in range(min(n_blocks, N_SEL_LANES)):
        col = score[:, jp:jp + 1]
        tie = jnp.where(jb > jp, 1.0, 0.0)
        rank = rank + jnp.where(col > score, 1.0, jnp.where(col == score, tie, 0.0))
    return jnp.where(valid, jnp.where(rank < SEL_TOPN, 1.0, 0.0), 0.0)


def _softmax_rows(s, ok):
    sm = jnp.where(ok, s, NEG)
    m = jnp.max(sm, axis=-1, keepdims=True)
    e = jnp.where(ok, jnp.exp2(sm - m), 0.0)
    l = jnp.sum(e, axis=-1, keepdims=True)
    return e / jnp.maximum(l, 1e-30)


def _select_blocks_t(imp_t, cur, n_blocks, n_live, score_ref):
    jb = lax.broadcasted_iota(jnp.int32, (N_SEL_LANES, 1), 0)
    valid = (jb <= cur) & (jb < n_blocks)
    rel = cur - jb
    forced = (jb == 0) | ((rel >= 0) & (rel < N_LOCAL_FORCED))
    score = jnp.where(valid, imp_t + jnp.where(forced, FORCE_BONUS, 0.0), NEG)
    score_ref[...] = score

    def beats(jp):
        other = score_ref[pl.ds(jp, 1), :]
        tie = jnp.where(jb > jp, 1.0, 0.0)
        return jnp.where(other > score, 1.0, jnp.where(other == score, tie, 0.0))

    rank = lax.fori_loop(0, n_live // 2, lambda j, rank: rank + beats(2 * j) + beats(2 * j + 1),
                         jnp.zeros(score.shape, F32))
    return jnp.where(valid, jnp.where(rank < SEL_TOPN, 1.0, 0.0), 0.0)


def _attn_prompt_t_body(qraw_ref, qrot_ref, gates_ref, kc_ref, vct_ref, ks_ref, vst_ref, kw_ref, vwt_ref,
                        mcst_ref, o_ref, score_ref):
    n_bat = qraw_ref.shape[0]
    t = ks_ref.shape[1] // n_bat
    n_seg = kc_ref.shape[2]
    i = pl.program_id(0)
    s0 = i * Q_BLOCK
    qpos = s0 + lax.broadcasted_iota(jnp.int32, (1, Q_BLOCK), 1)
    n_idx = lax.broadcasted_iota(jnp.int32, (n_seg, 1), 0)
    ok_cmp = (n_idx * CMP_STRIDE + (CMP_BLOCK - 1) <= qpos) & (n_idx < n_seg - 1)
    bias_cmp = jnp.where(ok_cmp, 0.0, MASK_BIAS)
    mcst = mcst_ref[...]
    n_full_chunks = i // (SEL_KEY_CHUNK // Q_BLOCK)
    n_live_blocks = (s0 + Q_BLOCK) // SEL_BLOCK
    win_start = pl.multiple_of(jnp.maximum(s0 - WINDOW, 0), Q_BLOCK)
    win_len = WINDOW + Q_BLOCK
    wpos = win_start + lax.broadcasted_iota(jnp.int32, (win_len, 1), 0)
    dlt = qpos - wpos
    bias_win = jnp.where((dlt >= 0) & (dlt <= WINDOW), 0.0, MASK_BIAS)
    g_cols = [slice(g * Q_BLOCK, (g + 1) * Q_BLOCK) for g in range(GQA)]
    units = [(bb, k) for bb in range(n_bat) for k in range(KVH)]
    n_units = len(units)
    head_rows = lambda ref, bb, k: jnp.concatenate(
        [ref[bb, :, HD * (GQA * k + g):HD * (GQA * k + g + 1)] for g in range(GQA)], axis=0)
    qr = [head_rows(qraw_ref, bb, k) for bb, k in units]
    qs = [head_rows(qrot_ref, bb, k) for bb, k in units]

    def softmax_cols(s_t, bias):
        e_b, inv = [], []
        for cols in g_cols:
            sb = s_t[:, cols] + bias
            m = jnp.maximum(jnp.max(sb, axis=0, keepdims=True), M_FLOOR)
            e = jnp.exp2(sb - m)
            e_b.append(e)
            inv.append(1.0 / jnp.maximum(jnp.sum(e, axis=0, keepdims=True), 1e-30))
        return e_b, inv

    o_cmp, q_sel = [], []
    cmp_scores = lambda u: _dot_nt(kc_ref[units[u][0], units[u][1]], qr[u])
    s_next = cmp_scores(0)
    for u, (bb, k) in enumerate(units):
        s_cmp = s_next
        if u + 1 < n_units:
            s_next = cmp_scores(u + 1)
        e_heads, inv_heads = softmax_cols(s_cmp, bias_cmp)
        p_heads = [e * inv for e, inv in zip(e_heads, inv_heads)]
        o_cmp.append(_dot(vct_ref[bb, k], jnp.concatenate([p.astype(BF16) for p in p_heads], axis=1)))
        p_sum = p_heads[0]
        for p in p_heads[1:]:
            p_sum = p_sum + p
        hi = p_sum.astype(BF16)
        r1 = p_sum - hi.astype(F32)
        mid = r1.astype(BF16)
        lo = (r1 - mid.astype(F32)).astype(BF16)
        imp_t = _dot(mcst, hi) + _dot(mcst, mid) + _dot(mcst, lo)
        sel_u = _select_blocks_t(imp_t, qpos // SEL_BLOCK, t // SEL_BLOCK, n_live_blocks, score_ref.at[u])
        sel_q = jnp.concatenate([sel_u, jnp.zeros_like(sel_u)], axis=0).T[:, 0:N_SEL_LANES]
        sel_bias = ((sel_q - 1.0) * (-MASK_BIAS)).astype(BF16)
        q_sel.append(jnp.concatenate([qs[u], jnp.concatenate([sel_bias] * GQA, axis=0)], axis=1))

    def chunk(c, carry, diagonal):
        k0 = pl.multiple_of(c * SEL_KEY_CHUNK, SEL_KEY_CHUNK)
        if diagonal:
            kpos = k0 + lax.broadcasted_iota(jnp.int32, (SEL_KEY_CHUNK, 1), 0)
            causal_bias = jnp.where(kpos <= qpos, 0.0, MASK_BIAS)

        def scores(u):
            bb, k = units[u]
            rows = pl.ds(pl.multiple_of(bb * t + k0, SEL_KEY_CHUNK), SEL_KEY_CHUNK)
            return _dot_nt(ks_ref[k, rows, :], q_sel[u])

        out = []
        nxt = scores(0)
        for u, (bb, k) in enumerate(units):
            s_k = nxt
            if u + 1 < n_units:
                nxt = scores(u + 1)
            m, l, acc = carry[u]
            m_new, p_b, p_tot = [], [], []
            for cols in g_cols:
                sb = s_k[:, cols] + causal_bias if diagonal else s_k[:, cols]
                m_g = jnp.maximum(m[:, cols], jnp.max(sb, axis=0, keepdims=True))
                p = jnp.exp2(sb - m_g)
                m_new.append(m_g)
                p_tot.append(jnp.sum(p, axis=0, keepdims=True))
                p_b.append(p.astype(BF16))
            m_new = jnp.concatenate(m_new, axis=1)
            alpha = jnp.exp2(m - m_new)
            l = alpha * l + jnp.concatenate(p_tot, axis=1)
            cols_v = pl.ds(pl.multiple_of(bb * t + k0, SEL_KEY_CHUNK), SEL_KEY_CHUNK)
            acc = alpha * acc + _dot(vst_ref[k, :, cols_v], jnp.concatenate(p_b, axis=1))
            out.append((m_new, l, acc))
        return tuple(out)

    init = tuple((jnp.full((1, GQA * Q_BLOCK), M_FLOOR, F32), jnp.zeros((1, GQA * Q_BLOCK), F32),
                  jnp.zeros((HD, GQA * Q_BLOCK), F32)) for _ in units)
    before = lax.fori_loop(0, n_full_chunks, lambda c, carry: chunk(c, carry, False), init)
    sel_out = chunk(n_full_chunks, before, True)
    o_sel = [acc * (1.0 / jnp.maximum(l, 1e-30)) for _, l, acc in sel_out]

    win_rows = lambda bb: pl.ds(pl.multiple_of(bb * t + win_start, Q_BLOCK), win_len)
    win_scores = lambda u: _dot_nt(kw_ref[units[u][1], win_rows(units[u][0]), :], qs[u])
    gates_all = [gates_ref[bb].T for bb in range(n_bat)]
    o_t = []
    s_next = win_scores(0)
    for u, (bb, k) in enumerate(units):
        gates_t = gates_all[bb]
        s_win = s_next
        if u + 1 < n_units:
            s_next = win_scores(u + 1)
        e_heads, inv_heads = softmax_cols(s_win, bias_win)
        o_win = (_dot(vwt_ref[k, :, win_rows(bb)],
                      jnp.concatenate([e.astype(BF16) for e in e_heads], axis=1))
                 * jnp.concatenate(inv_heads, axis=1))
        for g in range(GQA):
            h = GQA * k + g
            cols = g_cols[g]
            o_t.append(gates_t[3 * h:3 * h + 1] * o_cmp[u][:, cols] + gates_t[3 * h + 1:3 * h + 2] * o_sel[u][:, cols]
                       + gates_t[3 * h + 2:3 * h + 3] * o_win[:, cols])
    for bb in range(n_bat):
        for j in range(NH // 2):
            pair = jnp.concatenate([o_t[NH * bb + 2 * j], o_t[NH * bb + 2 * j + 1]], axis=0)
            o_ref[bb, :, 2 * HD * j:2 * HD * (j + 1)] = pair.T.astype(o_ref.dtype)


def _attn_prompt(q_raw, q_rot, gates, k_cmp, v_cmp, ks, vs, kw, vw, mcs_t, n_batch, t):
    nqb = t // Q_BLOCK
    qblock = lambda width: pl.BlockSpec((n_batch, Q_BLOCK, width), lambda i: (0, i, 0))
    full = lambda a: pl.BlockSpec(a.shape, lambda i: (0,) * a.ndim)
    seq3 = lambda a: a.reshape(n_batch, t, a.shape[1])
    args = [seq3(q_raw), seq3(q_rot), seq3(gates), k_cmp, v_cmp, ks, vs, kw, vw, mcs_t]
    o = pl.pallas_call(
        _attn_prompt_t_body,
        grid=(nqb,),
        in_specs=[qblock(DH), qblock(DH), qblock(128)] + [full(a) for a in args[3:]],
        out_specs=qblock(DH),
        out_shape=jax.ShapeDtypeStruct((n_batch, t, DH), F32),
        scratch_shapes=[pltpu.VMEM((n_batch * KVH, N_SEL_LANES, Q_BLOCK), F32)],
        compiler_params=_cparams(1),
        name="nsa_attn_prompt",
    )(*args)
    return o.reshape(n_batch * t, DH)


def _pick_kv_group(full):
    hk = lax.broadcasted_iota(jnp.int32, (NH, 1), 0) // GQA
    out = jnp.zeros((NH, HD), F32)
    for k in range(KVH):
        out = out + jnp.where(hk == k, full[:, HD * k:HD * (k + 1)], 0.0)
    return out


def _kv_block_diag(q):
    qt = jnp.concatenate([q] * KVH, axis=1)
    hk = lax.broadcasted_iota(jnp.int32, (NH, DKV), 0) // GQA
    lk = lax.broadcasted_iota(jnp.int32, (NH, DKV), 1) // HD
    return jnp.where(hk == lk, qt, 0.0)


def _nsa_sample_t_body(n_pages, n_seq, pt_ref, *refs):
    all_pages = refs[:n_seq * n_pages]
    (win_ref, qraw_ref, qrot_ref, gates_ref, kvnew_ref, kwnew_ref, kwvwt_ref, perm_ref,
     w1k_ref, w1v_ref, pek_ref, pev_ref, w1kf_ref, w1vf_ref, w2k_ref, w2v_ref, mcs_ref, e_ref,
     o_ref, winout_ref, stage_ref) = refs[n_seq * n_pages:]
    page = all_pages[0].shape[3]
    n_seg = n_pages * page // CMP_STRIDE
    seg_pp = page // CMP_STRIDE

    perm = perm_ref[...]
    for kk, pg in enumerate(all_pages):
        for c in range(2):
            xs = _dot_nt(perm, pg[0, c].astype(BF16))
            for s in range(CMP_STRIDE):
                for k in range(KVH):
                    stage_ref[c, k, seg_pp * kk:seg_pp * (kk + 1), HD * s:HD * (s + 1)] = (
                        xs[seg_pp * s:seg_pp * (s + 1), HD * k:HD * (k + 1)])
    accs = [[_dot(stage_ref[kv, k].astype(BF16), w1_ref[...]) for k in range(KVH)]
            for kv, w1_ref in enumerate((w1k_ref, w1v_ref))]
    for q in range(n_seq):
        one = lambda ref: ref.at[pl.ds(q, 1)]
        _nsa_sample_one(
            all_pages[q * n_pages:(q + 1) * n_pages],
            [[a[n_seg * q:n_seg * (q + 1)] for a in row] for row in accs],
            pl.program_id(0) * n_seq + q,
            one(win_ref), one(qraw_ref), one(qrot_ref), one(gates_ref), one(kvnew_ref), one(kwnew_ref), kwvwt_ref,
            pek_ref, pev_ref, w1kf_ref, w1vf_ref, w2k_ref, w2v_ref, mcs_ref, e_ref, one(o_ref), one(winout_ref))


def _nsa_sample_one(pages, accs, b_idx, win_ref, qraw_ref, qrot_ref, gates_ref, kvnew_ref, kwnew_ref, kwvwt_ref,
                    pek_ref, pev_ref, w1kf_ref, w1vf_ref, w2k_ref, w2v_ref, mcs_ref, e_ref, o_ref, winout_ref):
    page = pages[0].shape[3]
    past = len(pages) * page
    n_seg = past // CMP_STRIDE
    qr = _kv_block_diag(qraw_ref[0])
    qs = _kv_block_diag(qrot_ref[0])
    qr_b, qs_b = qr.astype(BF16), qs.astype(BF16)
    cmp_kv = []
    for kv, (pe_ref, w1f_ref, w2_ref) in enumerate(((pek_ref, w1kf_ref, w2k_ref), (pev_ref, w1vf_ref, w2v_ref))):
        pe0 = _dot(pe_ref[...], w1f_ref[...])
        w2 = w2_ref[...]
        cmp_kv.append(jnp.concatenate(
            [_compress_finish(accs[kv][k], pe0, w2, n_seg) for k in range(KVH)], axis=1).astype(BF16))
    k_cmp, v_cmp = cmp_kv

    n_idx = lax.broadcasted_iota(jnp.int32, (1, n_seg), 1)
    ok_cmp = (n_idx * CMP_STRIDE + (CMP_BLOCK - 1) <= past) & (n_idx < n_seg - 1)
    p_cmp = _softmax_rows(_dot_nt(qr_b, k_cmp), ok_cmp)
    o_cmp = _pick_kv_group(_dot(p_cmp.astype(BF16), v_cmp))
    p_sum = jnp.concatenate(
        [jnp.sum(p_cmp[GQA * k:GQA * (k + 1)], axis=0, keepdims=True) for k in range(KVH)], axis=0)
    imp = _dot_split3(p_sum, mcs_ref[...])
    n_blocks = past // SEL_BLOCK + 1
    cur = jnp.full((KVH, 1), past // SEL_BLOCK, jnp.int32)
    sel = _select_blocks(imp, cur, n_blocks)
    sel16 = jnp.concatenate([jnp.broadcast_to(sel[k:k + 1], (GQA, N_SEL_LANES)) for k in range(KVH)], axis=0)

    kv_new = kvnew_ref[0]
    ks_new, vs_new = kv_new[:, 2 * DKV:3 * DKV], kv_new[:, 3 * DKV:4 * DKV]
    s_c = jnp.concatenate([_dot(qs_b, pg[0, 2].astype(BF16)) for pg in pages], axis=1)
    s_n = jnp.sum(qs * ks_new, axis=-1, keepdims=True)
    ok_c = _dot(sel16.astype(BF16), e_ref[...]) > 0.5
    new_blk = past // SEL_BLOCK
    ok_n = sel16[:, new_blk:new_blk + 1] > 0.5
    sm_c = jnp.where(ok_c, s_c, NEG)
    sm_n = jnp.where(ok_n, s_n, NEG)
    m = jnp.maximum(jnp.max(sm_c, axis=-1, keepdims=True), sm_n)
    e_c = jnp.where(ok_c, jnp.exp2(sm_c - m), 0.0)
    e_n = jnp.where(ok_n, jnp.exp2(sm_n - m), 0.0)
    inv = 1.0 / jnp.maximum(jnp.sum(e_c, axis=-1, keepdims=True) + e_n, 1e-30)
    p_c = (e_c * inv).astype(BF16)
    o_sel_full = (e_n * inv).astype(BF16).astype(F32) * vs_new.astype(BF16).astype(F32)
    for kk, pg in enumerate(pages):
        o_sel_full = o_sel_full + _dot_nt(p_c[:, page * kk:page * (kk + 1)], pg[0, 3].astype(BF16))
    o_sel = _pick_kv_group(o_sel_full)

    kw_new = kwnew_ref[0]
    kwn, vwn = kw_new[:, 0:DKV], kw_new[:, DKV:2 * DKV]
    s_w = _dot(qs_b, win_ref[0, 0].astype(BF16))
    s_wn = jnp.sum(qs * kwn, axis=-1, keepdims=True)
    mw = jnp.maximum(jnp.max(s_w, axis=-1, keepdims=True), s_wn)
    e_w = jnp.exp2(s_w - mw)
    e_wn = jnp.exp2(s_wn - mw)
    inv_w = 1.0 / (jnp.sum(e_w, axis=-1, keepdims=True) + e_wn)
    o_win = _pick_kv_group(_dot_nt((e_w * inv_w).astype(BF16), win_ref[0, 1].astype(BF16))
                           + (e_wn * inv_w).astype(BF16).astype(F32) * vwn.astype(BF16).astype(F32))

    g = gates_ref[0]
    o_ref[0] = g[:, 0:1] * o_cmp + g[:, 1:2] * o_sel + g[:, 2:3] * o_win

    wb = win_ref.shape[3]
    nb = kwvwt_ref.shape[1]
    mine = lax.broadcasted_iota(jnp.int32, (1, nb), 1) == b_idx
    last = lax.broadcasted_iota(jnp.int32, (1, wb), 1) == wb - 1
    for c in range(2):
        new_col = jnp.sum(jnp.where(mine, kwvwt_ref[DKV * c:DKV * (c + 1), :], 0.0), axis=-1, keepdims=True)
        winout_ref[0, c] = jnp.where(last, new_col, pltpu.roll(win_ref[0, c], wb - 1, 1))


def _nsa_sample_t(cache4, page_table, win4, q_raw, q_rot, gates, kv_new, kw_new, kwvw_t, cmp_k, cmp_v, mcs, e_mat,
                  n_seq):
    nb, n_pages = page_table.shape
    page = cache4.shape[3]
    wb = win4.shape[3]
    seg_pp = page // CMP_STRIDE
    rows = np.arange(page)
    perm = np.zeros((page, page), np.float32)
    perm[rows, (rows % seg_pp) * CMP_STRIDE + rows // seg_pp] = 1.0
    w1k, pek, w1kf, w2k = cmp_k
    w1v, pev, w1vf, w2v = cmp_v
    flat = lambda w: w.reshape(CMP_STRIDE * HD, 2 * CMP_HIDDEN)
    consts = [kwvw_t, jnp.asarray(perm, BF16), flat(w1k), flat(w1v), pek, pev, w1kf, w1vf, w2k, w2v, mcs, e_mat]
    full = lambda a: pl.BlockSpec(a.shape, lambda b, pt: (0,) * a.ndim)
    page_specs = [pl.BlockSpec((1, 4, DKV, page), functools.partial(
        lambda b, pt, kk: (pt[b * n_seq * n_pages + kk], 0, 0, 0), kk=kk)) for kk in range(n_seq * n_pages)]
    per_b = lambda shape: pl.BlockSpec((n_seq,) + shape, lambda b, pt: (b,) + (0,) * len(shape))
    grid_spec = pltpu.PrefetchScalarGridSpec(
        num_scalar_prefetch=1,
        grid=(nb // n_seq,),
        in_specs=page_specs + [per_b((2, DKV, wb)), per_b((NH, HD)), per_b((NH, HD)), per_b((NH, 3)),
                               per_b((1, 4 * DKV)), per_b((1, 2 * DKV))] + [full(a) for a in consts],
        out_specs=[per_b((NH, HD)), per_b((2, DKV, wb))],
        scratch_shapes=[pltpu.VMEM((2, KVH, n_seq * n_pages * seg_pp, CMP_STRIDE * HD), F32)],
    )
    return pl.pallas_call(
        functools.partial(_nsa_sample_t_body, n_pages, n_seq),
        grid_spec=grid_spec,
        out_shape=[jax.ShapeDtypeStruct((nb, NH, HD), F32), jax.ShapeDtypeStruct((nb, 2, DKV, wb), F32)],
        compiler_params=_cparams(1),
        name="nsa_sample",
    )(page_table.reshape(-1), *([cache4] * (n_seq * n_pages)), win4, q_raw, q_rot, gates, kv_new, kw_new, *consts)


def _rwkv_prep(cols, prev, mu, w0, w2, a0, a2, g2, kk_p, ka):
    xs = cols + (prev - cols) * mu
    r = xs[:, 0:DH]
    k = xs[:, DH:2 * DH]
    v = xs[:, 2 * DH:3 * DH]
    o = 3 * DH
    wd = xs[:, o:o + LORA_W]
    ad = xs[:, o + LORA_W:o + LORA_W + LORA_A]
    gd = xs[:, o + LORA_W + LORA_A:o + LORA_W + LORA_A + LORA_G_PAD]
    w = w0 + _dot(jnp.tanh(wd).astype(BF16), w2)
    w_log = -jax.nn.softplus(-w) - 0.5
    lw = -jnp.exp(w_log)
    a = _sigmoid(a0 + _dot(ad.astype(BF16), a2))
    g = _dot(_sigmoid(gd).astype(BF16), g2)
    kkv = k * kk_p
    k_mod = k * (1.0 + (a - 1.0) * ka)
    return r, k_mod, v, kkv, a, lw, g


def _head_ones():
    i = lax.broadcasted_iota(jnp.int32, (4 * HD, 4 * HD), 0) // HD
    j = lax.broadcasted_iota(jnp.int32, (4 * HD, 4 * HD), 1) // HD
    return jnp.where(i == j, 1.0, 0.0).astype(BF16)


def _head_sums(x, head_ones):
    rows, width = x.shape
    groups = width // (4 * HD)
    stacked = jnp.concatenate([x[:, 4 * HD * j:4 * HD * (j + 1)] for j in range(groups)], axis=0)
    sums = _dot_split3(stacked, head_ones)
    return jnp.concatenate([sums[rows * j:rows * (j + 1)] for j in range(groups)], axis=1)


def _block_diag(x):
    lane_head = lax.broadcasted_iota(jnp.int32, (1, x.shape[1]), 1) // HD
    return jnp.concatenate([jnp.where(lane_head == h, x, jnp.zeros_like(x)) for h in range(RW_GROUP)], axis=0)


def _tri_inverse_all(a_list, eye, blk16, blk32):
    b = lambda x: x.astype(BF16)
    bd = lambda x: _block_diag(b(x))
    a16 = [jnp.where(blk16, a, 0.0) for a in a_list]
    p = [_dot(b(x), bd(x)) for x in a16]
    t = [_dot(b(eye - x), bd(eye + q)) for x, q in zip(a16, p)]
    for _ in range(2):
        p = [_dot(b(q), bd(q)) for q in p]
        t = [_dot(b(x), bd(eye + q)) for x, q in zip(t, p)]
    off32 = blk32 & jnp.logical_not(blk16)
    for mask in (off32, jnp.logical_not(blk32)):
        m = [_dot(b(x), bd(jnp.where(mask, a, 0.0))) for x, a in zip(t, a_list)]
        t = [x - _dot(b(y), bd(x)) for x, y in zip(t, m)]
    return t


def _rwkv_prompt_body(cols_ref, mu_ref, w0_ref, w2_ref, a0_ref, a2_ref, g2_ref, kk_ref, ka_ref, rk_ref,
                      gnw_ref, gnb_ref, o_ref, wkv_ref, s_ref, last_ref):
    c = pl.program_id(0)
    ch = RW_CHUNK
    n_seq = cols_ref.shape[0]
    seq_rows = [slice(ch * q, ch * (q + 1)) for q in range(n_seq)]

    @pl.when(c == 0)
    def _():
        s_ref[...] = jnp.zeros_like(s_ref)
        last_ref[...] = jnp.zeros_like(last_ref)

    cols = jnp.concatenate([cols_ref[q] for q in range(n_seq)], axis=0)
    row = lax.broadcasted_iota(jnp.int32, (ch, 1), 0)
    prev = jnp.concatenate([jnp.where(row == 0, last_ref[q], pltpu.roll(cols[seq_rows[q]], 1, 0))
                            for q in range(n_seq)], axis=0)
    for q in range(n_seq):
        last_ref[q] = cols[ch * (q + 1) - 1:ch * (q + 1), :]
    r, k_mod, v, kkv, a, lw, g = _rwkv_prep(cols, prev, mu_ref[...], w0_ref[...], w2_ref[...], a0_ref[...],
                                            a2_ref[...], g2_ref[...], kk_ref[...], ka_ref[...])
    ti = lax.broadcasted_iota(jnp.int32, (n_seq * ch, n_seq * ch), 0)
    si = lax.broadcasted_iota(jnp.int32, (n_seq * ch, n_seq * ch), 1)
    ltri = jnp.where((ti >= si) & (ti // ch == si // ch), 1.0, 0.0).astype(BF16)
    cl = _cumsum_rows(lw, ltri)
    e_in = jnp.exp(cl)
    e_ex = jnp.exp(cl - lw)
    e_ng = jnp.exp(-cl)
    cl_last = [cl[ch * (q + 1) - 1:ch * (q + 1), :] for q in range(n_seq)]
    e_end = jnp.exp(jnp.concatenate([jnp.broadcast_to(x, (ch, DH)) for x in cl_last], axis=0) - cl)
    g_end = [jnp.exp(x) for x in cl_last]
    rk, gnw, gnb = rk_ref[...], gnw_ref[...], gnb_ref[...]
    b = lambda x: x.astype(BF16)
    gw = RW_GROUP * HD
    n_grp = NH // RW_GROUP
    slabs = [(seq_rows[q], slice(gw * j, gw * (j + 1)), q) for q in range(n_seq) for j in range(n_grp)]

    t_row = lax.broadcasted_iota(jnp.int32, (ch, gw), 0)
    s_col = lax.broadcasted_iota(jnp.int32, (ch, gw), 1) % HD
    strict = t_row > s_col
    causal = t_row >= s_col
    eye = jnp.where(t_row == s_col, 1.0, 0.0)
    blk16 = (t_row // 16) == (s_col // 16)
    blk32 = (t_row // 32) == (s_col // 32)
    head_ones = _head_ones()
    kk_n = kkv * jnp.minimum(lax.rsqrt(_head_sums(kkv * kkv, head_ones)), 1e12)
    beta = kk_n * a
    kq, rq = b(kk_n * e_ex), b(r * e_in)
    bd, kd = b(beta * e_ng), b(k_mod * e_ng)
    bdec, kdec = b(beta * e_end), b(k_mod * e_end)
    v_b = b(v)
    ns = range(len(slabs))
    lhs = [jnp.concatenate([kq[rs, gs], rq[rs, gs]], axis=0) for rs, gs, _ in slabs]
    s0 = [s_ref[i] for i in ns]
    quad = [_dot_nt(lhs[i], jnp.concatenate([_block_diag(bd[rs, gs]), _block_diag(kd[rs, gs])], axis=0))
            for i, (rs, gs, _) in enumerate(slabs)]
    s0t = [_dot_nt(lhs[i], _block_diag(b(s0[i]))) for i in ns]
    a_b = [jnp.where(strict, q[0:ch, 0:gw], 0.0) for q in quad]
    a_kv = [_dot(b(jnp.where(strict, quad[i][0:ch, gw:2 * gw], 0.0)), _block_diag(v_b[rs, gs]))
            for i, (rs, gs, _) in enumerate(slabs)]
    t_inv = _tri_inverse_all(a_b, eye, blk16, blk32)
    u = [-_dot(b(t_inv[i]), _block_diag(b(s0t[i][0:ch] + a_kv[i]))) for i in ns]
    y = [s0t[i][ch:2 * ch]
         + _dot(b(jnp.where(causal, quad[i][ch:2 * ch, 0:gw], 0.0)), _block_diag(b(u[i])))
         + _dot(b(jnp.where(causal, quad[i][ch:2 * ch, gw:2 * gw], 0.0)), _block_diag(v_b[rs, gs]))
         for i, (rs, gs, _) in enumerate(slabs)]
    lane_head = lax.broadcasted_iota(jnp.int32, (1, gw), 1) // HD
    for i, (rs, gs, q) in enumerate(slabs):
        cross = _dot_tn(jnp.concatenate([b(u[i]), v_b[rs, gs]], axis=0),
                        jnp.concatenate([bdec[rs, gs], kdec[rs, gs]], axis=0))
        s_add = jnp.zeros((HD, gw), F32)
        for h in range(RW_GROUP):
            s_add = s_add + jnp.where(lane_head == h, cross[HD * h:HD * (h + 1)], 0.0)
        s_ref[i] = s0[i] * g_end[q][:, gs] + s_add
    y_all = jnp.concatenate([jnp.concatenate(y[n_grp * q:n_grp * (q + 1)], axis=1) for q in range(n_seq)],
                            axis=0)
    yc = y_all - _head_sums(y_all, head_ones) * (1.0 / HD)
    var = _head_sums(yc * yc, head_ones) * (1.0 / HD)
    yn = yc * lax.rsqrt(var + GN_EPS) * gnw + gnb
    bonus = _head_sums(r * k_mod * rk, head_ones) * v
    out = ((yn + bonus) * g).astype(o_ref.dtype)
    for q in range(n_seq):
        o_ref[q] = out[seq_rows[q]]

    @pl.when(c == pl.num_programs(0) - 1)
    def _():
        for q in range(n_seq):
            for h in range(NH):
                lanes = slice(HD * (h % RW_GROUP), HD * (h % RW_GROUP + 1))
                wkv_ref[q, h] = s_ref[n_grp * q + h // RW_GROUP][:, lanes]


def _cumsum_rows(x, ltri):
    hi = x.astype(BF16)
    r1 = x - hi.astype(F32)
    mid = r1.astype(BF16)
    lo = (r1 - mid.astype(F32)).astype(BF16)
    return _dot(ltri, hi) + _dot(ltri, mid) + _dot(ltri, lo)


def _rwkv_prompt(cols, rw, n_batch, t):
    nc = t // RW_CHUNK
    full = lambda a: pl.BlockSpec(a.shape, lambda c: (0,) * a.ndim)
    chunk = lambda c: (0, c, 0)
    o, wkv = pl.pallas_call(
        _rwkv_prompt_body,
        grid=(nc,),
        in_specs=[pl.BlockSpec((n_batch, RW_CHUNK, RWKV_PAD), chunk)] + [full(a) for a in rw],
        out_specs=[pl.BlockSpec((n_batch, RW_CHUNK, DH), chunk),
                   pl.BlockSpec((n_batch, NH, HD, HD), lambda c: (0, 0, 0, 0))],
        out_shape=[jax.ShapeDtypeStruct((n_batch, t, DH), F32),
                   jax.ShapeDtypeStruct((n_batch, NH, HD, HD), F32)],
        scratch_shapes=[pltpu.VMEM((n_batch * NH // RW_GROUP, HD, RW_GROUP * HD), F32),
                        pltpu.VMEM((n_batch, 1, RWKV_PAD), F32)],
        compiler_params=_cparams(1),
        name="rwkv_prompt",
    )(cols.reshape(n_batch, t, RWKV_PAD), *rw)
    return o.reshape(n_batch * t, DH), wkv


def _rwkv_sample_prep_t_body(cols_ref, kv_ref, kwvw_ref, prev_ref, mu_ref, w0_ref, w2t_ref, a0_ref, a2t_ref,
                             g2t_ref, kk_ref, ka_ref,
                             colst_ref, kvt_ref, kwvwt_ref, r_ref, k_ref, v_ref, kkn_ref, kka_ref, d_ref, g_ref,
                             tr_ref):
    nb = cols_ref.shape[0]
    for j in range(cols_ref.shape[1] // 128):
        tr_ref[128 * j:128 * (j + 1), :] = cols_ref[:, 128 * j:128 * (j + 1)].T
    for j in range(kv_ref.shape[1] // 128):
        kvt_ref[128 * j:128 * (j + 1), :] = kv_ref[:, 128 * j:128 * (j + 1)].T
    for j in range(kwvw_ref.shape[1] // 128):
        kwvwt_ref[128 * j:128 * (j + 1), :] = kwvw_ref[:, 128 * j:128 * (j + 1)].T
    cols = tr_ref[0:RWKV_COLS, :]
    colst_ref[...] = cols
    xs = cols + (prev_ref[...] - cols) * mu_ref[...]
    r = xs[0:DH]
    k = xs[DH:2 * DH]
    v = xs[2 * DH:3 * DH]
    o = 3 * DH
    wd = xs[o:o + LORA_W]
    ad = xs[o + LORA_W:o + LORA_W + LORA_A]
    gd = xs[o + LORA_W + LORA_A:RWKV_COLS]
    w = w0_ref[...] + _dot(w2t_ref[...], jnp.tanh(wd).astype(BF16))
    w_log = -jax.nn.softplus(-w) - 0.5
    a = _sigmoid(a0_ref[...] + _dot(a2t_ref[...], ad.astype(BF16)))
    kkv = (k * kk_ref[...]).reshape(NH, HD, nb)
    norm = jnp.maximum(jnp.sqrt(jnp.sum(kkv * kkv, axis=1, keepdims=True)), 1e-12)
    kkn = (kkv / norm).reshape(DH, nb)
    r_ref[...] = r
    k_ref[...] = k * (1.0 + (a - 1.0) * ka_ref[...])
    v_ref[...] = v
    kkn_ref[...] = kkn
    kka_ref[...] = kkn * a
    d_ref[...] = jnp.exp(-jnp.exp(w_log))
    g_ref[...] = _dot(g2t_ref[...], _sigmoid(gd).astype(BF16))


def _rwkv_sample_prep_t(cols, kv_rows, kwvw, prev_t, params_t):
    nb = cols.shape[0]
    args = [cols, kv_rows, kwvw, prev_t] + list(params_t)
    full = lambda shape: pl.BlockSpec(shape, lambda i: (0,) * len(shape))
    out_rows = [RWKV_COLS, kv_rows.shape[1], kwvw.shape[1]] + [DH] * 7
    return pl.pallas_call(
        _rwkv_sample_prep_t_body,
        grid=(1,),
        in_specs=[full(a.shape) for a in args],
        out_specs=[full((rows, nb)) for rows in out_rows],
        out_shape=[jax.ShapeDtypeStruct((rows, nb), F32) for rows in out_rows],
        scratch_shapes=[pltpu.VMEM((cols.shape[1], nb), F32)],
        compiler_params=_cparams(1),
        name="rwkv_sample_prep",
    )(*args)


def _rwkv_sample_step_t_body(s_ref, r_ref, k_ref, v_ref, kkn_ref, kka_ref, d_ref, g_ref, rk_ref, gnw_ref, gnb_ref,
                             o_ref, sout_ref, y_ref):
    r, k, kkn, kka, d = r_ref[...], k_ref[...], kkn_ref[...], kka_ref[...], d_ref[...]

    def value_row(i, carry):
        s_i = s_ref[0, i]
        sa = -jnp.sum(s_i * kkn, axis=0, keepdims=True)
        s_new = s_i * d + sa * kka + v_ref[pl.ds(i, 1), :] * k
        sout_ref[0, i] = s_new
        y_ref[pl.ds(i, 1), :] = jnp.sum(s_new * r, axis=0, keepdims=True)
        return carry

    lax.fori_loop(0, HD, value_row, 0)
    y = y_ref[...]
    v = v_ref[...]
    mean = jnp.mean(y, axis=0, keepdims=True)
    var = jnp.mean(jnp.square(y - mean), axis=0, keepdims=True)
    yn = (y - mean) * lax.rsqrt(var + GN_EPS) * gnw_ref[...] + gnb_ref[...]
    bonus = jnp.sum(r * k * rk_ref[...], axis=0, keepdims=True) * v
    o_ref[...] = (yn + bonus) * g_ref[...]


def _rwkv_sample_step_t(state4, vecs_t, rk_t, gnw_t, gnb_t):
    nh, hs, _, nb = state4.shape
    head_rows = pl.BlockSpec((hs, nb), lambda h: (h, 0))
    state_spec = pl.BlockSpec((1, hs, hs, nb), lambda h: (h, 0, 0, 0))
    return pl.pallas_call(
        _rwkv_sample_step_t_body,
        grid=(nh,),
        in_specs=[state_spec] + [head_rows] * 10,
        out_specs=[head_rows, state_spec],
        out_shape=[jax.ShapeDtypeStruct((nh * hs, nb), F32), jax.ShapeDtypeStruct(state4.shape, F32)],
        scratch_shapes=[pltpu.VMEM((hs, nb), F32)],
        compiler_params=_cparams(1),
        name="rwkv_sample_step",
    )(state4, *vecs_t, rk_t, gnw_t, gnb_t)


def _outproj_t_body(oa_ref, orwt_ref, wa_ref, wr_ref, h_ref, g_ref, o_ref):
    mixed = _dot(oa_ref[...].astype(BF16), wa_ref[...]) + _dot_tn(orwt_ref[...].astype(BF16), wr_ref[...])
    o_ref[...] = h_ref[...] + _rms(mixed, g_ref[...])


def _outproj_t(o_attn, o_rwkv_t, w_a, w_r, h, g_post):
    n, d = h.shape
    full = lambda a: pl.BlockSpec(a.shape, lambda i: (0,) * a.ndim)
    args = [o_attn, o_rwkv_t, w_a, w_r, h, g_post]
    return pl.pallas_call(
        _outproj_t_body,
        grid=(1,),
        in_specs=[full(a) for a in args],
        out_specs=pl.BlockSpec((n, d), lambda i: (0, 0)),
        out_shape=jax.ShapeDtypeStruct((n, d), F32),
        compiler_params=_cparams(1),
        name="outproj_sample",
    )(*args)


def _outproj_body(oa_ref, orw_ref, wa_ref, wr_ref, h_ref, g_ref, o_ref):
    mixed = _dot(oa_ref[...].astype(BF16), wa_ref[...]) + _dot(orw_ref[...].astype(BF16), wr_ref[...])
    o_ref[...] = h_ref[...] + _rms(mixed, g_ref[...])


def _outproj(o_attn, o_rwkv, w_a, w_r, h, g_post, tm):
    n, d = h.shape
    row = lambda i: (i, 0)
    const = lambda i: (0, 0)
    return pl.pallas_call(
        _outproj_body,
        grid=(n // tm,),
        in_specs=[pl.BlockSpec((tm, DH), row), pl.BlockSpec((tm, DH), row),
                  pl.BlockSpec((DH, d), const), pl.BlockSpec((DH, d), const),
                  pl.BlockSpec((tm, d), row), pl.BlockSpec((1, d), const)],
        out_specs=pl.BlockSpec((tm, d), row),
        out_shape=jax.ShapeDtypeStruct((n, d), F32),
        compiler_params=_cparams(1),
        name="outproj",
    )(o_attn, o_rwkv, w_a, w_r, h, g_post)


def _rope_tables(pos):
    inv = ROPE_THETA ** (-jnp.arange(ROPE_HALF, dtype=F32) / ROPE_HALF)
    ang = pos.astype(F32)[:, None] * inv[None, :]
    cos, sin = jnp.cos(ang), jnp.sin(ang)
    n = pos.shape[0]
    rest = HD - 2 * ROPE_HALF
    c = jnp.concatenate([cos, cos, jnp.ones((n, rest), F32)], axis=1)
    s_lo = jnp.concatenate([-sin, jnp.zeros((n, HD - ROPE_HALF), F32)], axis=1)
    s_hi = jnp.concatenate([jnp.zeros((n, ROPE_HALF), F32), sin, jnp.zeros((n, rest), F32)], axis=1)
    two = lambda x: jnp.concatenate([x, x], axis=1)
    return two(c), two(s_lo), two(s_hi)


def _cmp_to_sel(n_cmp, n_sel, rows):
    i = np.arange(n_cmp)[:, None] * CMP_STRIDE
    j = np.arange(n_sel)[None, :] * SEL_BLOCK
    ov = np.minimum(i + CMP_BLOCK, j + SEL_BLOCK) - np.maximum(i, j)
    m = np.zeros((rows, N_SEL_LANES), np.float32)
    m[:n_cmp, :n_sel] = np.maximum(ov, 0) // CMP_STRIDE
    return jnp.asarray(m, BF16)


def _block_expand(n_keys):
    j = np.arange(N_SEL_LANES)[:, None]
    t = np.arange(n_keys)[None, :]
    return jnp.asarray((t // SEL_BLOCK == j).astype(np.float32), BF16)


def _pad_cols(x, width):
    return jnp.pad(x, ((0, 0), (0, width - x.shape[1])))


def _row_tile(n, cap):
    tm = min(n, cap)
    while n % tm:
        tm //= 2
    return tm


def kernel(x_prompt, x_sample, cache_nsa, page_table, state_win, state_wkv, state_shift, norm_f1_pre, norm_f1_post, ffn1_gu, ffn1_dn, norm_mix_pre, w_in, cmp_pe_k, cmp_w1_k, cmp_w2_k, cmp_pe_v, cmp_w1_v, cmp_w2_v, rw_mu, rw_w0, rw_w2, rw_a0, rw_a2, rw_g2, rw_kk, rw_ka, rw_rk, rw_gn_w, rw_gn_b, w_out, norm_mix_post, norm_f2_pre, ffn2_gu, ffn2_dn, norm_f2_post):
    depth = w_in.shape[0]
    assert depth == 1, "single-layer step"
    bp, t, d = x_prompt.shape
    bs, ts, _ = x_sample.shape
    assert ts == 1 and t % SEL_KEY_CHUNK == 0 and t % RW_CHUNK == 0 and WINDOW + Q_BLOCK <= t <= SEL_BLOCK * N_SEL_LANES
    n_pages = page_table.shape[1]
    page = cache_nsa.shape[2]
    past = n_pages * page
    wb = state_win.shape[2]
    assert wb == min(WINDOW, past) and past % SEL_BLOCK == 0 and page % CMP_STRIDE == 0
    l = 0

    w1g, w1d = ffn1_gu[l].astype(BF16), ffn1_dn[l].astype(BF16)
    w2g, w2d = ffn2_gu[l].astype(BF16), ffn2_dn[l].astype(BF16)
    wi = w_in[l]
    w_nsa = _pad_cols(wi[:, :NSA_COLS], NSA_PAD).astype(BF16)
    w_rw = _pad_cols(wi[:, NSA_COLS:], RWKV_PAD).astype(BF16)
    w_oa, w_or = w_out[l, :DH].astype(BF16), w_out[l, DH:].astype(BF16)
    vec = lambda a: a.reshape(1, -1)
    cmp_k = _cmp_weights(cmp_pe_k[l], cmp_w1_k[l], cmp_w2_k[l])
    cmp_v = _cmp_weights(cmp_pe_v[l], cmp_w1_v[l], cmp_w2_v[l])
    g2_pad = jnp.pad(rw_g2[l], ((0, LORA_G_PAD - LORA_G), (0, 0))).astype(BF16)
    rw_prep = [_pad_cols(vec(rw_mu[l]), RWKV_PAD), vec(rw_w0[l]), rw_w2[l].astype(BF16), vec(rw_a0[l]),
               rw_a2[l].astype(BF16), g2_pad, vec(rw_kk[l]), vec(rw_ka[l])]
    rk_row, gnw_row, gnb_row = vec(rw_rk[l]), vec(rw_gn_w[l]), vec(rw_gn_b[l])

    outs = {}
    for name, x2, n_batch in (("p", x_prompt.reshape(bp * t, d), bp), ("s", x_sample.reshape(bs, d), bs)):
        n = x2.shape[0]
        tm = _row_tile(n, 512)
        is_prompt = name == "p"
        h1, hn = _ffn_half(x2, vec(norm_f1_pre[l]), w1g, w1d, vec(norm_f1_post[l]), vec(norm_mix_pre[l]),
                           tm, 512)
        pos = jnp.arange(t) if is_prompt else jnp.full((n,), past, jnp.int32)
        rc, rlo, rhi = _rope_tables(pos)
        q_raw, q_rot, kv_rows, kwvw, ks, vs, kw, vw, gates, *kv_t = _inproj_nsa(
            hn, w_nsa, rc, rlo, rhi, tm, t if is_prompt else None)
        cols = _inproj_rwkv(hn, w_rw, tm)
        if is_prompt:
            k_cmp, v_cmp = _compress_prompt(kv_rows, cmp_k, cmp_v, n_batch, t)
            n_seg = t // CMP_STRIDE
            mcs_t = _cmp_to_sel(n_seg - 1, t // SEL_BLOCK, n_seg).T
            o_attn = _attn_prompt(q_raw, q_rot, gates, k_cmp, v_cmp, ks, vs, kw, vw, mcs_t, n_batch, t)
            o_rwkv, wkv = _rwkv_prompt(cols, rw_prep + [rk_row, gnw_row, gnb_row], n_batch, t)
            win = kwvw.reshape(n_batch, t, 2 * DKV)[:, t - wb:]
            kv_out = jnp.transpose(kv_t[0].reshape(n_batch, 4, KVH, HD, t), (0, 4, 1, 2, 3))
            shift = cols.reshape(n_batch, t, RWKV_PAD)[:, t - 1:, :RWKV_COLS]
        else:
            cache4 = jnp.transpose(cache_nsa[l], (0, 2, 3, 4, 1)).reshape(cache_nsa.shape[1], 4, DKV, page)
            win4 = jnp.transpose(state_win[l], (0, 2, 3, 4, 1)).reshape(n, 2, DKV, wb)
            wkv4 = jnp.transpose(state_wkv[l], (1, 2, 3, 0))
            prev_t = jnp.transpose(state_shift[l].reshape(n, RWKV_COLS))
            lanes = lambda a: jnp.broadcast_to(a.reshape(-1, 1), (a.size, n))
            params_t = [lanes(rw_mu[l]), lanes(rw_w0[l]), rw_w2[l].T.astype(BF16), lanes(rw_a0[l]),
                        rw_a2[l].T.astype(BF16), rw_g2[l].T.astype(BF16), lanes(rw_kk[l]), lanes(rw_ka[l])]
            cols_t, kv_t, kwvw_t, *vecs_t = _rwkv_sample_prep_t(cols, kv_rows, kwvw, prev_t, params_t)
            n_seg = past // CMP_STRIDE
            mcs = _cmp_to_sel(n_seg - 1, past // SEL_BLOCK + 1, n_seg)
            o_attn, win4_new = _nsa_sample_t(
                cache4, page_table, win4,
                q_raw.astype(F32).reshape(n, NH, HD), q_rot.astype(F32).reshape(n, NH, HD),
                gates[:, :3 * NH].reshape(n, NH, 3), kv_rows.reshape(n, 1, 4 * DKV), kwvw.reshape(n, 1, 2 * DKV),
                kwvw_t, cmp_k, cmp_v, mcs, _block_expand(past), SAMPLE_SEQS_PER_STEP)
            o_attn = o_attn.reshape(n, DH)
            o_rwkv_t, wkv4_new = _rwkv_sample_step_t(wkv4, vecs_t, lanes(rw_rk[l]), lanes(rw_gn_w[l]),
                                                     lanes(rw_gn_b[l]))
            kv_out = jnp.transpose(kv_t.reshape(4, KVH, HD, n), (3, 0, 1, 2))
            win = jnp.transpose(win4_new.reshape(n, 2, KVH, HD, wb), (0, 4, 1, 2, 3))
            wkv = jnp.transpose(wkv4_new, (3, 0, 1, 2))
            shift = jnp.transpose(cols_t)[:, None, :]
            h2 = _outproj_t(o_attn, o_rwkv_t, w_oa, w_or, h1, vec(norm_mix_post[l]))
        if is_prompt:
            h2 = _outproj(o_attn, o_rwkv, w_oa, w_or, h1, vec(norm_mix_post[l]), tm)
        y = _ffn_half(h2, vec(norm_f2_pre[l]), w2g, w2d, vec(norm_f2_post[l]), None, tm, 512)
        outs[name] = (y, kv_out, win, wkv, shift)

    yp, kvp, winp, wkvp, shp = outs["p"]
    ys, kvs, wins, wkvs, shs = outs["s"]
    return (yp.reshape(bp, t, d), ys.reshape(bs, 1, d),
            kvp.reshape(1, bp, t, 4, KVH, HD), kvs.reshape(1, bs, 1, 4, KVH, HD),
            winp.reshape(1, bp, wb, 2, KVH, HD), wins.reshape(1, bs, wb, 2, KVH, HD),
            wkvp[None], wkvs[None], shp[None], shs[None])
```
